```python
import math
import jax, jax.numpy as jnp
from jax import lax
import numpy as np

D_MODEL = 2048
BATCH = 1
SEQ = 8192
DEPTH = 1

HEAD_DIM = 128
N_Q_HEADS = 8
N_KV_HEADS = 2
GQA_GROUP = N_Q_HEADS // N_KV_HEADS
ATTN_WIDTH = N_Q_HEADS * HEAD_DIM
KV_WIDTH = N_KV_HEADS * HEAD_DIM
WINDOW = 128
BLOCK = 128
SPAN = BLOCK + 2 * WINDOW
N_BUCKETS = 32
MAX_DISTANCE = 128
POOL_SIZES = (2, 4, 8, 16)
N_POOL_GROUPS = len(POOL_SIZES)
POOL_WIDTH = D_MODEL // 2
POOL_GROUP_WIDTH = POOL_WIDTH // N_POOL_GROUPS
SPLIT_WIDTHS = (ATTN_WIDTH, KV_WIDTH, KV_WIDTH, ATTN_WIDTH, POOL_WIDTH, POOL_WIDTH)
SPLIT_POINTS = tuple(int(s) for s in np.cumsum(SPLIT_WIDTHS)[:-1])
IN_WIDTH = int(sum(SPLIT_WIDTHS))
N_BRANCHES = 2
EPS = 1e-6
NEG_INF = -1e30

kernel_name = "hybrid_swa_pool_gated_encoder"


def rmsnorm(x, g):
    xf = x.astype(jnp.float32)
    xf = xf * lax.rsqrt(jnp.mean(xf * xf, axis=-1, keepdims=True) + EPS)
    return xf.astype(x.dtype) * g


def t5_bucket(rel):
    half = N_BUCKETS // 2
    max_exact = half // 2
    ret = jnp.where(rel > 0, half, 0)
    n = jnp.abs(rel)
    nf = jnp.maximum(n, 1).astype(jnp.float32)
    large = max_exact + (jnp.log(nf / max_exact) / math.log(MAX_DISTANCE / max_exact)
                         * (half - max_exact)).astype(jnp.int32)
    large = jnp.minimum(large, half - 1)
    return ret + jnp.where(n < max_exact, n, large)


def windowed_gqa(q, k, v, rel_table, sink):
    B, S, _ = q.shape
    nblk = S // BLOCK
    qb = q.reshape(B, nblk, BLOCK, N_KV_HEADS, GQA_GROUP, HEAD_DIM)
    pad = ((0, 0), (WINDOW, WINDOW), (0, 0), (0, 0))
    kp = jnp.pad(k.reshape(B, S, N_KV_HEADS, HEAD_DIM), pad)
    vp = jnp.pad(v.reshape(B, S, N_KV_HEADS, HEAD_DIM), pad)
    idx = jnp.arange(nblk)[:, None] * BLOCK + jnp.arange(SPAN)[None, :]
    kw = kp[:, idx]
    vw = vp[:, idx]
    logits = jnp.einsum('bnqkgd,bntkd->bnkgqt', qb, kw).astype(jnp.float32) * (HEAD_DIM ** -0.5)
    rel = jnp.arange(SPAN)[None, :] - WINDOW - jnp.arange(BLOCK)[:, None]
    bias = rel_table[t5_bucket(rel)]
    bias = jnp.transpose(bias, (2, 0, 1)).reshape(N_KV_HEADS, GQA_GROUP, BLOCK, SPAN)
    key_pos = idx - WINDOW
    key_ok = (key_pos >= 0) & (key_pos < S)
    valid = (jnp.abs(rel) <= WINDOW)[None] & key_ok[:, None, :]
    logits = jnp.where(valid[None, :, None, None], logits + bias.astype(jnp.float32), NEG_INF)
    sink_l = jnp.broadcast_to(sink.astype(jnp.float32).reshape(N_KV_HEADS, GQA_GROUP, 1, 1),
                              logits.shape[:-1] + (1,))
    probs = jax.nn.softmax(jnp.concatenate([logits, sink_l], axis=-1), axis=-1)[..., :-1]
    out = jnp.einsum('bnkgqt,bntkd->bnqkgd', probs.astype(vw.dtype), vw)
    return out.reshape(B, S, ATTN_WIDTH)


def multiscale_pool(u, w_grp, scale):
    B, S, _ = u.shape
    ug = u.reshape(B, S, N_POOL_GROUPS, POOL_GROUP_WIDTH).astype(jnp.float32)
    cs = jnp.pad(jnp.cumsum(ug, axis=1), ((0, 0), (1, 0), (0, 0), (0, 0)))
    pos = jnp.arange(S)
    outs = []
    for gi, w in enumerate(POOL_SIZES):
        lo = jnp.clip(pos - w // 2, 0, S)
        hi = jnp.clip(pos + w // 2, 0, S)
        cnt = (hi - lo).astype(jnp.float32)[None, :, None]
        outs.append((cs[:, hi, gi] - cs[:, lo, gi]) / cnt - ug[:, :, gi])
    pooled = jnp.stack(outs, axis=2).astype(u.dtype)
    mixed = jnp.einsum('bsgc,gcd->bsgd', pooled, w_grp).reshape(B, S, POOL_WIDTH)
    return mixed * scale


def hybrid_layer(x, c, rel_table, w_ada, b_ada, pre_g, post_g, w_in, sink,
                 w_pool_grp, pool_scale, w_br_attn, w_br_pool, w_merge, b_merge, w_out):
    mod = jax.nn.silu(c) @ w_ada + b_ada
    shift, scale, gate = jnp.split(mod, 3, axis=-1)
    h = rmsnorm(x, pre_g) * (1.0 + scale[:, None]) + shift[:, None]
    proj = h @ w_in
    q, k, v, gate_a, u, gate_p = jnp.split(proj, SPLIT_POINTS, axis=-1)
    y_a = windowed_gqa(q, k, v, rel_table, sink) * jax.nn.silu(gate_a)
    y_p = multiscale_pool(u, w_pool_grp, pool_scale) * jax.nn.silu(gate_p)
    g = jax.nn.sigmoid(h @ w_merge + b_merge)
    g_a, g_p = jnp.split(g, N_BRANCHES, axis=-1)
    merged = g_a * (y_a @ w_br_attn) + g_p * (y_p @ w_br_pool)
    o = merged @ w_out
    return x + gate[:, None] * rmsnorm(o, post_g)


def setup_inputs(seed: int = 0) -> dict:
    key = jax.random.key(seed)
    ks = jax.random.split(key, 18)
    D = D_MODEL
    nrm = lambda k, shape, s: jax.random.normal(k, shape, jnp.float32) * s
    return {
        "x": nrm(ks[0], (BATCH, SEQ, D), 1.0),
        "c": nrm(ks[1], (BATCH, D), 1.0),
        "rel_bias_table": nrm(ks[2], (N_BUCKETS, N_Q_HEADS), 0.5),
        "w_ada": nrm(ks[3], (DEPTH, D, 3 * D), 0.1 * D ** -0.5),
        "b_ada": nrm(ks[4], (DEPTH, 3 * D), 0.02),
        "pre_norm_g": 1.0 + nrm(ks[5], (DEPTH, D), 0.05),
        "post_norm_g": 1.0 + nrm(ks[6], (DEPTH, D), 0.05),
        "w_in": nrm(ks[7], (DEPTH, D, IN_WIDTH), D ** -0.5),
        "attn_sink": nrm(ks[8], (DEPTH, N_Q_HEADS), 0.5),
        "w_pool_group": nrm(ks[9], (DEPTH, N_POOL_GROUPS, POOL_GROUP_WIDTH, POOL_GROUP_WIDTH),
                            POOL_GROUP_WIDTH ** -0.5),
        "pool_scale": 1.0 + nrm(ks[10], (DEPTH, POOL_WIDTH), 0.1),
        "w_branch_attn": nrm(ks[11], (DEPTH, ATTN_WIDTH, D), ATTN_WIDTH ** -0.5),
        "w_branch_pool": nrm(ks[12], (DEPTH, POOL_WIDTH, D), POOL_WIDTH ** -0.5),
        "w_merge": nrm(ks[13], (DEPTH, D, N_BRANCHES * D), D ** -0.5),
        "b_merge": nrm(ks[14], (DEPTH, N_BRANCHES * D), 0.02),
        "w_out": nrm(ks[15], (DEPTH, D, D), D ** -0.5),
    }


def reference(x, c, rel_bias_table, w_ada, b_ada, pre_norm_g, post_norm_g, w_in, attn_sink,
              w_pool_group, pool_scale, w_branch_attn, w_branch_pool, w_merge, b_merge, w_out):
    for l in range(DEPTH):
        x = hybrid_layer(x, c, rel_bias_table, w_ada[l], b_ada[l], pre_norm_g[l], post_norm_g[l],
                         w_in[l], attn_sink[l], w_pool_group[l], pool_scale[l],
                         w_branch_attn[l], w_branch_pool[l], w_merge[l], b_merge[l], w_out[l])
    return x
```

```python
import functools
import math

import numpy as np
import jax
import jax.numpy as jnp
from jax import lax
from jax.experimental import pallas as pl
from jax.experimental.pallas import tpu as pltpu

HEAD_DIM = 128
N_Q_HEADS = 8
N_KV_HEADS = 2
GQA_GROUP = N_Q_HEADS // N_KV_HEADS
ATTN_WIDTH = N_Q_HEADS * HEAD_DIM
KV_WIDTH = N_KV_HEADS * HEAD_DIM
WINDOW = 128
BLOCK = 128
SPAN = BLOCK + 2 * WINDOW
N_BUCKETS = 32
MAX_DISTANCE = 128
POOL_SIZES = (2, 4, 8, 16)
N_POOL_GROUPS = len(POOL_SIZES)
EPS = 1e-6
NEG_INF = -1e30
MASKED_BUCKET = N_BUCKETS

LANES = 128
BF16_SUBLANES = 16
VMEM_LIMIT_BYTES = 60 * 1024 * 1024

MOD_TN = 512
PROJ_TM = 512
MIX_TM = 256
POOL_HALO = BF16_SUBLANES


def _silu(v):
    return v * (1.0 / (1.0 + jnp.exp(-v)))


def _sigmoid(v):
    return 1.0 / (1.0 + jnp.exp(-v))


def _mod_kernel(c_ref, w_ref, b_ref, o_ref):
    s = _silu(c_ref[...])
    o_ref[...] = jnp.sum(s * w_ref[...], axis=0, keepdims=True) + b_ref[...]


def _modulation(c_col, w_ada, b_ada_row):
    d, n = w_ada.shape
    return pl.pallas_call(
        _mod_kernel,
        grid=(n // MOD_TN,),
        in_specs=[
            pl.BlockSpec((d, 1), lambda j: (0, 0)),
            pl.BlockSpec((d, MOD_TN), lambda j: (0, j)),
            pl.BlockSpec((1, MOD_TN), lambda j: (0, j)),
        ],
        out_specs=pl.BlockSpec((1, MOD_TN), lambda j: (0, j)),
        out_shape=jax.ShapeDtypeStruct((1, n), jnp.float32),
        compiler_params=pltpu.CompilerParams(
            dimension_semantics=("parallel",), vmem_limit_bytes=VMEM_LIMIT_BYTES),
        name="adaln_modulation",
    )(c_col, w_ada, b_ada_row)


def _bias_kernel(tbl_ref, bucket_ref, o_ref):
    bk = bucket_ref[...]
    for h in range(N_Q_HEADS):
        def body(b, acc):
            return jnp.where(bk == b, tbl_ref[b, h], acc)
        o_ref[h] = lax.fori_loop(0, N_BUCKETS, body,
                                 jnp.full((BLOCK, SPAN), NEG_INF, jnp.float32))


def _bias_table(rel_table, bucket):
    return pl.pallas_call(
        _bias_kernel,
        in_specs=[
            pl.BlockSpec(memory_space=pltpu.SMEM),
            pl.BlockSpec(memory_space=pltpu.VMEM),
        ],
        out_specs=pl.BlockSpec(memory_space=pltpu.VMEM),
        out_shape=jax.ShapeDtypeStruct((N_Q_HEADS, BLOCK, SPAN), jnp.float32),
        name="rel_bias_table",
    )(rel_table, bucket)


def _t5_bucket(rel):
    half = N_BUCKETS // 2
    max_exact = half // 2
    ret = jnp.where(rel > 0, half, 0)
    n = jnp.abs(rel)
    nf = jnp.maximum(n, 1).astype(jnp.float32)
    large = max_exact + (jnp.log(nf / max_exact) / math.log(MAX_DISTANCE / max_exact)
                         * (half - max_exact)).astype(jnp.int32)
    large = jnp.minimum(large, half - 1)
    return ret + jnp.where(n < max_exact, n, large)


def _bucket_index_table():
    rel = jnp.arange(SPAN)[None, :] - WINDOW - jnp.arange(BLOCK)[:, None]
    return jnp.where(jnp.abs(rel) <= WINDOW, _t5_bucket(rel), MASKED_BUCKET).astype(jnp.int32)


def _proj_kernel(x_ref, shift_ref, scale_ref, g_ref, w_ref, o_ref):
    x = x_ref[...]
    ms = jnp.mean(x * x, axis=-1, keepdims=True)
    gain = g_ref[...] * (1.0 + scale_ref[...])
    h = (x * lax.rsqrt(ms + EPS)) * gain + shift_ref[...]
    o_ref[...] = jnp.dot(h.astype(jnp.bfloat16), w_ref[...],
                         preferred_element_type=jnp.float32).astype(o_ref.dtype)


def _projection(x2, mod, pre_g_row, w_bf16, tn, name):
    s, d = x2.shape
    n = w_bf16.shape[1]
    return pl.pallas_call(
        _proj_kernel,
        grid=(n // tn, s // PROJ_TM),
        in_specs=[
            pl.BlockSpec((PROJ_TM, d), lambda j, i: (i, 0)),
            pl.BlockSpec((1, d), lambda j, i: (0, 0)),
            pl.BlockSpec((1, d), lambda j, i: (0, 1)),
            pl.BlockSpec((1, d), lambda j, i: (0, 0)),
            pl.BlockSpec((d, tn), lambda j, i: (0, j)),
        ],
        out_specs=pl.BlockSpec((PROJ_TM, tn), lambda j, i: (i, j)),
        out_shape=jax.ShapeDtypeStruct((s, n), jnp.bfloat16),
        compiler_params=pltpu.CompilerParams(
            dimension_semantics=("parallel", "parallel"), vmem_limit_bytes=VMEM_LIMIT_BYTES),
        name=name,
    )(x2, mod, mod, pre_g_row, w_bf16)


def _mixer_kernel(sink_ref,
                  gm_ref, q_ref, ga_ref, u_ref, gp_ref,
                  kv_ref, kvp_ref, kvn_ref, up_ref, un_ref,
                  x_ref, gate_ref, postg_ref, bm_ref, pscale_ref, bias_ref,
                  wa_ref, wp_ref, wout_ref, wpool_ref,
                  o_ref,
                  ya_ref, yp_ref, uext_ref,
                  *, seq_len):
    i = pl.program_id(0)
    tm = q_ref.shape[0]
    n_sub = tm // BLOCK
    n_blocks = seq_len // BLOCK
    f32 = jnp.float32
    bf16 = jnp.bfloat16

    kv_ext = jnp.concatenate([kvp_ref[...], kv_ref[...], kvn_ref[...]], axis=0)
    col = lax.broadcasted_iota(jnp.int32, (1, SPAN), 1)
    for b in range(n_sub):
        blk = i * n_sub + b
        first_valid = jnp.where(blk == 0, WINDOW, 0)
        end_valid = jnp.where(blk == n_blocks - 1, WINDOW + BLOCK, SPAN)
        edge = jnp.where((col < first_valid) | (col >= end_valid), NEG_INF, 0.0).astype(f32)
        kvw = kv_ext[b * BLOCK:b * BLOCK + SPAN]
        heads_out = []
        for kvh in range(N_KV_HEADS):
            k = kvw[:, kvh * HEAD_DIM:(kvh + 1) * HEAD_DIM]
            v = kvw[:, KV_WIDTH + kvh * HEAD_DIM:KV_WIDTH + (kvh + 1) * HEAD_DIM]
            h0 = kvh * GQA_GROUP
            qs = jnp.concatenate(
                [q_ref[b * BLOCK:(b + 1) * BLOCK, (h0 + g) * HEAD_DIM:(h0 + g + 1) * HEAD_DIM]
                 for g in range(GQA_GROUP)], axis=0)
            logits = lax.dot_general(qs, k, (((1,), (1,)), ((), ())),
                                     preferred_element_type=f32)
            bias = bias_ref[h0:h0 + GQA_GROUP].reshape(GQA_GROUP * BLOCK, SPAN)
            logits = logits * (HEAD_DIM ** -0.5) + bias + edge
            sink = jnp.concatenate(
                [jnp.full((BLOCK, 1), sink_ref[h0 + g], f32) for g in range(GQA_GROUP)], axis=0)
            m = jnp.maximum(jnp.max(logits, axis=-1, keepdims=True), sink)
            p = jnp.exp(logits - m)
            denom = jnp.sum(p, axis=-1, keepdims=True) + jnp.exp(sink - m)
            o = jnp.dot(p.astype(bf16), v, preferred_element_type=f32) / denom
            heads_out += [o[g * BLOCK:(g + 1) * BLOCK] for g in range(GQA_GROUP)]
        y = jnp.concatenate(heads_out, axis=1)
        y = y * _silu(ga_ref[b * BLOCK:(b + 1) * BLOCK].astype(f32))
        ya_ref[b * BLOCK:(b + 1) * BLOCK] = y.astype(bf16)

    halo_row = lax.broadcasted_iota(jnp.int32, (POOL_HALO, 1), 0)
    up = jnp.where(i * tm - POOL_HALO + halo_row >= 0, up_ref[...].astype(f32), 0.0)
    un = jnp.where((i + 1) * tm + halo_row < seq_len, un_ref[...].astype(f32), 0.0)
    u_main = u_ref[...].astype(f32)
    uext_ref[0:POOL_HALO] = up
    uext_ref[POOL_HALO:POOL_HALO + tm] = u_main
    uext_ref[POOL_HALO + tm:POOL_HALO + tm + POOL_HALO] = un
    pos = i * tm + lax.broadcasted_iota(jnp.int32, (tm, 1), 0)
    gw = u_ref.shape[1] // N_POOL_GROUPS
    for gi, w in enumerate(POOL_SIZES):
        c0 = gi * gw
        half = w // 2
        acc = uext_ref[POOL_HALO - half:POOL_HALO - half + tm, c0:c0 + gw]
        for dlt in range(-half + 1, half):
            acc = acc + uext_ref[POOL_HALO + dlt:POOL_HALO + dlt + tm, c0:c0 + gw]
        cnt = (jnp.minimum(pos + half, seq_len) - jnp.maximum(pos - half, 0)).astype(f32)
        pooled = acc / cnt - u_main[:, c0:c0 + gw]
        mixed = jnp.dot(pooled.astype(bf16), wpool_ref[gi], preferred_element_type=f32)
        ypg = mixed * pscale_ref[:, c0:c0 + gw] * _silu(gp_ref[:, c0:c0 + gw].astype(f32))
        yp_ref[:, c0:c0 + gw] = ypg.astype(bf16)

    d = x_ref.shape[1]
    bra = jnp.dot(ya_ref[...], wa_ref[...], preferred_element_type=f32)
    brp = jnp.dot(yp_ref[...], wp_ref[...], preferred_element_type=f32)
    g_a = _sigmoid(gm_ref[:, 0:d].astype(f32) + bm_ref[:, 0:d])
    g_p = _sigmoid(gm_ref[:, d:2 * d].astype(f32) + bm_ref[:, d:2 * d])
    merged = (g_a * bra + g_p * brp).astype(bf16)
    o = jnp.dot(merged, wout_ref[...], preferred_element_type=f32)
    ms = jnp.mean(o * o, axis=-1, keepdims=True)
    on = (o * lax.rsqrt(ms + EPS)) * postg_ref[...]
    o_ref[...] = x_ref[...] + gate_ref[...] * on


def _mixer(gm, proj, x2, mod, post_g_row, b_merge_row, pool_scale_row, bias_tbl, sink,
           wa, wp, wout, wpool):
    s, d = x2.shape
    tm = MIX_TM
    r_kv = tm // BLOCK
    r_u = tm // POOL_HALO
    last_kv = s // BLOCK - 1
    last_u = s // POOL_HALO - 1
    aw = ATTN_WIDTH
    kvw = 2 * KV_WIDTH
    pw = (proj.shape[1] - 2 * aw - kvw) // 2
    assert aw == pw and (4 * aw) % kvw == 0, "column-block indices assume these widths"
    kv_col = (4 * aw) // kvw

    const2 = lambda i: (0, 0)
    single = pl.Buffered(1)
    in_specs = [
        pl.BlockSpec(memory_space=pltpu.SMEM),
        pl.BlockSpec((tm, 2 * d), lambda i: (i, 0)),
        pl.BlockSpec((tm, aw), lambda i: (i, 0)),
        pl.BlockSpec((tm, aw), lambda i: (i, 1)),
        pl.BlockSpec((tm, pw), lambda i: (i, 2)),
        pl.BlockSpec((tm, pw), lambda i: (i, 3)),
        pl.BlockSpec((tm, kvw), lambda i: (i, kv_col)),
        pl.BlockSpec((BLOCK, kvw), lambda i: (jnp.maximum(i * r_kv - 1, 0), kv_col)),
        pl.BlockSpec((BLOCK, kvw), lambda i: (jnp.minimum((i + 1) * r_kv, last_kv), kv_col)),
        pl.BlockSpec((POOL_HALO, pw), lambda i: (jnp.maximum(i * r_u - 1, 0), 2)),
        pl.BlockSpec((POOL_HALO, pw), lambda i: (jnp.minimum((i + 1) * r_u, last_u), 2)),
        pl.BlockSpec((tm, d), lambda i: (i, 0)),
        pl.BlockSpec((1, d), lambda i: (0, 2)),
        pl.BlockSpec((1, d), const2),
        pl.BlockSpec((1, 2 * d), const2),
        pl.BlockSpec((1, pw), const2),
        pl.BlockSpec((N_Q_HEADS, BLOCK, SPAN), lambda i: (0, 0, 0), pipeline_mode=single),
        pl.BlockSpec((aw, d), const2, pipeline_mode=single),
        pl.BlockSpec((pw, d), const2, pipeline_mode=single),
        pl.BlockSpec((d, d), const2, pipeline_mode=single),
        pl.BlockSpec(wpool.shape, lambda i: (0, 0, 0), pipeline_mode=single),
    ]
    return pl.pallas_call(
        functools.partial(_mixer_kernel, seq_len=s),
        grid=(s // tm,),
        in_specs=in_specs,
        out_specs=pl.BlockSpec((tm, d), lambda i: (i, 0)),
        out_shape=jax.ShapeDtypeStruct((s, d), jnp.float32),
        scratch_shapes=[
            pltpu.VMEM((tm, aw), jnp.bfloat16),
            pltpu.VMEM((tm, pw), jnp.bfloat16),
            pltpu.VMEM((tm + 2 * POOL_HALO, pw), jnp.float32),
        ],
        compiler_params=pltpu.CompilerParams(
            dimension_semantics=("parallel",), vmem_limit_bytes=VMEM_LIMIT_BYTES),
        name="mixer",
    )(sink, gm, proj, proj, proj, proj, proj, proj, proj, proj, proj,
      x2, mod, post_g_row, b_merge_row, pool_scale_row, bias_tbl, wa, wp, wout, wpool)


def _layer(x2, c, bias_tbl, w_ada, b_ada, pre_g, post_g, w_in, sink, w_pool, pool_scale,
           w_br_attn, w_br_pool, w_merge, b_merge, w_out):
    s, d = x2.shape
    bf16 = jnp.bfloat16
    a, kw = ATTN_WIDTH, KV_WIDTH
    q0, k0, ga0, u0 = 0, a, a + 2 * kw, 2 * a + 2 * kw
    pw = (w_in.shape[1] - u0) // 2
    gp0 = u0 + pw
    w_in_r = jnp.concatenate(
        [w_in[:, q0:q0 + a], w_in[:, ga0:ga0 + a], w_in[:, u0:u0 + pw], w_in[:, gp0:gp0 + pw],
         w_in[:, k0:k0 + 2 * kw]], axis=1).astype(bf16)

    mod = _modulation(c.reshape(d, 1), w_ada, b_ada.reshape(1, -1))
    pre_g_row = pre_g.reshape(1, d)
    gm = _projection(x2, mod, pre_g_row, w_merge.astype(bf16), 2 * d // 2, "merge_gate_projection")
    proj = _projection(x2, mod, pre_g_row, w_in_r, w_in_r.shape[1] // 2, "input_projection")
    return _mixer(gm, proj, x2, mod, post_g.reshape(1, d), b_merge.reshape(1, -1),
                  pool_scale.reshape(1, -1), bias_tbl, sink,
                  w_br_attn.astype(bf16), w_br_pool.astype(bf16), w_out.astype(bf16),
                  w_pool.astype(bf16))


def kernel(x, c, rel_bias_table, w_ada, b_ada, pre_norm_g, post_norm_g, w_in, attn_sink,
           w_pool_group, pool_scale, w_branch_attn, w_branch_pool, w_merge, b_merge, w_out):
    batch, s, d = x.shape
    assert batch == 1, "kernel is written for a single sequence"
    depth = w_ada.shape[0]
    bias_tbl = _bias_table(rel_bias_table, _bucket_index_table())
    x2 = x.reshape(s, d)
    for l in range(depth):
        x2 = _layer(x2, c[0], bias_tbl, w_ada[l], b_ada[l], pre_norm_g[l], post_norm_g[l],
                    w_in[l], attn_sink[l], w_pool_group[l], pool_scale[l],
                    w_branch_attn[l], w_branch_pool[l], w_merge[l], b_merge[l], w_out[l])
    return x2.reshape(batch, s, d)
```

```python
import functools
import math

import jax
import jax.numpy as jnp
from jax import lax
from jax.experimental import pallas as pl
from jax.experimental.pallas import tpu as pltpu

HEAD_DIM = 128
N_Q_HEADS = 8
N_KV_HEADS = 2
GQA_GROUP = N_Q_HEADS // N_KV_HEADS
ATTN_WIDTH = N_Q_HEADS * HEAD_DIM
KV_WIDTH = N_KV_HEADS * HEAD_DIM
WINDOW = 128
BLOCK = 128
SPAN = BLOCK + 2 * WINDOW
N_BUCKETS = 32
MAX_DISTANCE = 128
POOL_SIZES = (2, 4, 8, 16)
N_POOL_GROUPS = len(POOL_SIZES)
EPS = 1e-6
NEG_INF = -1e30
MASKED_BUCKET = N_BUCKETS
ATTN_SCALE = HEAD_DIM ** -0.5
INV_ATTN_SCALE = HEAD_DIM ** 0.5
EXP2_SCALE = ATTN_SCALE * math.log2(math.e)
BIAS_INTERIOR, BIAS_FIRST, BIAS_LAST = 0, 1, 2

SUBLANES = 8
BF16_SUBLANES = 16
VMEM_LIMIT_BYTES = 60 * 1024 * 1024

MOD_TN = 512
PROJ_TM = 512
MIX_TM = 256
MIX_CW = 512
POOL_HALO = BF16_SUBLANES


def _silu(v):
    return v * (1.0 / (1.0 + jnp.exp(-v)))


def _mod_kernel(c_ref, w_ref, b_ref, o_ref):
    s = _silu(c_ref[...])
    o_ref[...] = jnp.sum(s * w_ref[...], axis=0, keepdims=True) + b_ref[...]


def _modulation(c_col, w_ada, b_ada_row):
    d, n = w_ada.shape
    return pl.pallas_call(
        _mod_kernel,
        grid=(n // MOD_TN,),
        in_specs=[
            pl.BlockSpec((d, 1), lambda j: (0, 0)),
            pl.BlockSpec((d, MOD_TN), lambda j: (0, j)),
            pl.BlockSpec((1, MOD_TN), lambda j: (0, j)),
        ],
        out_specs=pl.BlockSpec((1, MOD_TN), lambda j: (0, j)),
        out_shape=jax.ShapeDtypeStruct((1, n), jnp.float32),
        compiler_params=pltpu.CompilerParams(
            dimension_semantics=("parallel",), vmem_limit_bytes=VMEM_LIMIT_BYTES),
        name="adaln_modulation",
    )(c_col, w_ada, b_ada_row)


def _bias_kernel(tbl_ref, bucket_ref, o_ref):
    bk = bucket_ref[...]
    col = lax.broadcasted_iota(jnp.int32, (BLOCK, SPAN), 1)
    for h in range(N_Q_HEADS):
        def body(b, acc):
            return jnp.where(bk == b, tbl_ref[b, h] * INV_ATTN_SCALE, acc)
        bias = lax.fori_loop(0, N_BUCKETS, body, jnp.full((BLOCK, SPAN), NEG_INF, jnp.float32))
        o_ref[BIAS_INTERIOR, h] = bias
        o_ref[BIAS_FIRST, h] = jnp.where(col < WINDOW, NEG_INF, bias)
        o_ref[BIAS_LAST, h] = jnp.where(col >= WINDOW + BLOCK, NEG_INF, bias)


def _bias_table(rel_table, bucket):
    return pl.pallas_call(
        _bias_kernel,
        in_specs=[
            pl.BlockSpec(memory_space=pltpu.SMEM),
            pl.BlockSpec(memory_space=pltpu.VMEM),
        ],
        out_specs=pl.BlockSpec(memory_space=pltpu.VMEM),
        out_shape=jax.ShapeDtypeStruct((3, N_Q_HEADS, BLOCK, SPAN), jnp.float32),
        name="rel_bias_table",
    )(rel_table, bucket)


def _t5_bucket(rel):
    half = N_BUCKETS // 2
    max_exact = half // 2
    ret = jnp.where(rel > 0, half, 0)
    n = jnp.abs(rel)
    nf = jnp.maximum(n, 1).astype(jnp.float32)
    large = max_exact + (jnp.log(nf / max_exact) / math.log(MAX_DISTANCE / max_exact)
                         * (half - max_exact)).astype(jnp.int32)
    large = jnp.minimum(large, half - 1)
    return ret + jnp.where(n < max_exact, n, large)


def _bucket_index_table():
    rel = jnp.arange(SPAN)[None, :] - WINDOW - jnp.arange(BLOCK)[:, None]
    return jnp.where(jnp.abs(rel) <= WINDOW, _t5_bucket(rel), MASKED_BUCKET).astype(jnp.int32)


def _proj_kernel(x_ref, shift_ref, scale_ref, g_ref, w_ref, o_ref):
    x = x_ref[...]
    ms = jnp.mean(x * x, axis=-1, keepdims=True)
    gain = g_ref[...] * (1.0 + scale_ref[...])
    h = (x * lax.rsqrt(ms + EPS)) * gain + shift_ref[...]
    o_ref[...] = jnp.dot(h.astype(jnp.bfloat16), w_ref[...],
                         preferred_element_type=jnp.float32).astype(o_ref.dtype)


def _projection(x2, mod, pre_g_row, w_bf16, tn, name):
    s, d = x2.shape
    n = w_bf16.shape[1]
    return pl.pallas_call(
        _proj_kernel,
        grid=(n // tn, s // PROJ_TM),
        in_specs=[
            pl.BlockSpec((PROJ_TM, d), lambda j, i: (i, 0)),
            pl.BlockSpec((1, d), lambda j, i: (0, 0)),
            pl.BlockSpec((1, d), lambda j, i: (0, 1)),
            pl.BlockSpec((1, d), lambda j, i: (0, 0)),
            pl.BlockSpec((d, tn), lambda j, i: (0, j)),
        ],
        out_specs=pl.BlockSpec((PROJ_TM, tn), lambda j, i: (i, j)),
        out_shape=jax.ShapeDtypeStruct((s, n), jnp.bfloat16),
        compiler_params=pltpu.CompilerParams(
            dimension_semantics=("parallel", "parallel"), vmem_limit_bytes=VMEM_LIMIT_BYTES),
        name=name,
    )(x2, mod, mod, pre_g_row, w_bf16)


def _mixer_kernel(sink_ref,
                  gm_ref, q_ref, ga_ref, u_ref, gp_ref,
                  kv_ref, kvp_ref, kvn_ref, up_ref, un_ref,
                  x_ref, gate_ref, postg_ref, bm_ref, pscale_ref, bias_ref,
                  wa_ref, wp_ref, wout_ref, wpool_ref,
                  o_ref,
                  ya_ref, yp_ref, uext_ref, merged_ref,
                  *, seq_len):
    s = pl.program_id(0)
    tm, d = x_ref.shape
    n_sub = tm // BLOCK
    n_blocks = seq_len // BLOCK
    f32 = jnp.float32
    bf16 = jnp.bfloat16
    slot_mix = s % 2
    ya_mm, yp_mm = ya_ref.at[1 - slot_mix], yp_ref.at[1 - slot_mix]
    ya_mix, yp_mix = ya_ref.at[slot_mix], yp_ref.at[slot_mix]
    i = jnp.minimum(s, seq_len // tm - 1)

    @pl.when(s == 0)
    def _():
        ya_ref[1] = jnp.zeros(ya_ref.shape[1:], ya_ref.dtype)
        yp_ref[1] = jnp.zeros(yp_ref.shape[1:], yp_ref.dtype)

    half_bm = 0.5 * bm_ref[...]

    def branch_chunk(c):
        lo, hi = c * MIX_CW, (c + 1) * MIX_CW
        bra = jnp.dot(ya_mm[...], wa_ref[:, lo:hi], preferred_element_type=f32)
        brp = jnp.dot(yp_mm[...], wp_ref[:, lo:hi], preferred_element_type=f32)
        g_a = 0.5 * jnp.tanh(gm_ref[:, lo:hi].astype(f32) + half_bm[:, lo:hi]) + 0.5
        g_p = 0.5 * jnp.tanh(gm_ref[:, d + lo:d + hi].astype(f32) + half_bm[:, d + lo:d + hi]) + 0.5
        merged_ref[:, lo:hi] = (g_a * bra + g_p * brp).astype(bf16)

    kv_ext = jnp.concatenate([kvp_ref[...], kv_ref[...], kvn_ref[...]], axis=0)

    def attn_scores(b, kvh):
        blk = i * n_sub + b
        variant = jnp.where(blk == 0, BIAS_FIRST, jnp.where(blk == n_blocks - 1, BIAS_LAST,
                                                            BIAS_INTERIOR))
        h0 = kvh * GQA_GROUP
        k = kv_ext[b * BLOCK:b * BLOCK + SPAN, kvh * HEAD_DIM:(kvh + 1) * HEAD_DIM]
        v = kv_ext[b * BLOCK:b * BLOCK + SPAN,
                   KV_WIDTH + kvh * HEAD_DIM:KV_WIDTH + (kvh + 1) * HEAD_DIM]
        qs = jnp.concatenate(
            [q_ref[b * BLOCK:(b + 1) * BLOCK, (h0 + g) * HEAD_DIM:(h0 + g + 1) * HEAD_DIM]
             for g in range(GQA_GROUP)], axis=0)
        z = lax.dot_general(qs, k, (((1,), (1,)), ((), ())), preferred_element_type=f32)
        z = z + bias_ref[variant, h0:h0 + GQA_GROUP].reshape(GQA_GROUP * BLOCK, SPAN)
        sink = jnp.concatenate(
            [jnp.full((BLOCK, 1), sink_ref[h0 + g] * INV_ATTN_SCALE, f32)
             for g in range(GQA_GROUP)], axis=0)
        m = jnp.maximum(jnp.max(z, axis=-1, keepdims=True), sink)
        p = jnp.exp2((z - m) * EXP2_SCALE)
        denom = jnp.sum(p, axis=-1, keepdims=True) + jnp.exp2((sink - m) * EXP2_SCALE)
        return p.astype(bf16), denom, v

    def attn_out(p, denom, v):
        o = jnp.dot(p, v, preferred_element_type=f32) / denom
        return [o[g * BLOCK:(g + 1) * BLOCK] for g in range(GQA_GROUP)]

    def finish_sub_block(b, heads):
        y = jnp.concatenate(heads, axis=1)
        hg = ga_ref[b * BLOCK:(b + 1) * BLOCK].astype(f32)
        ya_mix[b * BLOCK:(b + 1) * BLOCK] = (y * (hg + hg * jnp.tanh(hg))).astype(bf16)

    units = [(b, kvh) for b in range(n_sub) for kvh in range(N_KV_HEADS)]
    n_chunks = d // MIX_CW
    assert len(units) == n_chunks, "one attention unit is paired with each matmul chunk"
    pending = attn_scores(*units[0])
    heads = []
    for c in range(n_chunks):
        branch_chunk(c)
        heads += attn_out(*pending)
        if c + 1 < n_chunks:
            pending = attn_scores(*units[c + 1])
        b, kvh = units[c]
        if kvh == N_KV_HEADS - 1:
            finish_sub_block(b, heads)
            heads = []

    ext = tm + 2 * POOL_HALO
    halo_row = lax.broadcasted_iota(jnp.int32, (POOL_HALO, 1), 0)
    uext_ref[0:POOL_HALO] = jnp.where(i * tm - POOL_HALO + halo_row >= 0,
                                      up_ref[...].astype(f32), 0.0)
    uext_ref[POOL_HALO:POOL_HALO + tm] = u_ref[...].astype(f32)
    uext_ref[POOL_HALO + tm:ext] = jnp.where((i + 1) * tm + halo_row < seq_len,
                                             un_ref[...].astype(f32), 0.0)
    pos = i * tm + lax.broadcasted_iota(jnp.int32, (tm, 1), 0)
    gw = u_ref.shape[1] // N_POOL_GROUPS
    def pool_group(gi):
        w = POOL_SIZES[gi]
        c0, c1 = gi * gw, (gi + 1) * gw
        half = w // 2
        a = uext_ref[:, c0:c1]
        sh = 1
        while sh < w:
            a = a + pltpu.roll(a, ext - sh, axis=0)
            sh *= 2
        off = POOL_HALO - half
        if off % SUBLANES:
            a = pltpu.roll(a, ext - off, axis=0)
            off = 0
        cnt = (jnp.minimum(pos + half, seq_len) - jnp.maximum(pos - half, 0)).astype(f32)
        pooled = a[off:off + tm] / cnt - uext_ref[POOL_HALO:POOL_HALO + tm, c0:c1]
        mixed = jnp.dot(pooled.astype(bf16), wpool_ref[gi], preferred_element_type=f32)
        hg = gp_ref[:, c0:c1].astype(f32)
        yp_mix[:, c0:c1] = (mixed * pscale_ref[:, c0:c1] * (hg + hg * jnp.tanh(hg))).astype(bf16)

    assert N_POOL_GROUPS <= n_chunks
    ssq = jnp.zeros((tm, 1), f32)
    for c in range(n_chunks):
        lo, hi = c * MIX_CW, (c + 1) * MIX_CW
        o = jnp.dot(merged_ref[...], wout_ref[:, lo:hi], preferred_element_type=f32)
        ssq = ssq + jnp.sum(o * o, axis=-1, keepdims=True)
        o_ref[:, lo:hi] = o
        if c < N_POOL_GROUPS:
            pool_group(c)
    inv = lax.rsqrt(ssq * (1.0 / d) + EPS)
    for c in range(n_chunks):
        lo, hi = c * MIX_CW, (c + 1) * MIX_CW
        on = (o_ref[:, lo:hi] * inv) * postg_ref[:, lo:hi]
        o_ref[:, lo:hi] = x_ref[:, lo:hi] + gate_ref[:, lo:hi] * on


def _mixer(gm, proj, x2, mod, post_g_row, b_merge_row, pool_scale_row, bias_tbl, sink,
           wa, wp, wout, wpool):
    s, d = x2.shape
    tm = MIX_TM
    r_kv = tm // BLOCK
    r_u = tm // POOL_HALO
    last_kv = s // BLOCK - 1
    last_u = s // POOL_HALO - 1
    aw = ATTN_WIDTH
    kvw = 2 * KV_WIDTH
    pw = (proj.shape[1] - 2 * aw - kvw) // 2
    assert aw == pw and (4 * aw) % kvw == 0, "column-block indices assume these widths"
    assert s // BLOCK >= 2, "first and last attention block must differ"
    kv_col = (4 * aw) // kvw

    n_tok = s // tm
    mm = lambda st: jnp.maximum(st - 1, 0)
    mx = lambda st: jnp.minimum(st, n_tok - 1)
    const2 = lambda st: (0, 0)
    single = pl.Buffered(1)
    in_specs = [
        pl.BlockSpec(memory_space=pltpu.SMEM),
        pl.BlockSpec((tm, 2 * d), lambda st: (mm(st), 0)),
        pl.BlockSpec((tm, aw), lambda st: (mx(st), 0)),
        pl.BlockSpec((tm, aw), lambda st: (mx(st), 1)),
        pl.BlockSpec((tm, pw), lambda st: (mx(st), 2)),
        pl.BlockSpec((tm, pw), lambda st: (mx(st), 3)),
        pl.BlockSpec((tm, kvw), lambda st: (mx(st), kv_col)),
        pl.BlockSpec((BLOCK, kvw), lambda st: (jnp.maximum(mx(st) * r_kv - 1, 0), kv_col)),
        pl.BlockSpec((BLOCK, kvw), lambda st: (jnp.minimum((mx(st) + 1) * r_kv, last_kv), kv_col)),
        pl.BlockSpec((POOL_HALO, pw), lambda st: (jnp.maximum(mx(st) * r_u - 1, 0), 2)),
        pl.BlockSpec((POOL_HALO, pw), lambda st: (jnp.minimum((mx(st) + 1) * r_u, last_u), 2)),
        pl.BlockSpec((tm, d), lambda st: (mm(st), 0)),
        pl.BlockSpec((1, d), lambda st: (0, 2)),
        pl.BlockSpec((1, d), const2),
        pl.BlockSpec((1, 2 * d), const2),
        pl.BlockSpec((1, pw), const2),
        pl.BlockSpec(bias_tbl.shape, lambda st: (0, 0, 0, 0), pipeline_mode=single),
        pl.BlockSpec((aw, d), const2, pipeline_mode=single),
        pl.BlockSpec((pw, d), const2, pipeline_mode=single),
        pl.BlockSpec((d, d), const2, pipeline_mode=single),
        pl.BlockSpec(wpool.shape, lambda st: (0, 0, 0), pipeline_mode=single),
    ]
    return pl.pallas_call(
        functools.partial(_mixer_kernel, seq_len=s),
        grid=(n_tok + 1,),
        in_specs=in_specs,
        out_specs=pl.BlockSpec((tm, d), lambda st: (mm(st), 0)),
        out_shape=jax.ShapeDtypeStruct((s, d), jnp.float32),
        scratch_shapes=[
            pltpu.VMEM((2, tm, aw), jnp.bfloat16),
            pltpu.VMEM((2, tm, pw), jnp.bfloat16),
            pltpu.VMEM((tm + 2 * POOL_HALO, pw), jnp.float32),
            pltpu.VMEM((tm, d), jnp.bfloat16),
        ],
        compiler_params=pltpu.CompilerParams(
            dimension_semantics=("arbitrary",), vmem_limit_bytes=VMEM_LIMIT_BYTES),
        name="mixer",
    )(sink, gm, proj, proj, proj, proj, proj, proj, proj, proj, proj,
      x2, mod, post_g_row, b_merge_row, pool_scale_row, bias_tbl, wa, wp, wout, wpool)


def _layer(x2, c, bias_tbl, w_ada, b_ada, pre_g, post_g, w_in, sink, w_pool, pool_scale,
           w_br_attn, w_br_pool, w_merge, b_merge, w_out):
    s, d = x2.shape
    bf16 = jnp.bfloat16
    a, kw = ATTN_WIDTH, KV_WIDTH
    q0, k0, ga0, u0 = 0, a, a + 2 * kw, 2 * a + 2 * kw
    pw = (w_in.shape[1] - u0) // 2
    gp0 = u0 + pw
    w_in_r = jnp.concatenate(
        [w_in[:, q0:q0 + a], 0.5 * w_in[:, ga0:ga0 + a], w_in[:, u0:u0 + pw],
         0.5 * w_in[:, gp0:gp0 + pw], w_in[:, k0:k0 + 2 * kw]], axis=1).astype(bf16)

    mod = _modulation(c.reshape(d, 1), w_ada, b_ada.reshape(1, -1))
    pre_g_row = pre_g.reshape(1, d)
    gm = _projection(x2, mod, pre_g_row, (0.5 * w_merge).astype(bf16), d, "merge_gate_projection")
    proj = _projection(x2, mod, pre_g_row, w_in_r, w_in_r.shape[1] // 2, "input_projection")
    return _mixer(gm, proj, x2, mod, post_g.reshape(1, d), b_merge.reshape(1, -1),
                  pool_scale.reshape(1, -1), bias_tbl, sink,
                  w_br_attn.astype(bf16), w_br_pool.astype(bf16), w_out.astype(bf16),
                  w_pool.astype(bf16))


def kernel(x, c, rel_bias_table, w_ada, b_ada, pre_norm_g, post_norm_g, w_in, attn_sink,
           w_pool_group, pool_scale, w_branch_attn, w_branch_pool, w_merge, b_merge, w_out):
    batch, s, d = x.shape
    assert batch == 1, "kernel is written for a single sequence"
    depth = w_ada.shape[0]
    bias_tbl = _bias_table(rel_bias_table, _bucket_index_table())
    x2 = x.reshape(s, d)
    for l in range(depth):
        x2 = _layer(x2, c[0], bias_tbl, w_ada[l], b_ada[l], pre_norm_g[l], post_norm_g[l],
                    w_in[l], attn_sink[l], w_pool_group[l], pool_scale[l],
                    w_branch_attn[l], w_branch_pool[l], w_merge[l], b_merge[l], w_out[l])
    return x2.reshape(batch, s, d)
```

```python
import functools
import math

import jax
import jax.numpy as jnp
from jax import lax
from jax.experimental import pallas as pl
from jax.experimental.pallas import tpu as pltpu

HEAD_DIM = 128
N_Q_HEADS = 8
N_KV_HEADS = 2
GQA_GROUP = N_Q_HEADS // N_KV_HEADS
ATTN_WIDTH = N_Q_HEADS * HEAD_DIM
KV_WIDTH = N_KV_HEADS * HEAD_DIM
WINDOW = 128
BLOCK = 128
SPAN = BLOCK + 2 * WINDOW
N_BUCKETS = 32
MAX_DISTANCE = 128
POOL_SIZES = (2, 4, 8, 16)
N_POOL_GROUPS = len(POOL_SIZES)
EPS = 1e-6
NEG_INF = -1e30
MASKED_BUCKET = N_BUCKETS
ATTN_SCALE = HEAD_DIM ** -0.5
INV_ATTN_SCALE = HEAD_DIM ** 0.5
EXP2_SCALE = ATTN_SCALE * math.log2(math.e)

SUBLANES = 8
BF16_SUBLANES = 16
VMEM_LIMIT_BYTES = 60 * 1024 * 1024

MOD_TN = 512
TM = 512
CW = 512
GM_TM = 1024
GM_TN = 1024
POOL_HALO = BF16_SUBLANES

KV0 = 0
U0 = 2 * KV_WIDTH


def _silu(v):
    return v * (1.0 / (1.0 + jnp.exp(-v)))


def _half_silu(hv):
    return hv + hv * jnp.tanh(hv)


def _mod_kernel(c_ref, w_ref, b_ref, o_ref):
    s = _silu(c_ref[...])
    o_ref[...] = jnp.sum(s * w_ref[...], axis=0, keepdims=True) + b_ref[...]


def _modulation(c_col, w_ada, b_ada_row):
    d, n = w_ada.shape
    return pl.pallas_call(
        _mod_kernel,
        grid=(n // MOD_TN,),
        in_specs=[
            pl.BlockSpec((d, 1), lambda j: (0, 0)),
            pl.BlockSpec((d, MOD_TN), lambda j: (0, j)),
            pl.BlockSpec((1, MOD_TN), lambda j: (0, j)),
        ],
        out_specs=pl.BlockSpec((1, MOD_TN), lambda j: (0, j)),
        out_shape=jax.ShapeDtypeStruct((1, n), jnp.float32),
        compiler_params=pltpu.CompilerParams(
            dimension_semantics=("parallel",), vmem_limit_bytes=VMEM_LIMIT_BYTES),
        name="adaln_modulation",
    )(c_col, w_ada, b_ada_row)


def _bias_kernel(tbl_ref, bucket_ref, o_ref):
    bk = bucket_ref[...]
    for h in range(N_Q_HEADS):
        def body(b, acc):
            return jnp.where(bk == b, tbl_ref[b, h] * INV_ATTN_SCALE, acc)
        o_ref[h] = lax.fori_loop(0, N_BUCKETS, body,
                                 jnp.full((BLOCK, SPAN), NEG_INF, jnp.float32))


def _bias_table(rel_table, bucket):
    return pl.pallas_call(
        _bias_kernel,
        in_specs=[
            pl.BlockSpec(memory_space=pltpu.SMEM),
            pl.BlockSpec(memory_space=pltpu.VMEM),
        ],
        out_specs=pl.BlockSpec(memory_space=pltpu.VMEM),
        out_shape=jax.ShapeDtypeStruct((N_Q_HEADS, BLOCK, SPAN), jnp.float32),
        name="rel_bias_table",
    )(rel_table, bucket)


def _t5_bucket(rel):
    half = N_BUCKETS // 2
    max_exact = half // 2
    ret = jnp.where(rel > 0, half, 0)
    n = jnp.abs(rel)
    nf = jnp.maximum(n, 1).astype(jnp.float32)
    large = max_exact + (jnp.log(nf / max_exact) / math.log(MAX_DISTANCE / max_exact)
                         * (half - max_exact)).astype(jnp.int32)
    large = jnp.minimum(large, half - 1)
    return ret + jnp.where(n < max_exact, n, large)


def _bucket_index_table():
    rel = jnp.arange(SPAN)[None, :] - WINDOW - jnp.arange(BLOCK)[:, None]
    return jnp.where(jnp.abs(rel) <= WINDOW, _t5_bucket(rel), MASKED_BUCKET).astype(jnp.int32)


def _project_mix_kernel(sink_ref,
                        x_ref, shift_ref, scale_ref, g_ref, w_ref, wpool_ref, pscale_ref, bias_ref,
                        ya_ref, yp_ref, h_ref,
                        pbuf_ref, kvtail_ref, utail_ref, uext_ref,
                        *, seq_len):
    s = pl.program_id(0)
    n_tok = pl.num_programs(0) - 1
    tm, d = x_ref.shape
    n_sub = tm // BLOCK
    n_blocks = seq_len // BLOCK
    aw = ATTN_WIDTH
    pw = pscale_ref.shape[1]
    q0 = U0 + pw
    ga0 = q0 + aw
    gp0 = ga0 + aw
    n_chunks = w_ref.shape[1] // CW
    f32 = jnp.float32
    bf16 = jnp.bfloat16
    cur = s % 2
    prv = 1 - cur
    j = s - 1

    def prologue():
        x = x_ref[...]
        ms = jnp.mean(x * x, axis=-1, keepdims=True)
        gain = g_ref[...] * (1.0 + scale_ref[...])
        h_ref[...] = ((x * lax.rsqrt(ms + EPS)) * gain + shift_ref[...]).astype(bf16)

    def proj_chunk(c):
        lo, hi = c * CW, (c + 1) * CW
        pbuf_ref[cur, :, lo:hi] = jnp.dot(h_ref[...], w_ref[:, lo:hi],
                                          preferred_element_type=f32).astype(bf16)

    def window_rows(b, col0):
        r0, r1 = (b - 1) * BLOCK, (b + 2) * BLOCK
        parts = []
        if r0 < 0:
            parts.append(kvtail_ref[prv, :, col0:col0 + HEAD_DIM])
            r0 = 0
        parts.append(pbuf_ref[prv, r0:min(r1, tm), col0:col0 + HEAD_DIM])
        if r1 > tm:
            parts.append(pbuf_ref[cur, 0:r1 - tm, col0:col0 + HEAD_DIM])
        return jnp.concatenate(parts, axis=0) if len(parts) > 1 else parts[0]

    def attn_scores(b, kvh):
        blk = j * n_sub + b
        h0 = kvh * GQA_GROUP
        k = window_rows(b, KV0 + kvh * HEAD_DIM)
        v = window_rows(b, KV0 + KV_WIDTH + kvh * HEAD_DIM)
        qs = jnp.concatenate(
            [pbuf_ref[prv, b * BLOCK:(b + 1) * BLOCK,
                      q0 + (h0 + g) * HEAD_DIM:q0 + (h0 + g + 1) * HEAD_DIM]
             for g in range(GQA_GROUP)], axis=0)
        z = lax.dot_general(qs, k, (((1,), (1,)), ((), ())), preferred_element_type=f32)
        z = z + bias_ref[h0:h0 + GQA_GROUP].reshape(GQA_GROUP * BLOCK, SPAN)
        col = lax.broadcasted_iota(jnp.int32, (1, SPAN), 1)
        if b == 0:
            z = jnp.where(col < jnp.where(blk == 0, WINDOW, 0), NEG_INF, z)
        if b == n_sub - 1:
            z = jnp.where(col >= jnp.where(blk == n_blocks - 1, WINDOW + BLOCK, SPAN), NEG_INF, z)
        sink = jnp.concatenate(
            [jnp.full((BLOCK, 1), sink_ref[h0 + g] * INV_ATTN_SCALE, f32)
             for g in range(GQA_GROUP)], axis=0)
        m = jnp.maximum(jnp.max(z, axis=-1, keepdims=True), sink)
        p = jnp.exp2((z - m) * EXP2_SCALE)
        denom = jnp.sum(p, axis=-1, keepdims=True) + jnp.exp2((sink - m) * EXP2_SCALE)
        return p.astype(bf16), denom, v

    def attn_out(p, denom, v):
        o = jnp.dot(p, v, preferred_element_type=f32) / denom
        return [o[g * BLOCK:(g + 1) * BLOCK] for g in range(GQA_GROUP)]

    def finish_sub_block(b, heads):
        y = jnp.concatenate(heads, axis=1)
        hg = pbuf_ref[prv, b * BLOCK:(b + 1) * BLOCK, ga0:ga0 + aw].astype(f32)
        ya_ref[b * BLOCK:(b + 1) * BLOCK] = (y * _half_silu(hg)).astype(bf16)

    ext = tm + 2 * POOL_HALO

    def pool_group(gi):
        w = POOL_SIZES[gi]
        gw = pw // N_POOL_GROUPS
        c0, c1 = gi * gw, (gi + 1) * gw
        half = w // 2
        halo_row = lax.broadcasted_iota(jnp.int32, (POOL_HALO, 1), 0)
        uext_ref[0:POOL_HALO] = jnp.where(j * tm - POOL_HALO + halo_row >= 0,
                                          utail_ref[prv, :, c0:c1].astype(f32), 0.0)
        uext_ref[POOL_HALO:POOL_HALO + tm] = pbuf_ref[prv, :, U0 + c0:U0 + c1].astype(f32)
        uext_ref[POOL_HALO + tm:ext] = jnp.where(
            (j + 1) * tm + halo_row < seq_len,
            pbuf_ref[cur, 0:POOL_HALO, U0 + c0:U0 + c1].astype(f32), 0.0)
        a = uext_ref[...]
        sh = 1
        while sh < w:
            a = a + pltpu.roll(a, ext - sh, axis=0)
            sh *= 2
        off = POOL_HALO - half
        if off % SUBLANES:
            a = pltpu.roll(a, ext - off, axis=0)
            off = 0
        pos = j * tm + lax.broadcasted_iota(jnp.int32, (tm, 1), 0)
        cnt = (jnp.minimum(pos + half, seq_len) - jnp.maximum(pos - half, 0)).astype(f32)
        pooled = a[off:off + tm] / cnt - uext_ref[POOL_HALO:POOL_HALO + tm]
        mixed = jnp.dot(pooled.astype(bf16), wpool_ref[gi], preferred_element_type=f32)
        hg = pbuf_ref[prv, :, gp0 + c0:gp0 + c1].astype(f32)
        yp_ref[:, c0:c1] = (mixed * pscale_ref[:, c0:c1] * _half_silu(hg)).astype(bf16)

    def save_tails():
        kvtail_ref[cur] = pbuf_ref[prv, tm - BLOCK:tm, KV0:KV0 + 2 * KV_WIDTH]
        utail_ref[cur] = pbuf_ref[prv, tm - POOL_HALO:tm, U0:U0 + pw]

    units = [(b, kvh) for b in range(n_sub) for kvh in range(N_KV_HEADS)]
    u_chunks_done = (U0 + pw) // CW
    assert (U0 + pw) % CW == 0 and len(units) + 1 <= n_chunks
    assert u_chunks_done + N_POOL_GROUPS <= n_chunks

    def run(do_proj, do_mix):
        if do_proj:
            prologue()
        pending = attn_scores(*units[0]) if do_mix else None
        heads = []
        for c in range(n_chunks):
            if do_proj:
                proj_chunk(c)
            if not do_mix:
                continue
            if c < len(units):
                heads += attn_out(*pending)
                if c + 1 < len(units):
                    pending = attn_scores(*units[c + 1])
                b, kvh = units[c]
                if kvh == N_KV_HEADS - 1:
                    finish_sub_block(b, heads)
                    heads = []
            if u_chunks_done <= c < u_chunks_done + N_POOL_GROUPS:
                pool_group(c - u_chunks_done)
        if do_mix:
            save_tails()

    @pl.when(s == 0)
    def _():
        kvtail_ref[0] = jnp.zeros(kvtail_ref.shape[1:], kvtail_ref.dtype)
        utail_ref[0] = jnp.zeros(utail_ref.shape[1:], utail_ref.dtype)
        run(True, False)

    @pl.when((s > 0) & (s < n_tok))
    def _():
        run(True, True)

    @pl.when(s == n_tok)
    def _():
        run(False, True)


def _project_mix(x2, mod, pre_g_row, w_in_r, wpool, pool_scale_row, bias_tbl, sink):
    s, d = x2.shape
    tm = TM
    n_tok = s // tm
    pw = pool_scale_row.shape[1]
    assert s // BLOCK >= 2, "first and last attention block must differ"
    assert w_in_r.shape[1] == 2 * KV_WIDTH + pw + 2 * ATTN_WIDTH + pw
    cur_blk = lambda st: jnp.minimum(st, n_tok - 1)
    mix_blk = lambda st: jnp.maximum(st - 1, 0)
    const2 = lambda st: (0, 0)
    single = pl.Buffered(1)
    return pl.pallas_call(
        functools.partial(_project_mix_kernel, seq_len=s),
        grid=(n_tok + 1,),
        in_specs=[
            pl.BlockSpec(memory_space=pltpu.SMEM),
            pl.BlockSpec((tm, d), lambda st: (cur_blk(st), 0)),
            pl.BlockSpec((1, d), lambda st: (0, 0)),
            pl.BlockSpec((1, d), lambda st: (0, 1)),
            pl.BlockSpec((1, d), const2),
            pl.BlockSpec(w_in_r.shape, const2, pipeline_mode=single),
            pl.BlockSpec(wpool.shape, lambda st: (0, 0, 0), pipeline_mode=single),
            pl.BlockSpec((1, pw), const2),
            pl.BlockSpec(bias_tbl.shape, lambda st: (0, 0, 0), pipeline_mode=single),
        ],
        out_specs=[
            pl.BlockSpec((tm, ATTN_WIDTH), lambda st: (mix_blk(st), 0)),
            pl.BlockSpec((tm, pw), lambda st: (mix_blk(st), 0)),
            pl.BlockSpec((tm, d), lambda st: (cur_blk(st), 0)),
        ],
        out_shape=[
            jax.ShapeDtypeStruct((s, ATTN_WIDTH), jnp.bfloat16),
            jax.ShapeDtypeStruct((s, pw), jnp.bfloat16),
            jax.ShapeDtypeStruct((s, d), jnp.bfloat16),
        ],
        scratch_shapes=[
            pltpu.VMEM((2, tm, w_in_r.shape[1]), jnp.bfloat16),
            pltpu.VMEM((2, BLOCK, 2 * KV_WIDTH), jnp.bfloat16),
            pltpu.VMEM((2, POOL_HALO, pw), jnp.bfloat16),
            pltpu.VMEM((tm + 2 * POOL_HALO, pw // N_POOL_GROUPS), jnp.float32),
        ],
        compiler_params=pltpu.CompilerParams(
            dimension_semantics=("arbitrary",), vmem_limit_bytes=VMEM_LIMIT_BYTES),
        name="project_mix",
    )(sink, x2, mod, mod, pre_g_row, w_in_r, wpool, pool_scale_row, bias_tbl)


def _merge_gate_kernel(h_ref, w_ref, o_ref, wbf_ref):
    @pl.when(pl.program_id(1) == 0)
    def _():
        wbf_ref[...] = (0.5 * w_ref[...]).astype(wbf_ref.dtype)

    o_ref[...] = jnp.dot(h_ref[...], wbf_ref[...],
                         preferred_element_type=jnp.float32).astype(o_ref.dtype)


def _merge_gate(h, w_merge):
    s, d = h.shape
    n = w_merge.shape[1]
    return pl.pallas_call(
        _merge_gate_kernel,
        grid=(n // GM_TN, s // GM_TM),
        in_specs=[
            pl.BlockSpec((GM_TM, d), lambda j, i: (i, 0)),
            pl.BlockSpec((d, GM_TN), lambda j, i: (0, j)),
        ],
        out_specs=pl.BlockSpec((GM_TM, GM_TN), lambda j, i: (i, j)),
        out_shape=jax.ShapeDtypeStruct((s, n), jnp.bfloat16),
        scratch_shapes=[pltpu.VMEM((d, GM_TN), jnp.bfloat16)],
        compiler_params=pltpu.CompilerParams(
            dimension_semantics=("arbitrary", "arbitrary"), vmem_limit_bytes=VMEM_LIMIT_BYTES),
        name="merge_gate",
    )(h, w_merge)


def _merge_out_kernel(ya_ref, yp_ref, gm_ref, x_ref, gate_ref, postg_ref, bm_ref,
                      wa_ref, wp_ref, wout_ref, o_ref, merged_ref):
    tm, d = x_ref.shape
    n_chunks = d // CW
    f32 = jnp.float32
    half_bm = 0.5 * bm_ref[...]
    for c in range(n_chunks):
        lo, hi = c * CW, (c + 1) * CW
        bra = jnp.dot(ya_ref[...], wa_ref[:, lo:hi], preferred_element_type=f32)
        brp = jnp.dot(yp_ref[...], wp_ref[:, lo:hi], preferred_element_type=f32)
        g_a = 0.5 * jnp.tanh(gm_ref[:, lo:hi].astype(f32) + half_bm[:, lo:hi]) + 0.5
        g_p = 0.5 * jnp.tanh(gm_ref[:, d + lo:d + hi].astype(f32) + half_bm[:, d + lo:d + hi]) + 0.5
        merged_ref[:, lo:hi] = (g_a * bra + g_p * brp).astype(merged_ref.dtype)
    ssq = jnp.zeros((tm, 1), f32)
    for c in range(n_chunks):
        lo, hi = c * CW, (c + 1) * CW
        o = jnp.dot(merged_ref[...], wout_ref[:, lo:hi], preferred_element_type=f32)
        ssq = ssq + jnp.sum(o * o, axis=-1, keepdims=True)
        o_ref[:, lo:hi] = o
    inv = lax.rsqrt(ssq * (1.0 / d) + EPS)
    for c in range(n_chunks):
        lo, hi = c * CW, (c + 1) * CW
        on = (o_ref[:, lo:hi] * inv) * postg_ref[:, lo:hi]
        o_ref[:, lo:hi] = x_ref[:, lo:hi] + gate_ref[:, lo:hi] * on


def _merge_out(ya, yp, gm, x2, mod, post_g_row, b_merge_row, wa, wp, wout):
    s, d = x2.shape
    tm = TM
    aw, pw = ya.shape[1], yp.shape[1]
    row = lambda i: (i, 0)
    const2 = lambda i: (0, 0)
    single = pl.Buffered(1)
    return pl.pallas_call(
        _merge_out_kernel,
        grid=(s // tm,),
        in_specs=[
            pl.BlockSpec((tm, aw), row),
            pl.BlockSpec((tm, pw), row),
            pl.BlockSpec((tm, 2 * d), row),
            pl.BlockSpec((tm, d), row),
            pl.BlockSpec((1, d), lambda i: (0, 2)),
            pl.BlockSpec((1, d), const2),
            pl.BlockSpec((1, 2 * d), const2),
            pl.BlockSpec((aw, d), const2, pipeline_mode=single),
            pl.BlockSpec((pw, d), const2, pipeline_mode=single),
            pl.BlockSpec((d, d), const2, pipeline_mode=single),
        ],
        out_specs=pl.BlockSpec((tm, d), row),
        out_shape=jax.ShapeDtypeStruct((s, d), jnp.float32),
        scratch_shapes=[pltpu.VMEM((tm, d), jnp.bfloat16)],
        compiler_params=pltpu.CompilerParams(
            dimension_semantics=("parallel",), vmem_limit_bytes=VMEM_LIMIT_BYTES),
        name="merge_out",
    )(ya, yp, gm, x2, mod, post_g_row, b_merge_row, wa, wp, wout)


def _layer(x2, c, bias_tbl, w_ada, b_ada, pre_g, post_g, w_in, sink, w_pool, pool_scale,
           w_br_attn, w_br_pool, w_merge, b_merge, w_out):
    s, d = x2.shape
    bf16 = jnp.bfloat16
    a, kw = ATTN_WIDTH, KV_WIDTH
    q0, k0, ga0, u0 = 0, a, a + 2 * kw, 2 * a + 2 * kw
    pw = (w_in.shape[1] - u0) // 2
    gp0 = u0 + pw
    w_in_r = jnp.concatenate(
        [w_in[:, k0:k0 + 2 * kw], w_in[:, u0:u0 + pw], w_in[:, q0:q0 + a],
         0.5 * w_in[:, ga0:ga0 + a], 0.5 * w_in[:, gp0:gp0 + pw]], axis=1).astype(bf16)

    mod = _modulation(c.reshape(d, 1), w_ada, b_ada.reshape(1, -1))
    ya, yp, h = _project_mix(x2, mod, pre_g.reshape(1, d), w_in_r, w_pool.astype(bf16),
                             pool_scale.reshape(1, -1), bias_tbl, sink)
    gm = _merge_gate(h, w_merge)
    return _merge_out(ya, yp, gm, x2, mod, post_g.reshape(1, d), b_merge.reshape(1, -1),
                      w_br_attn.astype(bf16), w_br_pool.astype(bf16), w_out.astype(bf16))


def kernel(x, c, rel_bias_table, w_ada, b_ada, pre_norm_g, post_norm_g, w_in, attn_sink,
           w_pool_group, pool_scale, w_branch_attn, w_branch_pool, w_merge, b_merge, w_out):
    batch, s, d = x.shape
    assert batch == 1, "kernel is written for a single sequence"
    depth = w_ada.shape[0]
    bias_tbl = _bias_table(rel_bias_table, _bucket_index_table())
    x2 = x.reshape(s, d)
    for l in range(depth):
        x2 = _layer(x2, c[0], bias_tbl, w_ada[l], b_ada[l], pre_norm_g[l], post_norm_g[l],
                    w_in[l], attn_sink[l], w_pool_group[l], pool_scale[l],
                    w_branch_attn[l], w_branch_pool[l], w_merge[l], b_merge[l], w_out[l])
    return x2.reshape(batch, s, d)
```

```python
import functools
import math

import jax
import jax.numpy as jnp
from jax import lax
from jax.experimental import pallas as pl
from jax.experimental.pallas import tpu as pltpu

HEAD_DIM = 128
N_Q_HEADS = 8
N_KV_HEADS = 2
GQA_GROUP = N_Q_HEADS // N_KV_HEADS
ATTN_WIDTH = N_Q_HEADS * HEAD_DIM
KV_WIDTH = N_KV_HEADS * HEAD_DIM
WINDOW = 128
BLOCK = 128
SPAN = BLOCK + 2 * WINDOW
N_BUCKETS = 32
MAX_DISTANCE = 128
POOL_SIZES = (2, 4, 8, 16)
N_POOL_GROUPS = len(POOL_SIZES)
EPS = 1e-6
NEG_INF = -1e30
MASKED_BUCKET = N_BUCKETS
ATTN_SCALE = HEAD_DIM ** -0.5
INV_ATTN_SCALE = HEAD_DIM ** 0.5
EXP2_SCALE = ATTN_SCALE * math.log2(math.e)

SUBLANES = 8
BF16_SUBLANES = 16
VMEM_LIMIT_BYTES = 60 * 1024 * 1024

MOD_TK = 256
TM = 512
CW = 512
GM_TM = 2048
GM_ROWS = 512
GM_TN = 1024
POOL_HALO = BF16_SUBLANES

KV0 = 0
U0 = 2 * KV_WIDTH


def _silu(v):
    return v * (1.0 / (1.0 + jnp.exp(-v)))


def _half_silu(hv):
    return hv + hv * jnp.tanh(hv)


def _mod_kernel(c_ref, w_ref, b_ref, o_ref):
    @pl.when(pl.program_id(0) == 0)
    def _():
        o_ref[...] = b_ref[...]

    s = _silu(c_ref[...])
    o_ref[...] += jnp.sum(s * w_ref[...], axis=0, keepdims=True)


def _modulation(c_col, w_ada, b_ada_row):
    d, n = w_ada.shape
    return pl.pallas_call(
        _mod_kernel,
        grid=(d // MOD_TK,),
        in_specs=[
            pl.BlockSpec((MOD_TK, 1), lambda k: (k, 0)),
            pl.BlockSpec((MOD_TK, n), lambda k: (k, 0)),
            pl.BlockSpec((1, n), lambda k: (0, 0)),
        ],
        out_specs=pl.BlockSpec((1, n), lambda k: (0, 0)),
        out_shape=jax.ShapeDtypeStruct((1, n), jnp.float32),
        compiler_params=pltpu.CompilerParams(
            dimension_semantics=("arbitrary",), vmem_limit_bytes=VMEM_LIMIT_BYTES),
        name="adaln_modulation",
    )(c_col, w_ada, b_ada_row)


def _bias_kernel(tbl_ref, bucket_ref, o_ref):
    for t in range(SPAN // BLOCK):
        bk = bucket_ref[:, t * BLOCK:(t + 1) * BLOCK]
        for h in range(N_Q_HEADS):
            def body(b, acc):
                return jnp.where(bk == b, tbl_ref[b, h] * INV_ATTN_SCALE, acc)
            o_ref[h, :, t * BLOCK:(t + 1) * BLOCK] = lax.fori_loop(
                0, N_BUCKETS, body, jnp.full((BLOCK, BLOCK), NEG_INF, jnp.float32))


def _bias_table(rel_table, bucket):
    return pl.pallas_call(
        _bias_kernel,
        in_specs=[
            pl.BlockSpec(memory_space=pltpu.SMEM),
            pl.BlockSpec(memory_space=pltpu.VMEM),
        ],
        out_specs=pl.BlockSpec(memory_space=pltpu.VMEM),
        out_shape=jax.ShapeDtypeStruct((N_Q_HEADS, BLOCK, SPAN), jnp.float32),
        name="rel_bias_table",
    )(rel_table, bucket)


def _t5_bucket(rel):
    half = N_BUCKETS // 2
    max_exact = half // 2
    ret = jnp.where(rel > 0, half, 0)
    n = jnp.abs(rel)
    nf = jnp.maximum(n, 1).astype(jnp.float32)
    large = max_exact + (jnp.log(nf / max_exact) / math.log(MAX_DISTANCE / max_exact)
                         * (half - max_exact)).astype(jnp.int32)
    large = jnp.minimum(large, half - 1)
    return ret + jnp.where(n < max_exact, n, large)


def _bucket_index_table():
    rel = jnp.arange(SPAN)[None, :] - WINDOW - jnp.arange(BLOCK)[:, None]
    return jnp.where(jnp.abs(rel) <= WINDOW, _t5_bucket(rel), MASKED_BUCKET).astype(jnp.int32)


def _project_mix_kernel(sink_ref,
                        x_ref, shift_ref, scale_ref, g_ref, w_ref, wpool_ref, pscale_ref, bias_ref,
                        ya_ref, yp_ref, h_ref,
                        pbuf_ref, kvtail_ref, utail_ref, uext_ref,
                        *, seq_len):
    s = pl.program_id(0)
    n_tok = pl.num_programs(0) - 1
    tm, d = x_ref.shape
    n_sub = tm // BLOCK
    n_blocks = seq_len // BLOCK
    aw = ATTN_WIDTH
    pw = pscale_ref.shape[1]
    q0 = U0 + pw
    ga0 = q0 + aw
    gp0 = ga0 + aw
    n_chunks = w_ref.shape[1] // CW
    f32 = jnp.float32
    bf16 = jnp.bfloat16
    cur = s % 2
    prv = 1 - cur
    j = s - 1

    def prologue():
        x = x_ref[...]
        ms = jnp.mean(x * x, axis=-1, keepdims=True)
        gain = g_ref[...] * (1.0 + scale_ref[...])
        h_ref[...] = ((x * lax.rsqrt(ms + EPS)) * gain + shift_ref[...]).astype(bf16)

    def proj_chunk(c):
        lo, hi = c * CW, (c + 1) * CW
        pbuf_ref[cur, :, lo:hi] = jnp.dot(h_ref[...], w_ref[:, lo:hi],
                                          preferred_element_type=f32).astype(bf16)

    def window_rows(b, col0):
        r0, r1 = (b - 1) * BLOCK, (b + 2) * BLOCK
        parts = []
        if r0 < 0:
            parts.append(kvtail_ref[prv, :, col0:col0 + HEAD_DIM])
            r0 = 0
        parts.append(pbuf_ref[prv, r0:min(r1, tm), col0:col0 + HEAD_DIM])
        if r1 > tm:
            parts.append(pbuf_ref[cur, 0:r1 - tm, col0:col0 + HEAD_DIM])
        return jnp.concatenate(parts, axis=0) if len(parts) > 1 else parts[0]

    def attn_scores(b, kvh):
        blk = j * n_sub + b
        h0 = kvh * GQA_GROUP
        k = window_rows(b, KV0 + kvh * HEAD_DIM)
        v = window_rows(b, KV0 + KV_WIDTH + kvh * HEAD_DIM)
        qs = jnp.concatenate(
            [pbuf_ref[prv, b * BLOCK:(b + 1) * BLOCK,
                      q0 + (h0 + g) * HEAD_DIM:q0 + (h0 + g + 1) * HEAD_DIM]
             for g in range(GQA_GROUP)], axis=0)
        z = lax.dot_general(qs, k, (((1,), (1,)), ((), ())), preferred_element_type=f32)
        z = z + bias_ref[h0:h0 + GQA_GROUP].reshape(GQA_GROUP * BLOCK, SPAN)
        col = lax.broadcasted_iota(jnp.int32, (1, SPAN), 1)
        if b == 0:
            z = jnp.where(col < jnp.where(blk == 0, WINDOW, 0), NEG_INF, z)
        if b == n_sub - 1:
            z = jnp.where(col >= jnp.where(blk == n_blocks - 1, WINDOW + BLOCK, SPAN), NEG_INF, z)
        sink = jnp.concatenate(
            [jnp.full((BLOCK, 1), sink_ref[h0 + g] * INV_ATTN_SCALE, f32)
             for g in range(GQA_GROUP)], axis=0)
        m = jnp.maximum(jnp.max(z, axis=-1, keepdims=True), sink)
        p = jnp.exp2((z - m) * EXP2_SCALE)
        denom = jnp.sum(p, axis=-1, keepdims=True) + jnp.exp2((sink - m) * EXP2_SCALE)
        return p.astype(bf16), denom, v

    def attn_out(p, denom, v):
        o = jnp.dot(p, v, preferred_element_type=f32) / denom
        return [o[g * BLOCK:(g + 1) * BLOCK] for g in range(GQA_GROUP)]

    def finish_sub_block(b, heads):
        y = jnp.concatenate(heads, axis=1)
        hg = pbuf_ref[prv, b * BLOCK:(b + 1) * BLOCK, ga0:ga0 + aw].astype(f32)
        ya_ref[b * BLOCK:(b + 1) * BLOCK] = (y * _half_silu(hg)).astype(bf16)

    ext = tm + 2 * POOL_HALO

    def pool_group(gi):
        w = POOL_SIZES[gi]
        gw = pw // N_POOL_GROUPS
        c0, c1 = gi * gw, (gi + 1) * gw
        half = w // 2
        halo_row = lax.broadcasted_iota(jnp.int32, (POOL_HALO, 1), 0)
        uext_ref[0:POOL_HALO] = jnp.where(j * tm - POOL_HALO + halo_row >= 0,
                                          utail_ref[prv, :, c0:c1].astype(f32), 0.0)
        uext_ref[POOL_HALO:POOL_HALO + tm] = pbuf_ref[prv, :, U0 + c0:U0 + c1].astype(f32)
        uext_ref[POOL_HALO + tm:ext] = jnp.where(
            (j + 1) * tm + halo_row < seq_len,
            pbuf_ref[cur, 0:POOL_HALO, U0 + c0:U0 + c1].astype(f32), 0.0)
        a = uext_ref[...]
        sh = 1
        while sh < w:
            a = a + pltpu.roll(a, ext - sh, axis=0)
            sh *= 2
        off = POOL_HALO - half
        if off % SUBLANES:
            a = pltpu.roll(a, ext - off, axis=0)
            off = 0
        pos = j * tm + lax.broadcasted_iota(jnp.int32, (tm, 1), 0)
        cnt = (jnp.minimum(pos + half, seq_len) - jnp.maximum(pos - half, 0)).astype(f32)
        pooled = a[off:off + tm] / cnt - uext_ref[POOL_HALO:POOL_HALO + tm]
        mixed = jnp.dot(pooled.astype(bf16), wpool_ref[gi], preferred_element_type=f32)
        hg = pbuf_ref[prv, :, gp0 + c0:gp0 + c1].astype(f32)
        yp_ref[:, c0:c1] = (mixed * pscale_ref[:, c0:c1] * _half_silu(hg)).astype(bf16)

    def save_tails():
        kvtail_ref[cur] = pbuf_ref[prv, tm - BLOCK:tm, KV0:KV0 + 2 * KV_WIDTH]
        utail_ref[cur] = pbuf_ref[prv, tm - POOL_HALO:tm, U0:U0 + pw]

    units = [(b, kvh) for b in range(n_sub) for kvh in range(N_KV_HEADS)]
    u_chunks_done = (U0 + pw) // CW
    assert (U0 + pw) % CW == 0 and len(units) + 1 <= n_chunks
    assert u_chunks_done + N_POOL_GROUPS <= n_chunks

    def run(do_proj, do_mix):
        if do_proj:
            prologue()
        pending = attn_scores(*units[0]) if do_mix else None
        heads = []
        for c in range(n_chunks):
            if do_proj:
                proj_chunk(c)
            if not do_mix:
                continue
            if c < len(units):
                heads += attn_out(*pending)
                if c + 1 < len(units):
                    pending = attn_scores(*units[c + 1])
                b, kvh = units[c]
                if kvh == N_KV_HEADS - 1:
                    finish_sub_block(b, heads)
                    heads = []
            if u_chunks_done <= c < u_chunks_done + N_POOL_GROUPS:
                pool_group(c - u_chunks_done)
        if do_mix:
            save_tails()

    @pl.when(s == 0)
    def _():
        kvtail_ref[0] = jnp.zeros(kvtail_ref.shape[1:], kvtail_ref.dtype)
        utail_ref[0] = jnp.zeros(utail_ref.shape[1:], utail_ref.dtype)
        run(True, False)

    @pl.when((s > 0) & (s < n_tok))
    def _():
        run(True, True)

    @pl.when(s == n_tok)
    def _():
        run(False, True)


def _project_mix(x2, mod, pre_g_row, w_in_r, wpool, pool_scale_row, bias_tbl, sink):
    s, d = x2.shape
    tm = TM
    n_tok = s // tm
    pw = pool_scale_row.shape[1]
    assert s // BLOCK >= 2, "first and last attention block must differ"
    assert w_in_r.shape[1] == 2 * KV_WIDTH + pw + 2 * ATTN_WIDTH + pw
    cur_blk = lambda st: jnp.minimum(st, n_tok - 1)
    mix_blk = lambda st: jnp.maximum(st - 1, 0)
    const2 = lambda st: (0, 0)
    single = pl.Buffered(1)
    return pl.pallas_call(
        functools.partial(_project_mix_kernel, seq_len=s),
        grid=(n_tok + 1,),
        in_specs=[
            pl.BlockSpec(memory_space=pltpu.SMEM),
            pl.BlockSpec((tm, d), lambda st: (cur_blk(st), 0)),
            pl.BlockSpec((1, d), lambda st: (0, 0)),
            pl.BlockSpec((1, d), lambda st: (0, 1)),
            pl.BlockSpec((1, d), const2),
            pl.BlockSpec(w_in_r.shape, const2, pipeline_mode=single),
            pl.BlockSpec(wpool.shape, lambda st: (0, 0, 0), pipeline_mode=single),
            pl.BlockSpec((1, pw), const2),
            pl.BlockSpec(bias_tbl.shape, lambda st: (0, 0, 0), pipeline_mode=single),
        ],
        out_specs=[
            pl.BlockSpec((tm, ATTN_WIDTH), lambda st: (mix_blk(st), 0)),
            pl.BlockSpec((tm, pw), lambda st: (mix_blk(st), 0)),
            pl.BlockSpec((tm, d), lambda st: (cur_blk(st), 0)),
        ],
        out_shape=[
            jax.ShapeDtypeStruct((s, ATTN_WIDTH), jnp.bfloat16),
            jax.ShapeDtypeStruct((s, pw), jnp.bfloat16),
            jax.ShapeDtypeStruct((s, d), jnp.bfloat16),
        ],
        scratch_shapes=[
            pltpu.VMEM((2, tm, w_in_r.shape[1]), jnp.bfloat16),
            pltpu.VMEM((2, BLOCK, 2 * KV_WIDTH), jnp.bfloat16),
            pltpu.VMEM((2, POOL_HALO, pw), jnp.bfloat16),
            pltpu.VMEM((tm + 2 * POOL_HALO, pw // N_POOL_GROUPS), jnp.float32),
        ],
        compiler_params=pltpu.CompilerParams(
            dimension_semantics=("arbitrary",), vmem_limit_bytes=VMEM_LIMIT_BYTES),
        name="project_mix",
    )(sink, x2, mod, mod, pre_g_row, w_in_r, wpool, pool_scale_row, bias_tbl)


def _merge_gate_kernel(h_ref, w_ref, wa_ref, wp_ref, wout_ref, o_ref, wa_o_ref, wp_o_ref, wout_o_ref,
                       wbf_ref):
    @pl.when(pl.program_id(1) == 0)
    def _():
        wbf_ref[...] = (0.5 * w_ref[...]).astype(wbf_ref.dtype)

    wa_o_ref[...] = wa_ref[...].astype(wa_o_ref.dtype)
    wp_o_ref[...] = wp_ref[...].astype(wp_o_ref.dtype)
    wout_o_ref[...] = wout_ref[...].astype(wout_o_ref.dtype)

    for r in range(0, h_ref.shape[0], GM_ROWS):
        o_ref[r:r + GM_ROWS] = jnp.dot(h_ref[r:r + GM_ROWS], wbf_ref[...],
                                       preferred_element_type=jnp.float32).astype(o_ref.dtype)


def _merge_gate(h, w_merge, wa, wp, wout):
    s, d = h.shape
    n = w_merge.shape[1]
    n_i = s // GM_TM
    steps = (n // GM_TN) * n_i
    slab = lambda w: (w.shape[0] // steps, w.shape[1])
    step = lambda j, i: (j * n_i + i, 0)
    for w in (wa, wp, wout):
        assert w.shape[0] % (steps * BF16_SUBLANES) == 0
    return pl.pallas_call(
        _merge_gate_kernel,
        grid=(n // GM_TN, n_i),
        in_specs=[
            pl.BlockSpec((GM_TM, d), lambda j, i: (i, 0)),
            pl.BlockSpec((d, GM_TN), lambda j, i: (0, j)),
            pl.BlockSpec(slab(wa), step),
            pl.BlockSpec(slab(wp), step),
            pl.BlockSpec(slab(wout), step),
        ],
        out_specs=[
            pl.BlockSpec((GM_TM, GM_TN), lambda j, i: (i, j)),
            pl.BlockSpec(slab(wa), step),
            pl.BlockSpec(slab(wp), step),
            pl.BlockSpec(slab(wout), step),
        ],
        out_shape=[
            jax.ShapeDtypeStruct((s, n), jnp.bfloat16),
            jax.ShapeDtypeStruct(wa.shape, jnp.bfloat16),
            jax.ShapeDtypeStruct(wp.shape, jnp.bfloat16),
            jax.ShapeDtypeStruct(wout.shape, jnp.bfloat16),
        ],
        scratch_shapes=[pltpu.VMEM((d, GM_TN), jnp.bfloat16)],
        compiler_params=pltpu.CompilerParams(
            dimension_semantics=("arbitrary", "arbitrary"), vmem_limit_bytes=VMEM_LIMIT_BYTES),
        name="merge_gate",
    )(h, w_merge, wa, wp, wout)


def _merge_out_kernel(ya_ref, yp_ref, gm_ref, x_ref, gate_ref, postg_ref, bm_ref,
                      wa_ref, wp_ref, wout_ref, o_ref, merged_ref, oacc_ref, inv_ref):
    s = pl.program_id(0)
    n_tok = pl.num_programs(0) - 1
    tm, d = x_ref.shape
    n_chunks = d // CW
    f32 = jnp.float32

    def finish_previous():
        scale = gate_ref[...] * postg_ref[...]
        inv = inv_ref[...]
        for c in range(n_chunks):
            lo, hi = c * CW, (c + 1) * CW
            o_ref[:, lo:hi] = x_ref[:, lo:hi] + (oacc_ref[:, lo:hi] * inv) * scale[:, lo:hi]

    def matmuls():
        half_bm = 0.5 * bm_ref[...]
        for c in range(n_chunks):
            lo, hi = c * CW, (c + 1) * CW
            bra = jnp.dot(ya_ref[...], wa_ref[:, lo:hi], preferred_element_type=f32)
            brp = jnp.dot(yp_ref[...], wp_ref[:, lo:hi], preferred_element_type=f32)
            g_a = 0.5 * jnp.tanh(gm_ref[:, lo:hi].astype(f32) + half_bm[:, lo:hi]) + 0.5
            g_p = 0.5 * jnp.tanh(gm_ref[:, d + lo:d + hi].astype(f32)
                                 + half_bm[:, d + lo:d + hi]) + 0.5
            merged_ref[:, lo:hi] = (g_a * bra + g_p * brp).astype(merged_ref.dtype)
        ssq = jnp.zeros((tm, 1), f32)
        for c in range(n_chunks):
            lo, hi = c * CW, (c + 1) * CW
            o = jnp.dot(merged_ref[...], wout_ref[:, lo:hi], preferred_element_type=f32)
            ssq = ssq + jnp.sum(o * o, axis=-1, keepdims=True)
            oacc_ref[:, lo:hi] = o
        inv_ref[...] = lax.rsqrt(ssq * (1.0 / d) + EPS)

    @pl.when(s == 0)
    def _():
        matmuls()

    @pl.when((s > 0) & (s < n_tok))
    def _():
        finish_previous()
        matmuls()

    @pl.when(s == n_tok)
    def _():
        finish_previous()


def _merge_out(ya, yp, gm, x2, mod, post_g_row, b_merge_row, wa, wp, wout):
    s, d = x2.shape
    tm = TM
    n_tok = s // tm
    aw, pw = ya.shape[1], yp.shape[1]
    cur = lambda st: (jnp.minimum(st, n_tok - 1), 0)
    prev = lambda st: (jnp.maximum(st - 1, 0), 0)
    const2 = lambda st: (0, 0)
    single = pl.Buffered(1)
    return pl.pallas_call(
        _merge_out_kernel,
        grid=(n_tok + 1,),
        in_specs=[
            pl.BlockSpec((tm, aw), cur),
            pl.BlockSpec((tm, pw), cur),
            pl.BlockSpec((tm, 2 * d), cur),
            pl.BlockSpec((tm, d), prev),
            pl.BlockSpec((1, d), lambda st: (0, 2)),
            pl.BlockSpec((1, d), const2),
            pl.BlockSpec((1, 2 * d), const2),
            pl.BlockSpec((aw, d), const2, pipeline_mode=single),
            pl.BlockSpec((pw, d), const2, pipeline_mode=single),
            pl.BlockSpec((d, d), const2, pipeline_mode=single),
        ],
        out_specs=pl.BlockSpec((tm, d), prev),
        out_shape=jax.ShapeDtypeStruct((s, d), jnp.float32),
        scratch_shapes=[
            pltpu.VMEM((tm, d), jnp.bfloat16),
            pltpu.VMEM((tm, d), jnp.float32),
            pltpu.VMEM((tm, 1), jnp.float32),
        ],
        compiler_params=pltpu.CompilerParams(
            dimension_semantics=("arbitrary",), vmem_limit_bytes=VMEM_LIMIT_BYTES),
        name="merge_out",
    )(ya, yp, gm, x2, mod, post_g_row, b_merge_row, wa, wp, wout)


def _layer(x2, c, bias_tbl, w_ada, b_ada, pre_g, post_g, w_in, sink, w_pool, pool_scale,
           w_br_attn, w_br_pool, w_merge, b_merge, w_out):
    s, d = x2.shape
    bf16 = jnp.bfloat16
    a, kw = ATTN_WIDTH, KV_WIDTH
    q0, k0, ga0, u0 = 0, a, a + 2 * kw, 2 * a + 2 * kw
    pw = (w_in.shape[1] - u0) // 2
    gp0 = u0 + pw
    w_in_r = jnp.concatenate(
        [w_in[:, k0:k0 + 2 * kw], w_in[:, u0:u0 + pw], w_in[:, q0:q0 + a],
         0.5 * w_in[:, ga0:ga0 + a], 0.5 * w_in[:, gp0:gp0 + pw]], axis=1).astype(bf16)

    mod = _modulation(c.reshape(d, 1), w_ada, b_ada.reshape(1, -1))
    ya, yp, h = _project_mix(x2, mod, pre_g.reshape(1, d), w_in_r, w_pool.astype(bf16),
                             pool_scale.reshape(1, -1), bias_tbl, sink)
    gm, wa, wp, wout = _merge_gate(h, w_merge, w_br_attn, w_br_pool, w_out)
    return _merge_out(ya, yp, gm, x2, mod, post_g.reshape(1, d), b_merge.reshape(1, -1),
                      wa, wp, wout)


def kernel(x, c, rel_bias_table, w_ada, b_ada, pre_norm_g, post_norm_g, w_in, attn_sink,
           w_pool_group, pool_scale, w_branch_attn, w_branch_pool, w_merge, b_merge, w_out):
    batch, s, d = x.shape
    assert batch == 1, "kernel is written for a single sequence"
    depth = w_ada.shape[0]
    bias_tbl = _bias_table(rel_bias_table, _bucket_index_table())
    x2 = x.reshape(s, d)
    for l in range(depth):
        x2 = _layer(x2, c[0], bias_tbl, w_ada[l], b_ada[l], pre_norm_g[l], post_norm_g[l],
                    w_in[l], attn_sink[l], w_pool_group[l], pool_scale[l],
                    w_branch_attn[l], w_branch_pool[l], w_merge[l], b_merge[l], w_out[l])
    return x2.reshape(batch, s, d)
```

```python
import functools
import math

import numpy as np
import jax
import jax.numpy as jnp
from jax import lax
from jax.experimental import pallas as pl
from jax.experimental.pallas import tpu as pltpu

HEAD_DIM = 128
N_Q_HEADS = 8
N_KV_HEADS = 2
GQA_GROUP = N_Q_HEADS // N_KV_HEADS
ATTN_WIDTH = N_Q_HEADS * HEAD_DIM
KV_WIDTH = N_KV_HEADS * HEAD_DIM
WINDOW = 128
BLOCK = 128
SPAN = BLOCK + 2 * WINDOW
N_BUCKETS = 32
MAX_DISTANCE = 128
POOL_SIZES = (2, 4, 8, 16)
N_POOL_GROUPS = len(POOL_SIZES)
EPS = 1e-6
NEG_INF = -1e30
MASKED_BUCKET = N_BUCKETS
ATTN_SCALE = HEAD_DIM ** -0.5
INV_ATTN_SCALE = HEAD_DIM ** 0.5
EXP2_SCALE = ATTN_SCALE * math.log2(math.e)

SUBLANES = 8
BF16_SUBLANES = 16
VMEM_LIMIT_BYTES = 60 * 1024 * 1024

MOD_TK = 256
TM = 512
CW = 512
GM_TM = 2048
GM_ROWS = 512
GM_TN = 1024
POOL_HALO = BF16_SUBLANES

Q0 = 0
KV0 = Q0 + ATTN_WIDTH
GA0 = KV0 + 2 * KV_WIDTH
U0 = GA0 + ATTN_WIDTH


def _silu(v):
    return v * (1.0 / (1.0 + jnp.exp(-v)))


def _half_silu(hv):
    return hv + hv * jnp.tanh(hv)


def _mod_kernel(c_ref, w_ref, b_ref, o_ref):
    @pl.when(pl.program_id(0) == 0)
    def _():
        o_ref[...] = b_ref[...]

    s = _silu(c_ref[...])
    o_ref[...] += jnp.sum(s * w_ref[...], axis=0, keepdims=True)


def _modulation(c_col, w_ada, b_ada_row):
    d, n = w_ada.shape
    return pl.pallas_call(
        _mod_kernel,
        grid=(d // MOD_TK,),
        in_specs=[
            pl.BlockSpec((MOD_TK, 1), lambda k: (k, 0)),
            pl.BlockSpec((MOD_TK, n), lambda k: (k, 0)),
            pl.BlockSpec((1, n), lambda k: (0, 0)),
        ],
        out_specs=pl.BlockSpec((1, n), lambda k: (0, 0)),
        out_shape=jax.ShapeDtypeStruct((1, n), jnp.float32),
        compiler_params=pltpu.CompilerParams(
            dimension_semantics=("arbitrary",), vmem_limit_bytes=VMEM_LIMIT_BYTES),
        name="adaln_modulation",
    )(c_col, w_ada, b_ada_row)


def _bias_kernel(tbl_ref, bucket_ref, o_ref):
    for t in range(SPAN // BLOCK):
        bk = bucket_ref[:, t * BLOCK:(t + 1) * BLOCK]
        for h in range(N_Q_HEADS):
            def body(b, acc):
                return jnp.where(bk == b, tbl_ref[b, h] * INV_ATTN_SCALE, acc)
            o_ref[h, :, t * BLOCK:(t + 1) * BLOCK] = lax.fori_loop(
                0, N_BUCKETS, body, jnp.full((BLOCK, BLOCK), NEG_INF, jnp.float32), unroll=8)


def _bias_table(rel_table, bucket):
    return pl.pallas_call(
        _bias_kernel,
        in_specs=[
            pl.BlockSpec(memory_space=pltpu.SMEM),
            pl.BlockSpec(memory_space=pltpu.VMEM),
        ],
        out_specs=pl.BlockSpec(memory_space=pltpu.VMEM),
        out_shape=jax.ShapeDtypeStruct((N_Q_HEADS, BLOCK, SPAN), jnp.float32),
        name="rel_bias_table",
    )(rel_table, bucket)


def _t5_bucket(rel):
    half = N_BUCKETS // 2
    max_exact = half // 2
    assert (max_exact, MAX_DISTANCE // max_exact, half - max_exact) == (8, 16, 8)
    n = abs(rel)
    large = min(max_exact + (n * n).bit_length() - 7, half - 1)
    return (half if rel > 0 else 0) + (n if n < max_exact else large)


def _bucket_index_table():
    table = np.full((BLOCK, SPAN), MASKED_BUCKET, np.int32)
    for q in range(BLOCK):
        for t in range(SPAN):
            rel = t - WINDOW - q
            if abs(rel) <= WINDOW:
                table[q, t] = _t5_bucket(rel)
    return table


def _project_mix_kernel(sink_ref,
                        x_ref, shift_ref, scale_ref, g_ref, w_ref, wpool_ref, pscale_ref, bias_ref,
                        ya_ref, yp_ref, h_ref,
                        pbuf_ref, kvtail_ref, utail_ref, uext_ref,
                        *, seq_len):
    s = pl.program_id(0)
    n_tok = pl.num_programs(0) - 1
    tm, d = x_ref.shape
    n_sub = tm // BLOCK
    n_blocks = seq_len // BLOCK
    aw = ATTN_WIDTH
    pw = pscale_ref.shape[1]
    q0, ga0, gp0 = Q0, GA0, U0 + pw
    n_chunks = w_ref.shape[1] // CW
    f32 = jnp.float32
    bf16 = jnp.bfloat16
    cur = s % 2
    prv = 1 - cur
    j = s - 1

    def prologue():
        x = x_ref[...]
        ms = jnp.mean(x * x, axis=-1, keepdims=True)
        gain = g_ref[...] * (1.0 + scale_ref[...])
        h_ref[...] = ((x * lax.rsqrt(ms + EPS)) * gain + shift_ref[...]).astype(bf16)

    def is_gate_col(col):
        return ga0 <= col < ga0 + aw or gp0 <= col < gp0 + pw

    def proj_chunk(c):
        lo, hi = c * CW, (c + 1) * CW
        p = jnp.dot(h_ref[...], w_ref[:, lo:hi], preferred_element_type=f32)
        if is_gate_col(lo):
            assert is_gate_col(hi - 1)
            p = 0.5 * p
        pbuf_ref[cur, :, lo:hi] = p.astype(bf16)

    def window_rows(b, col0):
        r0, r1 = (b - 1) * BLOCK, (b + 2) * BLOCK
        parts = []
        if r0 < 0:
            parts.append(kvtail_ref[prv, :, col0 - KV0:col0 - KV0 + HEAD_DIM])
            r0 = 0
        parts.append(pbuf_ref[prv, r0:min(r1, tm), col0:col0 + HEAD_DIM])
        if r1 > tm:
            parts.append(pbuf_ref[cur, 0:r1 - tm, col0:col0 + HEAD_DIM])
        return jnp.concatenate(parts, axis=0) if len(parts) > 1 else parts[0]

    def attn_scores(b, kvh):
        blk = j * n_sub + b
        h0 = kvh * GQA_GROUP
        k = window_rows(b, KV0 + kvh * HEAD_DIM)
        v = window_rows(b, KV0 + KV_WIDTH + kvh * HEAD_DIM)
        qs = jnp.concatenate(
            [pbuf_ref[prv, b * BLOCK:(b + 1) * BLOCK,
                      q0 + (h0 + g) * HEAD_DIM:q0 + (h0 + g + 1) * HEAD_DIM]
             for g in range(GQA_GROUP)], axis=0)
        z = lax.dot_general(qs, k, (((1,), (1,)), ((), ())), preferred_element_type=f32)
        z = z + bias_ref[h0:h0 + GQA_GROUP].reshape(GQA_GROUP * BLOCK, SPAN)
        col = lax.broadcasted_iota(jnp.int32, (1, SPAN), 1)
        if b == 0:
            z = jnp.where(col < jnp.where(blk == 0, WINDOW, 0), NEG_INF, z)
        if b == n_sub - 1:
            z = jnp.where(col >= jnp.where(blk == n_blocks - 1, WINDOW + BLOCK, SPAN), NEG_INF, z)
        sink = jnp.concatenate(
            [jnp.full((BLOCK, 1), sink_ref[h0 + g] * INV_ATTN_SCALE, f32)
             for g in range(GQA_GROUP)], axis=0)
        m = jnp.maximum(jnp.max(z, axis=-1, keepdims=True), sink)
        p = jnp.exp2((z - m) * EXP2_SCALE)
        denom = jnp.sum(p, axis=-1, keepdims=True) + jnp.exp2((sink - m) * EXP2_SCALE)
        return p.astype(bf16), denom, v

    def attn_out(p, denom, v):
        o = jnp.dot(p, v, preferred_element_type=f32) / denom
        return [o[g * BLOCK:(g + 1) * BLOCK] for g in range(GQA_GROUP)]

    def finish_sub_block(b, heads):
        y = jnp.concatenate(heads, axis=1)
        hg = pbuf_ref[prv, b * BLOCK:(b + 1) * BLOCK, ga0:ga0 + aw].astype(f32)
        ya_ref[b * BLOCK:(b + 1) * BLOCK] = (y * _half_silu(hg)).astype(bf16)

    ext = tm + 2 * POOL_HALO

    def pool_group(gi):
        w = POOL_SIZES[gi]
        gw = pw // N_POOL_GROUPS
        c0, c1 = gi * gw, (gi + 1) * gw
        half = w // 2
        halo_row = lax.broadcasted_iota(jnp.int32, (POOL_HALO, 1), 0)
        uext_ref[0:POOL_HALO] = jnp.where(j * tm - POOL_HALO + halo_row >= 0,
                                          utail_ref[prv, :, c0:c1].astype(f32), 0.0)
        uext_ref[POOL_HALO:POOL_HALO + tm] = pbuf_ref[prv, :, U0 + c0:U0 + c1].astype(f32)
        uext_ref[POOL_HALO + tm:ext] = jnp.where(
            (j + 1) * tm + halo_row < seq_len,
            pbuf_ref[cur, 0:POOL_HALO, U0 + c0:U0 + c1].astype(f32), 0.0)
        a = uext_ref[...]
        sh = 1
        while sh < w:
            a = a + pltpu.roll(a, ext - sh, axis=0)
            sh *= 2
        off = POOL_HALO - half
        if off % SUBLANES:
            a = pltpu.roll(a, ext - off, axis=0)
            off = 0
        pos = j * tm + lax.broadcasted_iota(jnp.int32, (tm, 1), 0)
        cnt = (jnp.minimum(pos + half, seq_len) - jnp.maximum(pos - half, 0)).astype(f32)
        pooled = a[off:off + tm] / cnt - uext_ref[POOL_HALO:POOL_HALO + tm]
        mixed = jnp.dot(pooled.astype(bf16), wpool_ref[gi], preferred_element_type=f32)
        hg = pbuf_ref[prv, :, gp0 + c0:gp0 + c1].astype(f32)
        yp_ref[:, c0:c1] = (mixed * pscale_ref[:, c0:c1] * _half_silu(hg)).astype(bf16)

    def save_tails():
        kvtail_ref[cur] = pbuf_ref[prv, tm - BLOCK:tm, KV0:KV0 + 2 * KV_WIDTH]
        utail_ref[cur] = pbuf_ref[prv, tm - POOL_HALO:tm, U0:U0 + pw]

    units = [(b, kvh) for b in range(n_sub) for kvh in range(N_KV_HEADS)]
    assert KV0 % CW == 0 and (2 * KV_WIDTH) % CW == 0 and U0 % CW == 0 and pw % CW == 0
    first = list(range(KV0 // CW, (KV0 + 2 * KV_WIDTH) // CW)) + list(range(U0 // CW, (U0 + pw) // CW))
    order = first + [c for c in range(n_chunks) if c not in first]
    u_chunks_done = len(first)
    assert len(units) + 1 <= n_chunks and u_chunks_done + N_POOL_GROUPS <= n_chunks

    def run(do_proj, do_mix):
        if do_proj:
            prologue()
        pending = attn_scores(*units[0]) if do_mix else None
        heads = []
        for pos, c in enumerate(order):
            if do_proj:
                proj_chunk(c)
            if not do_mix:
                continue
            if pos < len(units):
                heads += attn_out(*pending)
                if pos + 1 < len(units):
                    pending = attn_scores(*units[pos + 1])
                b, kvh = units[pos]
                if kvh == N_KV_HEADS - 1:
                    finish_sub_block(b, heads)
                    heads = []
            if u_chunks_done <= pos < u_chunks_done + N_POOL_GROUPS:
                pool_group(pos - u_chunks_done)
        if do_mix:
            save_tails()

    @pl.when(s == 0)
    def _():
        kvtail_ref[0] = jnp.zeros(kvtail_ref.shape[1:], kvtail_ref.dtype)
        utail_ref[0] = jnp.zeros(utail_ref.shape[1:], utail_ref.dtype)
        run(True, False)

    @pl.when((s > 0) & (s < n_tok))
    def _():
        run(True, True)

    @pl.when(s == n_tok)
    def _():
        run(False, True)


def _project_mix(x2, mod, pre_g_row, w_in_r, wpool, pool_scale_row, bias_tbl, sink):
    s, d = x2.shape
    tm = TM
    n_tok = s // tm
    pw = pool_scale_row.shape[1]
    assert s // BLOCK >= 2, "first and last attention block must differ"
    assert w_in_r.shape[1] == U0 + 2 * pw
    cur_blk = lambda st: jnp.minimum(st, n_tok - 1)
    mix_blk = lambda st: jnp.maximum(st - 1, 0)
    const2 = lambda st: (0, 0)
    single = pl.Buffered(1)
    return pl.pallas_call(
        functools.partial(_project_mix_kernel, seq_len=s),
        grid=(n_tok + 1,),
        in_specs=[
            pl.BlockSpec(memory_space=pltpu.SMEM),
            pl.BlockSpec((tm, d), lambda st: (cur_blk(st), 0)),
            pl.BlockSpec((1, d), lambda st: (0, 0)),
            pl.BlockSpec((1, d), lambda st: (0, 1)),
            pl.BlockSpec((1, d), const2),
            pl.BlockSpec(w_in_r.shape, const2, pipeline_mode=single),
            pl.BlockSpec(wpool.shape, lambda st: (0, 0, 0), pipeline_mode=single),
            pl.BlockSpec((1, pw), const2),
            pl.BlockSpec(bias_tbl.shape, lambda st: (0, 0, 0), pipeline_mode=single),
        ],
        out_specs=[
            pl.BlockSpec((tm, ATTN_WIDTH), lambda st: (mix_blk(st), 0)),
            pl.BlockSpec((tm, pw), lambda st: (mix_blk(st), 0)),
            pl.BlockSpec((tm, d), lambda st: (cur_blk(st), 0)),
        ],
        out_shape=[
            jax.ShapeDtypeStruct((s, ATTN_WIDTH), jnp.bfloat16),
            jax.ShapeDtypeStruct((s, pw), jnp.bfloat16),
            jax.ShapeDtypeStruct((s, d), jnp.bfloat16),
        ],
        scratch_shapes=[
            pltpu.VMEM((2, tm, w_in_r.shape[1]), jnp.bfloat16),
            pltpu.VMEM((2, BLOCK, 2 * KV_WIDTH), jnp.bfloat16),
            pltpu.VMEM((2, POOL_HALO, pw), jnp.bfloat16),
            pltpu.VMEM((tm + 2 * POOL_HALO, pw // N_POOL_GROUPS), jnp.float32),
        ],
        compiler_params=pltpu.CompilerParams(
            dimension_semantics=("arbitrary",), vmem_limit_bytes=VMEM_LIMIT_BYTES),
        name="project_mix",
    )(sink, x2, mod, mod, pre_g_row, w_in_r, wpool, pool_scale_row, bias_tbl)


def _merge_gate_kernel(h_ref, w_ref, wa_ref, wp_ref, wout_ref, o_ref, wa_o_ref, wp_o_ref, wout_o_ref,
                       wbf_ref):
    @pl.when(pl.program_id(1) == 0)
    def _():
        wbf_ref[...] = (0.5 * w_ref[...]).astype(wbf_ref.dtype)

    wa_o_ref[...] = wa_ref[...].astype(wa_o_ref.dtype)
    wp_o_ref[...] = wp_ref[...].astype(wp_o_ref.dtype)
    wout_o_ref[...] = wout_ref[...].astype(wout_o_ref.dtype)

    for r in range(0, h_ref.shape[0], GM_ROWS):
        o_ref[r:r + GM_ROWS] = jnp.dot(h_ref[r:r + GM_ROWS], wbf_ref[...],
                                       preferred_element_type=jnp.float32).astype(o_ref.dtype)


def _merge_gate(h, w_merge, wa, wp, wout):
    s, d = h.shape
    n = w_merge.shape[1]
    n_i = s // GM_TM
    steps = (n // GM_TN) * n_i
    slab = lambda w: (w.shape[0] // steps, w.shape[1])
    step = lambda j, i: (j * n_i + i, 0)
    for w in (wa, wp, wout):
        assert w.shape[0] % (steps * BF16_SUBLANES) == 0
    return pl.pallas_call(
        _merge_gate_kernel,
        grid=(n // GM_TN, n_i),
        in_specs=[
            pl.BlockSpec((GM_TM, d), lambda j, i: (i, 0)),
            pl.BlockSpec((d, GM_TN), lambda j, i: (0, j)),
            pl.BlockSpec(slab(wa), step),
            pl.BlockSpec(slab(wp), step),
            pl.BlockSpec(slab(wout), step),
        ],
        out_specs=[
            pl.BlockSpec((GM_TM, GM_TN), lambda j, i: (i, j)),
            pl.BlockSpec(slab(wa), step),
            pl.BlockSpec(slab(wp), step),
            pl.BlockSpec(slab(wout), step),
        ],
        out_shape=[
            jax.ShapeDtypeStruct((s, n), jnp.bfloat16),
            jax.ShapeDtypeStruct(wa.shape, jnp.bfloat16),
            jax.ShapeDtypeStruct(wp.shape, jnp.bfloat16),
            jax.ShapeDtypeStruct(wout.shape, jnp.bfloat16),
        ],
        scratch_shapes=[pltpu.VMEM((d, GM_TN), jnp.bfloat16)],
        compiler_params=pltpu.CompilerParams(
            dimension_semantics=("arbitrary", "arbitrary"), vmem_limit_bytes=VMEM_LIMIT_BYTES),
        name="merge_gate",
    )(h, w_merge, wa, wp, wout)


def _merge_out_kernel(ya_ref, yp_ref, gm_ref, x_ref, gate_ref, postg_ref, bm_ref,
                      wa_ref, wp_ref, wout_ref, o_ref, merged_ref, oacc_ref, inv_ref):
    s = pl.program_id(0)
    n_tok = pl.num_programs(0) - 1
    tm, d = x_ref.shape
    n_chunks = d // CW
    f32 = jnp.float32

    def finish_previous():
        scale = gate_ref[...] * postg_ref[...]
        inv = inv_ref[...]
        for c in range(n_chunks):
            lo, hi = c * CW, (c + 1) * CW
            o_ref[:, lo:hi] = x_ref[:, lo:hi] + (oacc_ref[:, lo:hi] * inv) * scale[:, lo:hi]

    def matmuls():
        half_bm = 0.5 * bm_ref[...]
        for c in range(n_chunks):
            lo, hi = c * CW, (c + 1) * CW
            bra = jnp.dot(ya_ref[...], wa_ref[:, lo:hi], preferred_element_type=f32)
            brp = jnp.dot(yp_ref[...], wp_ref[:, lo:hi], preferred_element_type=f32)
            g_a = 0.5 * jnp.tanh(gm_ref[:, lo:hi].astype(f32) + half_bm[:, lo:hi]) + 0.5
            g_p = 0.5 * jnp.tanh(gm_ref[:, d + lo:d + hi].astype(f32)
                                 + half_bm[:, d + lo:d + hi]) + 0.5
            merged_ref[:, lo:hi] = (g_a * bra + g_p * brp).astype(merged_ref.dtype)
        ssq = jnp.zeros((tm, 1), f32)
        for c in range(n_chunks):
            lo, hi = c * CW, (c + 1) * CW
            o = jnp.dot(merged_ref[...], wout_ref[:, lo:hi], preferred_element_type=f32)
            ssq = ssq + jnp.sum(o * o, axis=-1, keepdims=True)
            oacc_ref[:, lo:hi] = o
        inv_ref[...] = lax.rsqrt(ssq * (1.0 / d) + EPS)

    @pl.when(s == 0)
    def _():
        matmuls()

    @pl.when((s > 0) & (s < n_tok))
    def _():
        finish_previous()
        matmuls()

    @pl.when(s == n_tok)
    def _():
        finish_previous()


def _merge_out(ya, yp, gm, x2, mod, post_g_row, b_merge_row, wa, wp, wout):
    s, d = x2.shape
    tm = TM
    n_tok = s // tm
    aw, pw = ya.shape[1], yp.shape[1]
    cur = lambda st: (jnp.minimum(st, n_tok - 1), 0)
    prev = lambda st: (jnp.maximum(st - 1, 0), 0)
    const2 = lambda st: (0, 0)
    single = pl.Buffered(1)
    return pl.pallas_call(
        _merge_out_kernel,
        grid=(n_tok + 1,),
        in_specs=[
            pl.BlockSpec((tm, aw), cur),
            pl.BlockSpec((tm, pw), cur),
            pl.BlockSpec((tm, 2 * d), cur),
            pl.BlockSpec((tm, d), prev),
            pl.BlockSpec((1, d), lambda st: (0, 2)),
            pl.BlockSpec((1, d), const2),
            pl.BlockSpec((1, 2 * d), const2),
            pl.BlockSpec((aw, d), const2, pipeline_mode=single),
            pl.BlockSpec((pw, d), const2, pipeline_mode=single),
            pl.BlockSpec((d, d), const2, pipeline_mode=single),
        ],
        out_specs=pl.BlockSpec((tm, d), prev),
        out_shape=jax.ShapeDtypeStruct((s, d), jnp.float32),
        scratch_shapes=[
            pltpu.VMEM((tm, d), jnp.bfloat16),
            pltpu.VMEM((tm, d), jnp.float32),
            pltpu.VMEM((tm, 1), jnp.float32),
        ],
        compiler_params=pltpu.CompilerParams(
            dimension_semantics=("arbitrary",), vmem_limit_bytes=VMEM_LIMIT_BYTES),
        name="merge_out",
    )(ya, yp, gm, x2, mod, post_g_row, b_merge_row, wa, wp, wout)


def _layer(x2, c, bias_tbl, w_ada, b_ada, pre_g, post_g, w_in, sink, w_pool, pool_scale,
           w_br_attn, w_br_pool, w_merge, b_merge, w_out):
    s, d = x2.shape
    bf16 = jnp.bfloat16
    mod = _modulation(c.reshape(d, 1), w_ada, b_ada.reshape(1, -1))
    ya, yp, h = _project_mix(x2, mod, pre_g.reshape(1, d), w_in.astype(bf16), w_pool.astype(bf16),
                             pool_scale.reshape(1, -1), bias_tbl, sink)
    gm, wa, wp, wout = _merge_gate(h, w_merge, w_br_attn, w_br_pool, w_out)
    return _merge_out(ya, yp, gm, x2, mod, post_g.reshape(1, d), b_merge.reshape(1, -1),
                      wa, wp, wout)


def kernel(x, c, rel_bias_table, w_ada, b_ada, pre_norm_g, post_norm_g, w_in, attn_sink,
           w_pool_group, pool_scale, w_branch_attn, w_branch_pool, w_merge, b_merge, w_out):
    batch, s, d = x.shape
    assert batch == 1, "kernel is written for a single sequence"
    depth = w_ada.shape[0]
    bias_tbl = _bias_table(rel_bias_table, _bucket_index_table())
    x2 = x.reshape(s, d)
    for l in range(depth):
        x2 = _layer(x2, c[0], bias_tbl, w_ada[l], b_ada[l], pre_norm_g[l], post_norm_g[l],
                    w_in[l], attn_sink[l], w_pool_group[l], pool_scale[l],
                    w_branch_attn[l], w_branch_pool[l], w_merge[l], b_merge[l], w_out[l])
    return x2.reshape(batch, s, d)
```

```python
import functools
import math

import numpy as np
import jax
import jax.numpy as jnp
from jax import lax
from jax.experimental import pallas as pl
from jax.experimental.pallas import tpu as pltpu

HEAD_DIM = 128
N_Q_HEADS = 8
N_KV_HEADS = 2
GQA_GROUP = N_Q_HEADS // N_KV_HEADS
ATTN_WIDTH = N_Q_HEADS * HEAD_DIM
KV_WIDTH = N_KV_HEADS * HEAD_DIM
WINDOW = 128
BLOCK = 128
SPAN = BLOCK + 2 * WINDOW
N_BUCKETS = 32
MAX_DISTANCE = 128
POOL_SIZES = (2, 4, 8, 16)
N_POOL_GROUPS = len(POOL_SIZES)
EPS = 1e-6
NEG_INF = -1e30
MASKED_BUCKET = N_BUCKETS
ATTN_SCALE = HEAD_DIM ** -0.5
INV_ATTN_SCALE = HEAD_DIM ** 0.5
EXP2_SCALE = ATTN_SCALE * math.log2(math.e)

SUBLANES = 8
BF16_SUBLANES = 16
VMEM_LIMIT_BYTES = 60 * 1024 * 1024

MOD_TK = 256
TM = 512
CW = 512
GATE_TM = 1024
GATE_ROWS = 512
GATE_TN = 1024
POOL_HALO = BF16_SUBLANES

Q0 = 0
KV0 = Q0 + ATTN_WIDTH
GA0 = KV0 + 2 * KV_WIDTH
U0 = GA0 + ATTN_WIDTH


def _silu(v):
    return v * (1.0 / (1.0 + jnp.exp(-v)))


def _half_silu(hv):
    return hv + hv * jnp.tanh(hv)


def _adaln_rmsnorm(x, gain, shift):
    ms = jnp.mean(x * x, axis=-1, keepdims=True)
    return (x * lax.rsqrt(ms + EPS)) * gain + shift


def _shift_scale_bias_kernel(tbl_ref, c_ref, wsh_ref, wsc_ref, bsh_ref, bsc_ref, bucket_ref,
                             shift_ref, scale_ref, bias_ref):
    k = pl.program_id(0)

    @pl.when(k == 0)
    def _():
        shift_ref[...] = bsh_ref[...]
        scale_ref[...] = bsc_ref[...]

    s = _silu(c_ref[...])
    shift_ref[...] += jnp.sum(s * wsh_ref[...], axis=0, keepdims=True)
    scale_ref[...] += jnp.sum(s * wsc_ref[...], axis=0, keepdims=True)

    for t in range(SPAN // BLOCK):
        bk = bucket_ref[:, t * BLOCK:(t + 1) * BLOCK]

        def body(b, acc):
            return jnp.where(bk == b, tbl_ref[b, k] * INV_ATTN_SCALE, acc)
        bias_ref[0, :, t * BLOCK:(t + 1) * BLOCK] = lax.fori_loop(
            0, N_BUCKETS, body, jnp.full((BLOCK, BLOCK), NEG_INF, jnp.float32), unroll=8)


def _shift_scale_bias(c_col, w_ada, b_ada_row, rel_table, bucket):
    d = w_ada.shape[0]
    steps = d // MOD_TK
    assert steps == N_Q_HEADS, "one bias-table head per grid step"
    return pl.pallas_call(
        _shift_scale_bias_kernel,
        grid=(steps,),
        in_specs=[
            pl.BlockSpec(memory_space=pltpu.SMEM),
            pl.BlockSpec((MOD_TK, 1), lambda k: (k, 0)),
            pl.BlockSpec((MOD_TK, d), lambda k: (k, 0)),
            pl.BlockSpec((MOD_TK, d), lambda k: (k, 1)),
            pl.BlockSpec((1, d), lambda k: (0, 0)),
            pl.BlockSpec((1, d), lambda k: (0, 1)),
            pl.BlockSpec(bucket.shape, lambda k: (0, 0)),
        ],
        out_specs=[
            pl.BlockSpec((1, d), lambda k: (0, 0)),
            pl.BlockSpec((1, d), lambda k: (0, 0)),
            pl.BlockSpec((1, BLOCK, SPAN), lambda k: (k, 0, 0)),
        ],
        out_shape=[
            jax.ShapeDtypeStruct((1, d), jnp.float32),
            jax.ShapeDtypeStruct((1, d), jnp.float32),
            jax.ShapeDtypeStruct((N_Q_HEADS, BLOCK, SPAN), jnp.float32),
        ],
        compiler_params=pltpu.CompilerParams(
            dimension_semantics=("arbitrary",), vmem_limit_bytes=VMEM_LIMIT_BYTES),
        name="shift_scale_bias",
    )(rel_table, c_col, w_ada, w_ada, b_ada_row, b_ada_row, bucket)


def _t5_bucket(rel):
    half = N_BUCKETS // 2
    max_exact = half // 2
    assert (max_exact, MAX_DISTANCE // max_exact, half - max_exact) == (8, 16, 8)
    n = abs(rel)
    large = min(max_exact + (n * n).bit_length() - 7, half - 1)
    return (half if rel > 0 else 0) + (n if n < max_exact else large)


def _bucket_index_table():
    table = np.full((BLOCK, SPAN), MASKED_BUCKET, np.int32)
    for q in range(BLOCK):
        for t in range(SPAN):
            rel = t - WINDOW - q
            if abs(rel) <= WINDOW:
                table[q, t] = _t5_bucket(rel)
    return table


def _merge_gate_kernel(x_ref, shift_ref, scale_ref, g_ref, w_ref,
                       win_ref, wa_ref, wp_ref, wout_ref, c_ref, wgate_ref, bgate_ref,
                       o_ref, win_o_ref, wa_o_ref, wp_o_ref, wout_o_ref, gate_ref,
                       wbf_ref, h_ref):
    j, i = pl.program_id(0), pl.program_id(1)

    @pl.when(i == 0)
    def _():
        wbf_ref[...] = (0.5 * w_ref[...]).astype(wbf_ref.dtype)

    @pl.when((j == 0) & (i == 0))
    def _():
        gate_ref[...] = bgate_ref[...]

    gain = g_ref[...] * (1.0 + scale_ref[...])
    for r in range(0, x_ref.shape[0], GATE_ROWS):
        h_ref[r:r + GATE_ROWS] = _adaln_rmsnorm(x_ref[r:r + GATE_ROWS], gain,
                                                shift_ref[...]).astype(h_ref.dtype)
    for r in range(0, x_ref.shape[0], GATE_ROWS):
        o_ref[r:r + GATE_ROWS] = jnp.dot(h_ref[r:r + GATE_ROWS], wbf_ref[...],
                                         preferred_element_type=jnp.float32).astype(o_ref.dtype)

    win_o_ref[...] = win_ref[...].astype(win_o_ref.dtype)
    wa_o_ref[...] = wa_ref[...].astype(wa_o_ref.dtype)
    wp_o_ref[...] = wp_ref[...].astype(wp_o_ref.dtype)
    wout_o_ref[...] = wout_ref[...].astype(wout_o_ref.dtype)
    gate_ref[...] += jnp.sum(_silu(c_ref[...]) * wgate_ref[...], axis=0, keepdims=True)


def _merge_gate(x2, shift, scale, pre_g_row, w_merge, w_in, wa, wp, wout, c_col, w_ada, b_ada_row):
    s, d = x2.shape
    n = w_merge.shape[1]
    n_i = s // GATE_TM
    steps = (n // GATE_TN) * n_i
    slab = lambda w: (w.shape[0] // steps, w.shape[1])
    step = lambda j, i: (j * n_i + i, 0)
    for w in (w_in, wa, wp, wout):
        assert w.shape[0] % (steps * BF16_SUBLANES) == 0
    assert d % (steps * SUBLANES) == 0
    const2 = lambda j, i: (0, 0)
    bf16 = jnp.bfloat16
    return pl.pallas_call(
        _merge_gate_kernel,
        grid=(n // GATE_TN, n_i),
        in_specs=[
            pl.BlockSpec((GATE_TM, d), lambda j, i: (i, 0)),
            pl.BlockSpec((1, d), const2),
            pl.BlockSpec((1, d), const2),
            pl.BlockSpec((1, d), const2),
            pl.BlockSpec((d, GATE_TN), lambda j, i: (0, j)),
            pl.BlockSpec(slab(w_in), step),
            pl.BlockSpec(slab(wa), step),
            pl.BlockSpec(slab(wp), step),
            pl.BlockSpec(slab(wout), step),
            pl.BlockSpec((d // steps, 1), step),
            pl.BlockSpec((d // steps, d), lambda j, i: (j * n_i + i, 2)),
            pl.BlockSpec((1, d), lambda j, i: (0, 2)),
        ],
        out_specs=[
            pl.BlockSpec((GATE_TM, GATE_TN), lambda j, i: (i, j)),
            pl.BlockSpec(slab(w_in), step),
            pl.BlockSpec(slab(wa), step),
            pl.BlockSpec(slab(wp), step),
            pl.BlockSpec(slab(wout), step),
            pl.BlockSpec((1, d), const2),
        ],
        out_shape=[
            jax.ShapeDtypeStruct((s, n), bf16),
            jax.ShapeDtypeStruct(w_in.shape, bf16),
            jax.ShapeDtypeStruct(wa.shape, bf16),
            jax.ShapeDtypeStruct(wp.shape, bf16),
            jax.ShapeDtypeStruct(wout.shape, bf16),
            jax.ShapeDtypeStruct((1, d), jnp.float32),
        ],
        scratch_shapes=[pltpu.VMEM((d, GATE_TN), bf16),
                        pltpu.VMEM((GATE_TM, d), bf16)],
        compiler_params=pltpu.CompilerParams(
            dimension_semantics=("arbitrary", "arbitrary"), vmem_limit_bytes=VMEM_LIMIT_BYTES),
        name="merge_gate",
    )(x2, shift, scale, pre_g_row, w_merge, w_in, wa, wp, wout, c_col, w_ada, b_ada_row)


def _project_mix_kernel(sink_ref,
                        x_ref, shift_ref, scale_ref, g_ref, w_ref, wpool_ref, pscale_ref, bias_ref,
                        ya_ref, yp_ref,
                        h_ref, pbuf_ref, kvtail_ref, utail_ref, uext_ref, wpool_bf_ref,
                        *, seq_len):
    s = pl.program_id(0)
    n_tok = pl.num_programs(0) - 1
    tm, d = x_ref.shape
    n_sub = tm // BLOCK
    n_blocks = seq_len // BLOCK
    aw = ATTN_WIDTH
    pw = pscale_ref.shape[1]
    q0, ga0, gp0 = Q0, GA0, U0 + pw
    n_chunks = w_ref.shape[1] // CW
    f32 = jnp.float32
    bf16 = jnp.bfloat16
    cur = s % 2
    prv = 1 - cur
    j = s - 1

    def prologue():
        gain = g_ref[...] * (1.0 + scale_ref[...])
        h_ref[...] = _adaln_rmsnorm(x_ref[...], gain, shift_ref[...]).astype(bf16)

    def is_gate_col(col):
        return ga0 <= col < ga0 + aw or gp0 <= col < gp0 + pw

    def proj_chunk(c):
        lo, hi = c * CW, (c + 1) * CW
        p = jnp.dot(h_ref[...], w_ref[:, lo:hi], preferred_element_type=f32)
        if is_gate_col(lo):
            assert is_gate_col(hi - 1)
            p = 0.5 * p
        pbuf_ref[cur, :, lo:hi] = p.astype(bf16)

    def window_rows(b, col0):
        r0, r1 = (b - 1) * BLOCK, (b + 2) * BLOCK
        parts = []
        if r0 < 0:
            parts.append(kvtail_ref[prv, :, col0 - KV0:col0 - KV0 + HEAD_DIM])
            r0 = 0
        parts.append(pbuf_ref[prv, r0:min(r1, tm), col0:col0 + HEAD_DIM])
        if r1 > tm:
            parts.append(pbuf_ref[cur, 0:r1 - tm, col0:col0 + HEAD_DIM])
        return jnp.concatenate(parts, axis=0) if len(parts) > 1 else parts[0]

    def attn_scores(b, kvh):
        blk = j * n_sub + b
        h0 = kvh * GQA_GROUP
        k = window_rows(b, KV0 + kvh * HEAD_DIM)
        v = window_rows(b, KV0 + KV_WIDTH + kvh * HEAD_DIM)
        qs = jnp.concatenate(
            [pbuf_ref[prv, b * BLOCK:(b + 1) * BLOCK,
                      q0 + (h0 + g) * HEAD_DIM:q0 + (h0 + g + 1) * HEAD_DIM]
             for g in range(GQA_GROUP)], axis=0)
        z = lax.dot_general(qs, k, (((1,), (1,)), ((), ())), preferred_element_type=f32)
        z = z + bias_ref[h0:h0 + GQA_GROUP].reshape(GQA_GROUP * BLOCK, SPAN)
        col = lax.broadcasted_iota(jnp.int32, (1, SPAN), 1)
        if b == 0:
            z = jnp.where(col < jnp.where(blk == 0, WINDOW, 0), NEG_INF, z)
        if b == n_sub - 1:
            z = jnp.where(col >= jnp.where(blk == n_blocks - 1, WINDOW + BLOCK, SPAN), NEG_INF, z)
        sink = jnp.concatenate(
            [jnp.full((BLOCK, 1), sink_ref[h0 + g] * INV_ATTN_SCALE, f32)
             for g in range(GQA_GROUP)], axis=0)
        m = jnp.maximum(jnp.max(z, axis=-1, keepdims=True), sink)
        p = jnp.exp2((z - m) * EXP2_SCALE)
        denom = jnp.sum(p, axis=-1, keepdims=True) + jnp.exp2((sink - m) * EXP2_SCALE)
        return p.astype(bf16), denom, v

    def attn_out(p, denom, v):
        o = jnp.dot(p, v, preferred_element_type=f32) / denom
        return [o[g * BLOCK:(g + 1) * BLOCK] for g in range(GQA_GROUP)]

    def finish_sub_block(b, heads):
        y = jnp.concatenate(heads, axis=1)
        hg = pbuf_ref[prv, b * BLOCK:(b + 1) * BLOCK, ga0:ga0 + aw].astype(f32)
        ya_ref[b * BLOCK:(b + 1) * BLOCK] = (y * _half_silu(hg)).astype(bf16)

    ext = tm + 2 * POOL_HALO

    def pool_group(gi):
        w = POOL_SIZES[gi]
        gw = pw // N_POOL_GROUPS
        c0, c1 = gi * gw, (gi + 1) * gw
        half = w // 2
        halo_row = lax.broadcasted_iota(jnp.int32, (POOL_HALO, 1), 0)
        uext_ref[0:POOL_HALO] = jnp.where(j * tm - POOL_HALO + halo_row >= 0,
                                          utail_ref[prv, :, c0:c1].astype(f32), 0.0)
        uext_ref[POOL_HALO:POOL_HALO + tm] = pbuf_ref[prv, :, U0 + c0:U0 + c1].astype(f32)
        uext_ref[POOL_HALO + tm:ext] = jnp.where(
            (j + 1) * tm + halo_row < seq_len,
            pbuf_ref[cur, 0:POOL_HALO, U0 + c0:U0 + c1].astype(f32), 0.0)
        a = uext_ref[...]
        sh = 1
        while sh < w:
            a = a + pltpu.roll(a, ext - sh, axis=0)
            sh *= 2
        off = POOL_HALO - half
        if off % SUBLANES:
            a = pltpu.roll(a, ext - off, axis=0)
            off = 0
        pos = j * tm + lax.broadcasted_iota(jnp.int32, (tm, 1), 0)
        cnt = (jnp.minimum(pos + half, seq_len) - jnp.maximum(pos - half, 0)).astype(f32)
        pooled = a[off:off + tm] / cnt - uext_ref[POOL_HALO:POOL_HALO + tm]
        mixed = jnp.dot(pooled.astype(bf16), wpool_bf_ref[gi], preferred_element_type=f32)
        hg = pbuf_ref[prv, :, gp0 + c0:gp0 + c1].astype(f32)
        yp_ref[:, c0:c1] = (mixed * pscale_ref[:, c0:c1] * _half_silu(hg)).astype(bf16)

    def save_tails():
        kvtail_ref[cur] = pbuf_ref[prv, tm - BLOCK:tm, KV0:KV0 + 2 * KV_WIDTH]
        utail_ref[cur] = pbuf_ref[prv, tm - POOL_HALO:tm, U0:U0 + pw]

    units = [(b, kvh) for b in range(n_sub) for kvh in range(N_KV_HEADS)]
    assert KV0 % CW == 0 and (2 * KV_WIDTH) % CW == 0 and U0 % CW == 0 and pw % CW == 0
    first = list(range(KV0 // CW, (KV0 + 2 * KV_WIDTH) // CW)) + list(range(U0 // CW, (U0 + pw) // CW))
    order = first + [c for c in range(n_chunks) if c not in first]
    u_chunks_done = len(first)
    assert len(units) + 1 <= n_chunks and u_chunks_done + N_POOL_GROUPS <= n_chunks

    def run(do_proj, do_mix):
        if do_proj:
            prologue()
        pending = attn_scores(*units[0]) if do_mix else None
        heads = []
        for pos, c in enumerate(order):
            if do_proj:
                proj_chunk(c)
            if not do_mix:
                continue
            if pos < len(units):
                heads += attn_out(*pending)
                if pos + 1 < len(units):
                    pending = attn_scores(*units[pos + 1])
                b, kvh = units[pos]
                if kvh == N_KV_HEADS - 1:
                    finish_sub_block(b, heads)
                    heads = []
            if u_chunks_done <= pos < u_chunks_done + N_POOL_GROUPS:
                pool_group(pos - u_chunks_done)
        if do_mix:
            save_tails()

    @pl.when(s == 0)
    def _():
        kvtail_ref[0] = jnp.zeros(kvtail_ref.shape[1:], kvtail_ref.dtype)
        utail_ref[0] = jnp.zeros(utail_ref.shape[1:], utail_ref.dtype)
        wpool_bf_ref[...] = wpool_ref[...].astype(bf16)
        run(True, False)

    @pl.when((s > 0) & (s < n_tok))
    def _():
        run(True, True)

    @pl.when(s == n_tok)
    def _():
        run(False, True)


def _project_mix(x2, shift, scale, pre_g_row, w_in_bf, wpool, pool_scale_row, bias_tbl, sink):
    s, d = x2.shape
    tm = TM
    n_tok = s // tm
    pw = pool_scale_row.shape[1]
    assert s // BLOCK >= 2, "first and last attention block must differ"
    assert w_in_bf.shape[1] == U0 + 2 * pw
    cur_blk = lambda st: jnp.minimum(st, n_tok - 1)
    mix_blk = lambda st: jnp.maximum(st - 1, 0)
    const2 = lambda st: (0, 0)
    single = pl.Buffered(1)
    bf16 = jnp.bfloat16
    return pl.pallas_call(
        functools.partial(_project_mix_kernel, seq_len=s),
        grid=(n_tok + 1,),
        in_specs=[
            pl.BlockSpec(memory_space=pltpu.SMEM),
            pl.BlockSpec((tm, d), lambda st: (cur_blk(st), 0)),
            pl.BlockSpec((1, d), const2),
            pl.BlockSpec((1, d), const2),
            pl.BlockSpec((1, d), const2),
            pl.BlockSpec(w_in_bf.shape, const2, pipeline_mode=single),
            pl.BlockSpec(wpool.shape, lambda st: (0, 0, 0), pipeline_mode=single),
            pl.BlockSpec((1, pw), const2),
            pl.BlockSpec(bias_tbl.shape, lambda st: (0, 0, 0), pipeline_mode=single),
        ],
        out_specs=[
            pl.BlockSpec((tm, ATTN_WIDTH), lambda st: (mix_blk(st), 0)),
            pl.BlockSpec((tm, pw), lambda st: (mix_blk(st), 0)),
        ],
        out_shape=[
            jax.ShapeDtypeStruct((s, ATTN_WIDTH), bf16),
            jax.ShapeDtypeStruct((s, pw), bf16),
        ],
        scratch_shapes=[
            pltpu.VMEM((tm, d), bf16),
            pltpu.VMEM((2, tm, w_in_bf.shape[1]), bf16),
            pltpu.VMEM((2, BLOCK, 2 * KV_WIDTH), bf16),
            pltpu.VMEM((2, POOL_HALO, pw), bf16),
            pltpu.VMEM((tm + 2 * POOL_HALO, pw // N_POOL_GROUPS), jnp.float32),
            pltpu.VMEM(wpool.shape, bf16),
        ],
        compiler_params=pltpu.CompilerParams(
            dimension_semantics=("arbitrary",), vmem_limit_bytes=VMEM_LIMIT_BYTES),
        name="project_mix",
    )(sink, x2, shift, scale, pre_g_row, w_in_bf, wpool, pool_scale_row, bias_tbl)


def _merge_out_kernel(ya_ref, yp_ref, gm_ref, x_ref, gate_ref, postg_ref, bm_ref,
                      wa_ref, wp_ref, wout_ref, o_ref, merged_ref, oacc_ref, inv_ref):
    s = pl.program_id(0)
    n_tok = pl.num_programs(0) - 1
    tm, d = x_ref.shape
    n_chunks = d // CW
    f32 = jnp.float32

    def finish_previous():
        scale = gate_ref[...] * postg_ref[...]
        inv = inv_ref[...]
        for c in range(n_chunks):
            lo, hi = c * CW, (c + 1) * CW
            o_ref[:, lo:hi] = x_ref[:, lo:hi] + (oacc_ref[:, lo:hi] * inv) * scale[:, lo:hi]

    def matmuls():
        half_bm = 0.5 * bm_ref[...]
        for c in range(n_chunks):
            lo, hi = c * CW, (c + 1) * CW
            bra = jnp.dot(ya_ref[...], wa_ref[:, lo:hi], preferred_element_type=f32)
            brp = jnp.dot(yp_ref[...], wp_ref[:, lo:hi], preferred_element_type=f32)
            g_a = 0.5 * jnp.tanh(gm_ref[:, lo:hi].astype(f32) + half_bm[:, lo:hi]) + 0.5
            g_p = 0.5 * jnp.tanh(gm_ref[:, d + lo:d + hi].astype(f32)
                                 + half_bm[:, d + lo:d + hi]) + 0.5
            merged_ref[:, lo:hi] = (g_a * bra + g_p * brp).astype(merged_ref.dtype)
        ssq = jnp.zeros((tm, 1), f32)
        for c in range(n_chunks):
            lo, hi = c * CW, (c + 1) * CW
            o = jnp.dot(merged_ref[...], wout_ref[:, lo:hi], preferred_element_type=f32)
            ssq = ssq + jnp.sum(o * o, axis=-1, keepdims=True)
            oacc_ref[:, lo:hi] = o
        inv_ref[...] = lax.rsqrt(ssq * (1.0 / d) + EPS)

    @pl.when(s == 0)
    def _():
        matmuls()

    @pl.when((s > 0) & (s < n_tok))
    def _():
        finish_previous()
        matmuls()

    @pl.when(s == n_tok)
    def _():
        finish_previous()


def _merge_out(ya, yp, gm, x2, gate, post_g_row, b_merge_row, wa, wp, wout):
    s, d = x2.shape
    tm = TM
    n_tok = s // tm
    aw, pw = ya.shape[1], yp.shape[1]
    cur = lambda st: (jnp.minimum(st, n_tok - 1), 0)
    prev = lambda st: (jnp.maximum(st - 1, 0), 0)
    const2 = lambda st: (0, 0)
    single = pl.Buffered(1)
    return pl.pallas_call(
        _merge_out_kernel,
        grid=(n_tok + 1,),
        in_specs=[
            pl.BlockSpec((tm, aw), cur),
            pl.BlockSpec((tm, pw), cur),
            pl.BlockSpec((tm, 2 * d), cur),
            pl.BlockSpec((tm, d), prev),
            pl.BlockSpec((1, d), const2),
            pl.BlockSpec((1, d), const2),
            pl.BlockSpec((1, 2 * d), const2),
            pl.BlockSpec((aw, d), const2, pipeline_mode=single),
            pl.BlockSpec((pw, d), const2, pipeline_mode=single),
            pl.BlockSpec((d, d), const2, pipeline_mode=single),
        ],
        out_specs=pl.BlockSpec((tm, d), prev),
        out_shape=jax.ShapeDtypeStruct((s, d), jnp.float32),
        scratch_shapes=[
            pltpu.VMEM((tm, d), jnp.bfloat16),
            pltpu.VMEM((tm, d), jnp.float32),
            pltpu.VMEM((tm, 1), jnp.float32),
        ],
        compiler_params=pltpu.CompilerParams(
            dimension_semantics=("arbitrary",), vmem_limit_bytes=VMEM_LIMIT_BYTES),
        name="merge_out",
    )(ya, yp, gm, x2, gate, post_g_row, b_merge_row, wa, wp, wout)


def _layer(x2, c, rel_table, bucket, w_ada, b_ada, pre_g, post_g, w_in, sink, w_pool, pool_scale,
           w_br_attn, w_br_pool, w_merge, b_merge, w_out):
    s, d = x2.shape
    c_col = c.reshape(d, 1)
    b_ada_row = b_ada.reshape(1, -1)
    pre_g_row = pre_g.reshape(1, d)
    shift, scale, bias_tbl = _shift_scale_bias(c_col, w_ada, b_ada_row, rel_table, bucket)
    gm, w_in_bf, wa, wp, wout, gate = _merge_gate(
        x2, shift, scale, pre_g_row, w_merge, w_in, w_br_attn, w_br_pool, w_out,
        c_col, w_ada, b_ada_row)
    ya, yp = _project_mix(x2, shift, scale, pre_g_row, w_in_bf, w_pool,
                          pool_scale.reshape(1, -1), bias_tbl, sink)
    return _merge_out(ya, yp, gm, x2, gate, post_g.reshape(1, d), b_merge.reshape(1, -1),
                      wa, wp, wout)


def kernel(x, c, rel_bias_table, w_ada, b_ada, pre_norm_g, post_norm_g, w_in, attn_sink,
           w_pool_group, pool_scale, w_branch_attn, w_branch_pool, w_merge, b_merge, w_out):
    batch, s, d = x.shape
    assert batch == 1, "kernel is written for a single sequence"
    depth = w_ada.shape[0]
    bucket = _bucket_index_table()
    x2 = x.reshape(s, d)
    for l in range(depth):
        x2 = _layer(x2, c[0], rel_bias_table, bucket, w_ada[l], b_ada[l], pre_norm_g[l],
                    post_norm_g[l], w_in[l], attn_sink[l], w_pool_group[l], pool_scale[l],
                    w_branch_attn[l], w_branch_pool[l], w_merge[l], b_merge[l], w_out[l])
    return x2.reshape(batch, s, d)
```

```python
import functools
import math

import numpy as np
import jax
import jax.numpy as jnp
from jax import lax
from jax.experimental import pallas as pl
from jax.experimental.pallas import tpu as pltpu

HEAD_DIM = 128
N_Q_HEADS = 8
N_KV_HEADS = 2
GQA_GROUP = N_Q_HEADS // N_KV_HEADS
ATTN_WIDTH = N_Q_HEADS * HEAD_DIM
KV_WIDTH = N_KV_HEADS * HEAD_DIM
WINDOW = 128
BLOCK = 128
SPAN = BLOCK + 2 * WINDOW
N_BUCKETS = 32
MAX_DISTANCE = 128
POOL_SIZES = (2, 4, 8, 16)
N_POOL_GROUPS = len(POOL_SIZES)
EPS = 1e-6
NEG_INF = -1e30
MASKED_BUCKET = N_BUCKETS
ATTN_SCALE = HEAD_DIM ** -0.5
INV_ATTN_SCALE = HEAD_DIM ** 0.5
EXP2_SCALE = ATTN_SCALE * math.log2(math.e)

SUBLANES = 8
BF16_SUBLANES = 16
VMEM_LIMIT_BYTES = 60 * 1024 * 1024

MOD_TK = 256
TM = 512
CW = 512
GATE_TM = 2048
GATE_ROWS = 512
GATE_TN = 1024
POOL_HALO = BF16_SUBLANES

Q0 = 0
KV0 = Q0 + ATTN_WIDTH
GA0 = KV0 + 2 * KV_WIDTH
U0 = GA0 + ATTN_WIDTH


def _silu(v):
    return v * (1.0 / (1.0 + jnp.exp(-v)))


def _half_silu(hv):
    return hv + hv * jnp.tanh(hv)


def _adaln_rmsnorm(x, gain, shift):
    ms = jnp.mean(x * x, axis=-1, keepdims=True)
    return (x * lax.rsqrt(ms + EPS)) * gain + shift


def _shift_scale_bias_kernel(tbl_ref, c_ref, wsh_ref, wsc_ref, bsh_ref, bsc_ref, bucket_ref,
                             shift_ref, scale_ref, bias_ref):
    k = pl.program_id(0)

    @pl.when(k == 0)
    def _():
        shift_ref[...] = bsh_ref[...]
        scale_ref[...] = bsc_ref[...]

    s = _silu(c_ref[...])
    shift_ref[...] += jnp.sum(s * wsh_ref[...], axis=0, keepdims=True)
    scale_ref[...] += jnp.sum(s * wsc_ref[...], axis=0, keepdims=True)

    for t in range(SPAN // BLOCK):
        bk = bucket_ref[:, t * BLOCK:(t + 1) * BLOCK]

        def body(b, acc):
            return jnp.where(bk == b, tbl_ref[b, k] * INV_ATTN_SCALE, acc)
        bias_ref[0, :, t * BLOCK:(t + 1) * BLOCK] = lax.fori_loop(
            0, N_BUCKETS, body, jnp.full((BLOCK, BLOCK), NEG_INF, jnp.float32), unroll=8)


def _shift_scale_bias(c_col, w_ada, b_ada_row, rel_table, bucket):
    d = w_ada.shape[0]
    steps = d // MOD_TK
    assert steps == N_Q_HEADS, "one bias-table head per grid step"
    return pl.pallas_call(
        _shift_scale_bias_kernel,
        grid=(steps,),
        in_specs=[
            pl.BlockSpec(memory_space=pltpu.SMEM),
            pl.BlockSpec((MOD_TK, 1), lambda k: (k, 0)),
            pl.BlockSpec((MOD_TK, d), lambda k: (k, 0)),
            pl.BlockSpec((MOD_TK, d), lambda k: (k, 1)),
            pl.BlockSpec((1, d), lambda k: (0, 0)),
            pl.BlockSpec((1, d), lambda k: (0, 1)),
            pl.BlockSpec(bucket.shape, lambda k: (0, 0)),
        ],
        out_specs=[
            pl.BlockSpec((1, d), lambda k: (0, 0)),
            pl.BlockSpec((1, d), lambda k: (0, 0)),
            pl.BlockSpec((1, BLOCK, SPAN), lambda k: (k, 0, 0)),
        ],
        out_shape=[
            jax.ShapeDtypeStruct((1, d), jnp.float32),
            jax.ShapeDtypeStruct((1, d), jnp.float32),
            jax.ShapeDtypeStruct((N_Q_HEADS, BLOCK, SPAN), jnp.float32),
        ],
        compiler_params=pltpu.CompilerParams(
            dimension_semantics=("arbitrary",), vmem_limit_bytes=VMEM_LIMIT_BYTES),
        name="shift_scale_bias",
    )(rel_table, c_col, w_ada, w_ada, b_ada_row, b_ada_row, bucket)


def _t5_bucket(rel):
    half = N_BUCKETS // 2
    max_exact = half // 2
    assert (max_exact, MAX_DISTANCE // max_exact, half - max_exact) == (8, 16, 8)
    n = abs(rel)
    large = min(max_exact + (n * n).bit_length() - 7, half - 1)
    return (half if rel > 0 else 0) + (n if n < max_exact else large)


def _bucket_index_table():
    table = np.full((BLOCK, SPAN), MASKED_BUCKET, np.int32)
    for q in range(BLOCK):
        for t in range(SPAN):
            rel = t - WINDOW - q
            if abs(rel) <= WINDOW:
                table[q, t] = _t5_bucket(rel)
    return table


def _merge_gate_kernel(h_ref, w_ref, wa_ref, wp_ref, wout_ref, c_ref, wgate_ref, bgate_ref,
                       o_ref, wa_o_ref, wp_o_ref, wout_o_ref, gate_ref,
                       wbf_ref):
    j, i = pl.program_id(0), pl.program_id(1)

    @pl.when(i == 0)
    def _():
        wbf_ref[...] = (0.5 * w_ref[...]).astype(wbf_ref.dtype)

    @pl.when((j == 0) & (i == 0))
    def _():
        gate_ref[...] = bgate_ref[...]

    for r in range(0, h_ref.shape[0], GATE_ROWS):
        o_ref[r:r + GATE_ROWS] = jnp.dot(h_ref[r:r + GATE_ROWS], wbf_ref[...],
                                         preferred_element_type=jnp.float32).astype(o_ref.dtype)

    wa_o_ref[...] = wa_ref[...].astype(wa_o_ref.dtype)
    wp_o_ref[...] = wp_ref[...].astype(wp_o_ref.dtype)
    wout_o_ref[...] = wout_ref[...].astype(wout_o_ref.dtype)
    gate_ref[...] += jnp.sum(_silu(c_ref[...]) * wgate_ref[...], axis=0, keepdims=True)


def _merge_gate(h, w_merge, wa, wp, wout, c_col, w_ada, b_ada_row):
    s, d = h.shape
    n = w_merge.shape[1]
    n_i = s // GATE_TM
    steps = (n // GATE_TN) * n_i
    slab = lambda w: (w.shape[0] // steps, w.shape[1])
    step = lambda j, i: (j * n_i + i, 0)
    for w in (wa, wp, wout):
        assert w.shape[0] % (steps * BF16_SUBLANES) == 0
    assert d % (steps * SUBLANES) == 0
    const2 = lambda j, i: (0, 0)
    bf16 = jnp.bfloat16
    return pl.pallas_call(
        _merge_gate_kernel,
        grid=(n // GATE_TN, n_i),
        in_specs=[
            pl.BlockSpec((GATE_TM, d), lambda j, i: (i, 0)),
            pl.BlockSpec((d, GATE_TN), lambda j, i: (0, j)),
            pl.BlockSpec(slab(wa), step),
            pl.BlockSpec(slab(wp), step),
            pl.BlockSpec(slab(wout), step),
            pl.BlockSpec((d // steps, 1), step),
            pl.BlockSpec((d // steps, d), lambda j, i: (j * n_i + i, 2)),
            pl.BlockSpec((1, d), lambda j, i: (0, 2)),
        ],
        out_specs=[
            pl.BlockSpec((GATE_TM, GATE_TN), lambda j, i: (i, j)),
            pl.BlockSpec(slab(wa), step),
            pl.BlockSpec(slab(wp), step),
            pl.BlockSpec(slab(wout), step),
            pl.BlockSpec((1, d), const2),
        ],
        out_shape=[
            jax.ShapeDtypeStruct((s, n), bf16),
            jax.ShapeDtypeStruct(wa.shape, bf16),
            jax.ShapeDtypeStruct(wp.shape, bf16),
            jax.ShapeDtypeStruct(wout.shape, bf16),
            jax.ShapeDtypeStruct((1, d), jnp.float32),
        ],
        scratch_shapes=[pltpu.VMEM((d, GATE_TN), bf16)],
        compiler_params=pltpu.CompilerParams(
            dimension_semantics=("arbitrary", "arbitrary"), vmem_limit_bytes=VMEM_LIMIT_BYTES),
        name="merge_gate",
    )(h, w_merge, wa, wp, wout, c_col, w_ada, b_ada_row)


def _project_mix_kernel(sink_ref,
                        x_ref, shift_ref, scale_ref, g_ref, w_ref, wpool_ref, pscale_ref, bias_ref,
                        ya_ref, yp_ref, h_ref,
                        pbuf_ref, kvtail_ref, utail_ref, uext_ref,
                        *, seq_len):
    s = pl.program_id(0)
    n_tok = pl.num_programs(0) - 1
    tm, d = x_ref.shape
    n_sub = tm // BLOCK
    n_blocks = seq_len // BLOCK
    aw = ATTN_WIDTH
    pw = pscale_ref.shape[1]
    q0, ga0, gp0 = Q0, GA0, U0 + pw
    n_chunks = w_ref.shape[1] // CW
    f32 = jnp.float32
    bf16 = jnp.bfloat16
    cur = s % 2
    prv = 1 - cur
    j = s - 1

    def prologue():
        gain = g_ref[...] * (1.0 + scale_ref[...])
        h_ref[...] = _adaln_rmsnorm(x_ref[...], gain, shift_ref[...]).astype(bf16)

    def is_gate_col(col):
        return ga0 <= col < ga0 + aw or gp0 <= col < gp0 + pw

    def proj_chunk(c):
        lo, hi = c * CW, (c + 1) * CW
        p = jnp.dot(h_ref[...], w_ref[:, lo:hi], preferred_element_type=f32)
        if is_gate_col(lo):
            assert is_gate_col(hi - 1)
            p = 0.5 * p
        pbuf_ref[cur, :, lo:hi] = p.astype(bf16)

    def window_rows(b, col0):
        r0, r1 = (b - 1) * BLOCK, (b + 2) * BLOCK
        parts = []
        if r0 < 0:
            parts.append(kvtail_ref[prv, :, col0 - KV0:col0 - KV0 + HEAD_DIM])
            r0 = 0
        parts.append(pbuf_ref[prv, r0:min(r1, tm), col0:col0 + HEAD_DIM])
        if r1 > tm:
            parts.append(pbuf_ref[cur, 0:r1 - tm, col0:col0 + HEAD_DIM])
        return jnp.concatenate(parts, axis=0) if len(parts) > 1 else parts[0]

    def attn_scores(b, kvh):
        blk = j * n_sub + b
        h0 = kvh * GQA_GROUP
        k = window_rows(b, KV0 + kvh * HEAD_DIM)
        v = window_rows(b, KV0 + KV_WIDTH + kvh * HEAD_DIM)
        qs = jnp.concatenate(
            [pbuf_ref[prv, b * BLOCK:(b + 1) * BLOCK,
                      q0 + (h0 + g) * HEAD_DIM:q0 + (h0 + g + 1) * HEAD_DIM]
             for g in range(GQA_GROUP)], axis=0)
        z = lax.dot_general(qs, k, (((1,), (1,)), ((), ())), preferred_element_type=f32)
        z = z + bias_ref[h0:h0 + GQA_GROUP].reshape(GQA_GROUP * BLOCK, SPAN)
        col = lax.broadcasted_iota(jnp.int32, (1, SPAN), 1)
        if b == 0:
            z = jnp.where(col < jnp.where(blk == 0, WINDOW, 0), NEG_INF, z)
        if b == n_sub - 1:
            z = jnp.where(col >= jnp.where(blk == n_blocks - 1, WINDOW + BLOCK, SPAN), NEG_INF, z)
        sink = jnp.concatenate(
            [jnp.full((BLOCK, 1), sink_ref[h0 + g] * INV_ATTN_SCALE, f32)
             for g in range(GQA_GROUP)], axis=0)
        m = jnp.maximum(jnp.max(z, axis=-1, keepdims=True), sink)
        p = jnp.exp2((z - m) * EXP2_SCALE)
        denom = jnp.sum(p, axis=-1, keepdims=True) + jnp.exp2((sink - m) * EXP2_SCALE)
        return p.astype(bf16), denom, v

    def attn_out(p, denom, v):
        o = jnp.dot(p, v, preferred_element_type=f32) / denom
        return [o[g * BLOCK:(g + 1) * BLOCK] for g in range(GQA_GROUP)]

    def finish_sub_block(b, heads):
        y = jnp.concatenate(heads, axis=1)
        hg = pbuf_ref[prv, b * BLOCK:(b + 1) * BLOCK, ga0:ga0 + aw].astype(f32)
        ya_ref[b * BLOCK:(b + 1) * BLOCK] = (y * _half_silu(hg)).astype(bf16)

    ext = tm + 2 * POOL_HALO

    def pool_group(gi):
        w = POOL_SIZES[gi]
        gw = pw // N_POOL_GROUPS
        c0, c1 = gi * gw, (gi + 1) * gw
        half = w // 2
        halo_row = lax.broadcasted_iota(jnp.int32, (POOL_HALO, 1), 0)
        uext_ref[0:POOL_HALO] = jnp.where(j * tm - POOL_HALO + halo_row >= 0,
                                          utail_ref[prv, :, c0:c1].astype(f32), 0.0)
        uext_ref[POOL_HALO:POOL_HALO + tm] = pbuf_ref[prv, :, U0 + c0:U0 + c1].astype(f32)
        uext_ref[POOL_HALO + tm:ext] = jnp.where(
            (j + 1) * tm + halo_row < seq_len,
            pbuf_ref[cur, 0:POOL_HALO, U0 + c0:U0 + c1].astype(f32), 0.0)
        a = uext_ref[...]
        sh = 1
        while sh < w:
            a = a + pltpu.roll(a, ext - sh, axis=0)
            sh *= 2
        off = POOL_HALO - half
        if off % SUBLANES:
            a = pltpu.roll(a, ext - off, axis=0)
            off = 0
        pos = j * tm + lax.broadcasted_iota(jnp.int32, (tm, 1), 0)
        cnt = (jnp.minimum(pos + half, seq_len) - jnp.maximum(pos - half, 0)).astype(f32)
        pooled = a[off:off + tm] / cnt - uext_ref[POOL_HALO:POOL_HALO + tm]
        mixed = jnp.dot(pooled.astype(bf16), wpool_ref[gi], preferred_element_type=f32)
        hg = pbuf_ref[prv, :, gp0 + c0:gp0 + c1].astype(f32)
        yp_ref[:, c0:c1] = (mixed * pscale_ref[:, c0:c1] * _half_silu(hg)).astype(bf16)

    def save_tails():
        kvtail_ref[cur] = pbuf_ref[prv, tm - BLOCK:tm, KV0:KV0 + 2 * KV_WIDTH]
        utail_ref[cur] = pbuf_ref[prv, tm - POOL_HALO:tm, U0:U0 + pw]

    units = [(b, kvh) for b in range(n_sub) for kvh in range(N_KV_HEADS)]
    assert KV0 % CW == 0 and (2 * KV_WIDTH) % CW == 0 and U0 % CW == 0 and pw % CW == 0
    first = list(range(KV0 // CW, (KV0 + 2 * KV_WIDTH) // CW)) + list(range(U0 // CW, (U0 + pw) // CW))
    order = first + [c for c in range(n_chunks) if c not in first]
    u_chunks_done = len(first)
    assert len(units) + 1 <= n_chunks and u_chunks_done + N_POOL_GROUPS <= n_chunks

    def run(do_proj, do_mix):
        if do_proj:
            prologue()
        pending = attn_scores(*units[0]) if do_mix else None
        heads = []
        for pos, c in enumerate(order):
            if do_proj:
                proj_chunk(c)
            if not do_mix:
                continue
            if pos < len(units):
                heads += attn_out(*pending)
                if pos + 1 < len(units):
                    pending = attn_scores(*units[pos + 1])
                b, kvh = units[pos]
                if kvh == N_KV_HEADS - 1:
                    finish_sub_block(b, heads)
                    heads = []
            if u_chunks_done <= pos < u_chunks_done + N_POOL_GROUPS:
                pool_group(pos - u_chunks_done)
        if do_mix:
            save_tails()

    @pl.when(s == 0)
    def _():
        kvtail_ref[0] = jnp.zeros(kvtail_ref.shape[1:], kvtail_ref.dtype)
        utail_ref[0] = jnp.zeros(utail_ref.shape[1:], utail_ref.dtype)
        run(True, False)

    @pl.when((s > 0) & (s < n_tok))
    def _():
        run(True, True)

    @pl.when(s == n_tok)
    def _():
        run(False, True)


def _project_mix(x2, shift, scale, pre_g_row, w_in_bf, wpool, pool_scale_row, bias_tbl, sink):
    s, d = x2.shape
    tm = TM
    n_tok = s // tm
    pw = pool_scale_row.shape[1]
    assert s // BLOCK >= 2, "first and last attention block must differ"
    assert w_in_bf.shape[1] == U0 + 2 * pw
    cur_blk = lambda st: jnp.minimum(st, n_tok - 1)
    mix_blk = lambda st: jnp.maximum(st - 1, 0)
    const2 = lambda st: (0, 0)
    single = pl.Buffered(1)
    bf16 = jnp.bfloat16
    return pl.pallas_call(
        functools.partial(_project_mix_kernel, seq_len=s),
        grid=(n_tok + 1,),
        in_specs=[
            pl.BlockSpec(memory_space=pltpu.SMEM),
            pl.BlockSpec((tm, d), lambda st: (cur_blk(st), 0)),
            pl.BlockSpec((1, d), const2),
            pl.BlockSpec((1, d), const2),
            pl.BlockSpec((1, d), const2),
            pl.BlockSpec(w_in_bf.shape, const2, pipeline_mode=single),
            pl.BlockSpec(wpool.shape, lambda st: (0, 0, 0), pipeline_mode=single),
            pl.BlockSpec((1, pw), const2),
            pl.BlockSpec(bias_tbl.shape, lambda st: (0, 0, 0), pipeline_mode=single),
        ],
        out_specs=[
            pl.BlockSpec((tm, ATTN_WIDTH), lambda st: (mix_blk(st), 0)),
            pl.BlockSpec((tm, pw), lambda st: (mix_blk(st), 0)),
            pl.BlockSpec((tm, d), lambda st: (cur_blk(st), 0)),
        ],
        out_shape=[
            jax.ShapeDtypeStruct((s, ATTN_WIDTH), bf16),
            jax.ShapeDtypeStruct((s, pw), bf16),
            jax.ShapeDtypeStruct((s, d), bf16),
        ],
        scratch_shapes=[
            pltpu.VMEM((2, tm, w_in_bf.shape[1]), bf16),
            pltpu.VMEM((2, BLOCK, 2 * KV_WIDTH), bf16),
            pltpu.VMEM((2, POOL_HALO, pw), bf16),
            pltpu.VMEM((tm + 2 * POOL_HALO, pw // N_POOL_GROUPS), jnp.float32),
        ],
        compiler_params=pltpu.CompilerParams(
            dimension_semantics=("arbitrary",), vmem_limit_bytes=VMEM_LIMIT_BYTES),
        name="project_mix",
    )(sink, x2, shift, scale, pre_g_row, w_in_bf, wpool, pool_scale_row, bias_tbl)


def _merge_out_kernel(ya_ref, yp_ref, gm_ref, x_ref, gate_ref, postg_ref, bm_ref,
                      wa_ref, wp_ref, wout_ref, o_ref, merged_ref, oacc_ref, inv_ref):
    s = pl.program_id(0)
    n_tok = pl.num_programs(0) - 1
    tm, d = x_ref.shape
    n_chunks = d // CW
    f32 = jnp.float32

    def finish_previous():
        scale = gate_ref[...] * postg_ref[...]
        inv = inv_ref[...]
        for c in range(n_chunks):
            lo, hi = c * CW, (c + 1) * CW
            o_ref[:, lo:hi] = x_ref[:, lo:hi] + (oacc_ref[:, lo:hi] * inv) * scale[:, lo:hi]

    def matmuls():
        half_bm = 0.5 * bm_ref[...]
        for c in range(n_chunks):
            lo, hi = c * CW, (c + 1) * CW
            bra = jnp.dot(ya_ref[...], wa_ref[:, lo:hi], preferred_element_type=f32)
            brp = jnp.dot(yp_ref[...], wp_ref[:, lo:hi], preferred_element_type=f32)
            g_a = 0.5 * jnp.tanh(gm_ref[:, lo:hi].astype(f32) + half_bm[:, lo:hi]) + 0.5
            g_p = 0.5 * jnp.tanh(gm_ref[:, d + lo:d + hi].astype(f32)
                                 + half_bm[:, d + lo:d + hi]) + 0.5
            merged_ref[:, lo:hi] = (g_a * bra + g_p * brp).astype(merged_ref.dtype)
        ssq = jnp.zeros((tm, 1), f32)
        for c in range(n_chunks):
            lo, hi = c * CW, (c + 1) * CW
            o = jnp.dot(merged_ref[...], wout_ref[:, lo:hi], preferred_element_type=f32)
            ssq = ssq + jnp.sum(o * o, axis=-1, keepdims=True)
            oacc_ref[:, lo:hi] = o
        inv_ref[...] = lax.rsqrt(ssq * (1.0 / d) + EPS)

    @pl.when(s == 0)
    def _():
        matmuls()

    @pl.when((s > 0) & (s < n_tok))
    def _():
        finish_previous()
        matmuls()

    @pl.when(s == n_tok)
    def _():
        finish_previous()


def _merge_out(ya, yp, gm, x2, gate, post_g_row, b_merge_row, wa, wp, wout):
    s, d = x2.shape
    tm = TM
    n_tok = s // tm
    aw, pw = ya.shape[1], yp.shape[1]
    cur = lambda st: (jnp.minimum(st, n_tok - 1), 0)
    prev = lambda st: (jnp.maximum(st - 1, 0), 0)
    const2 = lambda st: (0, 0)
    single = pl.Buffered(1)
    return pl.pallas_call(
        _merge_out_kernel,
        grid=(n_tok + 1,),
        in_specs=[
            pl.BlockSpec((tm, aw), cur),
            pl.BlockSpec((tm, pw), cur),
            pl.BlockSpec((tm, 2 * d), cur),
            pl.BlockSpec((tm, d), prev),
            pl.BlockSpec((1, d), const2),
            pl.BlockSpec((1, d), const2),
            pl.BlockSpec((1, 2 * d), const2),
            pl.BlockSpec((aw, d), const2, pipeline_mode=single),
            pl.BlockSpec((pw, d), const2, pipeline_mode=single),
            pl.BlockSpec((d, d), const2, pipeline_mode=single),
        ],
        out_specs=pl.BlockSpec((tm, d), prev),
        out_shape=jax.ShapeDtypeStruct((s, d), jnp.float32),
        scratch_shapes=[
            pltpu.VMEM((tm, d), jnp.bfloat16),
            pltpu.VMEM((tm, d), jnp.float32),
            pltpu.VMEM((tm, 1), jnp.float32),
        ],
        compiler_params=pltpu.CompilerParams(
            dimension_semantics=("arbitrary",), vmem_limit_bytes=VMEM_LIMIT_BYTES),
        name="merge_out",
    )(ya, yp, gm, x2, gate, post_g_row, b_merge_row, wa, wp, wout)


def _layer(x2, c, rel_table, bucket, w_ada, b_ada, pre_g, post_g, w_in, sink, w_pool, pool_scale,
           w_br_attn, w_br_pool, w_merge, b_merge, w_out):
    s, d = x2.shape
    c_col = c.reshape(d, 1)
    b_ada_row = b_ada.reshape(1, -1)
    pre_g_row = pre_g.reshape(1, d)
    shift, scale, bias_tbl = _shift_scale_bias(c_col, w_ada, b_ada_row, rel_table, bucket)
    bf16 = jnp.bfloat16
    ya, yp, h = _project_mix(x2, shift, scale, pre_g_row, w_in.astype(bf16), w_pool.astype(bf16),
                             pool_scale.reshape(1, -1), bias_tbl, sink)
    gm, wa, wp, wout, gate = _merge_gate(h, w_merge, w_br_attn, w_br_pool, w_out,
                                         c_col, w_ada, b_ada_row)
    return _merge_out(ya, yp, gm, x2, gate, post_g.reshape(1, d), b_merge.reshape(1, -1),
                      wa, wp, wout)


def kernel(x, c, rel_bias_table, w_ada, b_ada, pre_norm_g, post_norm_g, w_in, attn_sink,
           w_pool_group, pool_scale, w_branch_attn, w_branch_pool, w_merge, b_merge, w_out):
    batch, s, d = x.shape
    assert batch == 1, "kernel is written for a single sequence"
    depth = w_ada.shape[0]
    bucket = _bucket_index_table()
    x2 = x.reshape(s, d)
    for l in range(depth):
        x2 = _layer(x2, c[0], rel_bias_table, bucket, w_ada[l], b_ada[l], pre_norm_g[l],
                    post_norm_g[l], w_in[l], attn_sink[l], w_pool_group[l], pool_scale[l],
                    w_branch_attn[l], w_branch_pool[l], w_merge[l], b_merge[l], w_out[l])
    return x2.reshape(batch, s, d)
```

```python
import functools
import math

import numpy as np
import jax
import jax.numpy as jnp
from jax import lax
from jax.experimental import pallas as pl
from jax.experimental.pallas import tpu as pltpu

HEAD_DIM = 128
N_Q_HEADS = 8
N_KV_HEADS = 2
GQA_GROUP = N_Q_HEADS // N_KV_HEADS
ATTN_WIDTH = N_Q_HEADS * HEAD_DIM
KV_WIDTH = N_KV_HEADS * HEAD_DIM
WINDOW = 128
BLOCK = 128
SPAN = BLOCK + 2 * WINDOW
N_BUCKETS = 32
MAX_DISTANCE = 128
POOL_SIZES = (2, 4, 8, 16)
N_POOL_GROUPS = len(POOL_SIZES)
EPS = 1e-6
NEG_INF = -1e30
MASKED_BUCKET = N_BUCKETS
ATTN_SCALE = HEAD_DIM ** -0.5
INV_ATTN_SCALE = HEAD_DIM ** 0.5
EXP2_SCALE = ATTN_SCALE * math.log2(math.e)

SUBLANES = 8
BF16_SUBLANES = 16
VMEM_LIMIT_BYTES = 60 * 1024 * 1024

MOD_TK = 256
TM = 512
CW = 512
GATE_TM = 2048
GATE_ROWS = 512
GATE_TN = 1024
POOL_HALO = BF16_SUBLANES
W_STAGE_ROWS = 64

Q0 = 0
KV0 = Q0 + ATTN_WIDTH
GA0 = KV0 + 2 * KV_WIDTH
U0 = GA0 + ATTN_WIDTH


def _silu(v):
    return v * (1.0 / (1.0 + jnp.exp(-v)))


def _half_silu(hv):
    return hv + hv * jnp.tanh(hv)


def _adaln_rmsnorm(x, gain, shift):
    ms = jnp.mean(x * x, axis=-1, keepdims=True)
    return (x * lax.rsqrt(ms + EPS)) * gain + shift


def _shift_scale_bias_kernel(tbl_ref, c_ref, wsh_ref, wsc_ref, bsh_ref, bsc_ref, bucket_ref,
                             shift_ref, scale_ref, bias_ref):
    k = pl.program_id(0)

    @pl.when(k == 0)
    def _():
        shift_ref[...] = bsh_ref[...]
        scale_ref[...] = bsc_ref[...]

    s = _silu(c_ref[...])
    shift_ref[...] += jnp.sum(s * wsh_ref[...], axis=0, keepdims=True)
    scale_ref[...] += jnp.sum(s * wsc_ref[...], axis=0, keepdims=True)

    for t in range(SPAN // BLOCK):
        bk = bucket_ref[:, t * BLOCK:(t + 1) * BLOCK]

        def body(b, acc):
            return jnp.where(bk == b, tbl_ref[b, k] * INV_ATTN_SCALE, acc)
        bias_ref[0, :, t * BLOCK:(t + 1) * BLOCK] = lax.fori_loop(
            0, N_BUCKETS, body, jnp.full((BLOCK, BLOCK), NEG_INF, jnp.float32), unroll=8)


def _shift_scale_bias(c_col, w_ada, b_ada_row, rel_table, bucket):
    d = w_ada.shape[0]
    steps = d // MOD_TK
    assert steps == N_Q_HEADS, "one bias-table head per grid step"
    return pl.pallas_call(
        _shift_scale_bias_kernel,
        grid=(steps,),
        in_specs=[
            pl.BlockSpec(memory_space=pltpu.SMEM),
            pl.BlockSpec((MOD_TK, 1), lambda k: (k, 0)),
            pl.BlockSpec((MOD_TK, d), lambda k: (k, 0)),
            pl.BlockSpec((MOD_TK, d), lambda k: (k, 1)),
            pl.BlockSpec((1, d), lambda k: (0, 0)),
            pl.BlockSpec((1, d), lambda k: (0, 1)),
            pl.BlockSpec(bucket.shape, lambda k: (0, 0)),
        ],
        out_specs=[
            pl.BlockSpec((1, d), lambda k: (0, 0)),
            pl.BlockSpec((1, d), lambda k: (0, 0)),
            pl.BlockSpec((1, BLOCK, SPAN), lambda k: (k, 0, 0)),
        ],
        out_shape=[
            jax.ShapeDtypeStruct((1, d), jnp.float32),
            jax.ShapeDtypeStruct((1, d), jnp.float32),
            jax.ShapeDtypeStruct((N_Q_HEADS, BLOCK, SPAN), jnp.float32),
        ],
        compiler_params=pltpu.CompilerParams(
            dimension_semantics=("arbitrary",), vmem_limit_bytes=VMEM_LIMIT_BYTES),
        name="shift_scale_bias",
    )(rel_table, c_col, w_ada, w_ada, b_ada_row, b_ada_row, bucket)


def _t5_bucket(rel):
    half = N_BUCKETS // 2
    max_exact = half // 2
    assert (max_exact, MAX_DISTANCE // max_exact, half - max_exact) == (8, 16, 8)
    n = abs(rel)
    large = min(max_exact + (n * n).bit_length() - 7, half - 1)
    return (half if rel > 0 else 0) + (n if n < max_exact else large)


def _bucket_index_table():
    table = np.full((BLOCK, SPAN), MASKED_BUCKET, np.int32)
    for q in range(BLOCK):
        for t in range(SPAN):
            rel = t - WINDOW - q
            if abs(rel) <= WINDOW:
                table[q, t] = _t5_bucket(rel)
    return table


def _merge_gate_kernel(h_ref, w_ref, wa_ref, wp_ref, wout_ref, c_ref, wgate_ref, bgate_ref,
                       o_ref, wa_o_ref, wp_o_ref, wout_o_ref, gate_ref,
                       wbf_ref):
    j, i = pl.program_id(0), pl.program_id(1)

    @pl.when(i == 0)
    def _():
        wbf_ref[...] = (0.5 * w_ref[...]).astype(wbf_ref.dtype)

    @pl.when((j == 0) & (i == 0))
    def _():
        gate_ref[...] = bgate_ref[...]

    for r in range(0, h_ref.shape[0], GATE_ROWS):
        o_ref[r:r + GATE_ROWS] = jnp.dot(h_ref[r:r + GATE_ROWS], wbf_ref[...],
                                         preferred_element_type=jnp.float32).astype(o_ref.dtype)

    wa_o_ref[...] = wa_ref[...].astype(wa_o_ref.dtype)
    wp_o_ref[...] = wp_ref[...].astype(wp_o_ref.dtype)
    wout_o_ref[...] = wout_ref[...].astype(wout_o_ref.dtype)
    gate_ref[...] += jnp.sum(_silu(c_ref[...]) * wgate_ref[...], axis=0, keepdims=True)


def _merge_gate(h, w_merge, wa, wp, wout, c_col, w_ada, b_ada_row):
    s, d = h.shape
    n = w_merge.shape[1]
    n_i = s // GATE_TM
    steps = (n // GATE_TN) * n_i
    slab = lambda w: (w.shape[0] // steps, w.shape[1])
    step = lambda j, i: (j * n_i + i, 0)
    for w in (wa, wp, wout):
        assert w.shape[0] % (steps * BF16_SUBLANES) == 0
    assert d % (steps * SUBLANES) == 0
    const2 = lambda j, i: (0, 0)
    bf16 = jnp.bfloat16
    return pl.pallas_call(
        _merge_gate_kernel,
        grid=(n // GATE_TN, n_i),
        in_specs=[
            pl.BlockSpec((GATE_TM, d), lambda j, i: (i, 0)),
            pl.BlockSpec((d, GATE_TN), lambda j, i: (0, j)),
            pl.BlockSpec(slab(wa), step),
            pl.BlockSpec(slab(wp), step),
            pl.BlockSpec(slab(wout), step),
            pl.BlockSpec((d // steps, 1), step),
            pl.BlockSpec((d // steps, d), lambda j, i: (j * n_i + i, 2)),
            pl.BlockSpec((1, d), lambda j, i: (0, 2)),
        ],
        out_specs=[
            pl.BlockSpec((GATE_TM, GATE_TN), lambda j, i: (i, j)),
            pl.BlockSpec(slab(wa), step),
            pl.BlockSpec(slab(wp), step),
            pl.BlockSpec(slab(wout), step),
            pl.BlockSpec((1, d), const2),
        ],
        out_shape=[
            jax.ShapeDtypeStruct((s, n), bf16),
            jax.ShapeDtypeStruct(wa.shape, bf16),
            jax.ShapeDtypeStruct(wp.shape, bf16),
            jax.ShapeDtypeStruct(wout.shape, bf16),
            jax.ShapeDtypeStruct((1, d), jnp.float32),
        ],
        scratch_shapes=[pltpu.VMEM((d, GATE_TN), bf16)],
        compiler_params=pltpu.CompilerParams(
            dimension_semantics=("arbitrary", "arbitrary"), vmem_limit_bytes=VMEM_LIMIT_BYTES),
        name="merge_gate",
    )(h, w_merge, wa, wp, wout, c_col, w_ada, b_ada_row)


def _project_mix_kernel(sink_ref,
                        x_ref, shift_ref, scale_ref, g_ref, w_hbm_ref, wpool_ref, pscale_ref, bias_ref,
                        ya_ref, yp_ref, h_ref,
                        pbuf_ref, kvtail_ref, utail_ref, uext_ref, w_ref, wstage_ref, wsem_ref,
                        *, seq_len):
    s = pl.program_id(0)
    n_tok = pl.num_programs(0) - 1
    tm, d = x_ref.shape
    n_sub = tm // BLOCK
    n_blocks = seq_len // BLOCK
    aw = ATTN_WIDTH
    pw = pscale_ref.shape[1]
    q0, ga0, gp0 = Q0, GA0, U0 + pw
    n_chunks = w_ref.shape[1] // CW
    f32 = jnp.float32
    bf16 = jnp.bfloat16
    cur = s % 2
    prv = 1 - cur
    j = s - 1

    def load_weights():
        rows = wstage_ref.shape[1]
        n_slabs = w_hbm_ref.shape[0] // rows

        def slab_copy(k, slot):
            return pltpu.make_async_copy(w_hbm_ref.at[pl.ds(k * rows, rows)],
                                         wstage_ref.at[slot], wsem_ref.at[slot])

        slab_copy(0, 0).start()

        def body(k, carry):
            slot = k % 2

            @pl.when(k + 1 < n_slabs)
            def _():
                slab_copy(k + 1, 1 - slot).start()

            slab_copy(k, slot).wait()
            w_ref[pl.ds(pl.multiple_of(k * rows, rows), rows)] = wstage_ref[slot].astype(bf16)
            return carry

        lax.fori_loop(0, n_slabs, body, 0)

    def prologue():
        gain = g_ref[...] * (1.0 + scale_ref[...])
        h_ref[...] = _adaln_rmsnorm(x_ref[...], gain, shift_ref[...]).astype(bf16)

    def is_gate_col(col):
        return ga0 <= col < ga0 + aw or gp0 <= col < gp0 + pw

    def proj_chunk(c):
        lo, hi = c * CW, (c + 1) * CW
        p = jnp.dot(h_ref[...], w_ref[:, lo:hi], preferred_element_type=f32)
        if is_gate_col(lo):
            assert is_gate_col(hi - 1)
            p = 0.5 * p
        pbuf_ref[cur, :, lo:hi] = p.astype(bf16)

    def window_rows(b, col0):
        r0, r1 = (b - 1) * BLOCK, (b + 2) * BLOCK
        parts = []
        if r0 < 0:
            parts.append(kvtail_ref[prv, :, col0 - KV0:col0 - KV0 + HEAD_DIM])
            r0 = 0
        parts.append(pbuf_ref[prv, r0:min(r1, tm), col0:col0 + HEAD_DIM])
        if r1 > tm:
            parts.append(pbuf_ref[cur, 0:r1 - tm, col0:col0 + HEAD_DIM])
        return jnp.concatenate(parts, axis=0) if len(parts) > 1 else parts[0]

    def attn_scores(b, kvh):
        blk = j * n_sub + b
        h0 = kvh * GQA_GROUP
        k = window_rows(b, KV0 + kvh * HEAD_DIM)
        v = window_rows(b, KV0 + KV_WIDTH + kvh * HEAD_DIM)
        qs = jnp.concatenate(
            [pbuf_ref[prv, b * BLOCK:(b + 1) * BLOCK,
                      q0 + (h0 + g) * HEAD_DIM:q0 + (h0 + g + 1) * HEAD_DIM]
             for g in range(GQA_GROUP)], axis=0)
        z = lax.dot_general(qs, k, (((1,), (1,)), ((), ())), preferred_element_type=f32)
        z = z + bias_ref[h0:h0 + GQA_GROUP].reshape(GQA_GROUP * BLOCK, SPAN)
        col = lax.broadcasted_iota(jnp.int32, (1, SPAN), 1)
        if b == 0:
            z = jnp.where(col < jnp.where(blk == 0, WINDOW, 0), NEG_INF, z)
        if b == n_sub - 1:
            z = jnp.where(col >= jnp.where(blk == n_blocks - 1, WINDOW + BLOCK, SPAN), NEG_INF, z)
        sink = jnp.concatenate(
            [jnp.full((BLOCK, 1), sink_ref[h0 + g] * INV_ATTN_SCALE, f32)
             for g in range(GQA_GROUP)], axis=0)
        m = jnp.maximum(jnp.max(z, axis=-1, keepdims=True), sink)
        p = jnp.exp2((z - m) * EXP2_SCALE)
        denom = jnp.sum(p, axis=-1, keepdims=True) + jnp.exp2((sink - m) * EXP2_SCALE)
        return p.astype(bf16), denom, v

    def attn_out(p, denom, v):
        o = jnp.dot(p, v, preferred_element_type=f32) / denom
        return [o[g * BLOCK:(g + 1) * BLOCK] for g in range(GQA_GROUP)]

    def finish_sub_block(b, heads):
        y = jnp.concatenate(heads, axis=1)
        hg = pbuf_ref[prv, b * BLOCK:(b + 1) * BLOCK, ga0:ga0 + aw].astype(f32)
        ya_ref[b * BLOCK:(b + 1) * BLOCK] = (y * _half_silu(hg)).astype(bf16)

    ext = tm + 2 * POOL_HALO

    def pool_group(gi):
        w = POOL_SIZES[gi]
        gw = pw // N_POOL_GROUPS
        c0, c1 = gi * gw, (gi + 1) * gw
        half = w // 2
        halo_row = lax.broadcasted_iota(jnp.int32, (POOL_HALO, 1), 0)
        uext_ref[0:POOL_HALO] = jnp.where(j * tm - POOL_HALO + halo_row >= 0,
                                          utail_ref[prv, :, c0:c1].astype(f32), 0.0)
        uext_ref[POOL_HALO:POOL_HALO + tm] = pbuf_ref[prv, :, U0 + c0:U0 + c1].astype(f32)
        uext_ref[POOL_HALO + tm:ext] = jnp.where(
            (j + 1) * tm + halo_row < seq_len,
            pbuf_ref[cur, 0:POOL_HALO, U0 + c0:U0 + c1].astype(f32), 0.0)
        a = uext_ref[...]
        sh = 1
        while sh < w:
            a = a + pltpu.roll(a, ext - sh, axis=0)
            sh *= 2
        off = POOL_HALO - half
        if off % SUBLANES:
            a = pltpu.roll(a, ext - off, axis=0)
            off = 0
        pos = j * tm + lax.broadcasted_iota(jnp.int32, (tm, 1), 0)
        cnt = (jnp.minimum(pos + half, seq_len) - jnp.maximum(pos - half, 0)).astype(f32)
        pooled = a[off:off + tm] / cnt - uext_ref[POOL_HALO:POOL_HALO + tm]
        mixed = jnp.dot(pooled.astype(bf16), wpool_ref[gi], preferred_element_type=f32)
        hg = pbuf_ref[prv, :, gp0 + c0:gp0 + c1].astype(f32)
        yp_ref[:, c0:c1] = (mixed * pscale_ref[:, c0:c1] * _half_silu(hg)).astype(bf16)

    def save_tails():
        kvtail_ref[cur] = pbuf_ref[prv, tm - BLOCK:tm, KV0:KV0 + 2 * KV_WIDTH]
        utail_ref[cur] = pbuf_ref[prv, tm - POOL_HALO:tm, U0:U0 + pw]

    units = [(b, kvh) for b in range(n_sub) for kvh in range(N_KV_HEADS)]
    assert KV0 % CW == 0 and (2 * KV_WIDTH) % CW == 0 and U0 % CW == 0 and pw % CW == 0
    first = list(range(KV0 // CW, (KV0 + 2 * KV_WIDTH) // CW)) + list(range(U0 // CW, (U0 + pw) // CW))
    order = first + [c for c in range(n_chunks) if c not in first]
    u_chunks_done = len(first)
    assert len(units) + 1 <= n_chunks and u_chunks_done + N_POOL_GROUPS <= n_chunks

    def run(do_proj, do_mix):
        if do_proj:
            prologue()
        pending = attn_scores(*units[0]) if do_mix else None
        heads = []
        for pos, c in enumerate(order):
            if do_proj:
                proj_chunk(c)
            if not do_mix:
                continue
            if pos < len(units):
                heads += attn_out(*pending)
                if pos + 1 < len(units):
                    pending = attn_scores(*units[pos + 1])
                b, kvh = units[pos]
                if kvh == N_KV_HEADS - 1:
                    finish_sub_block(b, heads)
                    heads = []
            if u_chunks_done <= pos < u_chunks_done + N_POOL_GROUPS:
                pool_group(pos - u_chunks_done)
        if do_mix:
            save_tails()

    @pl.when(s == 0)
    def _():
        kvtail_ref[0] = jnp.zeros(kvtail_ref.shape[1:], kvtail_ref.dtype)
        utail_ref[0] = jnp.zeros(utail_ref.shape[1:], utail_ref.dtype)
        load_weights()
        run(True, False)

    @pl.when((s > 0) & (s < n_tok))
    def _():
        run(True, True)

    @pl.when(s == n_tok)
    def _():
        run(False, True)


def _project_mix(x2, shift, scale, pre_g_row, w_in, wpool, pool_scale_row, bias_tbl, sink):
    s, d = x2.shape
    tm = TM
    n_tok = s // tm
    pw = pool_scale_row.shape[1]
    n_in = w_in.shape[1]
    assert s // BLOCK >= 2, "first and last attention block must differ"
    assert n_in == U0 + 2 * pw and w_in.shape[0] % W_STAGE_ROWS == 0
    cur_blk = lambda st: jnp.minimum(st, n_tok - 1)
    mix_blk = lambda st: jnp.maximum(st - 1, 0)
    const2 = lambda st: (0, 0)
    single = pl.Buffered(1)
    bf16 = jnp.bfloat16
    return pl.pallas_call(
        functools.partial(_project_mix_kernel, seq_len=s),
        grid=(n_tok + 1,),
        in_specs=[
            pl.BlockSpec(memory_space=pltpu.SMEM),
            pl.BlockSpec((tm, d), lambda st: (cur_blk(st), 0)),
            pl.BlockSpec((1, d), const2),
            pl.BlockSpec((1, d), const2),
            pl.BlockSpec((1, d), const2),
            pl.BlockSpec(memory_space=pl.ANY),
            pl.BlockSpec(wpool.shape, lambda st: (0, 0, 0), pipeline_mode=single),
            pl.BlockSpec((1, pw), const2),
            pl.BlockSpec(bias_tbl.shape, lambda st: (0, 0, 0), pipeline_mode=single),
        ],
        out_specs=[
            pl.BlockSpec((tm, ATTN_WIDTH), lambda st: (mix_blk(st), 0)),
            pl.BlockSpec((tm, pw), lambda st: (mix_blk(st), 0)),
            pl.BlockSpec((tm, d), lambda st: (cur_blk(st), 0)),
        ],
        out_shape=[
            jax.ShapeDtypeStruct((s, ATTN_WIDTH), bf16),
            jax.ShapeDtypeStruct((s, pw), bf16),
            jax.ShapeDtypeStruct((s, d), bf16),
        ],
        scratch_shapes=[
            pltpu.VMEM((2, tm, n_in), bf16),
            pltpu.VMEM((2, BLOCK, 2 * KV_WIDTH), bf16),
            pltpu.VMEM((2, POOL_HALO, pw), bf16),
            pltpu.VMEM((tm + 2 * POOL_HALO, pw // N_POOL_GROUPS), jnp.float32),
            pltpu.VMEM(w_in.shape, bf16),
            pltpu.VMEM((2, W_STAGE_ROWS, n_in), jnp.float32),
            pltpu.SemaphoreType.DMA((2,)),
        ],
        compiler_params=pltpu.CompilerParams(
            dimension_semantics=("arbitrary",), vmem_limit_bytes=VMEM_LIMIT_BYTES),
        name="project_mix",
    )(sink, x2, shift, scale, pre_g_row, w_in, wpool, pool_scale_row, bias_tbl)


def _merge_out_kernel(ya_ref, yp_ref, gm_ref, x_ref, gate_ref, postg_ref, bm_ref,
                      wa_ref, wp_ref, wout_ref, o_ref, merged_ref, oacc_ref, inv_ref):
    s = pl.program_id(0)
    n_tok = pl.num_programs(0) - 1
    tm, d = x_ref.shape
    n_chunks = d // CW
    f32 = jnp.float32

    def finish_previous():
        scale = gate_ref[...] * postg_ref[...]
        inv = inv_ref[...]
        for c in range(n_chunks):
            lo, hi = c * CW, (c + 1) * CW
            o_ref[:, lo:hi] = x_ref[:, lo:hi] + (oacc_ref[:, lo:hi] * inv) * scale[:, lo:hi]

    def matmuls():
        half_bm = 0.5 * bm_ref[...]
        for c in range(n_chunks):
            lo, hi = c * CW, (c + 1) * CW
            bra = jnp.dot(ya_ref[...], wa_ref[:, lo:hi], preferred_element_type=f32)
            brp = jnp.dot(yp_ref[...], wp_ref[:, lo:hi], preferred_element_type=f32)
            g_a = 0.5 * jnp.tanh(gm_ref[:, lo:hi].astype(f32) + half_bm[:, lo:hi]) + 0.5
            g_p = 0.5 * jnp.tanh(gm_ref[:, d + lo:d + hi].astype(f32)
                                 + half_bm[:, d + lo:d + hi]) + 0.5
            merged_ref[:, lo:hi] = (g_a * bra + g_p * brp).astype(merged_ref.dtype)
        ssq = jnp.zeros((tm, 1), f32)
        for c in range(n_chunks):
            lo, hi = c * CW, (c + 1) * CW
            o = jnp.dot(merged_ref[...], wout_ref[:, lo:hi], preferred_element_type=f32)
            ssq = ssq + jnp.sum(o * o, axis=-1, keepdims=True)
            oacc_ref[:, lo:hi] = o
        inv_ref[...] = lax.rsqrt(ssq * (1.0 / d) + EPS)

    @pl.when(s == 0)
    def _():
        matmuls()

    @pl.when((s > 0) & (s < n_tok))
    def _():
        finish_previous()
        matmuls()

    @pl.when(s == n_tok)
    def _():
        finish_previous()


def _merge_out(ya, yp, gm, x2, gate, post_g_row, b_merge_row, wa, wp, wout):
    s, d = x2.shape
    tm = TM
    n_tok = s // tm
    aw, pw = ya.shape[1], yp.shape[1]
    cur = lambda st: (jnp.minimum(st, n_tok - 1), 0)
    prev = lambda st: (jnp.maximum(st - 1, 0), 0)
    const2 = lambda st: (0, 0)
    single = pl.Buffered(1)
    return pl.pallas_call(
        _merge_out_kernel,
        grid=(n_tok + 1,),
        in_specs=[
            pl.BlockSpec((tm, aw), cur),
            pl.BlockSpec((tm, pw), cur),
            pl.BlockSpec((tm, 2 * d), cur),
            pl.BlockSpec((tm, d), prev),
            pl.BlockSpec((1, d), const2),
            pl.BlockSpec((1, d), const2),
            pl.BlockSpec((1, 2 * d), const2),
            pl.BlockSpec((aw, d), const2, pipeline_mode=single),
            pl.BlockSpec((pw, d), const2, pipeline_mode=single),
            pl.BlockSpec((d, d), const2, pipeline_mode=single),
        ],
        out_specs=pl.BlockSpec((tm, d), prev),
        out_shape=jax.ShapeDtypeStruct((s, d), jnp.float32),
        scratch_shapes=[
            pltpu.VMEM((tm, d), jnp.bfloat16),
            pltpu.VMEM((tm, d), jnp.float32),
            pltpu.VMEM((tm, 1), jnp.float32),
        ],
        compiler_params=pltpu.CompilerParams(
            dimension_semantics=("arbitrary",), vmem_limit_bytes=VMEM_LIMIT_BYTES),
        name="merge_out",
    )(ya, yp, gm, x2, gate, post_g_row, b_merge_row, wa, wp, wout)


def _layer(x2, c, rel_table, bucket, w_ada, b_ada, pre_g, post_g, w_in, sink, w_pool, pool_scale,
           w_br_attn, w_br_pool, w_merge, b_merge, w_out):
    s, d = x2.shape
    c_col = c.reshape(d, 1)
    b_ada_row = b_ada.reshape(1, -1)
    pre_g_row = pre_g.reshape(1, d)
    shift, scale, bias_tbl = _shift_scale_bias(c_col, w_ada, b_ada_row, rel_table, bucket)
    bf16 = jnp.bfloat16
    ya, yp, h = _project_mix(x2, shift, scale, pre_g_row, w_in, w_pool.astype(bf16),
                             pool_scale.reshape(1, -1), bias_tbl, sink)
    gm, wa, wp, wout, gate = _merge_gate(h, w_merge, w_br_attn, w_br_pool, w_out,
                                         c_col, w_ada, b_ada_row)
    return _merge_out(ya, yp, gm, x2, gate, post_g.reshape(1, d), b_merge.reshape(1, -1),
                      wa, wp, wout)


def kernel(x, c, rel_bias_table, w_ada, b_ada, pre_norm_g, post_norm_g, w_in, attn_sink,
           w_pool_group, pool_scale, w_branch_attn, w_branch_pool, w_merge, b_merge, w_out):
    batch, s, d = x.shape
    assert batch == 1, "kernel is written for a single sequence"
    depth = w_ada.shape[0]
    bucket = _bucket_index_table()
    x2 = x.reshape(s, d)
    for l in range(depth):
        x2 = _layer(x2, c[0], rel_bias_table, bucket, w_ada[l], b_ada[l], pre_norm_g[l],
                    post_norm_g[l], w_in[l], attn_sink[l], w_pool_group[l], pool_scale[l],
                    w_branch_attn[l], w_branch_pool[l], w_merge[l], b_merge[l], w_out[l])
    return x2.reshape(batch, s, d)
```

```python
import functools
import math

import numpy as np
import jax
import jax.numpy as jnp
from jax import lax
from jax.experimental import pallas as pl
from jax.experimental.pallas import tpu as pltpu

HEAD_DIM = 128
N_Q_HEADS = 8
N_KV_HEADS = 2
GQA_GROUP = N_Q_HEADS // N_KV_HEADS
ATTN_WIDTH = N_Q_HEADS * HEAD_DIM
KV_WIDTH = N_KV_HEADS * HEAD_DIM
WINDOW = 128
BLOCK = 128
SPAN = BLOCK + 2 * WINDOW
N_BUCKETS = 32
MAX_DISTANCE = 128
POOL_SIZES = (2, 4, 8, 16)
N_POOL_GROUPS = len(POOL_SIZES)
EPS = 1e-6
NEG_INF = -1e30
MASKED_BUCKET = N_BUCKETS
ATTN_SCALE = HEAD_DIM ** -0.5
INV_ATTN_SCALE = HEAD_DIM ** 0.5
EXP2_SCALE = ATTN_SCALE * math.log2(math.e)

SUBLANES = 8
BF16_SUBLANES = 16
VMEM_LIMIT_BYTES = 60 * 1024 * 1024

MOD_TK = 256
TM = 512
CW = 512
GATE_TM = 2048
GATE_ROWS = 512
GATE_TN = 1024
POOL_HALO = BF16_SUBLANES
W_STAGE_ROWS = BF16_SUBLANES
W_STAGE_SLOTS = 8

Q0 = 0
KV0 = Q0 + ATTN_WIDTH
GA0 = KV0 + 2 * KV_WIDTH
U0 = GA0 + ATTN_WIDTH


def _silu(v):
    return v * (1.0 / (1.0 + jnp.exp(-v)))


def _half_silu(hv):
    return hv + hv * jnp.tanh(hv)


def _silu_column(c_row):
    n = c_row.shape[1]
    on_diag = (lax.broadcasted_iota(jnp.int32, (n, n), 0)
               == lax.broadcasted_iota(jnp.int32, (n, n), 1))
    return jnp.sum(jnp.where(on_diag, _silu(c_row), 0.0), axis=1, keepdims=True)


def _adaln_rmsnorm(x, gain, shift):
    ms = jnp.mean(x * x, axis=-1, keepdims=True)
    return (x * lax.rsqrt(ms + EPS)) * gain + shift


def _shift_scale_bias_kernel(tbl_ref, c_ref, wsh_ref, wsc_ref, bsh_ref, bsc_ref, bucket_ref,
                             shift_ref, scale_ref, bias_ref, *, tile_buckets):
    k = pl.program_id(0)

    @pl.when(k == 0)
    def _():
        shift_ref[...] = bsh_ref[...]
        scale_ref[...] = bsc_ref[...]

    s = _silu_column(c_ref[...])
    shift_ref[...] += jnp.sum(s * wsh_ref[...], axis=0, keepdims=True)
    scale_ref[...] += jnp.sum(s * wsc_ref[...], axis=0, keepdims=True)

    for t, buckets in enumerate(tile_buckets):
        bk = bucket_ref[:, t * BLOCK:(t + 1) * BLOCK]
        acc = jnp.full((BLOCK, BLOCK), NEG_INF, jnp.float32)
        for b in buckets:
            acc = jnp.where(bk == b, tbl_ref[b, k] * INV_ATTN_SCALE, acc)
        bias_ref[0, :, t * BLOCK:(t + 1) * BLOCK] = acc


def _shift_scale_bias(c_row, w_ada, b_ada_row, rel_table, bucket):
    d = w_ada.shape[0]
    steps = d // MOD_TK
    assert steps == N_Q_HEADS, "one bias-table head per grid step"
    tile_buckets = tuple(
        tuple(int(b) for b in np.unique(bucket[:, t * BLOCK:(t + 1) * BLOCK]) if b != MASKED_BUCKET)
        for t in range(SPAN // BLOCK))
    return pl.pallas_call(
        functools.partial(_shift_scale_bias_kernel, tile_buckets=tile_buckets),
        grid=(steps,),
        in_specs=[
            pl.BlockSpec(memory_space=pltpu.SMEM),
            pl.BlockSpec((1, MOD_TK), lambda k: (0, k)),
            pl.BlockSpec((MOD_TK, d), lambda k: (k, 0)),
            pl.BlockSpec((MOD_TK, d), lambda k: (k, 1)),
            pl.BlockSpec((1, d), lambda k: (0, 0)),
            pl.BlockSpec((1, d), lambda k: (0, 1)),
            pl.BlockSpec(bucket.shape, lambda k: (0, 0)),
        ],
        out_specs=[
            pl.BlockSpec((1, d), lambda k: (0, 0)),
            pl.BlockSpec((1, d), lambda k: (0, 0)),
            pl.BlockSpec((1, BLOCK, SPAN), lambda k: (k, 0, 0)),
        ],
        out_shape=[
            jax.ShapeDtypeStruct((1, d), jnp.float32),
            jax.ShapeDtypeStruct((1, d), jnp.float32),
            jax.ShapeDtypeStruct((N_Q_HEADS, BLOCK, SPAN), jnp.float32),
        ],
        compiler_params=pltpu.CompilerParams(
            dimension_semantics=("arbitrary",), vmem_limit_bytes=VMEM_LIMIT_BYTES),
        name="shift_scale_bias",
    )(rel_table, c_row, w_ada, w_ada, b_ada_row, b_ada_row, bucket)


def _t5_bucket(rel):
    half = N_BUCKETS // 2
    max_exact = half // 2
    assert (max_exact, MAX_DISTANCE // max_exact, half - max_exact) == (8, 16, 8)
    n = abs(rel)
    large = min(max_exact + (n * n).bit_length() - 7, half - 1)
    return (half if rel > 0 else 0) + (n if n < max_exact else large)


def _bucket_index_table():
    table = np.full((BLOCK, SPAN), MASKED_BUCKET, np.int32)
    for q in range(BLOCK):
        for t in range(SPAN):
            rel = t - WINDOW - q
            if abs(rel) <= WINDOW:
                table[q, t] = _t5_bucket(rel)
    return table


def _merge_gate_kernel(h_ref, w_ref, wa_ref, wp_ref, wout_ref, c_ref, wgate_ref, bgate_ref,
                       o_ref, wa_o_ref, wp_o_ref, wout_o_ref, gate_ref,
                       wbf_ref):
    j, i = pl.program_id(0), pl.program_id(1)

    @pl.when(i == 0)
    def _():
        wbf_ref[...] = (0.5 * w_ref[...]).astype(wbf_ref.dtype)

    @pl.when((j == 0) & (i == 0))
    def _():
        gate_ref[...] = bgate_ref[...]

    for r in range(0, h_ref.shape[0], GATE_ROWS):
        o_ref[r:r + GATE_ROWS] = jnp.dot(h_ref[r:r + GATE_ROWS], wbf_ref[...],
                                         preferred_element_type=jnp.float32).astype(o_ref.dtype)

    wa_o_ref[...] = wa_ref[...].astype(wa_o_ref.dtype)
    wp_o_ref[...] = wp_ref[...].astype(wp_o_ref.dtype)
    wout_o_ref[...] = wout_ref[...].astype(wout_o_ref.dtype)
    gate_ref[...] += jnp.sum(_silu_column(c_ref[...]) * wgate_ref[...], axis=0, keepdims=True)


def _merge_gate(h, w_merge, wa, wp, wout, c_row, w_ada, b_ada_row):
    s, d = h.shape
    n = w_merge.shape[1]
    n_i = s // GATE_TM
    steps = (n // GATE_TN) * n_i
    slab = lambda w: (w.shape[0] // steps, w.shape[1])
    step = lambda j, i: (j * n_i + i, 0)
    for w in (wa, wp, wout):
        assert w.shape[0] % (steps * BF16_SUBLANES) == 0
    assert d % (steps * SUBLANES) == 0
    const2 = lambda j, i: (0, 0)
    bf16 = jnp.bfloat16
    return pl.pallas_call(
        _merge_gate_kernel,
        grid=(n // GATE_TN, n_i),
        in_specs=[
            pl.BlockSpec((GATE_TM, d), lambda j, i: (i, 0)),
            pl.BlockSpec((d, GATE_TN), lambda j, i: (0, j)),
            pl.BlockSpec(slab(wa), step),
            pl.BlockSpec(slab(wp), step),
            pl.BlockSpec(slab(wout), step),
            pl.BlockSpec((1, d // steps), lambda j, i: (0, j * n_i + i)),
            pl.BlockSpec((d // steps, d), lambda j, i: (j * n_i + i, 2)),
            pl.BlockSpec((1, d), lambda j, i: (0, 2)),
        ],
        out_specs=[
            pl.BlockSpec((GATE_TM, GATE_TN), lambda j, i: (i, j)),
            pl.BlockSpec(slab(wa), step),
            pl.BlockSpec(slab(wp), step),
            pl.BlockSpec(slab(wout), step),
            pl.BlockSpec((1, d), const2),
        ],
        out_shape=[
            jax.ShapeDtypeStruct((s, n), bf16),
            jax.ShapeDtypeStruct(wa.shape, bf16),
            jax.ShapeDtypeStruct(wp.shape, bf16),
            jax.ShapeDtypeStruct(wout.shape, bf16),
            jax.ShapeDtypeStruct((1, d), jnp.float32),
        ],
        scratch_shapes=[pltpu.VMEM((d, GATE_TN), bf16)],
        compiler_params=pltpu.CompilerParams(
            dimension_semantics=("arbitrary", "arbitrary"), vmem_limit_bytes=VMEM_LIMIT_BYTES),
        name="merge_gate",
    )(h, w_merge, wa, wp, wout, c_row, w_ada, b_ada_row)


def _project_mix_kernel(sink_ref,
                        x_ref, shift_ref, scale_ref, g_ref, w_hbm_ref, wpool_ref, pscale_ref, bias_ref,
                        ya_ref, yp_ref, h_ref,
                        pbuf_ref, kvtail_ref, utail_ref, uext_ref, w_ref, wstage_ref, wsem_ref,
                        *, seq_len):
    s = pl.program_id(0)
    n_tok = pl.num_programs(0) - 1
    tm, d = x_ref.shape
    n_sub = tm // BLOCK
    n_blocks = seq_len // BLOCK
    aw = ATTN_WIDTH
    pw = pscale_ref.shape[1]
    q0, ga0, gp0 = Q0, GA0, U0 + pw
    n_chunks = w_ref.shape[1] // CW
    f32 = jnp.float32
    bf16 = jnp.bfloat16
    cur = s % 2
    prv = 1 - cur
    j = s - 1

    def load_weights():
        n_slots, rows = wstage_ref.shape[0], wstage_ref.shape[1]
        n_slabs = w_hbm_ref.shape[0] // rows
        ahead = n_slots - 1

        def slab_copy(k, slot):
            return pltpu.make_async_copy(w_hbm_ref.at[pl.ds(k * rows, rows)],
                                         wstage_ref.at[slot], wsem_ref.at[slot])

        for k in range(ahead):
            slab_copy(k, k).start()

        def body(k, carry):
            slot = k % n_slots

            @pl.when(k + ahead < n_slabs)
            def _():
                slab_copy(k + ahead, (k + ahead) % n_slots).start()

            slab_copy(k, slot).wait()
            w_ref[pl.ds(pl.multiple_of(k * rows, rows), rows)] = wstage_ref[slot].astype(bf16)
            return carry

        lax.fori_loop(0, n_slabs, body, 0)

    def prologue():
        gain = g_ref[...] * (1.0 + scale_ref[...])
        h_ref[...] = _adaln_rmsnorm(x_ref[...], gain, shift_ref[...]).astype(bf16)

    def is_gate_col(col):
        return ga0 <= col < ga0 + aw or gp0 <= col < gp0 + pw

    def proj_chunk(c):
        lo, hi = c * CW, (c + 1) * CW
        p = jnp.dot(h_ref[...], w_ref[:, lo:hi], preferred_element_type=f32)
        if is_gate_col(lo):
            assert is_gate_col(hi - 1)
            p = 0.5 * p
        pbuf_ref[cur, :, lo:hi] = p.astype(bf16)

    def window_rows(b, col0):
        r0, r1 = (b - 1) * BLOCK, (b + 2) * BLOCK
        parts = []
        if r0 < 0:
            parts.append(kvtail_ref[prv, :, col0 - KV0:col0 - KV0 + HEAD_DIM])
            r0 = 0
        parts.append(pbuf_ref[prv, r0:min(r1, tm), col0:col0 + HEAD_DIM])
        if r1 > tm:
            parts.append(pbuf_ref[cur, 0:r1 - tm, col0:col0 + HEAD_DIM])
        return jnp.concatenate(parts, axis=0) if len(parts) > 1 else parts[0]

    def attn_scores(b, kvh):
        blk = j * n_sub + b
        h0 = kvh * GQA_GROUP
        k = window_rows(b, KV0 + kvh * HEAD_DIM)
        v = window_rows(b, KV0 + KV_WIDTH + kvh * HEAD_DIM)
        qs = jnp.concatenate(
            [pbuf_ref[prv, b * BLOCK:(b + 1) * BLOCK,
                      q0 + (h0 + g) * HEAD_DIM:q0 + (h0 + g + 1) * HEAD_DIM]
             for g in range(GQA_GROUP)], axis=0)
        z = lax.dot_general(qs, k, (((1,), (1,)), ((), ())), preferred_element_type=f32)
        z = z + bias_ref[h0:h0 + GQA_GROUP].reshape(GQA_GROUP * BLOCK, SPAN)
        col = lax.broadcasted_iota(jnp.int32, (1, SPAN), 1)
        if b == 0:
            z = jnp.where(col < jnp.where(blk == 0, WINDOW, 0), NEG_INF, z)
        if b == n_sub - 1:
            z = jnp.where(col >= jnp.where(blk == n_blocks - 1, WINDOW + BLOCK, SPAN), NEG_INF, z)
        sink = jnp.concatenate(
            [jnp.full((BLOCK, 1), sink_ref[h0 + g] * INV_ATTN_SCALE, f32)
             for g in range(GQA_GROUP)], axis=0)
        m = jnp.maximum(jnp.max(z, axis=-1, keepdims=True), sink)
        p = jnp.exp2((z - m) * EXP2_SCALE)
        denom = jnp.sum(p, axis=-1, keepdims=True) + jnp.exp2((sink - m) * EXP2_SCALE)
        return p.astype(bf16), denom, v

    def attn_out(p, denom, v):
        o = jnp.dot(p, v, preferred_element_type=f32) / denom
        return [o[g * BLOCK:(g + 1) * BLOCK] for g in range(GQA_GROUP)]

    def finish_sub_block(b, heads):
        y = jnp.concatenate(heads, axis=1)
        hg = pbuf_ref[prv, b * BLOCK:(b + 1) * BLOCK, ga0:ga0 + aw].astype(f32)
        ya_ref[b * BLOCK:(b + 1) * BLOCK] = (y * _half_silu(hg)).astype(bf16)

    ext = tm + 2 * POOL_HALO

    def pool_group(gi):
        w = POOL_SIZES[gi]
        gw = pw // N_POOL_GROUPS
        c0, c1 = gi * gw, (gi + 1) * gw
        half = w // 2
        halo_row = lax.broadcasted_iota(jnp.int32, (POOL_HALO, 1), 0)
        uext_ref[0:POOL_HALO] = jnp.where(j * tm - POOL_HALO + halo_row >= 0,
                                          utail_ref[prv, :, c0:c1].astype(f32), 0.0)
        uext_ref[POOL_HALO:POOL_HALO + tm] = pbuf_ref[prv, :, U0 + c0:U0 + c1].astype(f32)
        uext_ref[POOL_HALO + tm:ext] = jnp.where(
            (j + 1) * tm + halo_row < seq_len,
            pbuf_ref[cur, 0:POOL_HALO, U0 + c0:U0 + c1].astype(f32), 0.0)
        a = uext_ref[...]
        sh = 1
        while sh < w:
            a = a + pltpu.roll(a, ext - sh, axis=0)
            sh *= 2
        off = POOL_HALO - half
        if off % SUBLANES:
            a = pltpu.roll(a, ext - off, axis=0)
            off = 0
        pos = j * tm + lax.broadcasted_iota(jnp.int32, (tm, 1), 0)
        cnt = (jnp.minimum(pos + half, seq_len) - jnp.maximum(pos - half, 0)).astype(f32)
        pooled = a[off:off + tm] / cnt - uext_ref[POOL_HALO:POOL_HALO + tm]
        mixed = jnp.dot(pooled.astype(bf16), wpool_ref[gi], preferred_element_type=f32)
        hg = pbuf_ref[prv, :, gp0 + c0:gp0 + c1].astype(f32)
        yp_ref[:, c0:c1] = (mixed * pscale_ref[:, c0:c1] * _half_silu(hg)).astype(bf16)

    def save_tails():
        kvtail_ref[cur] = pbuf_ref[prv, tm - BLOCK:tm, KV0:KV0 + 2 * KV_WIDTH]
        utail_ref[cur] = pbuf_ref[prv, tm - POOL_HALO:tm, U0:U0 + pw]

    units = [(b, kvh) for b in range(n_sub) for kvh in range(N_KV_HEADS)]
    assert KV0 % CW == 0 and (2 * KV_WIDTH) % CW == 0 and U0 % CW == 0 and pw % CW == 0
    first = list(range(KV0 // CW, (KV0 + 2 * KV_WIDTH) // CW)) + list(range(U0 // CW, (U0 + pw) // CW))
    order = first + [c for c in range(n_chunks) if c not in first]
    u_chunks_done = len(first)
    assert len(units) + 1 <= n_chunks and u_chunks_done + N_POOL_GROUPS <= n_chunks

    def run(do_proj, do_mix):
        if do_proj:
            prologue()
        pending = attn_scores(*units[0]) if do_mix else None
        heads = []
        for pos, c in enumerate(order):
            if do_proj:
                proj_chunk(c)
            if not do_mix:
                continue
            if pos < len(units):
                heads += attn_out(*pending)
                if pos + 1 < len(units):
                    pending = attn_scores(*units[pos + 1])
                b, kvh = units[pos]
                if kvh == N_KV_HEADS - 1:
                    finish_sub_block(b, heads)
                    heads = []
            if u_chunks_done <= pos < u_chunks_done + N_POOL_GROUPS:
                pool_group(pos - u_chunks_done)
        if do_mix:
            save_tails()

    @pl.when(s == 0)
    def _():
        kvtail_ref[0] = jnp.zeros(kvtail_ref.shape[1:], kvtail_ref.dtype)
        utail_ref[0] = jnp.zeros(utail_ref.shape[1:], utail_ref.dtype)
        load_weights()
        run(True, False)

    @pl.when((s > 0) & (s < n_tok))
    def _():
        run(True, True)

    @pl.when(s == n_tok)
    def _():
        run(False, True)


def _project_mix(x2, shift, scale, pre_g_row, w_in, wpool, pool_scale_row, bias_tbl, sink):
    s, d = x2.shape
    tm = TM
    n_tok = s // tm
    pw = pool_scale_row.shape[1]
    n_in = w_in.shape[1]
    assert s // BLOCK >= 2, "first and last attention block must differ"
    assert n_in == U0 + 2 * pw and w_in.shape[0] % W_STAGE_ROWS == 0
    cur_blk = lambda st: jnp.minimum(st, n_tok - 1)
    mix_blk = lambda st: jnp.maximum(st - 1, 0)
    const2 = lambda st: (0, 0)
    single = pl.Buffered(1)
    bf16 = jnp.bfloat16
    return pl.pallas_call(
        functools.partial(_project_mix_kernel, seq_len=s),
        grid=(n_tok + 1,),
        in_specs=[
            pl.BlockSpec(memory_space=pltpu.SMEM),
            pl.BlockSpec((tm, d), lambda st: (cur_blk(st), 0)),
            pl.BlockSpec((1, d), const2),
            pl.BlockSpec((1, d), const2),
            pl.BlockSpec((1, d), const2),
            pl.BlockSpec(memory_space=pl.ANY),
            pl.BlockSpec(wpool.shape, lambda st: (0, 0, 0), pipeline_mode=single),
            pl.BlockSpec((1, pw), const2),
            pl.BlockSpec(bias_tbl.shape, lambda st: (0, 0, 0), pipeline_mode=single),
        ],
        out_specs=[
            pl.BlockSpec((tm, ATTN_WIDTH), lambda st: (mix_blk(st), 0)),
            pl.BlockSpec((tm, pw), lambda st: (mix_blk(st), 0)),
            pl.BlockSpec((tm, d), lambda st: (cur_blk(st), 0)),
        ],
        out_shape=[
            jax.ShapeDtypeStruct((s, ATTN_WIDTH), bf16),
            jax.ShapeDtypeStruct((s, pw), bf16),
            jax.ShapeDtypeStruct((s, d), bf16),
        ],
        scratch_shapes=[
            pltpu.VMEM((2, tm, n_in), bf16),
            pltpu.VMEM((2, BLOCK, 2 * KV_WIDTH), bf16),
            pltpu.VMEM((2, POOL_HALO, pw), bf16),
            pltpu.VMEM((tm + 2 * POOL_HALO, pw // N_POOL_GROUPS), jnp.float32),
            pltpu.VMEM(w_in.shape, bf16),
            pltpu.VMEM((W_STAGE_SLOTS, W_STAGE_ROWS, n_in), jnp.float32),
            pltpu.SemaphoreType.DMA((W_STAGE_SLOTS,)),
        ],
        compiler_params=pltpu.CompilerParams(
            dimension_semantics=("arbitrary",), vmem_limit_bytes=VMEM_LIMIT_BYTES),
        name="project_mix",
    )(sink, x2, shift, scale, pre_g_row, w_in, wpool, pool_scale_row, bias_tbl)


def _merge_out_kernel(ya_ref, yp_ref, gm_ref, x_ref, gate_ref, postg_ref, bm_ref,
                      wa_ref, wp_ref, wout_ref, o_ref, merged_ref, oacc_ref, inv_ref):
    s = pl.program_id(0)
    n_tok = pl.num_programs(0) - 1
    tm, d = x_ref.shape
    n_chunks = d // CW
    f32 = jnp.float32

    def finish_previous():
        scale = gate_ref[...] * postg_ref[...]
        inv = inv_ref[...]
        for c in range(n_chunks):
            lo, hi = c * CW, (c + 1) * CW
            o_ref[:, lo:hi] = x_ref[:, lo:hi] + (oacc_ref[:, lo:hi] * inv) * scale[:, lo:hi]

    def matmuls():
        half_bm = 0.5 * bm_ref[...]
        for c in range(n_chunks):
            lo, hi = c * CW, (c + 1) * CW
            bra = jnp.dot(ya_ref[...], wa_ref[:, lo:hi], preferred_element_type=f32)
            brp = jnp.dot(yp_ref[...], wp_ref[:, lo:hi], preferred_element_type=f32)
            g_a = 0.5 * jnp.tanh(gm_ref[:, lo:hi].astype(f32) + half_bm[:, lo:hi]) + 0.5
            g_p = 0.5 * jnp.tanh(gm_ref[:, d + lo:d + hi].astype(f32)
                                 + half_bm[:, d + lo:d + hi]) + 0.5
            merged_ref[:, lo:hi] = (g_a * bra + g_p * brp).astype(merged_ref.dtype)
        ssq = jnp.zeros((tm, 1), f32)
        for c in range(n_chunks):
            lo, hi = c * CW, (c + 1) * CW
            o = jnp.dot(merged_ref[...], wout_ref[:, lo:hi], preferred_element_type=f32)
            ssq = ssq + jnp.sum(o * o, axis=-1, keepdims=True)
            oacc_ref[:, lo:hi] = o
        inv_ref[...] = lax.rsqrt(ssq * (1.0 / d) + EPS)

    @pl.when(s == 0)
    def _():
        matmuls()

    @pl.when((s > 0) & (s < n_tok))
    def _():
        finish_previous()
        matmuls()

    @pl.when(s == n_tok)
    def _():
        finish_previous()


def _merge_out(ya, yp, gm, x2, gate, post_g_row, b_merge_row, wa, wp, wout):
    s, d = x2.shape
    tm = TM
    n_tok = s // tm
    aw, pw = ya.shape[1], yp.shape[1]
    cur = lambda st: (jnp.minimum(st, n_tok - 1), 0)
    prev = lambda st: (jnp.maximum(st - 1, 0), 0)
    const2 = lambda st: (0, 0)
    single = pl.Buffered(1)
    return pl.pallas_call(
        _merge_out_kernel,
        grid=(n_tok + 1,),
        in_specs=[
            pl.BlockSpec((tm, aw), cur),
            pl.BlockSpec((tm, pw), cur),
            pl.BlockSpec((tm, 2 * d), cur),
            pl.BlockSpec((tm, d), prev),
            pl.BlockSpec((1, d), const2),
            pl.BlockSpec((1, d), const2),
            pl.BlockSpec((1, 2 * d), const2),
            pl.BlockSpec((aw, d), const2, pipeline_mode=single),
            pl.BlockSpec((pw, d), const2, pipeline_mode=single),
            pl.BlockSpec((d, d), const2, pipeline_mode=single),
        ],
        out_specs=pl.BlockSpec((tm, d), prev),
        out_shape=jax.ShapeDtypeStruct((s, d), jnp.float32),
        scratch_shapes=[
            pltpu.VMEM((tm, d), jnp.bfloat16),
            pltpu.VMEM((tm, d), jnp.float32),
            pltpu.VMEM((tm, 1), jnp.float32),
        ],
        compiler_params=pltpu.CompilerParams(
            dimension_semantics=("arbitrary",), vmem_limit_bytes=VMEM_LIMIT_BYTES),
        name="merge_out",
    )(ya, yp, gm, x2, gate, post_g_row, b_merge_row, wa, wp, wout)


def _layer(x2, c, rel_table, bucket, w_ada, b_ada, pre_g, post_g, w_in, sink, w_pool, pool_scale,
           w_br_attn, w_br_pool, w_merge, b_merge, w_out):
    s, d = x2.shape
    c_row = c.reshape(1, d)
    b_ada_row = b_ada.reshape(1, -1)
    pre_g_row = pre_g.reshape(1, d)
    shift, scale, bias_tbl = _shift_scale_bias(c_row, w_ada, b_ada_row, rel_table, bucket)
    bf16 = jnp.bfloat16
    ya, yp, h = _project_mix(x2, shift, scale, pre_g_row, w_in, w_pool.astype(bf16),
                             pool_scale.reshape(1, -1), bias_tbl, sink)
    gm, wa, wp, wout, gate = _merge_gate(h, w_merge, w_br_attn, w_br_pool, w_out,
                                         c_row, w_ada, b_ada_row)
    return _merge_out(ya, yp, gm, x2, gate, post_g.reshape(1, d), b_merge.reshape(1, -1),
                      wa, wp, wout)


def kernel(x, c, rel_bias_table, w_ada, b_ada, pre_norm_g, post_norm_g, w_in, attn_sink,
           w_pool_group, pool_scale, w_branch_attn, w_branch_pool, w_merge, b_merge, w_out):
    batch, s, d = x.shape
    assert batch == 1, "kernel is written for a single sequence"
    depth = w_ada.shape[0]
    bucket = _bucket_index_table()
    x2 = x.reshape(s, d)
    for l in range(depth):
        x2 = _layer(x2, c[0], rel_bias_table, bucket, w_ada[l], b_ada[l], pre_norm_g[l],
                    post_norm_g[l], w_in[l], attn_sink[l], w_pool_group[l], pool_scale[l],
                    w_branch_attn[l], w_branch_pool[l], w_merge[l], b_merge[l], w_out[l])
    return x2.reshape(batch, s, d)
```

```python
import functools
import math

import numpy as np
import jax
import jax.numpy as jnp
from jax import lax
from jax.experimental import pallas as pl
from jax.experimental.pallas import tpu as pltpu

HEAD_DIM = 128
N_Q_HEADS = 8
N_KV_HEADS = 2
GQA_GROUP = N_Q_HEADS // N_KV_HEADS
ATTN_WIDTH = N_Q_HEADS * HEAD_DIM
KV_WIDTH = N_KV_HEADS * HEAD_DIM
WINDOW = 128
BLOCK = 128
SPAN = BLOCK + 2 * WINDOW
N_BUCKETS = 32
MAX_DISTANCE = 128
POOL_SIZES = (2, 4, 8, 16)
N_POOL_GROUPS = len(POOL_SIZES)
EPS = 1e-6
NEG_INF = -1e30
MASKED_BUCKET = N_BUCKETS
ATTN_SCALE = HEAD_DIM ** -0.5
INV_ATTN_SCALE = HEAD_DIM ** 0.5
EXP2_SCALE = ATTN_SCALE * math.log2(math.e)

SUBLANES = 8
BF16_SUBLANES = 16
VMEM_LIMIT_BYTES = 60 * 1024 * 1024

MOD_TK = 256
TM = 512
CW = 512
GATE_TM = 2048
GATE_ROWS = 512
GATE_TN = 1024
POOL_HALO = BF16_SUBLANES
W_STAGE_ROWS = BF16_SUBLANES
W_STAGE_SLOTS = 8

Q0 = 0
KV0 = Q0 + ATTN_WIDTH
GA0 = KV0 + 2 * KV_WIDTH
U0 = GA0 + ATTN_WIDTH


def _silu(v):
    return v * (1.0 / (1.0 + jnp.exp(-v)))


def _half_silu(hv):
    return hv + hv * jnp.tanh(hv)


def _silu_column(c_row):
    n = c_row.shape[1]
    on_diag = (lax.broadcasted_iota(jnp.int32, (n, n), 0)
               == lax.broadcasted_iota(jnp.int32, (n, n), 1))
    return jnp.sum(jnp.where(on_diag, _silu(c_row), 0.0), axis=1, keepdims=True)


def _adaln_rmsnorm(x, gain, shift):
    ms = jnp.mean(x * x, axis=-1, keepdims=True)
    return (x * lax.rsqrt(ms + EPS)) * gain + shift


def _shift_scale_bias_kernel(tbl_ref, c_ref, wsh_ref, wsc_ref, bsh_ref, bsc_ref, bucket_ref,
                             shift_ref, scale_ref, bias_ref, *, tile_buckets):
    k = pl.program_id(0)

    @pl.when(k == 0)
    def _():
        shift_ref[...] = bsh_ref[...]
        scale_ref[...] = bsc_ref[...]

    s = _silu_column(c_ref[...])
    shift_ref[...] += jnp.sum(s * wsh_ref[...], axis=0, keepdims=True)
    scale_ref[...] += jnp.sum(s * wsc_ref[...], axis=0, keepdims=True)

    for t, buckets in enumerate(tile_buckets):
        bk = bucket_ref[:, t * BLOCK:(t + 1) * BLOCK]
        acc = jnp.full((BLOCK, BLOCK), NEG_INF, jnp.float32)
        for b in buckets:
            acc = jnp.where(bk == b, tbl_ref[b, k] * INV_ATTN_SCALE, acc)
        bias_ref[0, :, t * BLOCK:(t + 1) * BLOCK] = acc


def _shift_scale_bias(c_row, w_ada, b_ada_row, rel_table, bucket):
    d = w_ada.shape[0]
    steps = d // MOD_TK
    assert steps == N_Q_HEADS, "one bias-table head per grid step"
    tile_buckets = tuple(
        tuple(int(b) for b in np.unique(bucket[:, t * BLOCK:(t + 1) * BLOCK]) if b != MASKED_BUCKET)
        for t in range(SPAN // BLOCK))
    return pl.pallas_call(
        functools.partial(_shift_scale_bias_kernel, tile_buckets=tile_buckets),
        grid=(steps,),
        in_specs=[
            pl.BlockSpec(memory_space=pltpu.SMEM),
            pl.BlockSpec((1, MOD_TK), lambda k: (0, k)),
            pl.BlockSpec((MOD_TK, d), lambda k: (k, 0)),
            pl.BlockSpec((MOD_TK, d), lambda k: (k, 1)),
            pl.BlockSpec((1, d), lambda k: (0, 0)),
            pl.BlockSpec((1, d), lambda k: (0, 1)),
            pl.BlockSpec(bucket.shape, lambda k: (0, 0)),
        ],
        out_specs=[
            pl.BlockSpec((1, d), lambda k: (0, 0)),
            pl.BlockSpec((1, d), lambda k: (0, 0)),
            pl.BlockSpec((1, BLOCK, SPAN), lambda k: (k, 0, 0)),
        ],
        out_shape=[
            jax.ShapeDtypeStruct((1, d), jnp.float32),
            jax.ShapeDtypeStruct((1, d), jnp.float32),
            jax.ShapeDtypeStruct((N_Q_HEADS, BLOCK, SPAN), jnp.float32),
        ],
        compiler_params=pltpu.CompilerParams(
            dimension_semantics=("arbitrary",), vmem_limit_bytes=VMEM_LIMIT_BYTES),
        name="shift_scale_bias",
    )(rel_table, c_row, w_ada, w_ada, b_ada_row, b_ada_row, bucket)


def _t5_bucket(rel):
    half = N_BUCKETS // 2
    max_exact = half // 2
    assert (max_exact, MAX_DISTANCE // max_exact, half - max_exact) == (8, 16, 8)
    n = abs(rel)
    large = min(max_exact + (n * n).bit_length() - 7, half - 1)
    return (half if rel > 0 else 0) + (n if n < max_exact else large)


def _bucket_index_table():
    table = np.full((BLOCK, SPAN), MASKED_BUCKET, np.int32)
    for q in range(BLOCK):
        for t in range(SPAN):
            rel = t - WINDOW - q
            if abs(rel) <= WINDOW:
                table[q, t] = _t5_bucket(rel)
    return table


def _merge_gate_kernel(h_ref, w_ref, wa_ref, wp_ref, wout_ref, c_ref, wgate_ref, bgate_ref,
                       o_ref, wa_o_ref, wp_o_ref, wout_o_ref, gate_ref,
                       wbf_ref):
    j, i = pl.program_id(0), pl.program_id(1)

    @pl.when(i == 0)
    def _():
        wbf_ref[...] = (0.5 * w_ref[...]).astype(wbf_ref.dtype)

    @pl.when((j == 0) & (i == 0))
    def _():
        gate_ref[...] = bgate_ref[...]

    for r in range(0, h_ref.shape[0], GATE_ROWS):
        o_ref[r:r + GATE_ROWS] = jnp.dot(h_ref[r:r + GATE_ROWS], wbf_ref[...],
                                         preferred_element_type=jnp.float32).astype(o_ref.dtype)

    wa_o_ref[...] = wa_ref[...].astype(wa_o_ref.dtype)
    wp_o_ref[...] = wp_ref[...].astype(wp_o_ref.dtype)
    wout_o_ref[...] = wout_ref[...].astype(wout_o_ref.dtype)
    gate_ref[...] += jnp.sum(_silu_column(c_ref[...]) * wgate_ref[...], axis=0, keepdims=True)


def _merge_gate(h, w_merge, wa, wp, wout, c_row, w_ada, b_ada_row):
    s, d = h.shape
    n = w_merge.shape[1]
    n_i = s // GATE_TM
    steps = (n // GATE_TN) * n_i
    slab = lambda w: (w.shape[0] // steps, w.shape[1])
    step = lambda j, i: (j * n_i + i, 0)
    for w in (wa, wp, wout):
        assert w.shape[0] % (steps * BF16_SUBLANES) == 0
    assert d % (steps * SUBLANES) == 0
    const2 = lambda j, i: (0, 0)
    bf16 = jnp.bfloat16
    return pl.pallas_call(
        _merge_gate_kernel,
        grid=(n // GATE_TN, n_i),
        in_specs=[
            pl.BlockSpec((GATE_TM, d), lambda j, i: (i, 0)),
            pl.BlockSpec((d, GATE_TN), lambda j, i: (0, j)),
            pl.BlockSpec(slab(wa), step),
            pl.BlockSpec(slab(wp), step),
            pl.BlockSpec(slab(wout), step),
            pl.BlockSpec((1, d // steps), lambda j, i: (0, j * n_i + i)),
            pl.BlockSpec((d // steps, d), lambda j, i: (j * n_i + i, 2)),
            pl.BlockSpec((1, d), lambda j, i: (0, 2)),
        ],
        out_specs=[
            pl.BlockSpec((GATE_TM, GATE_TN), lambda j, i: (i, j)),
            pl.BlockSpec(slab(wa), step),
            pl.BlockSpec(slab(wp), step),
            pl.BlockSpec(slab(wout), step),
            pl.BlockSpec((1, d), const2),
        ],
        out_shape=[
            jax.ShapeDtypeStruct((s, n), bf16),
            jax.ShapeDtypeStruct(wa.shape, bf16),
            jax.ShapeDtypeStruct(wp.shape, bf16),
            jax.ShapeDtypeStruct(wout.shape, bf16),
            jax.ShapeDtypeStruct((1, d), jnp.float32),
        ],
        scratch_shapes=[pltpu.VMEM((d, GATE_TN), bf16)],
        compiler_params=pltpu.CompilerParams(
            dimension_semantics=("arbitrary", "arbitrary"), vmem_limit_bytes=VMEM_LIMIT_BYTES),
        name="merge_gate",
    )(h, w_merge, wa, wp, wout, c_row, w_ada, b_ada_row)


def _project_mix_kernel(sink_ref,
                        x_ref, shift_ref, scale_ref, g_ref, w_hbm_ref, wpool_ref, pscale_ref, bias_ref,
                        ya_ref, yp_ref, h_ref,
                        pbuf_ref, kvtail_ref, utail_ref, uext_ref, w_ref, wstage_ref, wsem_ref,
                        *, seq_len):
    s = pl.program_id(0)
    n_tok = pl.num_programs(0) - 1
    tm, d = x_ref.shape
    n_sub = tm // BLOCK
    n_blocks = seq_len // BLOCK
    aw = ATTN_WIDTH
    pw = pscale_ref.shape[1]
    q0, ga0, gp0 = Q0, GA0, U0 + pw
    n_chunks = w_ref.shape[1] // CW
    f32 = jnp.float32
    bf16 = jnp.bfloat16
    cur = s % 2
    prv = 1 - cur
    j = s - 1

    def load_weights():
        n_slots, rows = wstage_ref.shape[0], wstage_ref.shape[1]
        n_slabs = w_hbm_ref.shape[0] // rows
        ahead = n_slots - 1

        def slab_copy(k, slot):
            return pltpu.make_async_copy(w_hbm_ref.at[pl.ds(k * rows, rows)],
                                         wstage_ref.at[slot], wsem_ref.at[slot])

        for k in range(ahead):
            slab_copy(k, k).start()

        def body(k, carry):
            slot = k % n_slots

            @pl.when(k + ahead < n_slabs)
            def _():
                slab_copy(k + ahead, (k + ahead) % n_slots).start()

            slab_copy(k, slot).wait()
            w_ref[pl.ds(pl.multiple_of(k * rows, rows), rows)] = wstage_ref[slot].astype(bf16)
            return carry

        lax.fori_loop(0, n_slabs, body, 0)

    def prologue():
        gain = g_ref[...] * (1.0 + scale_ref[...])
        h_ref[...] = _adaln_rmsnorm(x_ref[...], gain, shift_ref[...]).astype(bf16)

    def is_gate_col(col):
        return ga0 <= col < ga0 + aw or gp0 <= col < gp0 + pw

    def proj_chunk(c):
        lo, hi = c * CW, (c + 1) * CW
        p = jnp.dot(h_ref[...], w_ref[:, lo:hi], preferred_element_type=f32)
        if is_gate_col(lo):
            assert is_gate_col(hi - 1)
            p = 0.5 * p
        pbuf_ref[cur, :, lo:hi] = p.astype(bf16)

    def window_rows(b, col0):
        r0, r1 = (b - 1) * BLOCK, (b + 2) * BLOCK
        parts = []
        if r0 < 0:
            parts.append(kvtail_ref[prv, :, col0 - KV0:col0 - KV0 + HEAD_DIM])
            r0 = 0
        parts.append(pbuf_ref[prv, r0:min(r1, tm), col0:col0 + HEAD_DIM])
        if r1 > tm:
            parts.append(pbuf_ref[cur, 0:r1 - tm, col0:col0 + HEAD_DIM])
        return jnp.concatenate(parts, axis=0) if len(parts) > 1 else parts[0]

    def attn_scores(b, kvh):
        blk = j * n_sub + b
        h0 = kvh * GQA_GROUP
        k = window_rows(b, KV0 + kvh * HEAD_DIM)
        v = window_rows(b, KV0 + KV_WIDTH + kvh * HEAD_DIM)
        qs = jnp.concatenate(
            [pbuf_ref[prv, b * BLOCK:(b + 1) * BLOCK,
                      q0 + (h0 + g) * HEAD_DIM:q0 + (h0 + g + 1) * HEAD_DIM]
             for g in range(GQA_GROUP)], axis=0)
        z = lax.dot_general(qs, k, (((1,), (1,)), ((), ())), preferred_element_type=f32)
        z = z + bias_ref[h0:h0 + GQA_GROUP].reshape(GQA_GROUP * BLOCK, SPAN)
        col = lax.broadcasted_iota(jnp.int32, (1, SPAN), 1)
        if b == 0:
            z = jnp.where(col < jnp.where(blk == 0, WINDOW, 0), NEG_INF, z)
        if b == n_sub - 1:
            z = jnp.where(col >= jnp.where(blk == n_blocks - 1, WINDOW + BLOCK, SPAN), NEG_INF, z)
        sink = jnp.concatenate(
            [jnp.full((BLOCK, 1), sink_ref[h0 + g] * INV_ATTN_SCALE, f32)
             for g in range(GQA_GROUP)], axis=0)
        m = jnp.maximum(jnp.max(z, axis=-1, keepdims=True), sink)
        p = jnp.exp2((z - m) * EXP2_SCALE)
        denom = jnp.sum(p, axis=-1, keepdims=True) + jnp.exp2((sink - m) * EXP2_SCALE)
        return p.astype(bf16), denom, v

    def attn_out(p, denom, v):
        o = jnp.dot(p, v, preferred_element_type=f32) / denom
        return [o[g * BLOCK:(g + 1) * BLOCK] for g in range(GQA_GROUP)]

    def finish_sub_block(b, heads):
        y = jnp.concatenate(heads, axis=1)
        hg = pbuf_ref[prv, b * BLOCK:(b + 1) * BLOCK, ga0:ga0 + aw].astype(f32)
        ya_ref[b * BLOCK:(b + 1) * BLOCK] = (y * _half_silu(hg)).astype(bf16)

    ext = tm + 2 * POOL_HALO

    def pool_group(gi):
        w = POOL_SIZES[gi]
        gw = pw // N_POOL_GROUPS
        c0, c1 = gi * gw, (gi + 1) * gw
        half = w // 2
        halo_row = lax.broadcasted_iota(jnp.int32, (POOL_HALO, 1), 0)
        uext_ref[0:POOL_HALO] = jnp.where(j * tm - POOL_HALO + halo_row >= 0,
                                          utail_ref[prv, :, c0:c1].astype(f32), 0.0)
        uext_ref[POOL_HALO:POOL_HALO + tm] = pbuf_ref[prv, :, U0 + c0:U0 + c1].astype(f32)
        uext_ref[POOL_HALO + tm:ext] = jnp.where(
            (j + 1) * tm + halo_row < seq_len,
            pbuf_ref[cur, 0:POOL_HALO, U0 + c0:U0 + c1].astype(f32), 0.0)
        a = uext_ref[...]
        sh = 1
        while sh < w:
            a = a + pltpu.roll(a, ext - sh, axis=0)
            sh *= 2
        off = POOL_HALO - half
        if off % SUBLANES:
            a = pltpu.roll(a, ext - off, axis=0)
            off = 0
        pos = j * tm + lax.broadcasted_iota(jnp.int32, (tm, 1), 0)
        cnt = (jnp.minimum(pos + half, seq_len) - jnp.maximum(pos - half, 0)).astype(f32)
        pooled = a[off:off + tm] / cnt - uext_ref[POOL_HALO:POOL_HALO + tm]
        mixed = jnp.dot(pooled.astype(bf16), wpool_ref[gi], preferred_element_type=f32)
        hg = pbuf_ref[prv, :, gp0 + c0:gp0 + c1].astype(f32)
        yp_ref[:, c0:c1] = (mixed * pscale_ref[:, c0:c1] * _half_silu(hg)).astype(bf16)

    def save_tails():
        kvtail_ref[cur] = pbuf_ref[prv, tm - BLOCK:tm, KV0:KV0 + 2 * KV_WIDTH]
        utail_ref[cur] = pbuf_ref[prv, tm - POOL_HALO:tm, U0:U0 + pw]

    units = [(b, kvh) for b in range(n_sub) for kvh in range(N_KV_HEADS)]
    assert KV0 % CW == 0 and (2 * KV_WIDTH) % CW == 0 and U0 % CW == 0 and pw % CW == 0
    first = list(range(KV0 // CW, (KV0 + 2 * KV_WIDTH) // CW)) + list(range(U0 // CW, (U0 + pw) // CW))
    order = first + [c for c in range(n_chunks) if c not in first]
    u_chunks_done = len(first)
    assert len(units) + 1 <= n_chunks and u_chunks_done + N_POOL_GROUPS <= n_chunks

    def mix_only():
        pending = [attn_scores(b, kvh) for b, kvh in units]
        heads = []
        for (b, kvh), scores in zip(units, pending):
            heads += attn_out(*scores)
            if kvh == N_KV_HEADS - 1:
                finish_sub_block(b, heads)
                heads = []
        for gi in range(N_POOL_GROUPS):
            pool_group(gi)
        save_tails()

    def project(do_mix):
        prologue()
        pending = attn_scores(*units[0]) if do_mix else None
        heads = []
        for pos, c in enumerate(order):
            proj_chunk(c)
            if not do_mix:
                continue
            if pos < len(units):
                heads += attn_out(*pending)
                if pos + 1 < len(units):
                    pending = attn_scores(*units[pos + 1])
                b, kvh = units[pos]
                if kvh == N_KV_HEADS - 1:
                    finish_sub_block(b, heads)
                    heads = []
            if u_chunks_done <= pos < u_chunks_done + N_POOL_GROUPS:
                pool_group(pos - u_chunks_done)
        if do_mix:
            save_tails()

    @pl.when(s == 0)
    def _():
        kvtail_ref[0] = jnp.zeros(kvtail_ref.shape[1:], kvtail_ref.dtype)
        utail_ref[0] = jnp.zeros(utail_ref.shape[1:], utail_ref.dtype)
        load_weights()
        project(do_mix=False)

    @pl.when((s > 0) & (s < n_tok))
    def _():
        project(do_mix=True)

    @pl.when(s == n_tok)
    def _():
        mix_only()


def _project_mix(x2, shift, scale, pre_g_row, w_in, wpool, pool_scale_row, bias_tbl, sink):
    s, d = x2.shape
    tm = TM
    n_tok = s // tm
    pw = pool_scale_row.shape[1]
    n_in = w_in.shape[1]
    assert s // BLOCK >= 2, "first and last attention block must differ"
    assert n_in == U0 + 2 * pw and w_in.shape[0] % W_STAGE_ROWS == 0
    cur_blk = lambda st: jnp.minimum(st, n_tok - 1)
    mix_blk = lambda st: jnp.maximum(st - 1, 0)
    const2 = lambda st: (0, 0)
    single = pl.Buffered(1)
    bf16 = jnp.bfloat16
    return pl.pallas_call(
        functools.partial(_project_mix_kernel, seq_len=s),
        grid=(n_tok + 1,),
        in_specs=[
            pl.BlockSpec(memory_space=pltpu.SMEM),
            pl.BlockSpec((tm, d), lambda st: (cur_blk(st), 0)),
            pl.BlockSpec((1, d), const2),
            pl.BlockSpec((1, d), const2),
            pl.BlockSpec((1, d), const2),
            pl.BlockSpec(memory_space=pl.ANY),
            pl.BlockSpec(wpool.shape, lambda st: (0, 0, 0), pipeline_mode=single),
            pl.BlockSpec((1, pw), const2),
            pl.BlockSpec(bias_tbl.shape, lambda st: (0, 0, 0), pipeline_mode=single),
        ],
        out_specs=[
            pl.BlockSpec((tm, ATTN_WIDTH), lambda st: (mix_blk(st), 0)),
            pl.BlockSpec((tm, pw), lambda st: (mix_blk(st), 0)),
            pl.BlockSpec((tm, d), lambda st: (cur_blk(st), 0)),
        ],
        out_shape=[
            jax.ShapeDtypeStruct((s, ATTN_WIDTH), bf16),
            jax.ShapeDtypeStruct((s, pw), bf16),
            jax.ShapeDtypeStruct((s, d), bf16),
        ],
        scratch_shapes=[
            pltpu.VMEM((2, tm, n_in), bf16),
            pltpu.VMEM((2, BLOCK, 2 * KV_WIDTH), bf16),
            pltpu.VMEM((2, POOL_HALO, pw), bf16),
            pltpu.VMEM((tm + 2 * POOL_HALO, pw // N_POOL_GROUPS), jnp.float32),
            pltpu.VMEM(w_in.shape, bf16),
            pltpu.VMEM((W_STAGE_SLOTS, W_STAGE_ROWS, n_in), jnp.float32),
            pltpu.SemaphoreType.DMA((W_STAGE_SLOTS,)),
        ],
        compiler_params=pltpu.CompilerParams(
            dimension_semantics=("arbitrary",), vmem_limit_bytes=VMEM_LIMIT_BYTES),
        name="project_mix",
    )(sink, x2, shift, scale, pre_g_row, w_in, wpool, pool_scale_row, bias_tbl)


def _merge_out_kernel(ya_ref, yp_ref, gm_ref, x_ref, gate_ref, postg_ref, bm_ref,
                      wa_ref, wp_ref, wout_ref, o_ref, merged_ref, oacc_ref, inv_ref):
    s = pl.program_id(0)
    n_tok = pl.num_programs(0) - 1
    tm, d = x_ref.shape
    n_chunks = d // CW
    f32 = jnp.float32

    def finish_previous():
        scale = gate_ref[...] * postg_ref[...]
        inv = inv_ref[...]
        for c in range(n_chunks):
            lo, hi = c * CW, (c + 1) * CW
            o_ref[:, lo:hi] = x_ref[:, lo:hi] + (oacc_ref[:, lo:hi] * inv) * scale[:, lo:hi]

    def matmuls():
        half_bm = 0.5 * bm_ref[...]
        for c in range(n_chunks):
            lo, hi = c * CW, (c + 1) * CW
            bra = jnp.dot(ya_ref[...], wa_ref[:, lo:hi], preferred_element_type=f32)
            brp = jnp.dot(yp_ref[...], wp_ref[:, lo:hi], preferred_element_type=f32)
            g_a = 0.5 * jnp.tanh(gm_ref[:, lo:hi].astype(f32) + half_bm[:, lo:hi]) + 0.5
            g_p = 0.5 * jnp.tanh(gm_ref[:, d + lo:d + hi].astype(f32)
                                 + half_bm[:, d + lo:d + hi]) + 0.5
            merged_ref[:, lo:hi] = (g_a * bra + g_p * brp).astype(merged_ref.dtype)
        ssq = jnp.zeros((tm, 1), f32)
        for c in range(n_chunks):
            lo, hi = c * CW, (c + 1) * CW
            o = jnp.dot(merged_ref[...], wout_ref[:, lo:hi], preferred_element_type=f32)
            ssq = ssq + jnp.sum(o * o, axis=-1, keepdims=True)
            oacc_ref[:, lo:hi] = o
        inv_ref[...] = lax.rsqrt(ssq * (1.0 / d) + EPS)

    @pl.when(s == 0)
    def _():
        matmuls()

    @pl.when((s > 0) & (s < n_tok))
    def _():
        finish_previous()
        matmuls()

    @pl.when(s == n_tok)
    def _():
        finish_previous()


def _merge_out(ya, yp, gm, x2, gate, post_g_row, b_merge_row, wa, wp, wout):
    s, d = x2.shape
    tm = TM
    n_tok = s // tm
    aw, pw = ya.shape[1], yp.shape[1]
    cur = lambda st: (jnp.minimum(st, n_tok - 1), 0)
    prev = lambda st: (jnp.maximum(st - 1, 0), 0)
    const2 = lambda st: (0, 0)
    single = pl.Buffered(1)
    return pl.pallas_call(
        _merge_out_kernel,
        grid=(n_tok + 1,),
        in_specs=[
            pl.BlockSpec((tm, aw), cur),
            pl.BlockSpec((tm, pw), cur),
            pl.BlockSpec((tm, 2 * d), cur),
            pl.BlockSpec((tm, d), prev),
            pl.BlockSpec((1, d), const2),
            pl.BlockSpec((1, d), const2),
            pl.BlockSpec((1, 2 * d), const2),
            pl.BlockSpec((aw, d), const2, pipeline_mode=single),
            pl.BlockSpec((pw, d), const2, pipeline_mode=single),
            pl.BlockSpec((d, d), const2, pipeline_mode=single),
        ],
        out_specs=pl.BlockSpec((tm, d), prev),
        out_shape=jax.ShapeDtypeStruct((s, d), jnp.float32),
        scratch_shapes=[
            pltpu.VMEM((tm, d), jnp.bfloat16),
            pltpu.VMEM((tm, d), jnp.float32),
            pltpu.VMEM((tm, 1), jnp.float32),
        ],
        compiler_params=pltpu.CompilerParams(
            dimension_semantics=("arbitrary",), vmem_limit_bytes=VMEM_LIMIT_BYTES),
        name="merge_out",
    )(ya, yp, gm, x2, gate, post_g_row, b_merge_row, wa, wp, wout)


def _layer(x2, c, rel_table, bucket, w_ada, b_ada, pre_g, post_g, w_in, sink, w_pool, pool_scale,
           w_br_attn, w_br_pool, w_merge, b_merge, w_out):
    s, d = x2.shape
    c_row = c.reshape(1, d)
    b_ada_row = b_ada.reshape(1, -1)
    pre_g_row = pre_g.reshape(1, d)
    shift, scale, bias_tbl = _shift_scale_bias(c_row, w_ada, b_ada_row, rel_table, bucket)
    bf16 = jnp.bfloat16
    ya, yp, h = _project_mix(x2, shift, scale, pre_g_row, w_in, w_pool.astype(bf16),
                             pool_scale.reshape(1, -1), bias_tbl, sink)
    gm, wa, wp, wout, gate = _merge_gate(h, w_merge, w_br_attn, w_br_pool, w_out,
                                         c_row, w_ada, b_ada_row)
    return _merge_out(ya, yp, gm, x2, gate, post_g.reshape(1, d), b_merge.reshape(1, -1),
                      wa, wp, wout)


def kernel(x, c, rel_bias_table, w_ada, b_ada, pre_norm_g, post_norm_g, w_in, attn_sink,
           w_pool_group, pool_scale, w_branch_attn, w_branch_pool, w_merge, b_merge, w_out):
    batch, s, d = x.shape
    assert batch == 1, "kernel is written for a single sequence"
    depth = w_ada.shape[0]
    bucket = _bucket_index_table()
    x2 = x.reshape(s, d)
    for l in range(depth):
        x2 = _layer(x2, c[0], rel_bias_table, bucket, w_ada[l], b_ada[l], pre_norm_g[l],
                    post_norm_g[l], w_in[l], attn_sink[l], w_pool_group[l], pool_scale[l],
                    w_branch_attn[l], w_branch_pool[l], w_merge[l], b_merge[l], w_out[l])
    return x2.reshape(batch, s, d)
```

```python
import functools
import math

import numpy as np
import jax
import jax.numpy as jnp
from jax import lax
from jax.experimental import pallas as pl
from jax.experimental.pallas import tpu as pltpu

HEAD_DIM = 128
N_Q_HEADS = 8
N_KV_HEADS = 2
GQA_GROUP = N_Q_HEADS // N_KV_HEADS
ATTN_WIDTH = N_Q_HEADS * HEAD_DIM
KV_WIDTH = N_KV_HEADS * HEAD_DIM
WINDOW = 128
BLOCK = 128
SPAN = BLOCK + 2 * WINDOW
N_BUCKETS = 32
MAX_DISTANCE = 128
POOL_SIZES = (2, 4, 8, 16)
N_POOL_GROUPS = len(POOL_SIZES)
EPS = 1e-6
NEG_INF = -1e30
MASKED_BUCKET = N_BUCKETS
ATTN_SCALE = HEAD_DIM ** -0.5
INV_ATTN_SCALE = HEAD_DIM ** 0.5
EXP2_SCALE = ATTN_SCALE * math.log2(math.e)

SUBLANES = 8
BF16_SUBLANES = 16
VMEM_LIMIT_BYTES = 60 * 1024 * 1024

MOD_TK = 256
TM = 512
CW = 512
GATE_TM = 2048
GATE_ROWS = 512
GATE_TN = 1024
POOL_HALO = BF16_SUBLANES
W_STAGE_SLOTS = 8

Q0 = 0
KV0 = Q0 + ATTN_WIDTH
GA0 = KV0 + 2 * KV_WIDTH
U0 = GA0 + ATTN_WIDTH


def _silu(v):
    return v * (1.0 / (1.0 + jnp.exp(-v)))


def _half_silu(hv):
    return hv + hv * jnp.tanh(hv)


def _silu_column(c_row):
    n = c_row.shape[1]
    on_diag = (lax.broadcasted_iota(jnp.int32, (n, n), 0)
               == lax.broadcasted_iota(jnp.int32, (n, n), 1))
    return jnp.sum(jnp.where(on_diag, _silu(c_row), 0.0), axis=1, keepdims=True)


def _adaln_rmsnorm(x, gain, shift):
    ms = jnp.mean(x * x, axis=-1, keepdims=True)
    return (x * lax.rsqrt(ms + EPS)) * gain + shift


def _shift_scale_bias_kernel(tbl_ref, c_ref, wsh_ref, wsc_ref, bsh_ref, bsc_ref, bucket_ref,
                             shift_ref, scale_ref, bias_ref, *, tile_buckets):
    k = pl.program_id(0)

    @pl.when(k == 0)
    def _():
        shift_ref[...] = bsh_ref[...]
        scale_ref[...] = bsc_ref[...]

    s = _silu_column(c_ref[...])
    shift_ref[...] += jnp.sum(s * wsh_ref[...], axis=0, keepdims=True)
    scale_ref[...] += jnp.sum(s * wsc_ref[...], axis=0, keepdims=True)

    for t, buckets in enumerate(tile_buckets):
        bk = bucket_ref[:, t * BLOCK:(t + 1) * BLOCK]
        acc = jnp.full((BLOCK, BLOCK), NEG_INF, jnp.float32)
        for b in buckets:
            acc = jnp.where(bk == b, tbl_ref[b, k] * INV_ATTN_SCALE, acc)
        bias_ref[0, :, t * BLOCK:(t + 1) * BLOCK] = acc


def _shift_scale_bias(c_row, w_ada, b_ada_row, rel_table, bucket):
    d = w_ada.shape[0]
    steps = d // MOD_TK
    assert steps == N_Q_HEADS, "one bias-table head per grid step"
    tile_buckets = tuple(
        tuple(int(b) for b in np.unique(bucket[:, t * BLOCK:(t + 1) * BLOCK]) if b != MASKED_BUCKET)
        for t in range(SPAN // BLOCK))
    return pl.pallas_call(
        functools.partial(_shift_scale_bias_kernel, tile_buckets=tile_buckets),
        grid=(steps,),
        in_specs=[
            pl.BlockSpec(memory_space=pltpu.SMEM),
            pl.BlockSpec((1, MOD_TK), lambda k: (0, k)),
            pl.BlockSpec((MOD_TK, d), lambda k: (k, 0)),
            pl.BlockSpec((MOD_TK, d), lambda k: (k, 1)),
            pl.BlockSpec((1, d), lambda k: (0, 0)),
            pl.BlockSpec((1, d), lambda k: (0, 1)),
            pl.BlockSpec(bucket.shape, lambda k: (0, 0)),
        ],
        out_specs=[
            pl.BlockSpec((1, d), lambda k: (0, 0)),
            pl.BlockSpec((1, d), lambda k: (0, 0)),
            pl.BlockSpec((1, BLOCK, SPAN), lambda k: (k, 0, 0)),
        ],
        out_shape=[
            jax.ShapeDtypeStruct((1, d), jnp.float32),
            jax.ShapeDtypeStruct((1, d), jnp.float32),
            jax.ShapeDtypeStruct((N_Q_HEADS, BLOCK, SPAN), jnp.float32),
        ],
        compiler_params=pltpu.CompilerParams(
            dimension_semantics=("arbitrary",), vmem_limit_bytes=VMEM_LIMIT_BYTES),
        name="shift_scale_bias",
    )(rel_table, c_row, w_ada, w_ada, b_ada_row, b_ada_row, bucket)


def _t5_bucket(rel):
    half = N_BUCKETS // 2
    max_exact = half // 2
    assert (max_exact, MAX_DISTANCE // max_exact, half - max_exact) == (8, 16, 8)
    n = abs(rel)
    large = min(max_exact + (n * n).bit_length() - 7, half - 1)
    return (half if rel > 0 else 0) + (n if n < max_exact else large)


def _bucket_index_table():
    table = np.full((BLOCK, SPAN), MASKED_BUCKET, np.int32)
    for q in range(BLOCK):
        for t in range(SPAN):
            rel = t - WINDOW - q
            if abs(rel) <= WINDOW:
                table[q, t] = _t5_bucket(rel)
    return table


def _merge_gate_kernel(h_ref, w_ref, wa_ref, wp_ref, wout_ref, c_ref, wgate_ref, bgate_ref,
                       o_ref, wa_o_ref, wp_o_ref, wout_o_ref, gate_ref,
                       wbf_ref):
    j, i = pl.program_id(0), pl.program_id(1)

    @pl.when(i == 0)
    def _():
        wbf_ref[...] = (0.5 * w_ref[...]).astype(wbf_ref.dtype)

    @pl.when((j == 0) & (i == 0))
    def _():
        gate_ref[...] = bgate_ref[...]

    for r in range(0, h_ref.shape[0], GATE_ROWS):
        o_ref[r:r + GATE_ROWS] = jnp.dot(h_ref[r:r + GATE_ROWS], wbf_ref[...],
                                         preferred_element_type=jnp.float32).astype(o_ref.dtype)

    wa_o_ref[...] = wa_ref[...].astype(wa_o_ref.dtype)
    wp_o_ref[...] = wp_ref[...].astype(wp_o_ref.dtype)
    wout_o_ref[...] = wout_ref[...].astype(wout_o_ref.dtype)
    gate_ref[...] += jnp.sum(_silu_column(c_ref[...]) * wgate_ref[...], axis=0, keepdims=True)


def _merge_gate(h, w_merge, wa, wp, wout, c_row, w_ada, b_ada_row):
    s, d = h.shape
    n = w_merge.shape[1]
    n_i = s // GATE_TM
    steps = (n // GATE_TN) * n_i
    slab = lambda w: (w.shape[0] // steps, w.shape[1])
    step = lambda j, i: (j * n_i + i, 0)
    for w in (wa, wp, wout):
        assert w.shape[0] % (steps * BF16_SUBLANES) == 0
    assert d % (steps * SUBLANES) == 0
    const2 = lambda j, i: (0, 0)
    bf16 = jnp.bfloat16
    return pl.pallas_call(
        _merge_gate_kernel,
        grid=(n // GATE_TN, n_i),
        in_specs=[
            pl.BlockSpec((GATE_TM, d), lambda j, i: (i, 0)),
            pl.BlockSpec((d, GATE_TN), lambda j, i: (0, j)),
            pl.BlockSpec(slab(wa), step),
            pl.BlockSpec(slab(wp), step),
            pl.BlockSpec(slab(wout), step),
            pl.BlockSpec((1, d // steps), lambda j, i: (0, j * n_i + i)),
            pl.BlockSpec((d // steps, d), lambda j, i: (j * n_i + i, 2)),
            pl.BlockSpec((1, d), lambda j, i: (0, 2)),
        ],
        out_specs=[
            pl.BlockSpec((GATE_TM, GATE_TN), lambda j, i: (i, j)),
            pl.BlockSpec(slab(wa), step),
            pl.BlockSpec(slab(wp), step),
            pl.BlockSpec(slab(wout), step),
            pl.BlockSpec((1, d), const2),
        ],
        out_shape=[
            jax.ShapeDtypeStruct((s, n), bf16),
            jax.ShapeDtypeStruct(wa.shape, bf16),
            jax.ShapeDtypeStruct(wp.shape, bf16),
            jax.ShapeDtypeStruct(wout.shape, bf16),
            jax.ShapeDtypeStruct((1, d), jnp.float32),
        ],
        scratch_shapes=[pltpu.VMEM((d, GATE_TN), bf16)],
        compiler_params=pltpu.CompilerParams(
            dimension_semantics=("arbitrary", "arbitrary"), vmem_limit_bytes=VMEM_LIMIT_BYTES),
        name="merge_gate",
    )(h, w_merge, wa, wp, wout, c_row, w_ada, b_ada_row)


def _project_mix_kernel(sink_ref,
                        x_ref, shift_ref, scale_ref, g_ref, w_hbm_ref, wpool_ref, pscale_ref, bias_ref,
                        ya_ref, yp_ref, h_ref,
                        pbuf_ref, kvtail_ref, utail_ref, uext_ref, w_ref, wstage_ref, wsem_ref,
                        *, seq_len):
    s = pl.program_id(0)
    n_tok = pl.num_programs(0) - 1
    tm, d = x_ref.shape
    n_sub = tm // BLOCK
    n_blocks = seq_len // BLOCK
    aw = ATTN_WIDTH
    pw = pscale_ref.shape[1]
    q0, ga0, gp0 = Q0, GA0, U0 + pw
    n_chunks = w_ref.shape[1] // CW
    f32 = jnp.float32
    bf16 = jnp.bfloat16
    cur = s % 2
    prv = 1 - cur
    j = s - 1

    n_slots, slab_rows = wstage_ref.shape[0], wstage_ref.shape[1]
    assert n_slots * slab_rows == d and wstage_ref.shape[2] == CW

    def slab_copy(c, r):
        return pltpu.make_async_copy(
            w_hbm_ref.at[pl.ds(r * slab_rows, slab_rows), pl.ds(c * CW, CW)],
            wstage_ref.at[r], wsem_ref.at[r])

    def fetch_chunk(c):
        for r in range(n_slots):
            slab_copy(c, r).start()

    def land_chunk(c, next_c):
        for r in range(n_slots):
            slab_copy(c, r).wait()
            w_ref[r * slab_rows:(r + 1) * slab_rows, c * CW:(c + 1) * CW] = wstage_ref[r].astype(bf16)
            if next_c is not None:
                slab_copy(next_c, r).start()

    def prologue():
        gain = g_ref[...] * (1.0 + scale_ref[...])
        h_ref[...] = _adaln_rmsnorm(x_ref[...], gain, shift_ref[...]).astype(bf16)

    def is_gate_col(col):
        return ga0 <= col < ga0 + aw or gp0 <= col < gp0 + pw

    def proj_chunk(c):
        lo, hi = c * CW, (c + 1) * CW
        p = jnp.dot(h_ref[...], w_ref[:, lo:hi], preferred_element_type=f32)
        if is_gate_col(lo):
            assert is_gate_col(hi - 1)
            p = 0.5 * p
        pbuf_ref[cur, :, lo:hi] = p.astype(bf16)

    def window_rows(b, col0):
        r0, r1 = (b - 1) * BLOCK, (b + 2) * BLOCK
        parts = []
        if r0 < 0:
            parts.append(kvtail_ref[prv, :, col0 - KV0:col0 - KV0 + HEAD_DIM])
            r0 = 0
        parts.append(pbuf_ref[prv, r0:min(r1, tm), col0:col0 + HEAD_DIM])
        if r1 > tm:
            parts.append(pbuf_ref[cur, 0:r1 - tm, col0:col0 + HEAD_DIM])
        return jnp.concatenate(parts, axis=0) if len(parts) > 1 else parts[0]

    def attn_scores(b, kvh):
        blk = j * n_sub + b
        h0 = kvh * GQA_GROUP
        k = window_rows(b, KV0 + kvh * HEAD_DIM)
        v = window_rows(b, KV0 + KV_WIDTH + kvh * HEAD_DIM)
        qs = jnp.concatenate(
            [pbuf_ref[prv, b * BLOCK:(b + 1) * BLOCK,
                      q0 + (h0 + g) * HEAD_DIM:q0 + (h0 + g + 1) * HEAD_DIM]
             for g in range(GQA_GROUP)], axis=0)
        z = lax.dot_general(qs, k, (((1,), (1,)), ((), ())), preferred_element_type=f32)
        z = z + bias_ref[h0:h0 + GQA_GROUP].reshape(GQA_GROUP * BLOCK, SPAN)
        col = lax.broadcasted_iota(jnp.int32, (1, SPAN), 1)
        if b == 0:
            z = jnp.where(col < jnp.where(blk == 0, WINDOW, 0), NEG_INF, z)
        if b == n_sub - 1:
            z = jnp.where(col >= jnp.where(blk == n_blocks - 1, WINDOW + BLOCK, SPAN), NEG_INF, z)
        sink = jnp.concatenate(
            [jnp.full((BLOCK, 1), sink_ref[h0 + g] * INV_ATTN_SCALE, f32)
             for g in range(GQA_GROUP)], axis=0)
        m = jnp.maximum(jnp.max(z, axis=-1, keepdims=True), sink)
        p = jnp.exp2((z - m) * EXP2_SCALE)
        denom = jnp.sum(p, axis=-1, keepdims=True) + jnp.exp2((sink - m) * EXP2_SCALE)
        return p.astype(bf16), denom, v

    def attn_out(p, denom, v):
        o = jnp.dot(p, v, preferred_element_type=f32) / denom
        return [o[g * BLOCK:(g + 1) * BLOCK] for g in range(GQA_GROUP)]

    def finish_sub_block(b, heads):
        y = jnp.concatenate(heads, axis=1)
        hg = pbuf_ref[prv, b * BLOCK:(b + 1) * BLOCK, ga0:ga0 + aw].astype(f32)
        ya_ref[b * BLOCK:(b + 1) * BLOCK] = (y * _half_silu(hg)).astype(bf16)

    ext = tm + 2 * POOL_HALO

    def pool_group(gi):
        w = POOL_SIZES[gi]
        gw = pw // N_POOL_GROUPS
        c0, c1 = gi * gw, (gi + 1) * gw
        half = w // 2
        halo_row = lax.broadcasted_iota(jnp.int32, (POOL_HALO, 1), 0)
        uext_ref[0:POOL_HALO] = jnp.where(j * tm - POOL_HALO + halo_row >= 0,
                                          utail_ref[prv, :, c0:c1].astype(f32), 0.0)
        uext_ref[POOL_HALO:POOL_HALO + tm] = pbuf_ref[prv, :, U0 + c0:U0 + c1].astype(f32)
        uext_ref[POOL_HALO + tm:ext] = jnp.where(
            (j + 1) * tm + halo_row < seq_len,
            pbuf_ref[cur, 0:POOL_HALO, U0 + c0:U0 + c1].astype(f32), 0.0)
        a = uext_ref[...]
        sh = 1
        while sh < w:
            a = a + pltpu.roll(a, ext - sh, axis=0)
            sh *= 2
        off = POOL_HALO - half
        if off % SUBLANES:
            a = pltpu.roll(a, ext - off, axis=0)
            off = 0
        pos = j * tm + lax.broadcasted_iota(jnp.int32, (tm, 1), 0)
        cnt = (jnp.minimum(pos + half, seq_len) - jnp.maximum(pos - half, 0)).astype(f32)
        pooled = a[off:off + tm] / cnt - uext_ref[POOL_HALO:POOL_HALO + tm]
        mixed = jnp.dot(pooled.astype(bf16), wpool_ref[gi], preferred_element_type=f32)
        hg = pbuf_ref[prv, :, gp0 + c0:gp0 + c1].astype(f32)
        yp_ref[:, c0:c1] = (mixed * pscale_ref[:, c0:c1] * _half_silu(hg)).astype(bf16)

    def save_tails():
        kvtail_ref[cur] = pbuf_ref[prv, tm - BLOCK:tm, KV0:KV0 + 2 * KV_WIDTH]
        utail_ref[cur] = pbuf_ref[prv, tm - POOL_HALO:tm, U0:U0 + pw]

    units = [(b, kvh) for b in range(n_sub) for kvh in range(N_KV_HEADS)]
    assert KV0 % CW == 0 and (2 * KV_WIDTH) % CW == 0 and U0 % CW == 0 and pw % CW == 0
    first = list(range(KV0 // CW, (KV0 + 2 * KV_WIDTH) // CW)) + list(range(U0 // CW, (U0 + pw) // CW))
    order = first + [c for c in range(n_chunks) if c not in first]
    u_chunks_done = len(first)
    assert len(units) + 1 <= n_chunks and u_chunks_done + N_POOL_GROUPS <= n_chunks

    def mix_only():
        pending = [attn_scores(b, kvh) for b, kvh in units]
        heads = []
        for (b, kvh), scores in zip(units, pending):
            heads += attn_out(*scores)
            if kvh == N_KV_HEADS - 1:
                finish_sub_block(b, heads)
                heads = []
        for gi in range(N_POOL_GROUPS):
            pool_group(gi)
        save_tails()

    def project_first():
        fetch_chunk(order[0])
        prologue()
        for pos, c in enumerate(order):
            land_chunk(c, order[pos + 1] if pos + 1 < len(order) else None)
            proj_chunk(c)

    def project_and_mix():
        prologue()
        pending = attn_scores(*units[0])
        heads = []
        for pos, c in enumerate(order):
            proj_chunk(c)
            if pos < len(units):
                heads += attn_out(*pending)
                if pos + 1 < len(units):
                    pending = attn_scores(*units[pos + 1])
                b, kvh = units[pos]
                if kvh == N_KV_HEADS - 1:
                    finish_sub_block(b, heads)
                    heads = []
            if u_chunks_done <= pos < u_chunks_done + N_POOL_GROUPS:
                pool_group(pos - u_chunks_done)
        save_tails()

    @pl.when(s == 0)
    def _():
        kvtail_ref[0] = jnp.zeros(kvtail_ref.shape[1:], kvtail_ref.dtype)
        utail_ref[0] = jnp.zeros(utail_ref.shape[1:], utail_ref.dtype)
        project_first()

    @pl.when((s > 0) & (s < n_tok))
    def _():
        project_and_mix()

    @pl.when(s == n_tok)
    def _():
        mix_only()


def _project_mix(x2, shift, scale, pre_g_row, w_in, wpool, pool_scale_row, bias_tbl, sink):
    s, d = x2.shape
    tm = TM
    n_tok = s // tm
    pw = pool_scale_row.shape[1]
    n_in = w_in.shape[1]
    assert s // BLOCK >= 2, "first and last attention block must differ"
    assert n_in == U0 + 2 * pw and d % (W_STAGE_SLOTS * BF16_SUBLANES) == 0
    cur_blk = lambda st: jnp.minimum(st, n_tok - 1)
    mix_blk = lambda st: jnp.maximum(st - 1, 0)
    const2 = lambda st: (0, 0)
    single = pl.Buffered(1)
    bf16 = jnp.bfloat16
    return pl.pallas_call(
        functools.partial(_project_mix_kernel, seq_len=s),
        grid=(n_tok + 1,),
        in_specs=[
            pl.BlockSpec(memory_space=pltpu.SMEM),
            pl.BlockSpec((tm, d), lambda st: (cur_blk(st), 0)),
            pl.BlockSpec((1, d), const2),
            pl.BlockSpec((1, d), const2),
            pl.BlockSpec((1, d), const2),
            pl.BlockSpec(memory_space=pl.ANY),
            pl.BlockSpec(wpool.shape, lambda st: (0, 0, 0), pipeline_mode=single),
            pl.BlockSpec((1, pw), const2),
            pl.BlockSpec(bias_tbl.shape, lambda st: (0, 0, 0), pipeline_mode=single),
        ],
        out_specs=[
            pl.BlockSpec((tm, ATTN_WIDTH), lambda st: (mix_blk(st), 0)),
            pl.BlockSpec((tm, pw), lambda st: (mix_blk(st), 0)),
            pl.BlockSpec((tm, d), lambda st: (cur_blk(st), 0)),
        ],
        out_shape=[
            jax.ShapeDtypeStruct((s, ATTN_WIDTH), bf16),
            jax.ShapeDtypeStruct((s, pw), bf16),
            jax.ShapeDtypeStruct((s, d), bf16),
        ],
        scratch_shapes=[
            pltpu.VMEM((2, tm, n_in), bf16),
            pltpu.VMEM((2, BLOCK, 2 * KV_WIDTH), bf16),
            pltpu.VMEM((2, POOL_HALO, pw), bf16),
            pltpu.VMEM((tm + 2 * POOL_HALO, pw // N_POOL_GROUPS), jnp.float32),
            pltpu.VMEM(w_in.shape, bf16),
            pltpu.VMEM((W_STAGE_SLOTS, d // W_STAGE_SLOTS, CW), jnp.float32),
            pltpu.SemaphoreType.DMA((W_STAGE_SLOTS,)),
        ],
        compiler_params=pltpu.CompilerParams(
            dimension_semantics=("arbitrary",), vmem_limit_bytes=VMEM_LIMIT_BYTES),
        name="project_mix",
    )(sink, x2, shift, scale, pre_g_row, w_in, wpool, pool_scale_row, bias_tbl)


def _merge_out_kernel(ya_ref, yp_ref, gm_ref, x_ref, gate_ref, postg_ref, bm_ref,
                      wa_ref, wp_ref, wout_ref, o_ref, merged_ref, oacc_ref, inv_ref):
    s = pl.program_id(0)
    n_tok = pl.num_programs(0) - 1
    tm, d = x_ref.shape
    n_chunks = d // CW
    f32 = jnp.float32

    def finish_previous():
        scale = gate_ref[...] * postg_ref[...]
        inv = inv_ref[...]
        for c in range(n_chunks):
            lo, hi = c * CW, (c + 1) * CW
            o_ref[:, lo:hi] = x_ref[:, lo:hi] + (oacc_ref[:, lo:hi] * inv) * scale[:, lo:hi]

    def matmuls():
        half_bm = 0.5 * bm_ref[...]
        for c in range(n_chunks):
            lo, hi = c * CW, (c + 1) * CW
            bra = jnp.dot(ya_ref[...], wa_ref[:, lo:hi], preferred_element_type=f32)
            brp = jnp.dot(yp_ref[...], wp_ref[:, lo:hi], preferred_element_type=f32)
            g_a = 0.5 * jnp.tanh(gm_ref[:, lo:hi].astype(f32) + half_bm[:, lo:hi]) + 0.5
            g_p = 0.5 * jnp.tanh(gm_ref[:, d + lo:d + hi].astype(f32)
                                 + half_bm[:, d + lo:d + hi]) + 0.5
            merged_ref[:, lo:hi] = (g_a * bra + g_p * brp).astype(merged_ref.dtype)
        ssq = jnp.zeros((tm, 1), f32)
        for c in range(n_chunks):
            lo, hi = c * CW, (c + 1) * CW
            o = jnp.dot(merged_ref[...], wout_ref[:, lo:hi], preferred_element_type=f32)
            ssq = ssq + jnp.sum(o * o, axis=-1, keepdims=True)
            oacc_ref[:, lo:hi] = o
        inv_ref[...] = lax.rsqrt(ssq * (1.0 / d) + EPS)

    @pl.when(s == 0)
    def _():
        matmuls()

    @pl.when((s > 0) & (s < n_tok))
    def _():
        finish_previous()
        matmuls()

    @pl.when(s == n_tok)
    def _():
        finish_previous()


def _merge_out(ya, yp, gm, x2, gate, post_g_row, b_merge_row, wa, wp, wout):
    s, d = x2.shape
    tm = TM
    n_tok = s // tm
    aw, pw = ya.shape[1], yp.shape[1]
    cur = lambda st: (jnp.minimum(st, n_tok - 1), 0)
    prev = lambda st: (jnp.maximum(st - 1, 0), 0)
    const2 = lambda st: (0, 0)
    single = pl.Buffered(1)
    return pl.pallas_call(
        _merge_out_kernel,
        grid=(n_tok + 1,),
        in_specs=[
            pl.BlockSpec((tm, aw), cur),
            pl.BlockSpec((tm, pw), cur),
            pl.BlockSpec((tm, 2 * d), cur),
            pl.BlockSpec((tm, d), prev),
            pl.BlockSpec((1, d), const2),
            pl.BlockSpec((1, d), const2),
            pl.BlockSpec((1, 2 * d), const2),
            pl.BlockSpec((aw, d), const2, pipeline_mode=single),
            pl.BlockSpec((pw, d), const2, pipeline_mode=single),
            pl.BlockSpec((d, d), const2, pipeline_mode=single),
        ],
        out_specs=pl.BlockSpec((tm, d), prev),
        out_shape=jax.ShapeDtypeStruct((s, d), jnp.float32),
        scratch_shapes=[
            pltpu.VMEM((tm, d), jnp.bfloat16),
            pltpu.VMEM((tm, d), jnp.float32),
            pltpu.VMEM((tm, 1), jnp.float32),
        ],
        compiler_params=pltpu.CompilerParams(
            dimension_semantics=("arbitrary",), vmem_limit_bytes=VMEM_LIMIT_BYTES),
        name="merge_out",
    )(ya, yp, gm, x2, gate, post_g_row, b_merge_row, wa, wp, wout)


def _layer(x2, c, rel_table, bucket, w_ada, b_ada, pre_g, post_g, w_in, sink, w_pool, pool_scale,
           w_br_attn, w_br_pool, w_merge, b_merge, w_out):
    s, d = x2.shape
    c_row = c.reshape(1, d)
    b_ada_row = b_ada.reshape(1, -1)
    pre_g_row = pre_g.reshape(1, d)
    shift, scale, bias_tbl = _shift_scale_bias(c_row, w_ada, b_ada_row, rel_table, bucket)
    bf16 = jnp.bfloat16
    ya, yp, h = _project_mix(x2, shift, scale, pre_g_row, w_in, w_pool.astype(bf16),
                             pool_scale.reshape(1, -1), bias_tbl, sink)
    gm, wa, wp, wout, gate = _merge_gate(h, w_merge, w_br_attn, w_br_pool, w_out,
                                         c_row, w_ada, b_ada_row)
    return _merge_out(ya, yp, gm, x2, gate, post_g.reshape(1, d), b_merge.reshape(1, -1),
                      wa, wp, wout)


def kernel(x, c, rel_bias_table, w_ada, b_ada, pre_norm_g, post_norm_g, w_in, attn_sink,
           w_pool_group, pool_scale, w_branch_attn, w_branch_pool, w_merge, b_merge, w_out):
    batch, s, d = x.shape
    assert batch == 1, "kernel is written for a single sequence"
    depth = w_ada.shape[0]
    bucket = _bucket_index_table()
    x2 = x.reshape(s, d)
    for l in range(depth):
        x2 = _layer(x2, c[0], rel_bias_table, bucket, w_ada[l], b_ada[l], pre_norm_g[l],
                    post_norm_g[l], w_in[l], attn_sink[l], w_pool_group[l], pool_scale[l],
                    w_branch_attn[l], w_branch_pool[l], w_merge[l], b_merge[l], w_out[l])
    return x2.reshape(batch, s, d)
```

```python
import functools
import math

import numpy as np
import jax
import jax.numpy as jnp
from jax import lax
from jax.experimental import pallas as pl
from jax.experimental.pallas import tpu as pltpu

HEAD_DIM = 128
N_Q_HEADS = 8
N_KV_HEADS = 2
GQA_GROUP = N_Q_HEADS // N_KV_HEADS
ATTN_WIDTH = N_Q_HEADS * HEAD_DIM
KV_WIDTH = N_KV_HEADS * HEAD_DIM
WINDOW = 128
BLOCK = 128
SPAN = BLOCK + 2 * WINDOW
N_BUCKETS = 32
MAX_DISTANCE = 128
POOL_SIZES = (2, 4, 8, 16)
N_POOL_GROUPS = len(POOL_SIZES)
EPS = 1e-6
NEG_INF = -1e30
MASKED_BUCKET = N_BUCKETS
ATTN_SCALE = HEAD_DIM ** -0.5
INV_ATTN_SCALE = HEAD_DIM ** 0.5
EXP2_SCALE = ATTN_SCALE * math.log2(math.e)

SUBLANES = 8
BF16_SUBLANES = 16
VMEM_LIMIT_BYTES = 60 * 1024 * 1024

MOD_TK = 256
TM = 512
CW = 512
GATE_TM = 2048
GATE_ROWS = 512
GATE_TN = 1024
POOL_HALO = BF16_SUBLANES
W_STAGE_SLOTS = 8

Q0 = 0
KV0 = Q0 + ATTN_WIDTH
GA0 = KV0 + 2 * KV_WIDTH
U0 = GA0 + ATTN_WIDTH


def _silu(v):
    return v * (1.0 / (1.0 + jnp.exp(-v)))


def _half_silu(hv):
    return hv + hv * jnp.tanh(hv)


def _silu_column(c_row):
    n = c_row.shape[1]
    on_diag = (lax.broadcasted_iota(jnp.int32, (n, n), 0)
               == lax.broadcasted_iota(jnp.int32, (n, n), 1))
    return jnp.sum(jnp.where(on_diag, _silu(c_row), 0.0), axis=1, keepdims=True)


def _adaln_rmsnorm(x, gain, shift):
    ms = jnp.mean(x * x, axis=-1, keepdims=True)
    return (x * lax.rsqrt(ms + EPS)) * gain + shift


def _shift_scale_bias_kernel(tbl_ref, c_ref, wsh_ref, wsc_ref, bsh_ref, bsc_ref, bucket_ref,
                             shift_ref, scale_ref, bias_ref, *, tile_buckets):
    k = pl.program_id(0)

    @pl.when(k == 0)
    def _():
        shift_ref[...] = bsh_ref[...]
        scale_ref[...] = bsc_ref[...]

    s = _silu_column(c_ref[...])
    shift_ref[...] += jnp.sum(s * wsh_ref[...], axis=0, keepdims=True)
    scale_ref[...] += jnp.sum(s * wsc_ref[...], axis=0, keepdims=True)

    for t, buckets in enumerate(tile_buckets):
        bk = bucket_ref[:, t * BLOCK:(t + 1) * BLOCK]
        acc = jnp.full((BLOCK, BLOCK), NEG_INF, jnp.float32)
        for b in buckets:
            acc = jnp.where(bk == b, tbl_ref[b, k] * INV_ATTN_SCALE, acc)
        bias_ref[0, :, t * BLOCK:(t + 1) * BLOCK] = acc


def _shift_scale_bias(c_row, w_ada, b_ada_row, rel_table, bucket):
    d = w_ada.shape[0]
    steps = d // MOD_TK
    assert steps == N_Q_HEADS, "one bias-table head per grid step"
    tile_buckets = tuple(
        tuple(int(b) for b in np.unique(bucket[:, t * BLOCK:(t + 1) * BLOCK]) if b != MASKED_BUCKET)
        for t in range(SPAN // BLOCK))
    return pl.pallas_call(
        functools.partial(_shift_scale_bias_kernel, tile_buckets=tile_buckets),
        grid=(steps,),
        in_specs=[
            pl.BlockSpec(memory_space=pltpu.SMEM),
            pl.BlockSpec((1, MOD_TK), lambda k: (0, k)),
            pl.BlockSpec((MOD_TK, d), lambda k: (k, 0)),
            pl.BlockSpec((MOD_TK, d), lambda k: (k, 1)),
            pl.BlockSpec((1, d), lambda k: (0, 0)),
            pl.BlockSpec((1, d), lambda k: (0, 1)),
            pl.BlockSpec(bucket.shape, lambda k: (0, 0)),
        ],
        out_specs=[
            pl.BlockSpec((1, d), lambda k: (0, 0)),
            pl.BlockSpec((1, d), lambda k: (0, 0)),
            pl.BlockSpec((1, BLOCK, SPAN), lambda k: (k, 0, 0)),
        ],
        out_shape=[
            jax.ShapeDtypeStruct((1, d), jnp.float32),
            jax.ShapeDtypeStruct((1, d), jnp.float32),
            jax.ShapeDtypeStruct((N_Q_HEADS, BLOCK, SPAN), jnp.float32),
        ],
        compiler_params=pltpu.CompilerParams(
            dimension_semantics=("arbitrary",), vmem_limit_bytes=VMEM_LIMIT_BYTES),
        name="shift_scale_bias",
    )(rel_table, c_row, w_ada, w_ada, b_ada_row, b_ada_row, bucket)


def _t5_bucket(rel):
    half = N_BUCKETS // 2
    max_exact = half // 2
    assert (max_exact, MAX_DISTANCE // max_exact, half - max_exact) == (8, 16, 8)
    n = abs(rel)
    large = min(max_exact + (n * n).bit_length() - 7, half - 1)
    return (half if rel > 0 else 0) + (n if n < max_exact else large)


def _bucket_index_table():
    table = np.full((BLOCK, SPAN), MASKED_BUCKET, np.int32)
    for q in range(BLOCK):
        for t in range(SPAN):
            rel = t - WINDOW - q
            if abs(rel) <= WINDOW:
                table[q, t] = _t5_bucket(rel)
    return table


def _merge_gate_kernel(h_ref, w_ref, wa_ref, wp_ref, wout_ref, c_ref, wgate_ref, bgate_ref,
                       o_ref, wa_o_ref, wp_o_ref, wout_o_ref, gate_ref,
                       wbf_ref):
    j, i = pl.program_id(0), pl.program_id(1)

    @pl.when(i == 0)
    def _():
        wbf_ref[...] = (0.5 * w_ref[...]).astype(wbf_ref.dtype)

    @pl.when((j == 0) & (i == 0))
    def _():
        gate_ref[...] = bgate_ref[...]

    for r in range(0, h_ref.shape[0], GATE_ROWS):
        o_ref[r:r + GATE_ROWS] = jnp.dot(h_ref[r:r + GATE_ROWS], wbf_ref[...],
                                         preferred_element_type=jnp.float32).astype(o_ref.dtype)

    wa_o_ref[...] = wa_ref[...].astype(wa_o_ref.dtype)
    wp_o_ref[...] = wp_ref[...].astype(wp_o_ref.dtype)
    wout_o_ref[...] = wout_ref[...].astype(wout_o_ref.dtype)
    gate_ref[...] += jnp.sum(_silu_column(c_ref[...]) * wgate_ref[...], axis=0, keepdims=True)


def _merge_gate(h, w_merge, wa, wp, wout, c_row, w_ada, b_ada_row):
    s, d = h.shape
    n = w_merge.shape[1]
    n_i = s // GATE_TM
    steps = (n // GATE_TN) * n_i
    slab = lambda w: (w.shape[0] // steps, w.shape[1])
    step = lambda j, i: (j * n_i + i, 0)
    for w in (wa, wp, wout):
        assert w.shape[0] % (steps * BF16_SUBLANES) == 0
    assert d % (steps * SUBLANES) == 0
    const2 = lambda j, i: (0, 0)
    bf16 = jnp.bfloat16
    return pl.pallas_call(
        _merge_gate_kernel,
        grid=(n // GATE_TN, n_i),
        in_specs=[
            pl.BlockSpec((GATE_TM, d), lambda j, i: (i, 0)),
            pl.BlockSpec((d, GATE_TN), lambda j, i: (0, j)),
            pl.BlockSpec(slab(wa), step),
            pl.BlockSpec(slab(wp), step),
            pl.BlockSpec(slab(wout), step),
            pl.BlockSpec((1, d // steps), lambda j, i: (0, j * n_i + i)),
            pl.BlockSpec((d // steps, d), lambda j, i: (j * n_i + i, 2)),
            pl.BlockSpec((1, d), lambda j, i: (0, 2)),
        ],
        out_specs=[
            pl.BlockSpec((GATE_TM, GATE_TN), lambda j, i: (i, j)),
            pl.BlockSpec(slab(wa), step),
            pl.BlockSpec(slab(wp), step),
            pl.BlockSpec(slab(wout), step),
            pl.BlockSpec((1, d), const2),
        ],
        out_shape=[
            jax.ShapeDtypeStruct((s, n), bf16),
            jax.ShapeDtypeStruct(wa.shape, bf16),
            jax.ShapeDtypeStruct(wp.shape, bf16),
            jax.ShapeDtypeStruct(wout.shape, bf16),
            jax.ShapeDtypeStruct((1, d), jnp.float32),
        ],
        scratch_shapes=[pltpu.VMEM((d, GATE_TN), bf16)],
        compiler_params=pltpu.CompilerParams(
            dimension_semantics=("arbitrary", "arbitrary"), vmem_limit_bytes=VMEM_LIMIT_BYTES),
        name="merge_gate",
    )(h, w_merge, wa, wp, wout, c_row, w_ada, b_ada_row)


def _project_mix_kernel(sink_ref,
                        x_ref, shift_ref, scale_ref, g_ref, w_hbm_ref, wpool_ref, pscale_ref, bias_ref,
                        ya_ref, yp_ref, h_ref,
                        pbuf_ref, kvtail_ref, utail_ref, uext_ref, w_ref, wstage_ref, wsem_ref,
                        *, seq_len):
    s = pl.program_id(0)
    n_tok = pl.num_programs(0) - 1
    tm, d = x_ref.shape
    n_sub = tm // BLOCK
    n_blocks = seq_len // BLOCK
    aw = ATTN_WIDTH
    pw = pscale_ref.shape[1]
    q0, ga0, gp0 = Q0, GA0, U0 + pw
    n_chunks = w_ref.shape[1] // CW
    f32 = jnp.float32
    bf16 = jnp.bfloat16
    cur = s % 2
    prv = 1 - cur
    j = s - 1

    n_slots, slab_rows = wstage_ref.shape[0], wstage_ref.shape[1]
    assert n_slots * slab_rows == d and wstage_ref.shape[2] == CW

    def slab_copy(c, r):
        return pltpu.make_async_copy(
            w_hbm_ref.at[pl.ds(r * slab_rows, slab_rows), pl.ds(c * CW, CW)],
            wstage_ref.at[r], wsem_ref.at[r])

    def fetch_chunk(c):
        for r in range(n_slots):
            slab_copy(c, r).start()

    def land_chunk(c, next_c):
        for r in range(n_slots):
            slab_copy(c, r).wait()
            w_ref[r * slab_rows:(r + 1) * slab_rows, c * CW:(c + 1) * CW] = wstage_ref[r].astype(bf16)
            if next_c is not None:
                slab_copy(next_c, r).start()

    def prologue():
        gain = g_ref[...] * (1.0 + scale_ref[...])
        h_ref[...] = _adaln_rmsnorm(x_ref[...], gain, shift_ref[...]).astype(bf16)

    def is_gate_col(col):
        return ga0 <= col < ga0 + aw or gp0 <= col < gp0 + pw

    def proj_chunk(c):
        lo, hi = c * CW, (c + 1) * CW
        p = jnp.dot(h_ref[...], w_ref[:, lo:hi], preferred_element_type=f32)
        if is_gate_col(lo):
            assert is_gate_col(hi - 1)
            p = 0.5 * p
        pbuf_ref[cur, :, lo:hi] = p.astype(bf16)

    def window_rows(b, col0):
        r0, r1 = (b - 1) * BLOCK, (b + 2) * BLOCK
        parts = []
        if r0 < 0:
            parts.append(kvtail_ref[prv, :, col0 - KV0:col0 - KV0 + HEAD_DIM])
            r0 = 0
        parts.append(pbuf_ref[prv, r0:min(r1, tm), col0:col0 + HEAD_DIM])
        if r1 > tm:
            parts.append(pbuf_ref[cur, 0:r1 - tm, col0:col0 + HEAD_DIM])
        return jnp.concatenate(parts, axis=0) if len(parts) > 1 else parts[0]

    def attn_scores(b, kvh):
        blk = j * n_sub + b
        h0 = kvh * GQA_GROUP
        k = window_rows(b, KV0 + kvh * HEAD_DIM)
        v = window_rows(b, KV0 + KV_WIDTH + kvh * HEAD_DIM)
        qs = jnp.concatenate(
            [pbuf_ref[prv, b * BLOCK:(b + 1) * BLOCK,
                      q0 + (h0 + g) * HEAD_DIM:q0 + (h0 + g + 1) * HEAD_DIM]
             for g in range(GQA_GROUP)], axis=0)
        z = lax.dot_general(qs, k, (((1,), (1,)), ((), ())), preferred_element_type=f32)
        z = z + bias_ref[h0:h0 + GQA_GROUP].reshape(GQA_GROUP * BLOCK, SPAN)
        col = lax.broadcasted_iota(jnp.int32, (1, SPAN), 1)
        if b == 0:
            z = jnp.where(col < jnp.where(blk == 0, WINDOW, 0), NEG_INF, z)
        if b == n_sub - 1:
            z = jnp.where(col >= jnp.where(blk == n_blocks - 1, WINDOW + BLOCK, SPAN), NEG_INF, z)
        sink = jnp.concatenate(
            [jnp.full((BLOCK, 1), sink_ref[h0 + g] * INV_ATTN_SCALE, f32)
             for g in range(GQA_GROUP)], axis=0)
        m = jnp.maximum(jnp.max(z, axis=-1, keepdims=True), sink)
        p = jnp.exp2((z - m) * EXP2_SCALE)
        denom = jnp.sum(p, axis=-1, keepdims=True) + jnp.exp2((sink - m) * EXP2_SCALE)
        return p.astype(bf16), denom, v

    def attn_out(p, denom, v):
        o = jnp.dot(p, v, preferred_element_type=f32) / denom
        return [o[g * BLOCK:(g + 1) * BLOCK] for g in range(GQA_GROUP)]

    def finish_sub_block(b, heads):
        y = jnp.concatenate(heads, axis=1)
        hg = pbuf_ref[prv, b * BLOCK:(b + 1) * BLOCK, ga0:ga0 + aw].astype(f32)
        ya_ref[b * BLOCK:(b + 1) * BLOCK] = (y * _half_silu(hg)).astype(bf16)

    ext = tm + 2 * POOL_HALO

    def pool_group(gi):
        w = POOL_SIZES[gi]
        gw = pw // N_POOL_GROUPS
        c0, c1 = gi * gw, (gi + 1) * gw
        half = w // 2
        halo_row = lax.broadcasted_iota(jnp.int32, (POOL_HALO, 1), 0)
        uext_ref[0:POOL_HALO] = jnp.where(j * tm - POOL_HALO + halo_row >= 0,
                                          utail_ref[prv, :, c0:c1].astype(f32), 0.0)
        uext_ref[POOL_HALO:POOL_HALO + tm] = pbuf_ref[prv, :, U0 + c0:U0 + c1].astype(f32)
        uext_ref[POOL_HALO + tm:ext] = jnp.where(
            (j + 1) * tm + halo_row < seq_len,
            pbuf_ref[cur, 0:POOL_HALO, U0 + c0:U0 + c1].astype(f32), 0.0)
        a = uext_ref[...]
        sh = 1
        while sh < w:
            a = a + pltpu.roll(a, ext - sh, axis=0)
            sh *= 2
        off = POOL_HALO - half
        if off % SUBLANES:
            a = pltpu.roll(a, ext - off, axis=0)
            off = 0
        pos = j * tm + lax.broadcasted_iota(jnp.int32, (tm, 1), 0)
        cnt = (jnp.minimum(pos + half, seq_len) - jnp.maximum(pos - half, 0)).astype(f32)
        pooled = a[off:off + tm] / cnt - uext_ref[POOL_HALO:POOL_HALO + tm]
        mixed = jnp.dot(pooled.astype(bf16), wpool_ref[gi], preferred_element_type=f32)
        hg = pbuf_ref[prv, :, gp0 + c0:gp0 + c1].astype(f32)
        yp_ref[:, c0:c1] = (mixed * pscale_ref[:, c0:c1] * _half_silu(hg)).astype(bf16)

    def save_tails():
        kvtail_ref[cur] = pbuf_ref[prv, tm - BLOCK:tm, KV0:KV0 + 2 * KV_WIDTH]
        utail_ref[cur] = pbuf_ref[prv, tm - POOL_HALO:tm, U0:U0 + pw]

    units = [(b, kvh) for b in range(n_sub) for kvh in range(N_KV_HEADS)]
    assert KV0 % CW == 0 and (2 * KV_WIDTH) % CW == 0 and U0 % CW == 0 and pw % CW == 0
    first = list(range(KV0 // CW, (KV0 + 2 * KV_WIDTH) // CW)) + list(range(U0 // CW, (U0 + pw) // CW))
    order = first + [c for c in range(n_chunks) if c not in first]
    u_chunks_done = len(first)
    assert len(units) + 1 <= n_chunks and u_chunks_done + N_POOL_GROUPS <= n_chunks

    def mix_only():
        pending = [attn_scores(b, kvh) for b, kvh in units]
        heads = []
        for (b, kvh), scores in zip(units, pending):
            heads += attn_out(*scores)
            if kvh == N_KV_HEADS - 1:
                finish_sub_block(b, heads)
                heads = []
        for gi in range(N_POOL_GROUPS):
            pool_group(gi)
        save_tails()

    def project_first():
        fetch_chunk(order[0])
        prologue()
        for pos, c in enumerate(order):
            land_chunk(c, order[pos + 1] if pos + 1 < len(order) else None)
            proj_chunk(c)

    def project_and_mix():
        def scores_of(b):
            return [attn_scores(b, kvh) for kvh in range(N_KV_HEADS)]

        def finish(b, pending):
            heads = []
            for scores in pending:
                heads += attn_out(*scores)
            finish_sub_block(b, heads)

        prologue()
        pending = scores_of(0)
        done = 0
        pools = 0
        for pos, c in enumerate(order):
            proj_chunk(c)
            if pos % 2 == 1 and done < n_sub:
                finish(done, pending)
                done += 1
                if done < n_sub:
                    pending = scores_of(done)
            elif pos % 2 == 0 and pos >= u_chunks_done and pools < N_POOL_GROUPS:
                for _ in range(2):
                    if pools < N_POOL_GROUPS:
                        pool_group(pools)
                        pools += 1
        assert done == n_sub and pools == N_POOL_GROUPS
        save_tails()

    @pl.when(s == 0)
    def _():
        kvtail_ref[0] = jnp.zeros(kvtail_ref.shape[1:], kvtail_ref.dtype)
        utail_ref[0] = jnp.zeros(utail_ref.shape[1:], utail_ref.dtype)
        project_first()

    @pl.when((s > 0) & (s < n_tok))
    def _():
        project_and_mix()

    @pl.when(s == n_tok)
    def _():
        mix_only()


def _project_mix(x2, shift, scale, pre_g_row, w_in, wpool, pool_scale_row, bias_tbl, sink):
    s, d = x2.shape
    tm = TM
    n_tok = s // tm
    pw = pool_scale_row.shape[1]
    n_in = w_in.shape[1]
    assert s // BLOCK >= 2, "first and last attention block must differ"
    assert n_in == U0 + 2 * pw and d % (W_STAGE_SLOTS * BF16_SUBLANES) == 0
    cur_blk = lambda st: jnp.minimum(st, n_tok - 1)
    mix_blk = lambda st: jnp.maximum(st - 1, 0)
    const2 = lambda st: (0, 0)
    single = pl.Buffered(1)
    bf16 = jnp.bfloat16
    return pl.pallas_call(
        functools.partial(_project_mix_kernel, seq_len=s),
        grid=(n_tok + 1,),
        in_specs=[
            pl.BlockSpec(memory_space=pltpu.SMEM),
            pl.BlockSpec((tm, d), lambda st: (cur_blk(st), 0)),
            pl.BlockSpec((1, d), const2),
            pl.BlockSpec((1, d), const2),
            pl.BlockSpec((1, d), const2),
            pl.BlockSpec(memory_space=pl.ANY),
            pl.BlockSpec(wpool.shape, lambda st: (0, 0, 0), pipeline_mode=single),
            pl.BlockSpec((1, pw), const2),
            pl.BlockSpec(bias_tbl.shape, lambda st: (0, 0, 0), pipeline_mode=single),
        ],
        out_specs=[
            pl.BlockSpec((tm, ATTN_WIDTH), lambda st: (mix_blk(st), 0)),
            pl.BlockSpec((tm, pw), lambda st: (mix_blk(st), 0)),
            pl.BlockSpec((tm, d), lambda st: (cur_blk(st), 0)),
        ],
        out_shape=[
            jax.ShapeDtypeStruct((s, ATTN_WIDTH), bf16),
            jax.ShapeDtypeStruct((s, pw), bf16),
            jax.ShapeDtypeStruct((s, d), bf16),
        ],
        scratch_shapes=[
            pltpu.VMEM((2, tm, n_in), bf16),
            pltpu.VMEM((2, BLOCK, 2 * KV_WIDTH), bf16),
            pltpu.VMEM((2, POOL_HALO, pw), bf16),
            pltpu.VMEM((tm + 2 * POOL_HALO, pw // N_POOL_GROUPS), jnp.float32),
            pltpu.VMEM(w_in.shape, bf16),
            pltpu.VMEM((W_STAGE_SLOTS, d // W_STAGE_SLOTS, CW), jnp.float32),
            pltpu.SemaphoreType.DMA((W_STAGE_SLOTS,)),
        ],
        compiler_params=pltpu.CompilerParams(
            dimension_semantics=("arbitrary",), vmem_limit_bytes=VMEM_LIMIT_BYTES),
        name="project_mix",
    )(sink, x2, shift, scale, pre_g_row, w_in, wpool, pool_scale_row, bias_tbl)


def _merge_out_kernel(ya_ref, yp_ref, gm_ref, x_ref, gate_ref, postg_ref, bm_ref,
                      wa_hbm_ref, wp_hbm_ref, wout_hbm_ref, o_ref,
                      merged_ref, oacc_ref, inv_ref, wa_ref, wp_ref, wout_ref, wsem_ref):
    s = pl.program_id(0)
    n_tok = pl.num_programs(0) - 1
    tm, d = x_ref.shape
    n_chunks = d // CW
    f32 = jnp.float32
    weights = ((wa_hbm_ref, wa_ref), (wp_hbm_ref, wp_ref), (wout_hbm_ref, wout_ref))

    def chunk_copy(which, c):
        src, dst = weights[which]
        return pltpu.make_async_copy(src.at[:, pl.ds(c * CW, CW)], dst.at[:, pl.ds(c * CW, CW)],
                                     wsem_ref.at[which, c])

    def start_weight_copies():
        for c in range(n_chunks):
            chunk_copy(0, c).start()
            chunk_copy(1, c).start()
        for c in range(n_chunks):
            chunk_copy(2, c).start()

    def finish_previous():
        scale = gate_ref[...] * postg_ref[...]
        inv = inv_ref[...]
        for c in range(n_chunks):
            lo, hi = c * CW, (c + 1) * CW
            o_ref[:, lo:hi] = x_ref[:, lo:hi] + (oacc_ref[:, lo:hi] * inv) * scale[:, lo:hi]

    def matmuls(weights_in_flight=False):
        half_bm = 0.5 * bm_ref[...]
        for c in range(n_chunks):
            lo, hi = c * CW, (c + 1) * CW
            if weights_in_flight:
                chunk_copy(0, c).wait()
                chunk_copy(1, c).wait()
            bra = jnp.dot(ya_ref[...], wa_ref[:, lo:hi], preferred_element_type=f32)
            brp = jnp.dot(yp_ref[...], wp_ref[:, lo:hi], preferred_element_type=f32)
            g_a = 0.5 * jnp.tanh(gm_ref[:, lo:hi].astype(f32) + half_bm[:, lo:hi]) + 0.5
            g_p = 0.5 * jnp.tanh(gm_ref[:, d + lo:d + hi].astype(f32)
                                 + half_bm[:, d + lo:d + hi]) + 0.5
            merged_ref[:, lo:hi] = (g_a * bra + g_p * brp).astype(merged_ref.dtype)
        ssq = jnp.zeros((tm, 1), f32)
        for c in range(n_chunks):
            lo, hi = c * CW, (c + 1) * CW
            if weights_in_flight:
                chunk_copy(2, c).wait()
            o = jnp.dot(merged_ref[...], wout_ref[:, lo:hi], preferred_element_type=f32)
            ssq = ssq + jnp.sum(o * o, axis=-1, keepdims=True)
            oacc_ref[:, lo:hi] = o
        inv_ref[...] = lax.rsqrt(ssq * (1.0 / d) + EPS)

    @pl.when(s == 0)
    def _():
        start_weight_copies()
        matmuls(weights_in_flight=True)

    @pl.when((s > 0) & (s < n_tok))
    def _():
        finish_previous()
        matmuls()

    @pl.when(s == n_tok)
    def _():
        finish_previous()


def _merge_out(ya, yp, gm, x2, gate, post_g_row, b_merge_row, wa, wp, wout):
    s, d = x2.shape
    tm = TM
    n_tok = s // tm
    aw, pw = ya.shape[1], yp.shape[1]
    cur = lambda st: (jnp.minimum(st, n_tok - 1), 0)
    prev = lambda st: (jnp.maximum(st - 1, 0), 0)
    const2 = lambda st: (0, 0)
    return pl.pallas_call(
        _merge_out_kernel,
        grid=(n_tok + 1,),
        in_specs=[
            pl.BlockSpec((tm, aw), cur),
            pl.BlockSpec((tm, pw), cur),
            pl.BlockSpec((tm, 2 * d), cur),
            pl.BlockSpec((tm, d), prev),
            pl.BlockSpec((1, d), const2),
            pl.BlockSpec((1, d), const2),
            pl.BlockSpec((1, 2 * d), const2),
            pl.BlockSpec(memory_space=pl.ANY),
            pl.BlockSpec(memory_space=pl.ANY),
            pl.BlockSpec(memory_space=pl.ANY),
        ],
        out_specs=pl.BlockSpec((tm, d), prev),
        out_shape=jax.ShapeDtypeStruct((s, d), jnp.float32),
        scratch_shapes=[
            pltpu.VMEM((tm, d), jnp.bfloat16),
            pltpu.VMEM((tm, d), jnp.float32),
            pltpu.VMEM((tm, 1), jnp.float32),
            pltpu.VMEM(wa.shape, wa.dtype),
            pltpu.VMEM(wp.shape, wp.dtype),
            pltpu.VMEM(wout.shape, wout.dtype),
            pltpu.SemaphoreType.DMA((3, d // CW)),
        ],
        compiler_params=pltpu.CompilerParams(
            dimension_semantics=("arbitrary",), vmem_limit_bytes=VMEM_LIMIT_BYTES),
        name="merge_out",
    )(ya, yp, gm, x2, gate, post_g_row, b_merge_row, wa, wp, wout)


def _layer(x2, c, rel_table, bucket, w_ada, b_ada, pre_g, post_g, w_in, sink, w_pool, pool_scale,
           w_br_attn, w_br_pool, w_merge, b_merge, w_out):
    s, d = x2.shape
    c_row = c.reshape(1, d)
    b_ada_row = b_ada.reshape(1, -1)
    pre_g_row = pre_g.reshape(1, d)
    shift, scale, bias_tbl = _shift_scale_bias(c_row, w_ada, b_ada_row, rel_table, bucket)
    bf16 = jnp.bfloat16
    ya, yp, h = _project_mix(x2, shift, scale, pre_g_row, w_in, w_pool.astype(bf16),
                             pool_scale.reshape(1, -1), bias_tbl, sink)
    gm, wa, wp, wout, gate = _merge_gate(h, w_merge, w_br_attn, w_br_pool, w_out,
                                         c_row, w_ada, b_ada_row)
    return _merge_out(ya, yp, gm, x2, gate, post_g.reshape(1, d), b_merge.reshape(1, -1),
                      wa, wp, wout)


def kernel(x, c, rel_bias_table, w_ada, b_ada, pre_norm_g, post_norm_g, w_in, attn_sink,
           w_pool_group, pool_scale, w_branch_attn, w_branch_pool, w_merge, b_merge, w_out):
    batch, s, d = x.shape
    assert batch == 1, "kernel is written for a single sequence"
    depth = w_ada.shape[0]
    bucket = _bucket_index_table()
    x2 = x.reshape(s, d)
    for l in range(depth):
        x2 = _layer(x2, c[0], rel_bias_table, bucket, w_ada[l], b_ada[l], pre_norm_g[l],
                    post_norm_g[l], w_in[l], attn_sink[l], w_pool_group[l], pool_scale[l],
                    w_branch_attn[l], w_branch_pool[l], w_merge[l], b_merge[l], w_out[l])
    return x2.reshape(batch, s, d)
```

```python
import functools
import math

import numpy as np
import jax
import jax.numpy as jnp
from jax import lax
from jax.experimental import pallas as pl
from jax.experimental.pallas import tpu as pltpu

HEAD_DIM = 128
N_Q_HEADS = 8
N_KV_HEADS = 2
GQA_GROUP = N_Q_HEADS // N_KV_HEADS
ATTN_WIDTH = N_Q_HEADS * HEAD_DIM
KV_WIDTH = N_KV_HEADS * HEAD_DIM
WINDOW = 128
BLOCK = 128
SPAN = BLOCK + 2 * WINDOW
N_BUCKETS = 32
MAX_DISTANCE = 128
POOL_SIZES = (2, 4, 8, 16)
N_POOL_GROUPS = len(POOL_SIZES)
EPS = 1e-6
NEG_INF = -1e30
MASKED_BUCKET = N_BUCKETS
ATTN_SCALE = HEAD_DIM ** -0.5
INV_ATTN_SCALE = HEAD_DIM ** 0.5
EXP2_SCALE = ATTN_SCALE * math.log2(math.e)

SUBLANES = 8
BF16_SUBLANES = 16
VMEM_LIMIT_BYTES = 60 * 1024 * 1024

MOD_TK = 256
TM = 512
CW = 512
OUT_CW = 512
GATE_TM = 2048
GATE_ROWS = 1024
GATE_TN = 1024
POOL_HALO = BF16_SUBLANES
W_STAGE_SLOTS = 8

Q0 = 0
KV0 = Q0 + ATTN_WIDTH
GA0 = KV0 + 2 * KV_WIDTH
U0 = GA0 + ATTN_WIDTH


def _silu(v):
    return v * (1.0 / (1.0 + jnp.exp(-v)))


def _half_silu(hv):
    return hv + hv * jnp.tanh(hv)


def _silu_column(c_row):
    n = c_row.shape[1]
    on_diag = (lax.broadcasted_iota(jnp.int32, (n, n), 0)
               == lax.broadcasted_iota(jnp.int32, (n, n), 1))
    return jnp.sum(jnp.where(on_diag, _silu(c_row), 0.0), axis=1, keepdims=True)


def _adaln_rmsnorm(x, gain, shift):
    ms = jnp.mean(x * x, axis=-1, keepdims=True)
    return (x * lax.rsqrt(ms + EPS)) * gain + shift


def _shift_scale_bias_kernel(tbl_ref, c_ref, w_ref, bsh_ref, bsc_ref, bucket_ref, wpool_ref,
                             shift_ref, scale_ref, bias_ref, wpool_o_ref, *, tile_buckets):
    k = pl.program_id(0)
    d = shift_ref.shape[1]

    @pl.when(k == 0)
    def _():
        shift_ref[...] = bsh_ref[...]
        scale_ref[...] = bsc_ref[...]

    s = _silu_column(c_ref[...])
    both = jnp.sum(s * w_ref[...], axis=0, keepdims=True)
    shift_ref[...] += both[:, :d]
    scale_ref[...] += both[:, d:]

    wpool_o_ref[...] = wpool_ref[...].astype(wpool_o_ref.dtype)

    for t, buckets in enumerate(tile_buckets):
        bk = bucket_ref[:, t * BLOCK:(t + 1) * BLOCK]
        acc = jnp.full((BLOCK, BLOCK), NEG_INF, jnp.float32)
        for b in buckets:
            acc = jnp.where(bk == b, tbl_ref[b, k] * INV_ATTN_SCALE, acc)
        bias_ref[0, :, t * BLOCK:(t + 1) * BLOCK] = acc


def _shift_scale_bias(c_row, w_ada, b_ada_row, rel_table, bucket, wpool):
    d = w_ada.shape[0]
    steps = d // MOD_TK
    assert steps == N_Q_HEADS, "one bias-table head per grid step"
    wpool2 = wpool.reshape(-1, wpool.shape[-1])
    pool_slab = (wpool2.shape[0] // steps, wpool2.shape[1])
    assert wpool2.shape[0] % (steps * BF16_SUBLANES) == 0
    tile_buckets = tuple(
        tuple(int(b) for b in np.unique(bucket[:, t * BLOCK:(t + 1) * BLOCK]) if b != MASKED_BUCKET)
        for t in range(SPAN // BLOCK))
    return pl.pallas_call(
        functools.partial(_shift_scale_bias_kernel, tile_buckets=tile_buckets),
        grid=(steps,),
        in_specs=[
            pl.BlockSpec(memory_space=pltpu.SMEM),
            pl.BlockSpec((1, MOD_TK), lambda k: (0, k)),
            pl.BlockSpec((MOD_TK, 2 * d), lambda k: (k, 0)),
            pl.BlockSpec((1, d), lambda k: (0, 0)),
            pl.BlockSpec((1, d), lambda k: (0, 1)),
            pl.BlockSpec(bucket.shape, lambda k: (0, 0)),
            pl.BlockSpec(pool_slab, lambda k: (k, 0)),
        ],
        out_specs=[
            pl.BlockSpec((1, d), lambda k: (0, 0)),
            pl.BlockSpec((1, d), lambda k: (0, 0)),
            pl.BlockSpec((1, BLOCK, SPAN), lambda k: (k, 0, 0)),
            pl.BlockSpec(pool_slab, lambda k: (k, 0)),
        ],
        out_shape=[
            jax.ShapeDtypeStruct((1, d), jnp.float32),
            jax.ShapeDtypeStruct((1, d), jnp.float32),
            jax.ShapeDtypeStruct((N_Q_HEADS, BLOCK, SPAN), jnp.float32),
            jax.ShapeDtypeStruct(wpool2.shape, jnp.bfloat16),
        ],
        compiler_params=pltpu.CompilerParams(
            dimension_semantics=("arbitrary",), vmem_limit_bytes=VMEM_LIMIT_BYTES),
        name="shift_scale_bias",
    )(rel_table, c_row, w_ada, b_ada_row, b_ada_row, bucket, wpool2)


def _t5_bucket(rel):
    half = N_BUCKETS // 2
    max_exact = half // 2
    assert (max_exact, MAX_DISTANCE // max_exact, half - max_exact) == (8, 16, 8)
    n = abs(rel)
    large = min(max_exact + (n * n).bit_length() - 7, half - 1)
    return (half if rel > 0 else 0) + (n if n < max_exact else large)


def _bucket_index_table():
    table = np.full((BLOCK, SPAN), MASKED_BUCKET, np.int32)
    for q in range(BLOCK):
        for t in range(SPAN):
            rel = t - WINDOW - q
            if abs(rel) <= WINDOW:
                table[q, t] = _t5_bucket(rel)
    return table


def _merge_gate_kernel(h_ref, w_ref, wa_ref, wp_ref, wout_ref, c_ref, wgate_ref, bgate_ref,
                       o_ref, wa_o_ref, wp_o_ref, wout_o_ref, gate_ref,
                       wbf_ref):
    j, i = pl.program_id(0), pl.program_id(1)

    @pl.when(i == 0)
    def _():
        wbf_ref[...] = (0.5 * w_ref[...]).astype(wbf_ref.dtype)

    @pl.when((j == 0) & (i == 0))
    def _():
        gate_ref[...] = bgate_ref[...]

    for r in range(0, h_ref.shape[0], GATE_ROWS):
        o_ref[r:r + GATE_ROWS] = jnp.dot(h_ref[r:r + GATE_ROWS], wbf_ref[...],
                                         preferred_element_type=jnp.float32).astype(o_ref.dtype)

    wa_o_ref[...] = wa_ref[...].astype(wa_o_ref.dtype)
    wp_o_ref[...] = wp_ref[...].astype(wp_o_ref.dtype)
    wout_o_ref[...] = wout_ref[...].astype(wout_o_ref.dtype)
    gate_ref[...] += jnp.sum(_silu_column(c_ref[...]) * wgate_ref[...], axis=0, keepdims=True)


def _merge_gate(h, w_merge, wa, wp, wout, c_row, w_ada, b_ada_row):
    s, d = h.shape
    n = w_merge.shape[1]
    n_i = s // GATE_TM
    steps = (n // GATE_TN) * n_i
    slab = lambda w: (w.shape[0] // steps, w.shape[1])
    step = lambda j, i: (j * n_i + i, 0)
    for w in (wa, wp, wout):
        assert w.shape[0] % (steps * BF16_SUBLANES) == 0
    assert d % (steps * SUBLANES) == 0
    const2 = lambda j, i: (0, 0)
    bf16 = jnp.bfloat16
    return pl.pallas_call(
        _merge_gate_kernel,
        grid=(n // GATE_TN, n_i),
        in_specs=[
            pl.BlockSpec((GATE_TM, d), lambda j, i: (i, 0)),
            pl.BlockSpec((d, GATE_TN), lambda j, i: (0, j)),
            pl.BlockSpec(slab(wa), step),
            pl.BlockSpec(slab(wp), step),
            pl.BlockSpec(slab(wout), step),
            pl.BlockSpec((1, d // steps), lambda j, i: (0, j * n_i + i)),
            pl.BlockSpec((d // steps, d), lambda j, i: (j * n_i + i, 2)),
            pl.BlockSpec((1, d), lambda j, i: (0, 2)),
        ],
        out_specs=[
            pl.BlockSpec((GATE_TM, GATE_TN), lambda j, i: (i, j)),
            pl.BlockSpec(slab(wa), step),
            pl.BlockSpec(slab(wp), step),
            pl.BlockSpec(slab(wout), step),
            pl.BlockSpec((1, d), const2),
        ],
        out_shape=[
            jax.ShapeDtypeStruct((s, n), bf16),
            jax.ShapeDtypeStruct(wa.shape, bf16),
            jax.ShapeDtypeStruct(wp.shape, bf16),
            jax.ShapeDtypeStruct(wout.shape, bf16),
            jax.ShapeDtypeStruct((1, d), jnp.float32),
        ],
        scratch_shapes=[pltpu.VMEM((d, GATE_TN), bf16)],
        compiler_params=pltpu.CompilerParams(
            dimension_semantics=("arbitrary", "arbitrary"), vmem_limit_bytes=VMEM_LIMIT_BYTES),
        name="merge_gate",
    )(h, w_merge, wa, wp, wout, c_row, w_ada, b_ada_row)


def _project_mix_kernel(sink_ref,
                        x_ref, shift_ref, scale_ref, g_ref, w_hbm_ref, wpool_ref, pscale_ref, bias_ref,
                        ya_ref, yp_ref, h_ref,
                        pbuf_ref, kvtail_ref, utail_ref, uext_ref, w_ref, wstage_ref, wsem_ref,
                        *, seq_len):
    s = pl.program_id(0)
    n_tok = pl.num_programs(0) - 1
    tm, d = x_ref.shape
    n_sub = tm // BLOCK
    n_blocks = seq_len // BLOCK
    aw = ATTN_WIDTH
    pw = pscale_ref.shape[1]
    q0, ga0, gp0 = Q0, GA0, U0 + pw
    n_chunks = w_ref.shape[1] // CW
    f32 = jnp.float32
    bf16 = jnp.bfloat16
    cur = s % 2
    prv = 1 - cur
    j = s - 1

    n_slots, slab_rows = wstage_ref.shape[0], wstage_ref.shape[1]
    assert n_slots * slab_rows == d and wstage_ref.shape[2] == CW

    def slab_copy(c, r):
        return pltpu.make_async_copy(
            w_hbm_ref.at[pl.ds(r * slab_rows, slab_rows), pl.ds(c * CW, CW)],
            wstage_ref.at[r], wsem_ref.at[r])

    def fetch_chunk(c):
        for r in range(n_slots):
            slab_copy(c, r).start()

    def land_chunk(c, next_c):
        for r in range(n_slots):
            slab_copy(c, r).wait()
            w_ref[r * slab_rows:(r + 1) * slab_rows, c * CW:(c + 1) * CW] = wstage_ref[r].astype(bf16)
            if next_c is not None:
                slab_copy(next_c, r).start()

    def prologue():
        gain = g_ref[...] * (1.0 + scale_ref[...])
        h_ref[...] = _adaln_rmsnorm(x_ref[...], gain, shift_ref[...]).astype(bf16)

    def is_gate_col(col):
        return ga0 <= col < ga0 + aw or gp0 <= col < gp0 + pw

    def proj_chunk(c):
        lo, hi = c * CW, (c + 1) * CW
        p = jnp.dot(h_ref[...], w_ref[:, lo:hi], preferred_element_type=f32)
        if is_gate_col(lo):
            assert is_gate_col(hi - 1)
            p = 0.5 * p
        pbuf_ref[cur, :, lo:hi] = p.astype(bf16)

    def window_rows(b, col0):
        r0, r1 = (b - 1) * BLOCK, (b + 2) * BLOCK
        parts = []
        if r0 < 0:
            parts.append(kvtail_ref[prv, :, col0 - KV0:col0 - KV0 + HEAD_DIM])
            r0 = 0
        parts.append(pbuf_ref[prv, r0:min(r1, tm), col0:col0 + HEAD_DIM])
        if r1 > tm:
            parts.append(pbuf_ref[cur, 0:r1 - tm, col0:col0 + HEAD_DIM])
        return jnp.concatenate(parts, axis=0) if len(parts) > 1 else parts[0]

    def attn_scores(b, kvh):
        blk = j * n_sub + b
        h0 = kvh * GQA_GROUP
        k = window_rows(b, KV0 + kvh * HEAD_DIM)
        v = window_rows(b, KV0 + KV_WIDTH + kvh * HEAD_DIM)
        qs = jnp.concatenate(
            [pbuf_ref[prv, b * BLOCK:(b + 1) * BLOCK,
                      q0 + (h0 + g) * HEAD_DIM:q0 + (h0 + g + 1) * HEAD_DIM]
             for g in range(GQA_GROUP)], axis=0)
        z = lax.dot_general(qs, k, (((1,), (1,)), ((), ())), preferred_element_type=f32)
        z = z + bias_ref[h0:h0 + GQA_GROUP].reshape(GQA_GROUP * BLOCK, SPAN)
        col = lax.broadcasted_iota(jnp.int32, (1, SPAN), 1)
        if b == 0:
            z = jnp.where(col < jnp.where(blk == 0, WINDOW, 0), NEG_INF, z)
        if b == n_sub - 1:
            z = jnp.where(col >= jnp.where(blk == n_blocks - 1, WINDOW + BLOCK, SPAN), NEG_INF, z)
        sink = jnp.concatenate(
            [jnp.full((BLOCK, 1), sink_ref[h0 + g] * INV_ATTN_SCALE, f32)
             for g in range(GQA_GROUP)], axis=0)
        m = jnp.maximum(jnp.max(z, axis=-1, keepdims=True), sink)
        p = jnp.exp2((z - m) * EXP2_SCALE)
        denom = jnp.sum(p, axis=-1, keepdims=True) + jnp.exp2((sink - m) * EXP2_SCALE)
        return p.astype(bf16), denom, v

    def attn_out(p, denom, v):
        o = jnp.dot(p, v, preferred_element_type=f32) / denom
        return [o[g * BLOCK:(g + 1) * BLOCK] for g in range(GQA_GROUP)]

    def finish_sub_block(b, heads):
        y = jnp.concatenate(heads, axis=1)
        hg = pbuf_ref[prv, b * BLOCK:(b + 1) * BLOCK, ga0:ga0 + aw].astype(f32)
        ya_ref[b * BLOCK:(b + 1) * BLOCK] = (y * _half_silu(hg)).astype(bf16)

    ext = tm + 2 * POOL_HALO

    def pool_group(gi):
        w = POOL_SIZES[gi]
        gw = pw // N_POOL_GROUPS
        c0, c1 = gi * gw, (gi + 1) * gw
        half = w // 2
        halo_row = lax.broadcasted_iota(jnp.int32, (POOL_HALO, 1), 0)
        uext_ref[0:POOL_HALO] = jnp.where(j * tm - POOL_HALO + halo_row >= 0,
                                          utail_ref[prv, :, c0:c1].astype(f32), 0.0)
        uext_ref[POOL_HALO:POOL_HALO + tm] = pbuf_ref[prv, :, U0 + c0:U0 + c1].astype(f32)
        uext_ref[POOL_HALO + tm:ext] = jnp.where(
            (j + 1) * tm + halo_row < seq_len,
            pbuf_ref[cur, 0:POOL_HALO, U0 + c0:U0 + c1].astype(f32), 0.0)
        a = uext_ref[...]
        sh = 1
        while sh < w:
            a = a + pltpu.roll(a, ext - sh, axis=0)
            sh *= 2
        off = POOL_HALO - half
        if off % SUBLANES:
            a = pltpu.roll(a, ext - off, axis=0)
            off = 0
        pos = j * tm + lax.broadcasted_iota(jnp.int32, (tm, 1), 0)
        cnt = (jnp.minimum(pos + half, seq_len) - jnp.maximum(pos - half, 0)).astype(f32)
        pooled = a[off:off + tm] / cnt - uext_ref[POOL_HALO:POOL_HALO + tm]
        mixed = jnp.dot(pooled.astype(bf16), wpool_ref[gi], preferred_element_type=f32)
        hg = pbuf_ref[prv, :, gp0 + c0:gp0 + c1].astype(f32)
        yp_ref[:, c0:c1] = (mixed * pscale_ref[:, c0:c1] * _half_silu(hg)).astype(bf16)

    def save_tails():
        kvtail_ref[cur] = pbuf_ref[prv, tm - BLOCK:tm, KV0:KV0 + 2 * KV_WIDTH]
        utail_ref[cur] = pbuf_ref[prv, tm - POOL_HALO:tm, U0:U0 + pw]

    units = [(b, kvh) for b in range(n_sub) for kvh in range(N_KV_HEADS)]
    assert KV0 % CW == 0 and (2 * KV_WIDTH) % CW == 0 and U0 % CW == 0 and pw % CW == 0
    first = list(range(KV0 // CW, (KV0 + 2 * KV_WIDTH) // CW)) + list(range(U0 // CW, (U0 + pw) // CW))
    order = first + [c for c in range(n_chunks) if c not in first]
    u_chunks_done = len(first)
    assert len(units) + 1 <= n_chunks and u_chunks_done + N_POOL_GROUPS <= n_chunks

    def mix_only():
        pending = [attn_scores(b, kvh) for b, kvh in units]
        heads = []
        for (b, kvh), scores in zip(units, pending):
            heads += attn_out(*scores)
            if kvh == N_KV_HEADS - 1:
                finish_sub_block(b, heads)
                heads = []
        for gi in range(N_POOL_GROUPS):
            pool_group(gi)
        save_tails()

    def project_first():
        fetch_chunk(order[0])
        prologue()
        for pos, c in enumerate(order):
            land_chunk(c, order[pos + 1] if pos + 1 < len(order) else None)
            proj_chunk(c)

    def project_and_mix():
        prologue()
        pending = attn_scores(*units[0])
        heads = []
        for pos, c in enumerate(order):
            proj_chunk(c)
            if pos < len(units):
                heads += attn_out(*pending)
                if pos + 1 < len(units):
                    pending = attn_scores(*units[pos + 1])
                b, kvh = units[pos]
                if kvh == N_KV_HEADS - 1:
                    finish_sub_block(b, heads)
                    heads = []
            if u_chunks_done <= pos < u_chunks_done + N_POOL_GROUPS:
                pool_group(pos - u_chunks_done)
        save_tails()

    @pl.when(s == 0)
    def _():
        kvtail_ref[0] = jnp.zeros(kvtail_ref.shape[1:], kvtail_ref.dtype)
        utail_ref[0] = jnp.zeros(utail_ref.shape[1:], utail_ref.dtype)
        project_first()

    @pl.when((s > 0) & (s < n_tok))
    def _():
        project_and_mix()

    @pl.when(s == n_tok)
    def _():
        mix_only()


def _project_mix(x2, shift, scale, pre_g_row, w_in, wpool, pool_scale_row, bias_tbl, sink):
    s, d = x2.shape
    tm = TM
    n_tok = s // tm
    pw = pool_scale_row.shape[1]
    n_in = w_in.shape[1]
    assert s // BLOCK >= 2, "first and last attention block must differ"
    assert n_in == U0 + 2 * pw and d % (W_STAGE_SLOTS * BF16_SUBLANES) == 0
    cur_blk = lambda st: jnp.minimum(st, n_tok - 1)
    mix_blk = lambda st: jnp.maximum(st - 1, 0)
    const2 = lambda st: (0, 0)
    single = pl.Buffered(1)
    bf16 = jnp.bfloat16
    return pl.pallas_call(
        functools.partial(_project_mix_kernel, seq_len=s),
        grid=(n_tok + 1,),
        in_specs=[
            pl.BlockSpec(memory_space=pltpu.SMEM),
            pl.BlockSpec((tm, d), lambda st: (cur_blk(st), 0)),
            pl.BlockSpec((1, d), const2),
            pl.BlockSpec((1, d), const2),
            pl.BlockSpec((1, d), const2),
            pl.BlockSpec(memory_space=pl.ANY),
            pl.BlockSpec(wpool.shape, lambda st: (0, 0, 0), pipeline_mode=single),
            pl.BlockSpec((1, pw), const2),
            pl.BlockSpec(bias_tbl.shape, lambda st: (0, 0, 0), pipeline_mode=single),
        ],
        out_specs=[
            pl.BlockSpec((tm, ATTN_WIDTH), lambda st: (mix_blk(st), 0)),
            pl.BlockSpec((tm, pw), lambda st: (mix_blk(st), 0)),
            pl.BlockSpec((tm, d), lambda st: (cur_blk(st), 0)),
        ],
        out_shape=[
            jax.ShapeDtypeStruct((s, ATTN_WIDTH), bf16),
            jax.ShapeDtypeStruct((s, pw), bf16),
            jax.ShapeDtypeStruct((s, d), bf16),
        ],
        scratch_shapes=[
            pltpu.VMEM((2, tm, n_in), bf16),
            pltpu.VMEM((2, BLOCK, 2 * KV_WIDTH), bf16),
            pltpu.VMEM((2, POOL_HALO, pw), bf16),
            pltpu.VMEM((tm + 2 * POOL_HALO, pw // N_POOL_GROUPS), jnp.float32),
            pltpu.VMEM(w_in.shape, bf16),
            pltpu.VMEM((W_STAGE_SLOTS, d // W_STAGE_SLOTS, CW), jnp.float32),
            pltpu.SemaphoreType.DMA((W_STAGE_SLOTS,)),
        ],
        compiler_params=pltpu.CompilerParams(
            dimension_semantics=("arbitrary",), vmem_limit_bytes=VMEM_LIMIT_BYTES),
        name="project_mix",
    )(sink, x2, shift, scale, pre_g_row, w_in, wpool, pool_scale_row, bias_tbl)


def _merge_out_kernel(wa_ref, wp_ref, wout_ref, gate_ref, postg_ref, bm_ref,
                      ya_ref, yp_ref, gm_ref, x_ref, o_ref, merged_ref, oacc_ref, inv_ref):
    s = pl.program_id(0)
    n_tok = pl.num_programs(0) - 1
    tm, d = x_ref.shape
    n_chunks = d // OUT_CW
    f32 = jnp.float32

    def finish_previous():
        scale = gate_ref[...] * postg_ref[...]
        inv = inv_ref[...]
        for c in range(n_chunks):
            lo, hi = c * OUT_CW, (c + 1) * OUT_CW
            o_ref[:, lo:hi] = x_ref[:, lo:hi] + (oacc_ref[:, lo:hi] * inv) * scale[:, lo:hi]

    def matmuls():
        half_bm = 0.5 * bm_ref[...]
        for c in range(n_chunks):
            lo, hi = c * OUT_CW, (c + 1) * OUT_CW
            bra = jnp.dot(ya_ref[...], wa_ref[:, lo:hi], preferred_element_type=f32)
            brp = jnp.dot(yp_ref[...], wp_ref[:, lo:hi], preferred_element_type=f32)
            g_a = 0.5 * jnp.tanh(gm_ref[:, lo:hi].astype(f32) + half_bm[:, lo:hi]) + 0.5
            g_p = 0.5 * jnp.tanh(gm_ref[:, d + lo:d + hi].astype(f32)
                                 + half_bm[:, d + lo:d + hi]) + 0.5
            merged_ref[:, lo:hi] = (g_a * bra + g_p * brp).astype(merged_ref.dtype)
        ssq = jnp.zeros((tm, 1), f32)
        for c in range(n_chunks):
            lo, hi = c * OUT_CW, (c + 1) * OUT_CW
            o = jnp.dot(merged_ref[...], wout_ref[:, lo:hi], preferred_element_type=f32)
            ssq = ssq + jnp.sum(o * o, axis=-1, keepdims=True)
            oacc_ref[:, lo:hi] = o
        inv_ref[...] = lax.rsqrt(ssq * (1.0 / d) + EPS)

    @pl.when(s == 0)
    def _():
        matmuls()

    @pl.when((s > 0) & (s < n_tok))
    def _():
        finish_previous()
        matmuls()

    @pl.when(s == n_tok)
    def _():
        finish_previous()


def _merge_out(ya, yp, gm, x2, gate, post_g_row, b_merge_row, wa, wp, wout):
    s, d = x2.shape
    tm = TM
    n_tok = s // tm
    aw, pw = ya.shape[1], yp.shape[1]
    cur = lambda st: (jnp.minimum(st, n_tok - 1), 0)
    prev = lambda st: (jnp.maximum(st - 1, 0), 0)
    const2 = lambda st: (0, 0)
    single = pl.Buffered(1)
    return pl.pallas_call(
        _merge_out_kernel,
        grid=(n_tok + 1,),
        in_specs=[
            pl.BlockSpec((aw, d), const2, pipeline_mode=single),
            pl.BlockSpec((pw, d), const2, pipeline_mode=single),
            pl.BlockSpec((d, d), const2, pipeline_mode=single),
            pl.BlockSpec((1, d), const2),
            pl.BlockSpec((1, d), const2),
            pl.BlockSpec((1, 2 * d), const2),
            pl.BlockSpec((tm, aw), cur),
            pl.BlockSpec((tm, pw), cur),
            pl.BlockSpec((tm, 2 * d), cur),
            pl.BlockSpec((tm, d), prev),
        ],
        out_specs=pl.BlockSpec((tm, d), prev),
        out_shape=jax.ShapeDtypeStruct((s, d), jnp.float32),
        scratch_shapes=[
            pltpu.VMEM((tm, d), jnp.bfloat16),
            pltpu.VMEM((tm, d), jnp.float32),
            pltpu.VMEM((tm, 1), jnp.float32),
        ],
        compiler_params=pltpu.CompilerParams(
            dimension_semantics=("arbitrary",), vmem_limit_bytes=VMEM_LIMIT_BYTES),
        name="merge_out",
    )(wa, wp, wout, gate, post_g_row, b_merge_row, ya, yp, gm, x2)


def _layer(x2, c, rel_table, bucket, w_ada, b_ada, pre_g, post_g, w_in, sink, w_pool, pool_scale,
           w_br_attn, w_br_pool, w_merge, b_merge, w_out):
    s, d = x2.shape
    c_row = c.reshape(1, d)
    b_ada_row = b_ada.reshape(1, -1)
    pre_g_row = pre_g.reshape(1, d)
    shift, scale, bias_tbl, w_pool_bf = _shift_scale_bias(c_row, w_ada, b_ada_row, rel_table, bucket,
                                                          w_pool)
    ya, yp, h = _project_mix(x2, shift, scale, pre_g_row, w_in, w_pool_bf.reshape(w_pool.shape),
                             pool_scale.reshape(1, -1), bias_tbl, sink)
    gm, wa, wp, wout, gate = _merge_gate(h, w_merge, w_br_attn, w_br_pool, w_out,
                                         c_row, w_ada, b_ada_row)
    return _merge_out(ya, yp, gm, x2, gate, post_g.reshape(1, d), b_merge.reshape(1, -1),
                      wa, wp, wout)


def kernel(x, c, rel_bias_table, w_ada, b_ada, pre_norm_g, post_norm_g, w_in, attn_sink,
           w_pool_group, pool_scale, w_branch_attn, w_branch_pool, w_merge, b_merge, w_out):
    batch, s, d = x.shape
    assert batch == 1, "kernel is written for a single sequence"
    depth = w_ada.shape[0]
    bucket = _bucket_index_table()
    x2 = x.reshape(s, d)
    for l in range(depth):
        x2 = _layer(x2, c[0], rel_bias_table, bucket, w_ada[l], b_ada[l], pre_norm_g[l],
                    post_norm_g[l], w_in[l], attn_sink[l], w_pool_group[l], pool_scale[l],
                    w_branch_attn[l], w_branch_pool[l], w_merge[l], b_merge[l], w_out[l])
    return x2.reshape(batch, s, d)
```

```python
import functools
import math

import numpy as np
import jax
import jax.numpy as jnp
from jax import lax
from jax.experimental import pallas as pl
from jax.experimental.pallas import tpu as pltpu

HEAD_DIM = 128
N_Q_HEADS = 8
N_KV_HEADS = 2
GQA_GROUP = N_Q_HEADS // N_KV_HEADS
ATTN_WIDTH = N_Q_HEADS * HEAD_DIM
KV_WIDTH = N_KV_HEADS * HEAD_DIM
WINDOW = 128
BLOCK = 128
SPAN = BLOCK + 2 * WINDOW
N_BUCKETS = 32
MAX_DISTANCE = 128
POOL_SIZES = (2, 4, 8, 16)
N_POOL_GROUPS = len(POOL_SIZES)
EPS = 1e-6
NEG_INF = -1e30
MASKED_BUCKET = N_BUCKETS
ATTN_SCALE = HEAD_DIM ** -0.5
INV_ATTN_SCALE = HEAD_DIM ** 0.5
EXP2_SCALE = ATTN_SCALE * math.log2(math.e)

SUBLANES = 8
BF16_SUBLANES = 16
VMEM_LIMIT_BYTES = 60 * 1024 * 1024

MOD_TK = 256
TM = 512
CW = 512
GATE_TM = 2048
GATE_ROWS = 1024
GATE_TN = 1024
POOL_HALO = BF16_SUBLANES
W_STAGE_SLOTS = 8

Q0 = 0
KV0 = Q0 + ATTN_WIDTH
GA0 = KV0 + 2 * KV_WIDTH
U0 = GA0 + ATTN_WIDTH


def _silu(v):
    return v * (1.0 / (1.0 + jnp.exp(-v)))


def _half_silu(hv):
    return hv + hv * jnp.tanh(hv)


def _silu_column(c_row):
    n = c_row.shape[1]
    on_diag = (lax.broadcasted_iota(jnp.int32, (n, n), 0)
               == lax.broadcasted_iota(jnp.int32, (n, n), 1))
    return jnp.sum(jnp.where(on_diag, _silu(c_row), 0.0), axis=1, keepdims=True)


def _adaln_rmsnorm(x, gain, shift):
    ms = jnp.mean(x * x, axis=-1, keepdims=True)
    return (x * lax.rsqrt(ms + EPS)) * gain + shift


def _shift_scale_bias_kernel(tbl_ref, c_ref, w_ref, bsh_ref, bsc_ref, bucket_ref, wpool_ref,
                             shift_ref, scale_ref, bias_ref, wpool_o_ref, *, tile_buckets):
    k = pl.program_id(0)
    d = shift_ref.shape[1]

    @pl.when(k == 0)
    def _():
        shift_ref[...] = bsh_ref[...]
        scale_ref[...] = bsc_ref[...]

    s = _silu_column(c_ref[...])
    both = jnp.sum(s * w_ref[...], axis=0, keepdims=True)
    shift_ref[...] += both[:, :d]
    scale_ref[...] += both[:, d:]

    wpool_o_ref[...] = wpool_ref[...].astype(wpool_o_ref.dtype)

    for t, buckets in enumerate(tile_buckets):
        bk = bucket_ref[:, t * BLOCK:(t + 1) * BLOCK]
        acc = jnp.full((BLOCK, BLOCK), NEG_INF, jnp.float32)
        for b in buckets:
            acc = jnp.where(bk == b, tbl_ref[k, b] * INV_ATTN_SCALE, acc)
        bias_ref[0, :, t * BLOCK:(t + 1) * BLOCK] = acc


def _shift_scale_bias(c_row, w_ada, b_ada_row, rel_table, bucket, wpool):
    d = w_ada.shape[0]
    steps = d // MOD_TK
    assert steps == N_Q_HEADS, "one bias-table head per grid step"
    wpool2 = wpool.reshape(-1, wpool.shape[-1])
    pool_slab = (wpool2.shape[0] // steps, wpool2.shape[1])
    assert wpool2.shape[0] % (steps * BF16_SUBLANES) == 0
    tile_buckets = tuple(
        tuple(int(b) for b in np.unique(bucket[:, t * BLOCK:(t + 1) * BLOCK]) if b != MASKED_BUCKET)
        for t in range(SPAN // BLOCK))
    return pl.pallas_call(
        functools.partial(_shift_scale_bias_kernel, tile_buckets=tile_buckets),
        grid=(steps,),
        in_specs=[
            pl.BlockSpec(memory_space=pltpu.SMEM),
            pl.BlockSpec((1, MOD_TK), lambda k: (0, k)),
            pl.BlockSpec((MOD_TK, 2 * d), lambda k: (k, 0)),
            pl.BlockSpec((1, d), lambda k: (0, 0)),
            pl.BlockSpec((1, d), lambda k: (0, 1)),
            pl.BlockSpec(bucket.shape, lambda k: (0, 0)),
            pl.BlockSpec(pool_slab, lambda k: (k, 0)),
        ],
        out_specs=[
            pl.BlockSpec((1, d), lambda k: (0, 0)),
            pl.BlockSpec((1, d), lambda k: (0, 0)),
            pl.BlockSpec((1, BLOCK, SPAN), lambda k: (k, 0, 0)),
            pl.BlockSpec(pool_slab, lambda k: (k, 0)),
        ],
        out_shape=[
            jax.ShapeDtypeStruct((1, d), jnp.float32),
            jax.ShapeDtypeStruct((1, d), jnp.float32),
            jax.ShapeDtypeStruct((N_Q_HEADS, BLOCK, SPAN), jnp.float32),
            jax.ShapeDtypeStruct(wpool2.shape, jnp.bfloat16),
        ],
        compiler_params=pltpu.CompilerParams(
            dimension_semantics=("arbitrary",), vmem_limit_bytes=VMEM_LIMIT_BYTES),
        name="shift_scale_bias",
    )(rel_table.T, c_row, w_ada, b_ada_row, b_ada_row, bucket, wpool2)


def _t5_bucket(rel):
    half = N_BUCKETS // 2
    max_exact = half // 2
    assert (max_exact, MAX_DISTANCE // max_exact, half - max_exact) == (8, 16, 8)
    n = abs(rel)
    large = min(max_exact + (n * n).bit_length() - 7, half - 1)
    return (half if rel > 0 else 0) + (n if n < max_exact else large)


def _bucket_index_table():
    table = np.full((BLOCK, SPAN), MASKED_BUCKET, np.int32)
    for q in range(BLOCK):
        for t in range(SPAN):
            rel = t - WINDOW - q
            if abs(rel) <= WINDOW:
                table[q, t] = _t5_bucket(rel)
    return table


def _merge_gate_kernel(h_ref, w_ref, wa_ref, wp_ref, wout_ref, c_ref, wgate_ref, bgate_ref,
                       o_ref, wa_o_ref, wp_o_ref, wout_o_ref, gate_ref,
                       wbf_ref):
    j, i = pl.program_id(0), pl.program_id(1)

    @pl.when(i == 0)
    def _():
        wbf_ref[...] = (0.5 * w_ref[...]).astype(wbf_ref.dtype)

    @pl.when((j == 0) & (i == 0))
    def _():
        gate_ref[...] = bgate_ref[...]

    for r in range(0, h_ref.shape[0], GATE_ROWS):
        o_ref[r:r + GATE_ROWS] = jnp.dot(h_ref[r:r + GATE_ROWS], wbf_ref[...],
                                         preferred_element_type=jnp.float32).astype(o_ref.dtype)

    wa_o_ref[...] = wa_ref[...].astype(wa_o_ref.dtype)
    wp_o_ref[...] = wp_ref[...].astype(wp_o_ref.dtype)
    wout_o_ref[...] = wout_ref[...].astype(wout_o_ref.dtype)
    gate_ref[...] += jnp.sum(_silu_column(c_ref[...]) * wgate_ref[...], axis=0, keepdims=True)


def _merge_gate(h, w_merge, wa, wp, wout, c_row, w_ada, b_ada_row):
    s, d = h.shape
    n = w_merge.shape[1]
    n_i = s // GATE_TM
    steps = (n // GATE_TN) * n_i
    slab = lambda w: (w.shape[0] // steps, w.shape[1])
    step = lambda j, i: (j * n_i + i, 0)
    for w in (wa, wp, wout):
        assert w.shape[0] % (steps * BF16_SUBLANES) == 0
    assert d % (steps * SUBLANES) == 0
    const2 = lambda j, i: (0, 0)
    bf16 = jnp.bfloat16
    return pl.pallas_call(
        _merge_gate_kernel,
        grid=(n // GATE_TN, n_i),
        in_specs=[
            pl.BlockSpec((GATE_TM, d), lambda j, i: (i, 0)),
            pl.BlockSpec((d, GATE_TN), lambda j, i: (0, j)),
            pl.BlockSpec(slab(wa), step),
            pl.BlockSpec(slab(wp), step),
            pl.BlockSpec(slab(wout), step),
            pl.BlockSpec((1, d // steps), lambda j, i: (0, j * n_i + i)),
            pl.BlockSpec((d // steps, d), lambda j, i: (j * n_i + i, 2)),
            pl.BlockSpec((1, d), lambda j, i: (0, 2)),
        ],
        out_specs=[
            pl.BlockSpec((GATE_TM, GATE_TN), lambda j, i: (i, j)),
            pl.BlockSpec(slab(wa), step),
            pl.BlockSpec(slab(wp), step),
            pl.BlockSpec(slab(wout), step),
            pl.BlockSpec((1, d), const2),
        ],
        out_shape=[
            jax.ShapeDtypeStruct((s, n), bf16),
            jax.ShapeDtypeStruct(wa.shape, bf16),
            jax.ShapeDtypeStruct(wp.shape, bf16),
            jax.ShapeDtypeStruct(wout.shape, bf16),
            jax.ShapeDtypeStruct((1, d), jnp.float32),
        ],
        scratch_shapes=[pltpu.VMEM((d, GATE_TN), bf16)],
        compiler_params=pltpu.CompilerParams(
            dimension_semantics=("arbitrary", "arbitrary"), vmem_limit_bytes=VMEM_LIMIT_BYTES),
        name="merge_gate",
    )(h, w_merge, wa, wp, wout, c_row, w_ada, b_ada_row)


def _project_mix_kernel(sink_ref,
                        x_ref, shift_ref, scale_ref, g_ref, w_hbm_ref, wpool_ref, pscale_ref, bias_ref,
                        ya_ref, yp_ref, h_ref,
                        pbuf_ref, kvtail_ref, utail_ref, uext_ref, w_ref, wstage_ref, wsem_ref,
                        *, seq_len):
    s = pl.program_id(0)
    n_tok = pl.num_programs(0) - 1
    tm, d = x_ref.shape
    n_sub = tm // BLOCK
    n_blocks = seq_len // BLOCK
    aw = ATTN_WIDTH
    pw = pscale_ref.shape[1]
    q0, ga0, gp0 = Q0, GA0, U0 + pw
    n_chunks = w_ref.shape[1] // CW
    f32 = jnp.float32
    bf16 = jnp.bfloat16
    cur = s % 2
    prv = 1 - cur
    j = s - 1

    n_slots, slab_rows = wstage_ref.shape[0], wstage_ref.shape[1]
    assert n_slots * slab_rows == d and wstage_ref.shape[2] == CW

    def slab_copy(c, r):
        return pltpu.make_async_copy(
            w_hbm_ref.at[pl.ds(r * slab_rows, slab_rows), pl.ds(c * CW, CW)],
            wstage_ref.at[r], wsem_ref.at[r])

    def fetch_chunk(c):
        for r in range(n_slots):
            slab_copy(c, r).start()

    def land_chunk(c, next_c):
        for r in range(n_slots):
            slab_copy(c, r).wait()
            w_ref[r * slab_rows:(r + 1) * slab_rows, c * CW:(c + 1) * CW] = wstage_ref[r].astype(bf16)
            if next_c is not None:
                slab_copy(next_c, r).start()

    def prologue():
        gain = g_ref[...] * (1.0 + scale_ref[...])
        h_ref[...] = _adaln_rmsnorm(x_ref[...], gain, shift_ref[...]).astype(bf16)

    def is_gate_col(col):
        return ga0 <= col < ga0 + aw or gp0 <= col < gp0 + pw

    def proj_chunk(c):
        lo, hi = c * CW, (c + 1) * CW
        p = jnp.dot(h_ref[...], w_ref[:, lo:hi], preferred_element_type=f32)
        if is_gate_col(lo):
            assert is_gate_col(hi - 1)
            p = 0.5 * p
        pbuf_ref[cur, :, lo:hi] = p.astype(bf16)

    def window_rows(b, col0):
        r0, r1 = (b - 1) * BLOCK, (b + 2) * BLOCK
        parts = []
        if r0 < 0:
            parts.append(kvtail_ref[prv, :, col0 - KV0:col0 - KV0 + HEAD_DIM])
            r0 = 0
        parts.append(pbuf_ref[prv, r0:min(r1, tm), col0:col0 + HEAD_DIM])
        if r1 > tm:
            parts.append(pbuf_ref[cur, 0:r1 - tm, col0:col0 + HEAD_DIM])
        return jnp.concatenate(parts, axis=0) if len(parts) > 1 else parts[0]

    def attn_scores(b, kvh):
        blk = j * n_sub + b
        h0 = kvh * GQA_GROUP
        k = window_rows(b, KV0 + kvh * HEAD_DIM)
        v = window_rows(b, KV0 + KV_WIDTH + kvh * HEAD_DIM)
        qs = jnp.concatenate(
            [pbuf_ref[prv, b * BLOCK:(b + 1) * BLOCK,
                      q0 + (h0 + g) * HEAD_DIM:q0 + (h0 + g + 1) * HEAD_DIM]
             for g in range(GQA_GROUP)], axis=0)
        z = lax.dot_general(qs, k, (((1,), (1,)), ((), ())), preferred_element_type=f32)
        z = z + bias_ref[h0:h0 + GQA_GROUP].reshape(GQA_GROUP * BLOCK, SPAN)
        col = lax.broadcasted_iota(jnp.int32, (1, SPAN), 1)
        if b == 0:
            z = jnp.where(col < jnp.where(blk == 0, WINDOW, 0), NEG_INF, z)
        if b == n_sub - 1:
            z = jnp.where(col >= jnp.where(blk == n_blocks - 1, WINDOW + BLOCK, SPAN), NEG_INF, z)
        sink = jnp.concatenate(
            [jnp.full((BLOCK, 1), sink_ref[h0 + g] * INV_ATTN_SCALE, f32)
             for g in range(GQA_GROUP)], axis=0)
        m = jnp.maximum(jnp.max(z, axis=-1, keepdims=True), sink)
        p = jnp.exp2((z - m) * EXP2_SCALE)
        denom = jnp.sum(p, axis=-1, keepdims=True) + jnp.exp2((sink - m) * EXP2_SCALE)
        return p.astype(bf16), denom, v

    def attn_out(p, denom, v):
        o = jnp.dot(p, v, preferred_element_type=f32) / denom
        return [o[g * BLOCK:(g + 1) * BLOCK] for g in range(GQA_GROUP)]

    def finish_sub_block(b, heads):
        y = jnp.concatenate(heads, axis=1)
        hg = pbuf_ref[prv, b * BLOCK:(b + 1) * BLOCK, ga0:ga0 + aw].astype(f32)
        ya_ref[b * BLOCK:(b + 1) * BLOCK] = (y * _half_silu(hg)).astype(bf16)

    ext = tm + 2 * POOL_HALO

    def pool_group(gi):
        w = POOL_SIZES[gi]
        gw = pw // N_POOL_GROUPS
        c0, c1 = gi * gw, (gi + 1) * gw
        half = w // 2
        halo_row = lax.broadcasted_iota(jnp.int32, (POOL_HALO, 1), 0)
        uext_ref[0:POOL_HALO] = jnp.where(j * tm - POOL_HALO + halo_row >= 0,
                                          utail_ref[prv, :, c0:c1].astype(f32), 0.0)
        uext_ref[POOL_HALO:POOL_HALO + tm] = pbuf_ref[prv, :, U0 + c0:U0 + c1].astype(f32)
        uext_ref[POOL_HALO + tm:ext] = jnp.where(
            (j + 1) * tm + halo_row < seq_len,
            pbuf_ref[cur, 0:POOL_HALO, U0 + c0:U0 + c1].astype(f32), 0.0)
        a = uext_ref[...]
        sh = 1
        while sh < w:
            a = a + pltpu.roll(a, ext - sh, axis=0)
            sh *= 2
        off = POOL_HALO - half
        if off % SUBLANES:
            a = pltpu.roll(a, ext - off, axis=0)
            off = 0
        pos = j * tm + lax.broadcasted_iota(jnp.int32, (tm, 1), 0)
        cnt = (jnp.minimum(pos + half, seq_len) - jnp.maximum(pos - half, 0)).astype(f32)
        pooled = a[off:off + tm] / cnt - uext_ref[POOL_HALO:POOL_HALO + tm]
        mixed = jnp.dot(pooled.astype(bf16), wpool_ref[gi], preferred_element_type=f32)
        hg = pbuf_ref[prv, :, gp0 + c0:gp0 + c1].astype(f32)
        yp_ref[:, c0:c1] = (mixed * pscale_ref[:, c0:c1] * _half_silu(hg)).astype(bf16)

    def save_tails():
        kvtail_ref[cur] = pbuf_ref[prv, tm - BLOCK:tm, KV0:KV0 + 2 * KV_WIDTH]
        utail_ref[cur] = pbuf_ref[prv, tm - POOL_HALO:tm, U0:U0 + pw]

    units = [(b, kvh) for b in range(n_sub) for kvh in range(N_KV_HEADS)]
    assert KV0 % CW == 0 and (2 * KV_WIDTH) % CW == 0 and U0 % CW == 0 and pw % CW == 0
    first = list(range(KV0 // CW, (KV0 + 2 * KV_WIDTH) // CW)) + list(range(U0 // CW, (U0 + pw) // CW))
    order = first + [c for c in range(n_chunks) if c not in first]
    u_chunks_done = len(first)
    assert len(units) + 1 <= n_chunks and u_chunks_done + N_POOL_GROUPS <= n_chunks

    def mix_only():
        pending = [attn_scores(b, kvh) for b, kvh in units]
        heads = []
        for (b, kvh), scores in zip(units, pending):
            heads += attn_out(*scores)
            if kvh == N_KV_HEADS - 1:
                finish_sub_block(b, heads)
                heads = []
        for gi in range(N_POOL_GROUPS):
            pool_group(gi)
        save_tails()

    def project_first():
        fetch_chunk(order[0])
        prologue()
        for pos, c in enumerate(order):
            land_chunk(c, order[pos + 1] if pos + 1 < len(order) else None)
            proj_chunk(c)

    def project_and_mix():
        prologue()
        pending = attn_scores(*units[0])
        heads = []
        for pos, c in enumerate(order):
            proj_chunk(c)
            if pos < len(units):
                heads += attn_out(*pending)
                if pos + 1 < len(units):
                    pending = attn_scores(*units[pos + 1])
                b, kvh = units[pos]
                if kvh == N_KV_HEADS - 1:
                    finish_sub_block(b, heads)
                    heads = []
            if u_chunks_done <= pos < u_chunks_done + N_POOL_GROUPS:
                pool_group(pos - u_chunks_done)
        save_tails()

    @pl.when(s == 0)
    def _():
        kvtail_ref[0] = jnp.zeros(kvtail_ref.shape[1:], kvtail_ref.dtype)
        utail_ref[0] = jnp.zeros(utail_ref.shape[1:], utail_ref.dtype)
        project_first()

    @pl.when((s > 0) & (s < n_tok))
    def _():
        project_and_mix()

    @pl.when(s == n_tok)
    def _():
        mix_only()


def _project_mix(x2, shift, scale, pre_g_row, w_in, wpool, pool_scale_row, bias_tbl, sink):
    s, d = x2.shape
    tm = TM
    n_tok = s // tm
    pw = pool_scale_row.shape[1]
    n_in = w_in.shape[1]
    assert s // BLOCK >= 2, "first and last attention block must differ"
    assert n_in == U0 + 2 * pw and d % (W_STAGE_SLOTS * BF16_SUBLANES) == 0
    cur_blk = lambda st: jnp.minimum(st, n_tok - 1)
    mix_blk = lambda st: jnp.maximum(st - 1, 0)
    const2 = lambda st: (0, 0)
    single = pl.Buffered(1)
    bf16 = jnp.bfloat16
    return pl.pallas_call(
        functools.partial(_project_mix_kernel, seq_len=s),
        grid=(n_tok + 1,),
        in_specs=[
            pl.BlockSpec(memory_space=pltpu.SMEM),
            pl.BlockSpec((tm, d), lambda st: (cur_blk(st), 0)),
            pl.BlockSpec((1, d), const2),
            pl.BlockSpec((1, d), const2),
            pl.BlockSpec((1, d), const2),
            pl.BlockSpec(memory_space=pl.ANY),
            pl.BlockSpec(wpool.shape, lambda st: (0, 0, 0), pipeline_mode=single),
            pl.BlockSpec((1, pw), const2),
            pl.BlockSpec(bias_tbl.shape, lambda st: (0, 0, 0), pipeline_mode=single),
        ],
        out_specs=[
            pl.BlockSpec((tm, ATTN_WIDTH), lambda st: (mix_blk(st), 0)),
            pl.BlockSpec((tm, pw), lambda st: (mix_blk(st), 0)),
            pl.BlockSpec((tm, d), lambda st: (cur_blk(st), 0)),
        ],
        out_shape=[
            jax.ShapeDtypeStruct((s, ATTN_WIDTH), bf16),
            jax.ShapeDtypeStruct((s, pw), bf16),
            jax.ShapeDtypeStruct((s, d), bf16),
        ],
        scratch_shapes=[
            pltpu.VMEM((2, tm, n_in), bf16),
            pltpu.VMEM((2, BLOCK, 2 * KV_WIDTH), bf16),
            pltpu.VMEM((2, POOL_HALO, pw), bf16),
            pltpu.VMEM((tm + 2 * POOL_HALO, pw // N_POOL_GROUPS), jnp.float32),
            pltpu.VMEM(w_in.shape, bf16),
            pltpu.VMEM((W_STAGE_SLOTS, d // W_STAGE_SLOTS, CW), jnp.float32),
            pltpu.SemaphoreType.DMA((W_STAGE_SLOTS,)),
        ],
        compiler_params=pltpu.CompilerParams(
            dimension_semantics=("arbitrary",), vmem_limit_bytes=VMEM_LIMIT_BYTES),
        name="project_mix",
    )(sink, x2, shift, scale, pre_g_row, w_in, wpool, pool_scale_row, bias_tbl)


def _merge_out_kernel(ya_ref, yp_ref, gm_ref, x_ref, gate_ref, postg_ref, bm_ref,
                      wa_ref, wp_ref, wout_ref, o_ref, merged_ref, oacc_ref, inv_ref):
    s = pl.program_id(0)
    n_tok = pl.num_programs(0) - 1
    tm, d = x_ref.shape
    n_chunks = d // CW
    f32 = jnp.float32

    def finish_previous():
        scale = gate_ref[...] * postg_ref[...]
        inv = inv_ref[...]
        for c in range(n_chunks):
            lo, hi = c * CW, (c + 1) * CW
            o_ref[:, lo:hi] = x_ref[:, lo:hi] + (oacc_ref[:, lo:hi] * inv) * scale[:, lo:hi]

    def matmuls():
        half_bm = 0.5 * bm_ref[...]
        for c in range(n_chunks):
            lo, hi = c * CW, (c + 1) * CW
            bra = jnp.dot(ya_ref[...], wa_ref[:, lo:hi], preferred_element_type=f32)
            brp = jnp.dot(yp_ref[...], wp_ref[:, lo:hi], preferred_element_type=f32)
            g_a = 0.5 * jnp.tanh(gm_ref[:, lo:hi].astype(f32) + half_bm[:, lo:hi]) + 0.5
            g_p = 0.5 * jnp.tanh(gm_ref[:, d + lo:d + hi].astype(f32)
                                 + half_bm[:, d + lo:d + hi]) + 0.5
            merged_ref[:, lo:hi] = (g_a * bra + g_p * brp).astype(merged_ref.dtype)
        ssq = jnp.zeros((tm, 1), f32)
        for c in range(n_chunks):
            lo, hi = c * CW, (c + 1) * CW
            o = jnp.dot(merged_ref[...], wout_ref[:, lo:hi], preferred_element_type=f32)
            ssq = ssq + jnp.sum(o * o, axis=-1, keepdims=True)
            oacc_ref[:, lo:hi] = o
        inv_ref[...] = lax.rsqrt(ssq * (1.0 / d) + EPS)

    @pl.when(s == 0)
    def _():
        matmuls()

    @pl.when((s > 0) & (s < n_tok))
    def _():
        finish_previous()
        matmuls()

    @pl.when(s == n_tok)
    def _():
        finish_previous()


def _merge_out(ya, yp, gm, x2, gate, post_g_row, b_merge_row, wa, wp, wout):
    s, d = x2.shape
    tm = TM
    n_tok = s // tm
    aw, pw = ya.shape[1], yp.shape[1]
    cur = lambda st: (jnp.minimum(st, n_tok - 1), 0)
    prev = lambda st: (jnp.maximum(st - 1, 0), 0)
    const2 = lambda st: (0, 0)
    single = pl.Buffered(1)
    return pl.pallas_call(
        _merge_out_kernel,
        grid=(n_tok + 1,),
        in_specs=[
            pl.BlockSpec((tm, aw), cur),
            pl.BlockSpec((tm, pw), cur),
            pl.BlockSpec((tm, 2 * d), cur),
            pl.BlockSpec((tm, d), prev),
            pl.BlockSpec((1, d), const2),
            pl.BlockSpec((1, d), const2),
            pl.BlockSpec((1, 2 * d), const2),
            pl.BlockSpec((aw, d), const2, pipeline_mode=single),
            pl.BlockSpec((pw, d), const2, pipeline_mode=single),
            pl.BlockSpec((d, d), const2, pipeline_mode=single),
        ],
        out_specs=pl.BlockSpec((tm, d), prev),
        out_shape=jax.ShapeDtypeStruct((s, d), jnp.float32),
        scratch_shapes=[
            pltpu.VMEM((tm, d), jnp.bfloat16),
            pltpu.VMEM((tm, d), jnp.float32),
            pltpu.VMEM((tm, 1), jnp.float32),
        ],
        compiler_params=pltpu.CompilerParams(
            dimension_semantics=("arbitrary",), vmem_limit_bytes=VMEM_LIMIT_BYTES),
        name="merge_out",
    )(ya, yp, gm, x2, gate, post_g_row, b_merge_row, wa, wp, wout)


def _layer(x2, c, rel_table, bucket, w_ada, b_ada, pre_g, post_g, w_in, sink, w_pool, pool_scale,
           w_br_attn, w_br_pool, w_merge, b_merge, w_out):
    s, d = x2.shape
    c_row = c.reshape(1, d)
    b_ada_row = b_ada.reshape(1, -1)
    pre_g_row = pre_g.reshape(1, d)
    shift, scale, bias_tbl, w_pool_bf = _shift_scale_bias(c_row, w_ada, b_ada_row, rel_table, bucket,
                                                          w_pool)
    ya, yp, h = _project_mix(x2, shift, scale, pre_g_row, w_in, w_pool_bf.reshape(w_pool.shape),
                             pool_scale.reshape(1, -1), bias_tbl, sink)
    gm, wa, wp, wout, gate = _merge_gate(h, w_merge, w_br_attn, w_br_pool, w_out,
                                         c_row, w_ada, b_ada_row)
    return _merge_out(ya, yp, gm, x2, gate, post_g.reshape(1, d), b_merge.reshape(1, -1),
                      wa, wp, wout)


def kernel(x, c, rel_bias_table, w_ada, b_ada, pre_norm_g, post_norm_g, w_in, attn_sink,
           w_pool_group, pool_scale, w_branch_attn, w_branch_pool, w_merge, b_merge, w_out):
    batch, s, d = x.shape
    assert batch == 1, "kernel is written for a single sequence"
    depth = w_ada.shape[0]
    bucket = _bucket_index_table()
    x2 = x.reshape(s, d)
    for l in range(depth):
        x2 = _layer(x2, c[0], rel_bias_table, bucket, w_ada[l], b_ada[l], pre_norm_g[l],
                    post_norm_g[l], w_in[l], attn_sink[l], w_pool_group[l], pool_scale[l],
                    w_branch_attn[l], w_branch_pool[l], w_merge[l], b_merge[l], w_out[l])
    return x2.reshape(batch, s, d)
```

```python
import functools
import math

import numpy as np
import jax
import jax.numpy as jnp
from jax import lax
from jax.experimental import pallas as pl
from jax.experimental.pallas import tpu as pltpu

HEAD_DIM = 128
N_Q_HEADS = 8
N_KV_HEADS = 2
GQA_GROUP = N_Q_HEADS // N_KV_HEADS
ATTN_WIDTH = N_Q_HEADS * HEAD_DIM
KV_WIDTH = N_KV_HEADS * HEAD_DIM
WINDOW = 128
BLOCK = 128
SPAN = BLOCK + 2 * WINDOW
N_BUCKETS = 32
MAX_DISTANCE = 128
POOL_SIZES = (2, 4, 8, 16)
N_POOL_GROUPS = len(POOL_SIZES)
EPS = 1e-6
NEG_INF = -1e30
MASKED_BUCKET = N_BUCKETS
ATTN_SCALE = HEAD_DIM ** -0.5
INV_ATTN_SCALE = HEAD_DIM ** 0.5
EXP2_SCALE = ATTN_SCALE * math.log2(math.e)

SUBLANES = 8
BF16_SUBLANES = 16
VMEM_LIMIT_BYTES = 60 * 1024 * 1024

MOD_TK = 256
TM = 512
CW = 512
GATE_TM = 2048
GATE_ROWS = 1024
GATE_TN = 1024
POOL_HALO = BF16_SUBLANES
W_STAGE_SLOTS = 8

Q0 = 0
KV0 = Q0 + ATTN_WIDTH
GA0 = KV0 + 2 * KV_WIDTH
U0 = GA0 + ATTN_WIDTH


def _silu(v):
    return v * (1.0 / (1.0 + jnp.exp(-v)))


def _half_silu(hv):
    return hv + hv * jnp.tanh(hv)


def _silu_column(c_row):
    n = c_row.shape[1]
    on_diag = (lax.broadcasted_iota(jnp.int32, (n, n), 0)
               == lax.broadcasted_iota(jnp.int32, (n, n), 1))
    return jnp.sum(jnp.where(on_diag, _silu(c_row), 0.0), axis=1, keepdims=True)


def _adaln_rmsnorm(x, gain, shift):
    ms = jnp.mean(x * x, axis=-1, keepdims=True)
    return (x * lax.rsqrt(ms + EPS)) * gain + shift


def _shift_scale_bias_kernel(tbl_ref, c_ref, w_ref, bsh_ref, bsc_ref, bucket_ref, wpool_ref,
                             shift_ref, scale_ref, bias_ref, wpool_o_ref, *, tile_buckets):
    k = pl.program_id(0)
    d = shift_ref.shape[1]

    @pl.when(k == 0)
    def _():
        shift_ref[...] = bsh_ref[...]
        scale_ref[...] = bsc_ref[...]

    s = _silu_column(c_ref[...])
    both = jnp.sum(s * w_ref[...], axis=0, keepdims=True)
    shift_ref[...] += both[:, :d]
    scale_ref[...] += both[:, d:]

    wpool_o_ref[...] = wpool_ref[...].astype(wpool_o_ref.dtype)

    for t, buckets in enumerate(tile_buckets):
        bk = bucket_ref[:, t * BLOCK:(t + 1) * BLOCK]
        acc = jnp.full((BLOCK, BLOCK), NEG_INF, jnp.float32)
        for b in buckets:
            acc = jnp.where(bk == b, tbl_ref[k, b] * INV_ATTN_SCALE, acc)
        bias_ref[0, :, t * BLOCK:(t + 1) * BLOCK] = acc


def _shift_scale_bias(c_row, w_ada, b_ada_row, rel_table, bucket, wpool):
    d = w_ada.shape[0]
    steps = d // MOD_TK
    assert steps == N_Q_HEADS, "one bias-table head per grid step"
    wpool2 = wpool.reshape(-1, wpool.shape[-1])
    pool_slab = (wpool2.shape[0] // steps, wpool2.shape[1])
    assert wpool2.shape[0] % (steps * BF16_SUBLANES) == 0
    tile_buckets = tuple(
        tuple(int(b) for b in np.unique(bucket[:, t * BLOCK:(t + 1) * BLOCK]) if b != MASKED_BUCKET)
        for t in range(SPAN // BLOCK))
    return pl.pallas_call(
        functools.partial(_shift_scale_bias_kernel, tile_buckets=tile_buckets),
        grid=(steps,),
        in_specs=[
            pl.BlockSpec(memory_space=pltpu.SMEM),
            pl.BlockSpec((1, MOD_TK), lambda k: (0, k)),
            pl.BlockSpec((MOD_TK, 2 * d), lambda k: (k, 0)),
            pl.BlockSpec((1, d), lambda k: (0, 0)),
            pl.BlockSpec((1, d), lambda k: (0, 1)),
            pl.BlockSpec(bucket.shape, lambda k: (0, 0)),
            pl.BlockSpec(pool_slab, lambda k: (k, 0)),
        ],
        out_specs=[
            pl.BlockSpec((1, d), lambda k: (0, 0)),
            pl.BlockSpec((1, d), lambda k: (0, 0)),
            pl.BlockSpec((1, BLOCK, SPAN), lambda k: (k, 0, 0)),
            pl.BlockSpec(pool_slab, lambda k: (k, 0)),
        ],
        out_shape=[
            jax.ShapeDtypeStruct((1, d), jnp.float32),
            jax.ShapeDtypeStruct((1, d), jnp.float32),
            jax.ShapeDtypeStruct((N_Q_HEADS, BLOCK, SPAN), jnp.float32),
            jax.ShapeDtypeStruct(wpool2.shape, jnp.bfloat16),
        ],
        compiler_params=pltpu.CompilerParams(
            dimension_semantics=("arbitrary",), vmem_limit_bytes=VMEM_LIMIT_BYTES),
        name="shift_scale_bias",
    )(rel_table.T, c_row, w_ada, b_ada_row, b_ada_row, bucket, wpool2)


def _t5_bucket(rel):
    half = N_BUCKETS // 2
    max_exact = half // 2
    assert (max_exact, MAX_DISTANCE // max_exact, half - max_exact) == (8, 16, 8)
    n = abs(rel)
    large = min(max_exact + (n * n).bit_length() - 7, half - 1)
    return (half if rel > 0 else 0) + (n if n < max_exact else large)


def _bucket_index_table():
    table = np.full((BLOCK, SPAN), MASKED_BUCKET, np.int32)
    for q in range(BLOCK):
        for t in range(SPAN):
            rel = t - WINDOW - q
            if abs(rel) <= WINDOW:
                table[q, t] = _t5_bucket(rel)
    return table


def _merge_gate_kernel(h_ref, w_ref, wa_ref, wp_ref, wout_ref, c_ref, wgate_ref, bgate_ref,
                       o_ref, wa_o_ref, wp_o_ref, wout_o_ref, gate_ref,
                       wbf_ref):
    j, i = pl.program_id(0), pl.program_id(1)

    @pl.when(i == 0)
    def _():
        wbf_ref[...] = (0.5 * w_ref[...]).astype(wbf_ref.dtype)

    @pl.when((j == 0) & (i == 0))
    def _():
        gate_ref[...] = bgate_ref[...]

    for r in range(0, h_ref.shape[0], GATE_ROWS):
        o_ref[r:r + GATE_ROWS] = jnp.dot(h_ref[r:r + GATE_ROWS], wbf_ref[...],
                                         preferred_element_type=jnp.float32).astype(o_ref.dtype)

    wa_o_ref[...] = wa_ref[...].astype(wa_o_ref.dtype)
    wp_o_ref[...] = wp_ref[...].astype(wp_o_ref.dtype)
    wout_o_ref[...] = wout_ref[...].astype(wout_o_ref.dtype)
    gate_ref[...] += jnp.sum(_silu_column(c_ref[...]) * wgate_ref[...], axis=0, keepdims=True)


def _merge_gate(h, w_merge, wa, wp, wout, c_row, w_ada, b_ada_row):
    s, d = h.shape
    n = w_merge.shape[1]
    n_i = s // GATE_TM
    steps = (n // GATE_TN) * n_i
    slab = lambda w: (w.shape[0] // steps, w.shape[1])
    step = lambda j, i: (j * n_i + i, 0)
    for w in (wa, wp, wout):
        assert w.shape[0] % (steps * BF16_SUBLANES) == 0
    assert d % (steps * SUBLANES) == 0
    const2 = lambda j, i: (0, 0)
    bf16 = jnp.bfloat16
    return pl.pallas_call(
        _merge_gate_kernel,
        grid=(n // GATE_TN, n_i),
        in_specs=[
            pl.BlockSpec((GATE_TM, d), lambda j, i: (i, 0)),
            pl.BlockSpec((d, GATE_TN), lambda j, i: (0, j)),
            pl.BlockSpec(slab(wa), step),
            pl.BlockSpec(slab(wp), step),
            pl.BlockSpec(slab(wout), step),
            pl.BlockSpec((1, d // steps), lambda j, i: (0, j * n_i + i)),
            pl.BlockSpec((d // steps, d), lambda j, i: (j * n_i + i, 2)),
            pl.BlockSpec((1, d), lambda j, i: (0, 2)),
        ],
        out_specs=[
            pl.BlockSpec((GATE_TM, GATE_TN), lambda j, i: (i, j)),
            pl.BlockSpec(slab(wa), step),
            pl.BlockSpec(slab(wp), step),
            pl.BlockSpec(slab(wout), step),
            pl.BlockSpec((1, d), const2),
        ],
        out_shape=[
            jax.ShapeDtypeStruct((s, n), bf16),
            jax.ShapeDtypeStruct(wa.shape, bf16),
            jax.ShapeDtypeStruct(wp.shape, bf16),
            jax.ShapeDtypeStruct(wout.shape, bf16),
            jax.ShapeDtypeStruct((1, d), jnp.float32),
        ],
        scratch_shapes=[pltpu.VMEM((d, GATE_TN), bf16)],
        compiler_params=pltpu.CompilerParams(
            dimension_semantics=("arbitrary", "arbitrary"), vmem_limit_bytes=VMEM_LIMIT_BYTES),
        name="merge_gate",
    )(h, w_merge, wa, wp, wout, c_row, w_ada, b_ada_row)


def _project_mix_kernel(sink_ref, tbl_ref,
                        x_ref, c_ref, bsh_ref, bsc_ref, g_ref, wada_hbm_ref, w_hbm_ref, wpool_hbm_ref,
                        pscale_ref,
                        ya_ref, yp_ref, h_ref,
                        pbuf_ref, kvtail_ref, utail_ref, uext_ref, w_ref, wstage_ref, wsem_ref,
                        ss_ref, bias_ref, wpool_ref,
                        *, seq_len, tile_buckets):
    s = pl.program_id(0)
    n_tok = pl.num_programs(0) - 1
    tm, d = x_ref.shape
    n_sub = tm // BLOCK
    n_blocks = seq_len // BLOCK
    aw = ATTN_WIDTH
    pw = pscale_ref.shape[1]
    q0, ga0, gp0 = Q0, GA0, U0 + pw
    n_chunks = w_ref.shape[1] // CW
    f32 = jnp.float32
    bf16 = jnp.bfloat16
    cur = s % 2
    prv = 1 - cur
    j = s - 1

    n_slots, slab_rows = wstage_ref.shape[0], wstage_ref.shape[1]
    assert n_slots * slab_rows == d and wstage_ref.shape[2] == CW

    def slab_copy(c, r):
        return pltpu.make_async_copy(
            w_hbm_ref.at[pl.ds(r * slab_rows, slab_rows), pl.ds(c * CW, CW)],
            wstage_ref.at[r], wsem_ref.at[r])

    def land_chunk(c, next_c):
        for r in range(n_slots):
            slab_copy(c, r).wait()
            w_ref[r * slab_rows:(r + 1) * slab_rows, c * CW:(c + 1) * CW] = wstage_ref[r].astype(bf16)
            if next_c is not None:
                slab_copy(next_c, r).start()

    n_ada_rows = wada_hbm_ref.shape[0] // slab_rows
    assert 2 * d == n_slots * CW and n_ada_rows == N_Q_HEADS

    def ada_copy(rr, cc):
        return pltpu.make_async_copy(
            wada_hbm_ref.at[pl.ds(rr * slab_rows, slab_rows), pl.ds(cc * CW, CW)],
            wstage_ref.at[cc], wsem_ref.at[cc])

    def bucket_tile(t):
        row = lax.broadcasted_iota(jnp.int32, (BLOCK, BLOCK), 0)
        col = lax.broadcasted_iota(jnp.int32, (BLOCK, BLOCK), 1)
        rel = t * BLOCK + col - WINDOW - row
        n = jnp.abs(rel)
        nsq = n * n
        half, max_exact = N_BUCKETS // 2, N_BUCKETS // 4
        large = jnp.full_like(n, max_exact)
        for e in range(7, 15):
            large = large + (nsq >= 2 ** e).astype(jnp.int32)
        bkt = jnp.where(rel > 0, half, 0) + jnp.where(n < max_exact, n, jnp.minimum(large, half - 1))
        return jnp.where(n <= WINDOW, bkt, MASKED_BUCKET)

    def bias_head(hd):
        for t, buckets in enumerate(tile_buckets):
            bk = bucket_tile(t)
            acc = jnp.full((BLOCK, BLOCK), NEG_INF, f32)
            for b in buckets:
                acc = jnp.where(bk == b, tbl_ref[hd, b] * INV_ATTN_SCALE, acc)
            bias_ref[hd, :, t * BLOCK:(t + 1) * BLOCK] = acc

    def load_shift_scale_and_bias():
        ss_ref[0:1] = bsh_ref[...]
        ss_ref[1:2] = bsc_ref[...]
        for cc in range(n_slots):
            ada_copy(0, cc).start()
        for rr in range(n_ada_rows):
            s_col = _silu_column(c_ref[:, rr * slab_rows:(rr + 1) * slab_rows])
            for cc in range(n_slots):
                ada_copy(rr, cc).wait()
                part = jnp.sum(s_col * wstage_ref[cc], axis=0, keepdims=True)
                which, c0 = (cc * CW) // d, (cc * CW) % d
                ss_ref[which:which + 1, c0:c0 + CW] += part
                if rr + 1 < n_ada_rows:
                    ada_copy(rr + 1, cc).start()
                else:
                    slab_copy(order[0], cc).start()
            bias_head(rr)

    def pool_copy(g):
        gw = wpool_hbm_ref.shape[2]
        return pltpu.make_async_copy(wpool_hbm_ref.at[g], wstage_ref.at[g, :, pl.ds(0, gw)],
                                     wsem_ref.at[g])

    def load_pool_weights():
        assert wpool_hbm_ref.shape[0] <= n_slots and wpool_hbm_ref.shape[1] == slab_rows
        for g in range(wpool_hbm_ref.shape[0]):
            pool_copy(g).start()
        for g in range(wpool_hbm_ref.shape[0]):
            pool_copy(g).wait()
            wpool_ref[g] = wstage_ref[g, :, 0:wpool_hbm_ref.shape[2]].astype(bf16)

    def prologue():
        gain = g_ref[...] * (1.0 + ss_ref[1:2])
        h_ref[...] = _adaln_rmsnorm(x_ref[...], gain, ss_ref[0:1]).astype(bf16)

    def is_gate_col(col):
        return ga0 <= col < ga0 + aw or gp0 <= col < gp0 + pw

    def proj_chunk(c):
        lo, hi = c * CW, (c + 1) * CW
        p = jnp.dot(h_ref[...], w_ref[:, lo:hi], preferred_element_type=f32)
        if is_gate_col(lo):
            assert is_gate_col(hi - 1)
            p = 0.5 * p
        pbuf_ref[cur, :, lo:hi] = p.astype(bf16)

    def window_rows(b, col0):
        r0, r1 = (b - 1) * BLOCK, (b + 2) * BLOCK
        parts = []
        if r0 < 0:
            parts.append(kvtail_ref[prv, :, col0 - KV0:col0 - KV0 + HEAD_DIM])
            r0 = 0
        parts.append(pbuf_ref[prv, r0:min(r1, tm), col0:col0 + HEAD_DIM])
        if r1 > tm:
            parts.append(pbuf_ref[cur, 0:r1 - tm, col0:col0 + HEAD_DIM])
        return jnp.concatenate(parts, axis=0) if len(parts) > 1 else parts[0]

    def attn_scores(b, kvh):
        blk = j * n_sub + b
        h0 = kvh * GQA_GROUP
        k = window_rows(b, KV0 + kvh * HEAD_DIM)
        v = window_rows(b, KV0 + KV_WIDTH + kvh * HEAD_DIM)
        qs = jnp.concatenate(
            [pbuf_ref[prv, b * BLOCK:(b + 1) * BLOCK,
                      q0 + (h0 + g) * HEAD_DIM:q0 + (h0 + g + 1) * HEAD_DIM]
             for g in range(GQA_GROUP)], axis=0)
        z = lax.dot_general(qs, k, (((1,), (1,)), ((), ())), preferred_element_type=f32)
        z = z + bias_ref[h0:h0 + GQA_GROUP].reshape(GQA_GROUP * BLOCK, SPAN)
        col = lax.broadcasted_iota(jnp.int32, (1, SPAN), 1)
        if b == 0:
            z = jnp.where(col < jnp.where(blk == 0, WINDOW, 0), NEG_INF, z)
        if b == n_sub - 1:
            z = jnp.where(col >= jnp.where(blk == n_blocks - 1, WINDOW + BLOCK, SPAN), NEG_INF, z)
        sink = jnp.concatenate(
            [jnp.full((BLOCK, 1), sink_ref[h0 + g] * INV_ATTN_SCALE, f32)
             for g in range(GQA_GROUP)], axis=0)
        m = jnp.maximum(jnp.max(z, axis=-1, keepdims=True), sink)
        p = jnp.exp2((z - m) * EXP2_SCALE)
        denom = jnp.sum(p, axis=-1, keepdims=True) + jnp.exp2((sink - m) * EXP2_SCALE)
        return p.astype(bf16), denom, v

    def attn_out(p, denom, v):
        o = jnp.dot(p, v, preferred_element_type=f32) / denom
        return [o[g * BLOCK:(g + 1) * BLOCK] for g in range(GQA_GROUP)]

    def finish_sub_block(b, heads):
        y = jnp.concatenate(heads, axis=1)
        hg = pbuf_ref[prv, b * BLOCK:(b + 1) * BLOCK, ga0:ga0 + aw].astype(f32)
        ya_ref[b * BLOCK:(b + 1) * BLOCK] = (y * _half_silu(hg)).astype(bf16)

    ext = tm + 2 * POOL_HALO

    def pool_group(gi):
        w = POOL_SIZES[gi]
        gw = pw // N_POOL_GROUPS
        c0, c1 = gi * gw, (gi + 1) * gw
        half = w // 2
        halo_row = lax.broadcasted_iota(jnp.int32, (POOL_HALO, 1), 0)
        uext_ref[0:POOL_HALO] = jnp.where(j * tm - POOL_HALO + halo_row >= 0,
                                          utail_ref[prv, :, c0:c1].astype(f32), 0.0)
        uext_ref[POOL_HALO:POOL_HALO + tm] = pbuf_ref[prv, :, U0 + c0:U0 + c1].astype(f32)
        uext_ref[POOL_HALO + tm:ext] = jnp.where(
            (j + 1) * tm + halo_row < seq_len,
            pbuf_ref[cur, 0:POOL_HALO, U0 + c0:U0 + c1].astype(f32), 0.0)
        a = uext_ref[...]
        sh = 1
        while sh < w:
            a = a + pltpu.roll(a, ext - sh, axis=0)
            sh *= 2
        off = POOL_HALO - half
        if off % SUBLANES:
            a = pltpu.roll(a, ext - off, axis=0)
            off = 0
        pos = j * tm + lax.broadcasted_iota(jnp.int32, (tm, 1), 0)
        cnt = (jnp.minimum(pos + half, seq_len) - jnp.maximum(pos - half, 0)).astype(f32)
        pooled = a[off:off + tm] / cnt - uext_ref[POOL_HALO:POOL_HALO + tm]
        mixed = jnp.dot(pooled.astype(bf16), wpool_ref[gi], preferred_element_type=f32)
        hg = pbuf_ref[prv, :, gp0 + c0:gp0 + c1].astype(f32)
        yp_ref[:, c0:c1] = (mixed * pscale_ref[:, c0:c1] * _half_silu(hg)).astype(bf16)

    def save_tails():
        kvtail_ref[cur] = pbuf_ref[prv, tm - BLOCK:tm, KV0:KV0 + 2 * KV_WIDTH]
        utail_ref[cur] = pbuf_ref[prv, tm - POOL_HALO:tm, U0:U0 + pw]

    units = [(b, kvh) for b in range(n_sub) for kvh in range(N_KV_HEADS)]
    assert KV0 % CW == 0 and (2 * KV_WIDTH) % CW == 0 and U0 % CW == 0 and pw % CW == 0
    first = list(range(KV0 // CW, (KV0 + 2 * KV_WIDTH) // CW)) + list(range(U0 // CW, (U0 + pw) // CW))
    order = first + [c for c in range(n_chunks) if c not in first]
    u_chunks_done = len(first)
    assert len(units) + 1 <= n_chunks and u_chunks_done + N_POOL_GROUPS <= n_chunks

    def mix_only():
        pending = [attn_scores(b, kvh) for b, kvh in units]
        heads = []
        for (b, kvh), scores in zip(units, pending):
            heads += attn_out(*scores)
            if kvh == N_KV_HEADS - 1:
                finish_sub_block(b, heads)
                heads = []
        for gi in range(N_POOL_GROUPS):
            pool_group(gi)
        save_tails()

    def project_first():
        load_shift_scale_and_bias()
        prologue()
        for pos, c in enumerate(order):
            land_chunk(c, order[pos + 1] if pos + 1 < len(order) else None)
            proj_chunk(c)
        load_pool_weights()

    def project_and_mix():
        prologue()
        pending = attn_scores(*units[0])
        heads = []
        for pos, c in enumerate(order):
            proj_chunk(c)
            if pos < len(units):
                heads += attn_out(*pending)
                if pos + 1 < len(units):
                    pending = attn_scores(*units[pos + 1])
                b, kvh = units[pos]
                if kvh == N_KV_HEADS - 1:
                    finish_sub_block(b, heads)
                    heads = []
            if u_chunks_done <= pos < u_chunks_done + N_POOL_GROUPS:
                pool_group(pos - u_chunks_done)
        save_tails()

    @pl.when(s == 0)
    def _():
        kvtail_ref[0] = jnp.zeros(kvtail_ref.shape[1:], kvtail_ref.dtype)
        utail_ref[0] = jnp.zeros(utail_ref.shape[1:], utail_ref.dtype)
        project_first()

    @pl.when((s > 0) & (s < n_tok))
    def _():
        project_and_mix()

    @pl.when(s == n_tok)
    def _():
        mix_only()


def _project_mix(x2, c_row, b_ada_row, pre_g_row, w_ada, w_in, wpool, pool_scale_row, rel_table, sink):
    s, d = x2.shape
    tm = TM
    n_tok = s // tm
    pw = pool_scale_row.shape[1]
    n_in = w_in.shape[1]
    assert s // BLOCK >= 2, "first and last attention block must differ"
    assert n_in == U0 + 2 * pw and d % (W_STAGE_SLOTS * BF16_SUBLANES) == 0
    bucket = _bucket_index_table()
    tile_buckets = tuple(
        tuple(int(b) for b in np.unique(bucket[:, t * BLOCK:(t + 1) * BLOCK]) if b != MASKED_BUCKET)
        for t in range(SPAN // BLOCK))
    cur_blk = lambda st: jnp.minimum(st, n_tok - 1)
    mix_blk = lambda st: jnp.maximum(st - 1, 0)
    const2 = lambda st: (0, 0)
    single = pl.Buffered(1)
    bf16 = jnp.bfloat16
    return pl.pallas_call(
        functools.partial(_project_mix_kernel, seq_len=s, tile_buckets=tile_buckets),
        grid=(n_tok + 1,),
        in_specs=[
            pl.BlockSpec(memory_space=pltpu.SMEM),
            pl.BlockSpec(memory_space=pltpu.SMEM),
            pl.BlockSpec((tm, d), lambda st: (cur_blk(st), 0)),
            pl.BlockSpec((1, d), const2, pipeline_mode=single),
            pl.BlockSpec((1, d), lambda st: (0, 0), pipeline_mode=single),
            pl.BlockSpec((1, d), lambda st: (0, 1), pipeline_mode=single),
            pl.BlockSpec((1, d), const2, pipeline_mode=single),
            pl.BlockSpec(memory_space=pl.ANY),
            pl.BlockSpec(memory_space=pl.ANY),
            pl.BlockSpec(memory_space=pl.ANY),
            pl.BlockSpec((1, pw), const2, pipeline_mode=single),
        ],
        out_specs=[
            pl.BlockSpec((tm, ATTN_WIDTH), lambda st: (mix_blk(st), 0)),
            pl.BlockSpec((tm, pw), lambda st: (mix_blk(st), 0)),
            pl.BlockSpec((tm, d), lambda st: (cur_blk(st), 0)),
        ],
        out_shape=[
            jax.ShapeDtypeStruct((s, ATTN_WIDTH), bf16),
            jax.ShapeDtypeStruct((s, pw), bf16),
            jax.ShapeDtypeStruct((s, d), bf16),
        ],
        scratch_shapes=[
            pltpu.VMEM((2, tm, n_in), bf16),
            pltpu.VMEM((2, BLOCK, 2 * KV_WIDTH), bf16),
            pltpu.VMEM((2, POOL_HALO, pw), bf16),
            pltpu.VMEM((tm + 2 * POOL_HALO, pw // N_POOL_GROUPS), jnp.float32),
            pltpu.VMEM(w_in.shape, bf16),
            pltpu.VMEM((W_STAGE_SLOTS, d // W_STAGE_SLOTS, CW), jnp.float32),
            pltpu.SemaphoreType.DMA((W_STAGE_SLOTS,)),
            pltpu.VMEM((SUBLANES, d), jnp.float32),
            pltpu.VMEM((N_Q_HEADS, BLOCK, SPAN), jnp.float32),
            pltpu.VMEM(wpool.shape, bf16),
        ],
        compiler_params=pltpu.CompilerParams(
            dimension_semantics=("arbitrary",), vmem_limit_bytes=VMEM_LIMIT_BYTES),
        name="project_mix",
    )(sink, rel_table.T, x2, c_row, b_ada_row, b_ada_row, pre_g_row, w_ada, w_in, wpool,
      pool_scale_row)


def _merge_out_kernel(ya_ref, yp_ref, gm_ref, x_ref, gate_ref, postg_ref, bm_ref,
                      wa_ref, wp_ref, wout_ref, o_ref, merged_ref, oacc_ref, inv_ref):
    s = pl.program_id(0)
    n_tok = pl.num_programs(0) - 1
    tm, d = x_ref.shape
    n_chunks = d // CW
    f32 = jnp.float32

    def finish_previous():
        scale = gate_ref[...] * postg_ref[...]
        inv = inv_ref[...]
        for c in range(n_chunks):
            lo, hi = c * CW, (c + 1) * CW
            o_ref[:, lo:hi] = x_ref[:, lo:hi] + (oacc_ref[:, lo:hi] * inv) * scale[:, lo:hi]

    def matmuls():
        half_bm = 0.5 * bm_ref[...]
        for c in range(n_chunks):
            lo, hi = c * CW, (c + 1) * CW
            bra = jnp.dot(ya_ref[...], wa_ref[:, lo:hi], preferred_element_type=f32)
            brp = jnp.dot(yp_ref[...], wp_ref[:, lo:hi], preferred_element_type=f32)
            g_a = 0.5 * jnp.tanh(gm_ref[:, lo:hi].astype(f32) + half_bm[:, lo:hi]) + 0.5
            g_p = 0.5 * jnp.tanh(gm_ref[:, d + lo:d + hi].astype(f32)
                                 + half_bm[:, d + lo:d + hi]) + 0.5
            merged_ref[:, lo:hi] = (g_a * bra + g_p * brp).astype(merged_ref.dtype)
        ssq = jnp.zeros((tm, 1), f32)
        for c in range(n_chunks):
            lo, hi = c * CW, (c + 1) * CW
            o = jnp.dot(merged_ref[...], wout_ref[:, lo:hi], preferred_element_type=f32)
            ssq = ssq + jnp.sum(o * o, axis=-1, keepdims=True)
            oacc_ref[:, lo:hi] = o
        inv_ref[...] = lax.rsqrt(ssq * (1.0 / d) + EPS)

    @pl.when(s == 0)
    def _():
        matmuls()

    @pl.when((s > 0) & (s < n_tok))
    def _():
        finish_previous()
        matmuls()

    @pl.when(s == n_tok)
    def _():
        finish_previous()


def _merge_out(ya, yp, gm, x2, gate, post_g_row, b_merge_row, wa, wp, wout):
    s, d = x2.shape
    tm = TM
    n_tok = s // tm
    aw, pw = ya.shape[1], yp.shape[1]
    cur = lambda st: (jnp.minimum(st, n_tok - 1), 0)
    prev = lambda st: (jnp.maximum(st - 1, 0), 0)
    const2 = lambda st: (0, 0)
    single = pl.Buffered(1)
    return pl.pallas_call(
        _merge_out_kernel,
        grid=(n_tok + 1,),
        in_specs=[
            pl.BlockSpec((tm, aw), cur),
            pl.BlockSpec((tm, pw), cur),
            pl.BlockSpec((tm, 2 * d), cur),
            pl.BlockSpec((tm, d), prev),
            pl.BlockSpec((1, d), const2),
            pl.BlockSpec((1, d), const2),
            pl.BlockSpec((1, 2 * d), const2),
            pl.BlockSpec((aw, d), const2, pipeline_mode=single),
            pl.BlockSpec((pw, d), const2, pipeline_mode=single),
            pl.BlockSpec((d, d), const2, pipeline_mode=single),
        ],
        out_specs=pl.BlockSpec((tm, d), prev),
        out_shape=jax.ShapeDtypeStruct((s, d), jnp.float32),
        scratch_shapes=[
            pltpu.VMEM((tm, d), jnp.bfloat16),
            pltpu.VMEM((tm, d), jnp.float32),
            pltpu.VMEM((tm, 1), jnp.float32),
        ],
        compiler_params=pltpu.CompilerParams(
            dimension_semantics=("arbitrary",), vmem_limit_bytes=VMEM_LIMIT_BYTES),
        name="merge_out",
    )(ya, yp, gm, x2, gate, post_g_row, b_merge_row, wa, wp, wout)


def _layer(x2, c, rel_table, bucket, w_ada, b_ada, pre_g, post_g, w_in, sink, w_pool, pool_scale,
           w_br_attn, w_br_pool, w_merge, b_merge, w_out):
    s, d = x2.shape
    c_row = c.reshape(1, d)
    b_ada_row = b_ada.reshape(1, -1)
    pre_g_row = pre_g.reshape(1, d)
    ya, yp, h = _project_mix(x2, c_row, b_ada_row, pre_g_row, w_ada, w_in, w_pool,
                             pool_scale.reshape(1, -1), rel_table, sink)
    gm, wa, wp, wout, gate = _merge_gate(h, w_merge, w_br_attn, w_br_pool, w_out,
                                         c_row, w_ada, b_ada_row)
    return _merge_out(ya, yp, gm, x2, gate, post_g.reshape(1, d), b_merge.reshape(1, -1),
                      wa, wp, wout)


def kernel(x, c, rel_bias_table, w_ada, b_ada, pre_norm_g, post_norm_g, w_in, attn_sink,
           w_pool_group, pool_scale, w_branch_attn, w_branch_pool, w_merge, b_merge, w_out):
    batch, s, d = x.shape
    assert batch == 1, "kernel is written for a single sequence"
    depth = w_ada.shape[0]
    bucket = _bucket_index_table()
    x2 = x.reshape(s, d)
    for l in range(depth):
        x2 = _layer(x2, c[0], rel_bias_table, bucket, w_ada[l], b_ada[l], pre_norm_g[l],
                    post_norm_g[l], w_in[l], attn_sink[l], w_pool_group[l], pool_scale[l],
                    w_branch_attn[l], w_branch_pool[l], w_merge[l], b_merge[l], w_out[l])
    return x2.reshape(batch, s, d)
```

```python
import functools
import math

import numpy as np
import jax
import jax.numpy as jnp
from jax import lax
from jax.experimental import pallas as pl
from jax.experimental.pallas import tpu as pltpu

HEAD_DIM = 128
N_Q_HEADS = 8
N_KV_HEADS = 2
GQA_GROUP = N_Q_HEADS // N_KV_HEADS
ATTN_WIDTH = N_Q_HEADS * HEAD_DIM
KV_WIDTH = N_KV_HEADS * HEAD_DIM
WINDOW = 128
BLOCK = 128
SPAN = BLOCK + 2 * WINDOW
N_BUCKETS = 32
MAX_DISTANCE = 128
POOL_SIZES = (2, 4, 8, 16)
N_POOL_GROUPS = len(POOL_SIZES)
EPS = 1e-6
NEG_INF = -1e30
MASKED_BUCKET = N_BUCKETS
ATTN_SCALE = HEAD_DIM ** -0.5
LOG2E = math.log2(math.e)
EXP2_SCALE = ATTN_SCALE * LOG2E

SUBLANES = 8
BF16_SUBLANES = 16
VMEM_LIMIT_BYTES = 60 * 1024 * 1024

MOD_TK = 256
TM = 512
CW = 512
GATE_TM = 2048
GATE_ROWS = 1024
GATE_TN = 1024
POOL_HALO = BF16_SUBLANES
W_STAGE_SLOTS = 8

Q0 = 0
KV0 = Q0 + ATTN_WIDTH
GA0 = KV0 + 2 * KV_WIDTH
U0 = GA0 + ATTN_WIDTH


def _silu(v):
    return v * (1.0 / (1.0 + jnp.exp(-v)))


def _half_silu(hv):
    return hv + hv * jnp.tanh(hv)


def _silu_column(c_row):
    n = c_row.shape[1]
    on_diag = (lax.broadcasted_iota(jnp.int32, (n, n), 0)
               == lax.broadcasted_iota(jnp.int32, (n, n), 1))
    return jnp.sum(jnp.where(on_diag, _silu(c_row), 0.0), axis=1, keepdims=True)


def _adaln_rmsnorm(x, gain, shift):
    ms = jnp.mean(x * x, axis=-1, keepdims=True)
    return (x * lax.rsqrt(ms + EPS)) * gain + shift


def _shift_scale_bias_kernel(tbl_ref, c_ref, w_ref, bsh_ref, bsc_ref, bucket_ref, wpool_ref,
                             shift_ref, scale_ref, bias_ref, wpool_o_ref, *, tile_buckets):
    k = pl.program_id(0)
    d = shift_ref.shape[1]

    @pl.when(k == 0)
    def _():
        shift_ref[...] = bsh_ref[...]
        scale_ref[...] = bsc_ref[...]

    s = _silu_column(c_ref[...])
    both = jnp.sum(s * w_ref[...], axis=0, keepdims=True)
    shift_ref[...] += both[:, :d]
    scale_ref[...] += both[:, d:]

    wpool_o_ref[...] = wpool_ref[...].astype(wpool_o_ref.dtype)

    for t, buckets in enumerate(tile_buckets):
        bk = bucket_ref[:, t * BLOCK:(t + 1) * BLOCK]
        acc = jnp.full((BLOCK, BLOCK), NEG_INF, jnp.float32)
        for b in buckets:
            acc = jnp.where(bk == b, tbl_ref[k, b] * LOG2E, acc)
        bias_ref[0, :, t * BLOCK:(t + 1) * BLOCK] = acc


def _shift_scale_bias(c_row, w_ada, b_ada_row, rel_table, bucket, wpool):
    d = w_ada.shape[0]
    steps = d // MOD_TK
    assert steps == N_Q_HEADS, "one bias-table head per grid step"
    wpool2 = wpool.reshape(-1, wpool.shape[-1])
    pool_slab = (wpool2.shape[0] // steps, wpool2.shape[1])
    assert wpool2.shape[0] % (steps * BF16_SUBLANES) == 0
    tile_buckets = tuple(
        tuple(int(b) for b in np.unique(bucket[:, t * BLOCK:(t + 1) * BLOCK]) if b != MASKED_BUCKET)
        for t in range(SPAN // BLOCK))
    return pl.pallas_call(
        functools.partial(_shift_scale_bias_kernel, tile_buckets=tile_buckets),
        grid=(steps,),
        in_specs=[
            pl.BlockSpec(memory_space=pltpu.SMEM),
            pl.BlockSpec((1, MOD_TK), lambda k: (0, k)),
            pl.BlockSpec((MOD_TK, 2 * d), lambda k: (k, 0)),
            pl.BlockSpec((1, d), lambda k: (0, 0)),
            pl.BlockSpec((1, d), lambda k: (0, 1)),
            pl.BlockSpec(bucket.shape, lambda k: (0, 0)),
            pl.BlockSpec(pool_slab, lambda k: (k, 0)),
        ],
        out_specs=[
            pl.BlockSpec((1, d), lambda k: (0, 0)),
            pl.BlockSpec((1, d), lambda k: (0, 0)),
            pl.BlockSpec((1, BLOCK, SPAN), lambda k: (k, 0, 0)),
            pl.BlockSpec(pool_slab, lambda k: (k, 0)),
        ],
        out_shape=[
            jax.ShapeDtypeStruct((1, d), jnp.float32),
            jax.ShapeDtypeStruct((1, d), jnp.float32),
            jax.ShapeDtypeStruct((N_Q_HEADS, BLOCK, SPAN), jnp.float32),
            jax.ShapeDtypeStruct(wpool2.shape, jnp.bfloat16),
        ],
        compiler_params=pltpu.CompilerParams(
            dimension_semantics=("arbitrary",), vmem_limit_bytes=VMEM_LIMIT_BYTES),
        name="shift_scale_bias",
    )(rel_table.T, c_row, w_ada, b_ada_row, b_ada_row, bucket, wpool2)


def _t5_bucket(rel):
    half = N_BUCKETS // 2
    max_exact = half // 2
    assert (max_exact, MAX_DISTANCE // max_exact, half - max_exact) == (8, 16, 8)
    n = abs(rel)
    large = min(max_exact + (n * n).bit_length() - 7, half - 1)
    return (half if rel > 0 else 0) + (n if n < max_exact else large)


def _bucket_index_table():
    table = np.full((BLOCK, SPAN), MASKED_BUCKET, np.int32)
    for q in range(BLOCK):
        for t in range(SPAN):
            rel = t - WINDOW - q
            if abs(rel) <= WINDOW:
                table[q, t] = _t5_bucket(rel)
    return table


def _merge_gate_kernel(h_ref, w_ref, bm_ref, wa_ref, wp_ref, wout_ref, c_ref, wgate_ref, bgate_ref,
                       o_ref, wa_o_ref, wp_o_ref, wout_o_ref, gate_ref,
                       wbf_ref):
    j, i = pl.program_id(0), pl.program_id(1)

    @pl.when(i == 0)
    def _():
        wbf_ref[...] = (0.5 * w_ref[...]).astype(wbf_ref.dtype)

    @pl.when((j == 0) & (i == 0))
    def _():
        gate_ref[...] = bgate_ref[...]

    half_bm = 0.5 * bm_ref[...]
    for r in range(0, h_ref.shape[0], GATE_ROWS):
        half_z = jnp.dot(h_ref[r:r + GATE_ROWS], wbf_ref[...],
                         preferred_element_type=jnp.float32) + half_bm
        o_ref[r:r + GATE_ROWS] = jnp.tanh(half_z).astype(o_ref.dtype)

    wa_o_ref[...] = wa_ref[...].astype(wa_o_ref.dtype)
    wp_o_ref[...] = wp_ref[...].astype(wp_o_ref.dtype)
    wout_o_ref[...] = wout_ref[...].astype(wout_o_ref.dtype)
    gate_ref[...] += jnp.sum(_silu_column(c_ref[...]) * wgate_ref[...], axis=0, keepdims=True)


def _merge_gate(h, w_merge, b_merge_row, wa, wp, wout, c_row, w_ada, b_ada_row):
    s, d = h.shape
    n = w_merge.shape[1]
    n_i = s // GATE_TM
    steps = (n // GATE_TN) * n_i
    slab = lambda w: (w.shape[0] // steps, w.shape[1])
    step = lambda j, i: (j * n_i + i, 0)
    for w in (wa, wp, wout):
        assert w.shape[0] % (steps * BF16_SUBLANES) == 0
    assert d % (steps * SUBLANES) == 0
    const2 = lambda j, i: (0, 0)
    bf16 = jnp.bfloat16
    return pl.pallas_call(
        _merge_gate_kernel,
        grid=(n // GATE_TN, n_i),
        in_specs=[
            pl.BlockSpec((GATE_TM, d), lambda j, i: (i, 0)),
            pl.BlockSpec((d, GATE_TN), lambda j, i: (0, j)),
            pl.BlockSpec((1, GATE_TN), lambda j, i: (0, j)),
            pl.BlockSpec(slab(wa), step),
            pl.BlockSpec(slab(wp), step),
            pl.BlockSpec(slab(wout), step),
            pl.BlockSpec((1, d // steps), lambda j, i: (0, j * n_i + i)),
            pl.BlockSpec((d // steps, d), lambda j, i: (j * n_i + i, 2)),
            pl.BlockSpec((1, d), lambda j, i: (0, 2)),
        ],
        out_specs=[
            pl.BlockSpec((GATE_TM, GATE_TN), lambda j, i: (i, j)),
            pl.BlockSpec(slab(wa), step),
            pl.BlockSpec(slab(wp), step),
            pl.BlockSpec(slab(wout), step),
            pl.BlockSpec((1, d), const2),
        ],
        out_shape=[
            jax.ShapeDtypeStruct((s, n), bf16),
            jax.ShapeDtypeStruct(wa.shape, bf16),
            jax.ShapeDtypeStruct(wp.shape, bf16),
            jax.ShapeDtypeStruct(wout.shape, bf16),
            jax.ShapeDtypeStruct((1, d), jnp.float32),
        ],
        scratch_shapes=[pltpu.VMEM((d, GATE_TN), bf16)],
        compiler_params=pltpu.CompilerParams(
            dimension_semantics=("arbitrary", "arbitrary"), vmem_limit_bytes=VMEM_LIMIT_BYTES),
        name="merge_gate",
    )(h, w_merge, b_merge_row, wa, wp, wout, c_row, w_ada, b_ada_row)


def _project_mix_kernel(sink_ref,
                        x_ref, shift_ref, scale_ref, g_ref, w_hbm_ref, wpool_ref, pscale_ref, bias_ref,
                        ya_ref, yp_ref, h_ref,
                        pbuf_ref, kvtail_ref, utail_ref, uext_ref, w_ref, wstage_ref, wsem_ref,
                        *, seq_len):
    s = pl.program_id(0)
    n_tok = pl.num_programs(0) - 1
    tm, d = x_ref.shape
    n_sub = tm // BLOCK
    n_blocks = seq_len // BLOCK
    aw = ATTN_WIDTH
    pw = pscale_ref.shape[1]
    q0, ga0, gp0 = Q0, GA0, U0 + pw
    n_chunks = w_ref.shape[1] // CW
    f32 = jnp.float32
    bf16 = jnp.bfloat16
    cur = s % 2
    prv = 1 - cur
    j = s - 1

    n_slots, slab_rows = wstage_ref.shape[0], wstage_ref.shape[1]
    assert n_slots * slab_rows == d and wstage_ref.shape[2] == CW

    def slab_copy(c, r):
        return pltpu.make_async_copy(
            w_hbm_ref.at[pl.ds(r * slab_rows, slab_rows), pl.ds(c * CW, CW)],
            wstage_ref.at[r], wsem_ref.at[r])

    def fetch_chunk(c):
        for r in range(n_slots):
            slab_copy(c, r).start()

    def land_chunk(c, next_c):
        for r in range(n_slots):
            slab_copy(c, r).wait()
            w_ref[r * slab_rows:(r + 1) * slab_rows, c * CW:(c + 1) * CW] = wstage_ref[r].astype(bf16)
            if next_c is not None:
                slab_copy(next_c, r).start()

    def prologue():
        gain = g_ref[...] * (1.0 + scale_ref[...])
        h_ref[...] = _adaln_rmsnorm(x_ref[...], gain, shift_ref[...]).astype(bf16)

    def is_gate_col(col):
        return ga0 <= col < ga0 + aw or gp0 <= col < gp0 + pw

    def proj_chunk(c):
        lo, hi = c * CW, (c + 1) * CW
        p = jnp.dot(h_ref[...], w_ref[:, lo:hi], preferred_element_type=f32)
        if is_gate_col(lo):
            assert is_gate_col(hi - 1)
            p = 0.5 * p
        if q0 <= lo < q0 + aw:
            assert hi <= q0 + aw
            p = EXP2_SCALE * p
        pbuf_ref[cur, :, lo:hi] = p.astype(bf16)

    def window_rows(b, col0):
        r0, r1 = (b - 1) * BLOCK, (b + 2) * BLOCK
        parts = []
        if r0 < 0:
            parts.append(kvtail_ref[prv, :, col0 - KV0:col0 - KV0 + HEAD_DIM])
            r0 = 0
        parts.append(pbuf_ref[prv, r0:min(r1, tm), col0:col0 + HEAD_DIM])
        if r1 > tm:
            parts.append(pbuf_ref[cur, 0:r1 - tm, col0:col0 + HEAD_DIM])
        return jnp.concatenate(parts, axis=0) if len(parts) > 1 else parts[0]

    def attn_scores(b, kvh):
        blk = j * n_sub + b
        h0 = kvh * GQA_GROUP
        k = window_rows(b, KV0 + kvh * HEAD_DIM)
        v = window_rows(b, KV0 + KV_WIDTH + kvh * HEAD_DIM)
        qs = jnp.concatenate(
            [pbuf_ref[prv, b * BLOCK:(b + 1) * BLOCK,
                      q0 + (h0 + g) * HEAD_DIM:q0 + (h0 + g + 1) * HEAD_DIM]
             for g in range(GQA_GROUP)], axis=0)
        z = lax.dot_general(qs, k, (((1,), (1,)), ((), ())), preferred_element_type=f32)
        z = z + bias_ref[h0:h0 + GQA_GROUP].reshape(GQA_GROUP * BLOCK, SPAN)
        col = lax.broadcasted_iota(jnp.int32, (1, SPAN), 1)
        if b == 0:
            z = jnp.where(col < jnp.where(blk == 0, WINDOW, 0), NEG_INF, z)
        if b == n_sub - 1:
            z = jnp.where(col >= jnp.where(blk == n_blocks - 1, WINDOW + BLOCK, SPAN), NEG_INF, z)
        sink = jnp.concatenate(
            [jnp.full((BLOCK, 1), sink_ref[h0 + g] * LOG2E, f32)
             for g in range(GQA_GROUP)], axis=0)
        m = jnp.maximum(jnp.max(z, axis=-1, keepdims=True), sink)
        p = jnp.exp2(z - m)
        denom = jnp.sum(p, axis=-1, keepdims=True) + jnp.exp2(sink - m)
        return p.astype(bf16), denom, v

    def attn_out(p, denom, v):
        o = jnp.dot(p, v, preferred_element_type=f32) / denom
        return [o[g * BLOCK:(g + 1) * BLOCK] for g in range(GQA_GROUP)]

    def finish_sub_block(b, heads):
        y = jnp.concatenate(heads, axis=1)
        hg = pbuf_ref[prv, b * BLOCK:(b + 1) * BLOCK, ga0:ga0 + aw].astype(f32)
        ya_ref[b * BLOCK:(b + 1) * BLOCK] = (y * _half_silu(hg)).astype(bf16)

    ext = tm + 2 * POOL_HALO

    def pool_group(gi):
        w = POOL_SIZES[gi]
        gw = pw // N_POOL_GROUPS
        c0, c1 = gi * gw, (gi + 1) * gw
        half = w // 2
        halo_row = lax.broadcasted_iota(jnp.int32, (POOL_HALO, 1), 0)
        uext_ref[0:POOL_HALO] = jnp.where(j * tm - POOL_HALO + halo_row >= 0,
                                          utail_ref[prv, :, c0:c1].astype(f32), 0.0)
        uext_ref[POOL_HALO:POOL_HALO + tm] = pbuf_ref[prv, :, U0 + c0:U0 + c1].astype(f32)
        uext_ref[POOL_HALO + tm:ext] = jnp.where(
            (j + 1) * tm + halo_row < seq_len,
            pbuf_ref[cur, 0:POOL_HALO, U0 + c0:U0 + c1].astype(f32), 0.0)
        a = uext_ref[...]
        sh = 1
        while sh < w:
            a = a + pltpu.roll(a, ext - sh, axis=0)
            sh *= 2
        off = POOL_HALO - half
        if off % SUBLANES:
            a = pltpu.roll(a, ext - off, axis=0)
            off = 0
        pos = j * tm + lax.broadcasted_iota(jnp.int32, (tm, 1), 0)
        cnt = (jnp.minimum(pos + half, seq_len) - jnp.maximum(pos - half, 0)).astype(f32)
        pooled = a[off:off + tm] / cnt - uext_ref[POOL_HALO:POOL_HALO + tm]
        mixed = jnp.dot(pooled.astype(bf16), wpool_ref[gi], preferred_element_type=f32)
        hg = pbuf_ref[prv, :, gp0 + c0:gp0 + c1].astype(f32)
        yp_ref[:, c0:c1] = (mixed * pscale_ref[:, c0:c1] * _half_silu(hg)).astype(bf16)

    def save_tails():
        kvtail_ref[cur] = pbuf_ref[prv, tm - BLOCK:tm, KV0:KV0 + 2 * KV_WIDTH]
        utail_ref[cur] = pbuf_ref[prv, tm - POOL_HALO:tm, U0:U0 + pw]

    units = [(b, kvh) for b in range(n_sub) for kvh in range(N_KV_HEADS)]
    assert KV0 % CW == 0 and (2 * KV_WIDTH) % CW == 0 and U0 % CW == 0 and pw % CW == 0
    first = list(range(KV0 // CW, (KV0 + 2 * KV_WIDTH) // CW)) + list(range(U0 // CW, (U0 + pw) // CW))
    order = first + [c for c in range(n_chunks) if c not in first]
    u_chunks_done = len(first)
    assert len(units) + 1 <= n_chunks and u_chunks_done + N_POOL_GROUPS <= n_chunks

    def mix_only():
        pending = [attn_scores(b, kvh) for b, kvh in units]
        heads = []
        for (b, kvh), scores in zip(units, pending):
            heads += attn_out(*scores)
            if kvh == N_KV_HEADS - 1:
                finish_sub_block(b, heads)
                heads = []
        for gi in range(N_POOL_GROUPS):
            pool_group(gi)
        save_tails()

    def project_first():
        fetch_chunk(order[0])
        prologue()
        for pos, c in enumerate(order):
            land_chunk(c, order[pos + 1] if pos + 1 < len(order) else None)
            proj_chunk(c)

    def project_and_mix():
        prologue()
        pending = attn_scores(*units[0])
        heads = []
        for pos, c in enumerate(order):
            proj_chunk(c)
            if pos < len(units):
                heads += attn_out(*pending)
                if pos + 1 < len(units):
                    pending = attn_scores(*units[pos + 1])
                b, kvh = units[pos]
                if kvh == N_KV_HEADS - 1:
                    finish_sub_block(b, heads)
                    heads = []
            if u_chunks_done <= pos < u_chunks_done + N_POOL_GROUPS:
                pool_group(pos - u_chunks_done)
        save_tails()

    @pl.when(s == 0)
    def _():
        kvtail_ref[0] = jnp.zeros(kvtail_ref.shape[1:], kvtail_ref.dtype)
        utail_ref[0] = jnp.zeros(utail_ref.shape[1:], utail_ref.dtype)
        project_first()

    @pl.when((s > 0) & (s < n_tok))
    def _():
        project_and_mix()

    @pl.when(s == n_tok)
    def _():
        mix_only()


def _project_mix(x2, shift, scale, pre_g_row, w_in, wpool, pool_scale_row, bias_tbl, sink):
    s, d = x2.shape
    tm = TM
    n_tok = s // tm
    pw = pool_scale_row.shape[1]
    n_in = w_in.shape[1]
    assert s // BLOCK >= 2, "first and last attention block must differ"
    assert n_in == U0 + 2 * pw and d % (W_STAGE_SLOTS * BF16_SUBLANES) == 0
    cur_blk = lambda st: jnp.minimum(st, n_tok - 1)
    mix_blk = lambda st: jnp.maximum(st - 1, 0)
    const2 = lambda st: (0, 0)
    single = pl.Buffered(1)
    bf16 = jnp.bfloat16
    return pl.pallas_call(
        functools.partial(_project_mix_kernel, seq_len=s),
        grid=(n_tok + 1,),
        in_specs=[
            pl.BlockSpec(memory_space=pltpu.SMEM),
            pl.BlockSpec((tm, d), lambda st: (cur_blk(st), 0)),
            pl.BlockSpec((1, d), const2),
            pl.BlockSpec((1, d), const2),
            pl.BlockSpec((1, d), const2),
            pl.BlockSpec(memory_space=pl.ANY),
            pl.BlockSpec(wpool.shape, lambda st: (0, 0, 0), pipeline_mode=single),
            pl.BlockSpec((1, pw), const2),
            pl.BlockSpec(bias_tbl.shape, lambda st: (0, 0, 0), pipeline_mode=single),
        ],
        out_specs=[
            pl.BlockSpec((tm, ATTN_WIDTH), lambda st: (mix_blk(st), 0)),
            pl.BlockSpec((tm, pw), lambda st: (mix_blk(st), 0)),
            pl.BlockSpec((tm, d), lambda st: (cur_blk(st), 0)),
        ],
        out_shape=[
            jax.ShapeDtypeStruct((s, ATTN_WIDTH), bf16),
            jax.ShapeDtypeStruct((s, pw), bf16),
            jax.ShapeDtypeStruct((s, d), bf16),
        ],
        scratch_shapes=[
            pltpu.VMEM((2, tm, n_in), bf16),
            pltpu.VMEM((2, BLOCK, 2 * KV_WIDTH), bf16),
            pltpu.VMEM((2, POOL_HALO, pw), bf16),
            pltpu.VMEM((tm + 2 * POOL_HALO, pw // N_POOL_GROUPS), jnp.float32),
            pltpu.VMEM(w_in.shape, bf16),
            pltpu.VMEM((W_STAGE_SLOTS, d // W_STAGE_SLOTS, CW), jnp.float32),
            pltpu.SemaphoreType.DMA((W_STAGE_SLOTS,)),
        ],
        compiler_params=pltpu.CompilerParams(
            dimension_semantics=("arbitrary",), vmem_limit_bytes=VMEM_LIMIT_BYTES),
        name="project_mix",
    )(sink, x2, shift, scale, pre_g_row, w_in, wpool, pool_scale_row, bias_tbl)


def _merge_out_kernel(ya_ref, yp_ref, tg_ref, x_ref, gate_ref, postg_ref,
                      wa_ref, wp_ref, wout_ref, o_ref, merged_ref, oacc_ref, inv_ref):
    s = pl.program_id(0)
    n_tok = pl.num_programs(0) - 1
    tm, d = x_ref.shape
    n_chunks = d // CW
    f32 = jnp.float32

    def finish_previous():
        scale = gate_ref[...] * postg_ref[...]
        inv = inv_ref[...]
        for c in range(n_chunks):
            lo, hi = c * CW, (c + 1) * CW
            o_ref[:, lo:hi] = x_ref[:, lo:hi] + (oacc_ref[:, lo:hi] * inv) * scale[:, lo:hi]

    def matmuls():
        for c in range(n_chunks):
            lo, hi = c * CW, (c + 1) * CW
            bra = jnp.dot(ya_ref[...], wa_ref[:, lo:hi], preferred_element_type=f32)
            brp = jnp.dot(yp_ref[...], wp_ref[:, lo:hi], preferred_element_type=f32)
            g_a = 0.5 * tg_ref[:, lo:hi].astype(f32) + 0.5
            g_p = 0.5 * tg_ref[:, d + lo:d + hi].astype(f32) + 0.5
            merged_ref[:, lo:hi] = (g_a * bra + g_p * brp).astype(merged_ref.dtype)
        ssq = jnp.zeros((tm, 1), f32)
        for c in range(n_chunks):
            lo, hi = c * CW, (c + 1) * CW
            o = jnp.dot(merged_ref[...], wout_ref[:, lo:hi], preferred_element_type=f32)
            ssq = ssq + jnp.sum(o * o, axis=-1, keepdims=True)
            oacc_ref[:, lo:hi] = o
        inv_ref[...] = lax.rsqrt(ssq * (1.0 / d) + EPS)

    @pl.when(s == 0)
    def _():
        matmuls()

    @pl.when((s > 0) & (s < n_tok))
    def _():
        finish_previous()
        matmuls()

    @pl.when(s == n_tok)
    def _():
        finish_previous()


def _merge_out(ya, yp, tg, x2, gate, post_g_row, wa, wp, wout):
    s, d = x2.shape
    tm = TM
    n_tok = s // tm
    aw, pw = ya.shape[1], yp.shape[1]
    cur = lambda st: (jnp.minimum(st, n_tok - 1), 0)
    prev = lambda st: (jnp.maximum(st - 1, 0), 0)
    const2 = lambda st: (0, 0)
    single = pl.Buffered(1)
    return pl.pallas_call(
        _merge_out_kernel,
        grid=(n_tok + 1,),
        in_specs=[
            pl.BlockSpec((tm, aw), cur),
            pl.BlockSpec((tm, pw), cur),
            pl.BlockSpec((tm, 2 * d), cur),
            pl.BlockSpec((tm, d), prev),
            pl.BlockSpec((1, d), const2),
            pl.BlockSpec((1, d), const2),
            pl.BlockSpec((aw, d), const2, pipeline_mode=single),
            pl.BlockSpec((pw, d), const2, pipeline_mode=single),
            pl.BlockSpec((d, d), const2, pipeline_mode=single),
        ],
        out_specs=pl.BlockSpec((tm, d), prev),
        out_shape=jax.ShapeDtypeStruct((s, d), jnp.float32),
        scratch_shapes=[
            pltpu.VMEM((tm, d), jnp.bfloat16),
            pltpu.VMEM((tm, d), jnp.float32),
            pltpu.VMEM((tm, 1), jnp.float32),
        ],
        compiler_params=pltpu.CompilerParams(
            dimension_semantics=("arbitrary",), vmem_limit_bytes=VMEM_LIMIT_BYTES),
        name="merge_out",
    )(ya, yp, tg, x2, gate, post_g_row, wa, wp, wout)


def _layer(x2, c, rel_table, bucket, w_ada, b_ada, pre_g, post_g, w_in, sink, w_pool, pool_scale,
           w_br_attn, w_br_pool, w_merge, b_merge, w_out):
    s, d = x2.shape
    c_row = c.reshape(1, d)
    b_ada_row = b_ada.reshape(1, -1)
    pre_g_row = pre_g.reshape(1, d)
    shift, scale, bias_tbl, w_pool_bf = _shift_scale_bias(c_row, w_ada, b_ada_row, rel_table, bucket,
                                                          w_pool)
    ya, yp, h = _project_mix(x2, shift, scale, pre_g_row, w_in, w_pool_bf.reshape(w_pool.shape),
                             pool_scale.reshape(1, -1), bias_tbl, sink)
    tg, wa, wp, wout, gate = _merge_gate(h, w_merge, b_merge.reshape(1, -1), w_br_attn, w_br_pool,
                                         w_out, c_row, w_ada, b_ada_row)
    return _merge_out(ya, yp, tg, x2, gate, post_g.reshape(1, d), wa, wp, wout)


def kernel(x, c, rel_bias_table, w_ada, b_ada, pre_norm_g, post_norm_g, w_in, attn_sink,
           w_pool_group, pool_scale, w_branch_attn, w_branch_pool, w_merge, b_merge, w_out):
    batch, s, d = x.shape
    assert batch == 1, "kernel is written for a single sequence"
    depth = w_ada.shape[0]
    bucket = _bucket_index_table()
    x2 = x.reshape(s, d)
    for l in range(depth):
        x2 = _layer(x2, c[0], rel_bias_table, bucket, w_ada[l], b_ada[l], pre_norm_g[l],
                    post_norm_g[l], w_in[l], attn_sink[l], w_pool_group[l], pool_scale[l],
                    w_branch_attn[l], w_branch_pool[l], w_merge[l], b_merge[l], w_out[l])
    return x2.reshape(batch, s, d)
```

```python
import functools
import math

import numpy as np
import jax
import jax.numpy as jnp
from jax import lax
from jax.experimental import pallas as pl
from jax.experimental.pallas import tpu as pltpu

HEAD_DIM = 128
N_Q_HEADS = 8
N_KV_HEADS = 2
GQA_GROUP = N_Q_HEADS // N_KV_HEADS
ATTN_WIDTH = N_Q_HEADS * HEAD_DIM
KV_WIDTH = N_KV_HEADS * HEAD_DIM
WINDOW = 128
BLOCK = 128
SPAN = BLOCK + 2 * WINDOW
N_BUCKETS = 32
MAX_DISTANCE = 128
POOL_SIZES = (2, 4, 8, 16)
N_POOL_GROUPS = len(POOL_SIZES)
EPS = 1e-6
NEG_INF = -1e30
MASKED_BUCKET = N_BUCKETS
ATTN_SCALE = HEAD_DIM ** -0.5
INV_ATTN_SCALE = HEAD_DIM ** 0.5
EXP2_SCALE = ATTN_SCALE * math.log2(math.e)

SUBLANES = 8
BF16_SUBLANES = 16
VMEM_LIMIT_BYTES = 60 * 1024 * 1024

MOD_TK = 256
TM = 512
CW = 512
GATE_TM = 2048
GATE_ROWS = 1024
GATE_TN = 1024
POOL_HALO = BF16_SUBLANES
W_STAGE_SLOTS = 8

Q0 = 0
KV0 = Q0 + ATTN_WIDTH
GA0 = KV0 + 2 * KV_WIDTH
U0 = GA0 + ATTN_WIDTH


def _silu(v):
    return v * (1.0 / (1.0 + jnp.exp(-v)))


def _half_silu(hv):
    return hv + hv * jnp.tanh(hv)


def _silu_column(c_row):
    n = c_row.shape[1]
    on_diag = (lax.broadcasted_iota(jnp.int32, (n, n), 0)
               == lax.broadcasted_iota(jnp.int32, (n, n), 1))
    return jnp.sum(jnp.where(on_diag, _silu(c_row), 0.0), axis=1, keepdims=True)


def _adaln_rmsnorm(x, gain, shift):
    ms = jnp.mean(x * x, axis=-1, keepdims=True)
    return (x * lax.rsqrt(ms + EPS)) * gain + shift


def _shift_scale_bias_kernel(tbl_ref, c_ref, w_ref, bsh_ref, bsc_ref, bucket_ref, wpool_ref,
                             shift_ref, scale_ref, bias_ref, wpool_o_ref, *, tile_buckets):
    k = pl.program_id(0)
    d = shift_ref.shape[1]

    @pl.when(k == 0)
    def _():
        shift_ref[...] = bsh_ref[...]
        scale_ref[...] = bsc_ref[...]

    s = _silu_column(c_ref[...])
    both = jnp.sum(s * w_ref[...], axis=0, keepdims=True)
    shift_ref[...] += both[:, :d]
    scale_ref[...] += both[:, d:]

    wpool_o_ref[...] = wpool_ref[...].astype(wpool_o_ref.dtype)

    for t, buckets in enumerate(tile_buckets):
        bk = bucket_ref[t * BLOCK:(t + 1) * BLOCK, :]
        acc = jnp.full((BLOCK, BLOCK), NEG_INF, jnp.float32)
        for b in buckets:
            acc = jnp.where(bk == b, tbl_ref[k, b] * INV_ATTN_SCALE, acc)
        bias_ref[0, t * BLOCK:(t + 1) * BLOCK, :] = acc


def _shift_scale_bias(c_row, w_ada, b_ada_row, rel_table, bucket, wpool):
    d = w_ada.shape[0]
    steps = d // MOD_TK
    assert steps == N_Q_HEADS, "one bias-table head per grid step"
    bucket_t = np.ascontiguousarray(bucket.T)
    wpool2 = wpool.reshape(-1, wpool.shape[-1])
    pool_slab = (wpool2.shape[0] // steps, wpool2.shape[1])
    assert wpool2.shape[0] % (steps * BF16_SUBLANES) == 0
    tile_buckets = tuple(
        tuple(int(b) for b in np.unique(bucket[:, t * BLOCK:(t + 1) * BLOCK]) if b != MASKED_BUCKET)
        for t in range(SPAN // BLOCK))
    return pl.pallas_call(
        functools.partial(_shift_scale_bias_kernel, tile_buckets=tile_buckets),
        grid=(steps,),
        in_specs=[
            pl.BlockSpec(memory_space=pltpu.SMEM),
            pl.BlockSpec((1, MOD_TK), lambda k: (0, k)),
            pl.BlockSpec((MOD_TK, 2 * d), lambda k: (k, 0)),
            pl.BlockSpec((1, d), lambda k: (0, 0)),
            pl.BlockSpec((1, d), lambda k: (0, 1)),
            pl.BlockSpec(bucket_t.shape, lambda k: (0, 0)),
            pl.BlockSpec(pool_slab, lambda k: (k, 0)),
        ],
        out_specs=[
            pl.BlockSpec((1, d), lambda k: (0, 0)),
            pl.BlockSpec((1, d), lambda k: (0, 0)),
            pl.BlockSpec((1, SPAN, BLOCK), lambda k: (k // GQA_GROUP, 0, k % GQA_GROUP)),
            pl.BlockSpec(pool_slab, lambda k: (k, 0)),
        ],
        out_shape=[
            jax.ShapeDtypeStruct((1, d), jnp.float32),
            jax.ShapeDtypeStruct((1, d), jnp.float32),
            jax.ShapeDtypeStruct((N_KV_HEADS, SPAN, GQA_GROUP * BLOCK), jnp.float32),
            jax.ShapeDtypeStruct(wpool2.shape, jnp.bfloat16),
        ],
        compiler_params=pltpu.CompilerParams(
            dimension_semantics=("arbitrary",), vmem_limit_bytes=VMEM_LIMIT_BYTES),
        name="shift_scale_bias",
    )(rel_table.T, c_row, w_ada, b_ada_row, b_ada_row, bucket_t, wpool2)


def _t5_bucket(rel):
    half = N_BUCKETS // 2
    max_exact = half // 2
    assert (max_exact, MAX_DISTANCE // max_exact, half - max_exact) == (8, 16, 8)
    n = abs(rel)
    large = min(max_exact + (n * n).bit_length() - 7, half - 1)
    return (half if rel > 0 else 0) + (n if n < max_exact else large)


def _bucket_index_table():
    table = np.full((BLOCK, SPAN), MASKED_BUCKET, np.int32)
    for q in range(BLOCK):
        for t in range(SPAN):
            rel = t - WINDOW - q
            if abs(rel) <= WINDOW:
                table[q, t] = _t5_bucket(rel)
    return table


def _merge_gate_kernel(h_ref, w_ref, wa_ref, wp_ref, wout_ref, c_ref, wgate_ref, bgate_ref,
                       o_ref, wa_o_ref, wp_o_ref, wout_o_ref, gate_ref,
                       wbf_ref):
    j, i = pl.program_id(0), pl.program_id(1)

    @pl.when(i == 0)
    def _():
        wbf_ref[...] = (0.5 * w_ref[...]).astype(wbf_ref.dtype)

    @pl.when((j == 0) & (i == 0))
    def _():
        gate_ref[...] = bgate_ref[...]

    for r in range(0, h_ref.shape[0], GATE_ROWS):
        o_ref[r:r + GATE_ROWS] = jnp.dot(h_ref[r:r + GATE_ROWS], wbf_ref[...],
                                         preferred_element_type=jnp.float32).astype(o_ref.dtype)

    wa_o_ref[...] = wa_ref[...].astype(wa_o_ref.dtype)
    wp_o_ref[...] = wp_ref[...].astype(wp_o_ref.dtype)
    wout_o_ref[...] = wout_ref[...].astype(wout_o_ref.dtype)
    gate_ref[...] += jnp.sum(_silu_column(c_ref[...]) * wgate_ref[...], axis=0, keepdims=True)


def _merge_gate(h, w_merge, wa, wp, wout, c_row, w_ada, b_ada_row):
    s, d = h.shape
    n = w_merge.shape[1]
    n_i = s // GATE_TM
    steps = (n // GATE_TN) * n_i
    slab = lambda w: (w.shape[0] // steps, w.shape[1])
    step = lambda j, i: (j * n_i + i, 0)
    for w in (wa, wp, wout):
        assert w.shape[0] % (steps * BF16_SUBLANES) == 0
    assert d % (steps * SUBLANES) == 0
    const2 = lambda j, i: (0, 0)
    bf16 = jnp.bfloat16
    return pl.pallas_call(
        _merge_gate_kernel,
        grid=(n // GATE_TN, n_i),
        in_specs=[
            pl.BlockSpec((GATE_TM, d), lambda j, i: (i, 0)),
            pl.BlockSpec((d, GATE_TN), lambda j, i: (0, j)),
            pl.BlockSpec(slab(wa), step),
            pl.BlockSpec(slab(wp), step),
            pl.BlockSpec(slab(wout), step),
            pl.BlockSpec((1, d // steps), lambda j, i: (0, j * n_i + i)),
            pl.BlockSpec((d // steps, d), lambda j, i: (j * n_i + i, 2)),
            pl.BlockSpec((1, d), lambda j, i: (0, 2)),
        ],
        out_specs=[
            pl.BlockSpec((GATE_TM, GATE_TN), lambda j, i: (i, j)),
            pl.BlockSpec(slab(wa), step),
            pl.BlockSpec(slab(wp), step),
            pl.BlockSpec(slab(wout), step),
            pl.BlockSpec((1, d), const2),
        ],
        out_shape=[
            jax.ShapeDtypeStruct((s, n), bf16),
            jax.ShapeDtypeStruct(wa.shape, bf16),
            jax.ShapeDtypeStruct(wp.shape, bf16),
            jax.ShapeDtypeStruct(wout.shape, bf16),
            jax.ShapeDtypeStruct((1, d), jnp.float32),
        ],
        scratch_shapes=[pltpu.VMEM((d, GATE_TN), bf16)],
        compiler_params=pltpu.CompilerParams(
            dimension_semantics=("arbitrary", "arbitrary"), vmem_limit_bytes=VMEM_LIMIT_BYTES),
        name="merge_gate",
    )(h, w_merge, wa, wp, wout, c_row, w_ada, b_ada_row)


def _project_mix_kernel(sink_ref,
                        x_ref, shift_ref, scale_ref, g_ref, w_hbm_ref, wpool_ref, pscale_ref, bias_ref,
                        ya_ref, yp_ref, h_ref,
                        pbuf_ref, kvtail_ref, utail_ref, uext_ref, w_ref, wstage_ref, wsem_ref,
                        *, seq_len):
    s = pl.program_id(0)
    n_tok = pl.num_programs(0) - 1
    tm, d = x_ref.shape
    n_sub = tm // BLOCK
    n_blocks = seq_len // BLOCK
    aw = ATTN_WIDTH
    pw = pscale_ref.shape[1]
    q0, ga0, gp0 = Q0, GA0, U0 + pw
    n_chunks = w_ref.shape[1] // CW
    f32 = jnp.float32
    bf16 = jnp.bfloat16
    cur = s % 2
    prv = 1 - cur
    j = s - 1

    n_slots, slab_rows = wstage_ref.shape[0], wstage_ref.shape[1]
    assert n_slots * slab_rows == d and wstage_ref.shape[2] == CW

    def slab_copy(c, r):
        return pltpu.make_async_copy(
            w_hbm_ref.at[pl.ds(r * slab_rows, slab_rows), pl.ds(c * CW, CW)],
            wstage_ref.at[r], wsem_ref.at[r])

    def fetch_chunk(c):
        for r in range(n_slots):
            slab_copy(c, r).start()

    def land_chunk(c, next_c):
        for r in range(n_slots):
            slab_copy(c, r).wait()
            w_ref[r * slab_rows:(r + 1) * slab_rows, c * CW:(c + 1) * CW] = wstage_ref[r].astype(bf16)
            if next_c is not None:
                slab_copy(next_c, r).start()

    def prologue():
        gain = g_ref[...] * (1.0 + scale_ref[...])
        h_ref[...] = _adaln_rmsnorm(x_ref[...], gain, shift_ref[...]).astype(bf16)

    def is_gate_col(col):
        return ga0 <= col < ga0 + aw or gp0 <= col < gp0 + pw

    def proj_chunk(c):
        lo, hi = c * CW, (c + 1) * CW
        p = jnp.dot(h_ref[...], w_ref[:, lo:hi], preferred_element_type=f32)
        if is_gate_col(lo):
            assert is_gate_col(hi - 1)
            p = 0.5 * p
        pbuf_ref[cur, :, lo:hi] = p.astype(bf16)

    def window_rows(b, col0):
        r0, r1 = (b - 1) * BLOCK, (b + 2) * BLOCK
        parts = []
        if r0 < 0:
            parts.append(kvtail_ref[prv, :, col0 - KV0:col0 - KV0 + HEAD_DIM])
            r0 = 0
        parts.append(pbuf_ref[prv, r0:min(r1, tm), col0:col0 + HEAD_DIM])
        if r1 > tm:
            parts.append(pbuf_ref[cur, 0:r1 - tm, col0:col0 + HEAD_DIM])
        return jnp.concatenate(parts, axis=0) if len(parts) > 1 else parts[0]

    def attn_scores(b, kvh):
        blk = j * n_sub + b
        h0 = kvh * GQA_GROUP
        k = window_rows(b, KV0 + kvh * HEAD_DIM)
        v = window_rows(b, KV0 + KV_WIDTH + kvh * HEAD_DIM)
        qs = jnp.concatenate(
            [pbuf_ref[prv, b * BLOCK:(b + 1) * BLOCK,
                      q0 + (h0 + g) * HEAD_DIM:q0 + (h0 + g + 1) * HEAD_DIM]
             for g in range(GQA_GROUP)], axis=0)
        z = lax.dot_general(k, qs, (((1,), (1,)), ((), ())), preferred_element_type=f32)
        z = z + bias_ref[kvh]
        row = lax.broadcasted_iota(jnp.int32, (SPAN, 1), 0)
        if b == 0:
            z = jnp.where(row < jnp.where(blk == 0, WINDOW, 0), NEG_INF, z)
        if b == n_sub - 1:
            z = jnp.where(row >= jnp.where(blk == n_blocks - 1, WINDOW + BLOCK, SPAN), NEG_INF, z)
        sink = jnp.concatenate(
            [jnp.full((1, BLOCK), sink_ref[h0 + g] * INV_ATTN_SCALE, f32)
             for g in range(GQA_GROUP)], axis=1)
        m = jnp.maximum(jnp.max(z, axis=0, keepdims=True), sink)
        p = jnp.exp2((z - m) * EXP2_SCALE)
        denom = jnp.sum(p, axis=0, keepdims=True) + jnp.exp2((sink - m) * EXP2_SCALE)
        return p.astype(bf16), denom, v

    def attn_out(p, denom, v):
        ot = lax.dot_general(v, p, (((0,), (0,)), ((), ())), preferred_element_type=f32) / denom
        return [ot[:, g * BLOCK:(g + 1) * BLOCK].T for g in range(GQA_GROUP)]

    def finish_sub_block(b, heads):
        y = jnp.concatenate(heads, axis=1)
        hg = pbuf_ref[prv, b * BLOCK:(b + 1) * BLOCK, ga0:ga0 + aw].astype(f32)
        ya_ref[b * BLOCK:(b + 1) * BLOCK] = (y * _half_silu(hg)).astype(bf16)

    ext = tm + 2 * POOL_HALO

    def pool_group(gi):
        w = POOL_SIZES[gi]
        gw = pw // N_POOL_GROUPS
        c0, c1 = gi * gw, (gi + 1) * gw
        half = w // 2
        halo_row = lax.broadcasted_iota(jnp.int32, (POOL_HALO, 1), 0)
        uext_ref[0:POOL_HALO] = jnp.where(j * tm - POOL_HALO + halo_row >= 0,
                                          utail_ref[prv, :, c0:c1].astype(f32), 0.0)
        uext_ref[POOL_HALO:POOL_HALO + tm] = pbuf_ref[prv, :, U0 + c0:U0 + c1].astype(f32)
        uext_ref[POOL_HALO + tm:ext] = jnp.where(
            (j + 1) * tm + halo_row < seq_len,
            pbuf_ref[cur, 0:POOL_HALO, U0 + c0:U0 + c1].astype(f32), 0.0)
        a = uext_ref[...]
        sh = 1
        while sh < w:
            a = a + pltpu.roll(a, ext - sh, axis=0)
            sh *= 2
        off = POOL_HALO - half
        if off % SUBLANES:
            a = pltpu.roll(a, ext - off, axis=0)
            off = 0
        pos = j * tm + lax.broadcasted_iota(jnp.int32, (tm, 1), 0)
        cnt = (jnp.minimum(pos + half, seq_len) - jnp.maximum(pos - half, 0)).astype(f32)
        pooled = a[off:off + tm] / cnt - uext_ref[POOL_HALO:POOL_HALO + tm]
        mixed = jnp.dot(pooled.astype(bf16), wpool_ref[gi], preferred_element_type=f32)
        hg = pbuf_ref[prv, :, gp0 + c0:gp0 + c1].astype(f32)
        yp_ref[:, c0:c1] = (mixed * pscale_ref[:, c0:c1] * _half_silu(hg)).astype(bf16)

    def save_tails():
        kvtail_ref[cur] = pbuf_ref[prv, tm - BLOCK:tm, KV0:KV0 + 2 * KV_WIDTH]
        utail_ref[cur] = pbuf_ref[prv, tm - POOL_HALO:tm, U0:U0 + pw]

    units = [(b, kvh) for b in range(n_sub) for kvh in range(N_KV_HEADS)]
    assert KV0 % CW == 0 and (2 * KV_WIDTH) % CW == 0 and U0 % CW == 0 and pw % CW == 0
    first = list(range(KV0 // CW, (KV0 + 2 * KV_WIDTH) // CW)) + list(range(U0 // CW, (U0 + pw) // CW))
    order = first + [c for c in range(n_chunks) if c not in first]
    u_chunks_done = len(first)
    assert len(units) + 1 <= n_chunks and u_chunks_done + N_POOL_GROUPS <= n_chunks

    def mix_only():
        pending = [attn_scores(b, kvh) for b, kvh in units]
        heads = []
        for (b, kvh), scores in zip(units, pending):
            heads += attn_out(*scores)
            if kvh == N_KV_HEADS - 1:
                finish_sub_block(b, heads)
                heads = []
        for gi in range(N_POOL_GROUPS):
            pool_group(gi)
        save_tails()

    def project_first():
        fetch_chunk(order[0])
        prologue()
        for pos, c in enumerate(order):
            land_chunk(c, order[pos + 1] if pos + 1 < len(order) else None)
            proj_chunk(c)

    def project_and_mix():
        prologue()
        pending = attn_scores(*units[0])
        heads = []
        for pos, c in enumerate(order):
            proj_chunk(c)
            if pos < len(units):
                heads += attn_out(*pending)
                if pos + 1 < len(units):
                    pending = attn_scores(*units[pos + 1])
                b, kvh = units[pos]
                if kvh == N_KV_HEADS - 1:
                    finish_sub_block(b, heads)
                    heads = []
            if u_chunks_done <= pos < u_chunks_done + N_POOL_GROUPS:
                pool_group(pos - u_chunks_done)
        save_tails()

    @pl.when(s == 0)
    def _():
        kvtail_ref[0] = jnp.zeros(kvtail_ref.shape[1:], kvtail_ref.dtype)
        utail_ref[0] = jnp.zeros(utail_ref.shape[1:], utail_ref.dtype)
        project_first()

    @pl.when((s > 0) & (s < n_tok))
    def _():
        project_and_mix()

    @pl.when(s == n_tok)
    def _():
        mix_only()


def _project_mix(x2, shift, scale, pre_g_row, w_in, wpool, pool_scale_row, bias_tbl, sink):
    s, d = x2.shape
    tm = TM
    n_tok = s // tm
    pw = pool_scale_row.shape[1]
    n_in = w_in.shape[1]
    assert s // BLOCK >= 2, "first and last attention block must differ"
    assert n_in == U0 + 2 * pw and d % (W_STAGE_SLOTS * BF16_SUBLANES) == 0
    cur_blk = lambda st: jnp.minimum(st, n_tok - 1)
    mix_blk = lambda st: jnp.maximum(st - 1, 0)
    const2 = lambda st: (0, 0)
    single = pl.Buffered(1)
    bf16 = jnp.bfloat16
    return pl.pallas_call(
        functools.partial(_project_mix_kernel, seq_len=s),
        grid=(n_tok + 1,),
        in_specs=[
            pl.BlockSpec(memory_space=pltpu.SMEM),
            pl.BlockSpec((tm, d), lambda st: (cur_blk(st), 0)),
            pl.BlockSpec((1, d), const2),
            pl.BlockSpec((1, d), const2),
            pl.BlockSpec((1, d), const2),
            pl.BlockSpec(memory_space=pl.ANY),
            pl.BlockSpec(wpool.shape, lambda st: (0, 0, 0), pipeline_mode=single),
            pl.BlockSpec((1, pw), const2),
            pl.BlockSpec(bias_tbl.shape, lambda st: (0, 0, 0), pipeline_mode=single),
        ],
        out_specs=[
            pl.BlockSpec((tm, ATTN_WIDTH), lambda st: (mix_blk(st), 0)),
            pl.BlockSpec((tm, pw), lambda st: (mix_blk(st), 0)),
            pl.BlockSpec((tm, d), lambda st: (cur_blk(st), 0)),
        ],
        out_shape=[
            jax.ShapeDtypeStruct((s, ATTN_WIDTH), bf16),
            jax.ShapeDtypeStruct((s, pw), bf16),
            jax.ShapeDtypeStruct((s, d), bf16),
        ],
        scratch_shapes=[
            pltpu.VMEM((2, tm, n_in), bf16),
            pltpu.VMEM((2, BLOCK, 2 * KV_WIDTH), bf16),
            pltpu.VMEM((2, POOL_HALO, pw), bf16),
            pltpu.VMEM((tm + 2 * POOL_HALO, pw // N_POOL_GROUPS), jnp.float32),
            pltpu.VMEM(w_in.shape, bf16),
            pltpu.VMEM((W_STAGE_SLOTS, d // W_STAGE_SLOTS, CW), jnp.float32),
            pltpu.SemaphoreType.DMA((W_STAGE_SLOTS,)),
        ],
        compiler_params=pltpu.CompilerParams(
            dimension_semantics=("arbitrary",), vmem_limit_bytes=VMEM_LIMIT_BYTES),
        name="project_mix",
    )(sink, x2, shift, scale, pre_g_row, w_in, wpool, pool_scale_row, bias_tbl)


def _merge_out_kernel(ya_ref, yp_ref, gm_ref, x_ref, gate_ref, postg_ref, bm_ref,
                      wa_ref, wp_ref, wout_ref, o_ref, merged_ref, oacc_ref, inv_ref):
    s = pl.program_id(0)
    n_tok = pl.num_programs(0) - 1
    tm, d = x_ref.shape
    n_chunks = d // CW
    f32 = jnp.float32

    def finish_previous():
        scale = gate_ref[...] * postg_ref[...]
        inv = inv_ref[...]
        for c in range(n_chunks):
            lo, hi = c * CW, (c + 1) * CW
            o_ref[:, lo:hi] = x_ref[:, lo:hi] + (oacc_ref[:, lo:hi] * inv) * scale[:, lo:hi]

    def matmuls():
        half_bm = 0.5 * bm_ref[...]
        for c in range(n_chunks):
            lo, hi = c * CW, (c + 1) * CW
            bra = jnp.dot(ya_ref[...], wa_ref[:, lo:hi], preferred_element_type=f32)
            brp = jnp.dot(yp_ref[...], wp_ref[:, lo:hi], preferred_element_type=f32)
            g_a = 0.5 * jnp.tanh(gm_ref[:, lo:hi].astype(f32) + half_bm[:, lo:hi]) + 0.5
            g_p = 0.5 * jnp.tanh(gm_ref[:, d + lo:d + hi].astype(f32)
                                 + half_bm[:, d + lo:d + hi]) + 0.5
            merged_ref[:, lo:hi] = (g_a * bra + g_p * brp).astype(merged_ref.dtype)
        ssq = jnp.zeros((tm, 1), f32)
        for c in range(n_chunks):
            lo, hi = c * CW, (c + 1) * CW
            o = jnp.dot(merged_ref[...], wout_ref[:, lo:hi], preferred_element_type=f32)
            ssq = ssq + jnp.sum(o * o, axis=-1, keepdims=True)
            oacc_ref[:, lo:hi] = o
        inv_ref[...] = lax.rsqrt(ssq * (1.0 / d) + EPS)

    @pl.when(s == 0)
    def _():
        matmuls()

    @pl.when((s > 0) & (s < n_tok))
    def _():
        finish_previous()
        matmuls()

    @pl.when(s == n_tok)
    def _():
        finish_previous()


def _merge_out(ya, yp, gm, x2, gate, post_g_row, b_merge_row, wa, wp, wout):
    s, d = x2.shape
    tm = TM
    n_tok = s // tm
    aw, pw = ya.shape[1], yp.shape[1]
    cur = lambda st: (jnp.minimum(st, n_tok - 1), 0)
    prev = lambda st: (jnp.maximum(st - 1, 0), 0)
    const2 = lambda st: (0, 0)
    single = pl.Buffered(1)
    return pl.pallas_call(
        _merge_out_kernel,
        grid=(n_tok + 1,),
        in_specs=[
            pl.BlockSpec((tm, aw), cur),
            pl.BlockSpec((tm, pw), cur),
            pl.BlockSpec((tm, 2 * d), cur),
            pl.BlockSpec((tm, d), prev),
            pl.BlockSpec((1, d), const2),
            pl.BlockSpec((1, d), const2),
            pl.BlockSpec((1, 2 * d), const2),
            pl.BlockSpec((aw, d), const2, pipeline_mode=single),
            pl.BlockSpec((pw, d), const2, pipeline_mode=single),
            pl.BlockSpec((d, d), const2, pipeline_mode=single),
        ],
        out_specs=pl.BlockSpec((tm, d), prev),
        out_shape=jax.ShapeDtypeStruct((s, d), jnp.float32),
        scratch_shapes=[
            pltpu.VMEM((tm, d), jnp.bfloat16),
            pltpu.VMEM((tm, d), jnp.float32),
            pltpu.VMEM((tm, 1), jnp.float32),
        ],
        compiler_params=pltpu.CompilerParams(
            dimension_semantics=("arbitrary",), vmem_limit_bytes=VMEM_LIMIT_BYTES),
        name="merge_out",
    )(ya, yp, gm, x2, gate, post_g_row, b_merge_row, wa, wp, wout)


def _layer(x2, c, rel_table, bucket, w_ada, b_ada, pre_g, post_g, w_in, sink, w_pool, pool_scale,
           w_br_attn, w_br_pool, w_merge, b_merge, w_out):
    s, d = x2.shape
    c_row = c.reshape(1, d)
    b_ada_row = b_ada.reshape(1, -1)
    pre_g_row = pre_g.reshape(1, d)
    shift, scale, bias_tbl, w_pool_bf = _shift_scale_bias(c_row, w_ada, b_ada_row, rel_table, bucket,
                                                          w_pool)
    ya, yp, h = _project_mix(x2, shift, scale, pre_g_row, w_in, w_pool_bf.reshape(w_pool.shape),
                             pool_scale.reshape(1, -1), bias_tbl, sink)
    gm, wa, wp, wout, gate = _merge_gate(h, w_merge, w_br_attn, w_br_pool, w_out,
                                         c_row, w_ada, b_ada_row)
    return _merge_out(ya, yp, gm, x2, gate, post_g.reshape(1, d), b_merge.reshape(1, -1),
                      wa, wp, wout)


def kernel(x, c, rel_bias_table, w_ada, b_ada, pre_norm_g, post_norm_g, w_in, attn_sink,
           w_pool_group, pool_scale, w_branch_attn, w_branch_pool, w_merge, b_merge, w_out):
    batch, s, d = x.shape
    assert batch == 1, "kernel is written for a single sequence"
    depth = w_ada.shape[0]
    bucket = _bucket_index_table()
    x2 = x.reshape(s, d)
    for l in range(depth):
        x2 = _layer(x2, c[0], rel_bias_table, bucket, w_ada[l], b_ada[l], pre_norm_g[l],
                    post_norm_g[l], w_in[l], attn_sink[l], w_pool_group[l], pool_scale[l],
                    w_branch_attn[l], w_branch_pool[l], w_merge[l], b_merge[l], w_out[l])
    return x2.reshape(batch, s, d)
```

```python
import functools
import math

import numpy as np
import jax
import jax.numpy as jnp
from jax import lax
from jax.experimental import pallas as pl
from jax.experimental.pallas import tpu as pltpu

HEAD_DIM = 128
N_Q_HEADS = 8
N_KV_HEADS = 2
GQA_GROUP = N_Q_HEADS // N_KV_HEADS
ATTN_WIDTH = N_Q_HEADS * HEAD_DIM
KV_WIDTH = N_KV_HEADS * HEAD_DIM
WINDOW = 128
BLOCK = 128
SPAN = BLOCK + 2 * WINDOW
N_BUCKETS = 32
MAX_DISTANCE = 128
POOL_SIZES = (2, 4, 8, 16)
N_POOL_GROUPS = len(POOL_SIZES)
EPS = 1e-6
NEG_INF = -1e30
MASKED_BUCKET = N_BUCKETS
ATTN_SCALE = HEAD_DIM ** -0.5
INV_ATTN_SCALE = HEAD_DIM ** 0.5
EXP2_SCALE = ATTN_SCALE * math.log2(math.e)

SUBLANES = 8
BF16_SUBLANES = 16
VMEM_LIMIT_BYTES = 61 * 1024 * 1024

MOD_TK = 256
TM = 512
CW = 512
GATE_TM = 2048
GATE_ROWS = 1024
GATE_TN = 1024
POOL_HALO = BF16_SUBLANES
W_STAGE_SLOTS = 8

Q0 = 0
KV0 = Q0 + ATTN_WIDTH
GA0 = KV0 + 2 * KV_WIDTH
U0 = GA0 + ATTN_WIDTH


def _silu(v):
    return v * (1.0 / (1.0 + jnp.exp(-v)))


def _half_silu(hv):
    return hv + hv * jnp.tanh(hv)


def _silu_column(c_row):
    n = c_row.shape[1]
    on_diag = (lax.broadcasted_iota(jnp.int32, (n, n), 0)
               == lax.broadcasted_iota(jnp.int32, (n, n), 1))
    return jnp.sum(jnp.where(on_diag, _silu(c_row), 0.0), axis=1, keepdims=True)


def _adaln_rmsnorm(x, gain, shift):
    ms = jnp.mean(x * x, axis=-1, keepdims=True)
    return (x * lax.rsqrt(ms + EPS)) * gain + shift


def _shift_scale_bias_kernel(tbl_ref, c_ref, w_ref, bsh_ref, bsc_ref, bucket_ref, wpool_ref,
                             shift_ref, scale_ref, bias_ref, wpool_o_ref, *, tile_buckets):
    k = pl.program_id(0)
    d = shift_ref.shape[1]

    @pl.when(k == 0)
    def _():
        shift_ref[...] = bsh_ref[...]
        scale_ref[...] = bsc_ref[...]

    s = _silu_column(c_ref[...])
    both = jnp.sum(s * w_ref[...], axis=0, keepdims=True)
    shift_ref[...] += both[:, :d]
    scale_ref[...] += both[:, d:]

    wpool_o_ref[...] = wpool_ref[...].astype(wpool_o_ref.dtype)

    for t, buckets in enumerate(tile_buckets):
        bk = bucket_ref[:, t * BLOCK:(t + 1) * BLOCK]
        acc = jnp.full((BLOCK, BLOCK), NEG_INF, jnp.float32)
        for b in buckets:
            acc = jnp.where(bk == b, tbl_ref[k, b] * INV_ATTN_SCALE, acc)
        bias_ref[0, :, t * BLOCK:(t + 1) * BLOCK] = acc


def _shift_scale_bias(c_row, w_ada, b_ada_row, rel_table, bucket, wpool):
    d = w_ada.shape[0]
    steps = d // MOD_TK
    assert steps == N_Q_HEADS, "one bias-table head per grid step"
    wpool2 = wpool.reshape(-1, wpool.shape[-1])
    pool_slab = (wpool2.shape[0] // steps, wpool2.shape[1])
    assert wpool2.shape[0] % (steps * BF16_SUBLANES) == 0
    tile_buckets = tuple(
        tuple(int(b) for b in np.unique(bucket[:, t * BLOCK:(t + 1) * BLOCK]) if b != MASKED_BUCKET)
        for t in range(SPAN // BLOCK))
    return pl.pallas_call(
        functools.partial(_shift_scale_bias_kernel, tile_buckets=tile_buckets),
        grid=(steps,),
        in_specs=[
            pl.BlockSpec(memory_space=pltpu.SMEM),
            pl.BlockSpec((1, MOD_TK), lambda k: (0, k)),
            pl.BlockSpec((MOD_TK, 2 * d), lambda k: (k, 0)),
            pl.BlockSpec((1, d), lambda k: (0, 0)),
            pl.BlockSpec((1, d), lambda k: (0, 1)),
            pl.BlockSpec(bucket.shape, lambda k: (0, 0)),
            pl.BlockSpec(pool_slab, lambda k: (k, 0)),
        ],
        out_specs=[
            pl.BlockSpec((1, d), lambda k: (0, 0)),
            pl.BlockSpec((1, d), lambda k: (0, 0)),
            pl.BlockSpec((1, BLOCK, SPAN), lambda k: (k, 0, 0)),
            pl.BlockSpec(pool_slab, lambda k: (k, 0)),
        ],
        out_shape=[
            jax.ShapeDtypeStruct((1, d), jnp.float32),
            jax.ShapeDtypeStruct((1, d), jnp.float32),
            jax.ShapeDtypeStruct((N_Q_HEADS, BLOCK, SPAN), jnp.float32),
            jax.ShapeDtypeStruct(wpool2.shape, jnp.bfloat16),
        ],
        compiler_params=pltpu.CompilerParams(
            dimension_semantics=("arbitrary",), vmem_limit_bytes=VMEM_LIMIT_BYTES),
        name="shift_scale_bias",
    )(rel_table.T, c_row, w_ada, b_ada_row, b_ada_row, bucket, wpool2)


def _t5_bucket(rel):
    half = N_BUCKETS // 2
    max_exact = half // 2
    assert (max_exact, MAX_DISTANCE // max_exact, half - max_exact) == (8, 16, 8)
    n = abs(rel)
    large = min(max_exact + (n * n).bit_length() - 7, half - 1)
    return (half if rel > 0 else 0) + (n if n < max_exact else large)


def _bucket_index_table():
    table = np.full((BLOCK, SPAN), MASKED_BUCKET, np.int32)
    for q in range(BLOCK):
        for t in range(SPAN):
            rel = t - WINDOW - q
            if abs(rel) <= WINDOW:
                table[q, t] = _t5_bucket(rel)
    return table


def _merge_gate_kernel(h_ref, w_ref, wa_ref, wp_ref, wout_ref, c_ref, wgate_ref, bgate_ref,
                       o_ref, wa_o_ref, wp_o_ref, wout_o_ref, gate_ref,
                       wbf_ref):
    j, i = pl.program_id(0), pl.program_id(1)

    @pl.when(i == 0)
    def _():
        wbf_ref[...] = (0.5 * w_ref[...]).astype(wbf_ref.dtype)

    @pl.when((j == 0) & (i == 0))
    def _():
        gate_ref[...] = bgate_ref[...]

    for r in range(0, h_ref.shape[0], GATE_ROWS):
        o_ref[r:r + GATE_ROWS] = jnp.dot(h_ref[r:r + GATE_ROWS], wbf_ref[...],
                                         preferred_element_type=jnp.float32).astype(o_ref.dtype)

    wa_o_ref[...] = wa_ref[...].astype(wa_o_ref.dtype)
    wp_o_ref[...] = wp_ref[...].astype(wp_o_ref.dtype)
    wout_o_ref[...] = wout_ref[...].astype(wout_o_ref.dtype)
    gate_ref[...] += jnp.sum(_silu_column(c_ref[...]) * wgate_ref[...], axis=0, keepdims=True)


def _merge_gate(h, w_merge, wa, wp, wout, c_row, w_ada, b_ada_row):
    s, d = h.shape
    n = w_merge.shape[1]
    n_i = s // GATE_TM
    steps = (n // GATE_TN) * n_i
    slab = lambda w: (w.shape[0] // steps, w.shape[1])
    step = lambda j, i: (j * n_i + i, 0)
    for w in (wa, wp, wout):
        assert w.shape[0] % (steps * BF16_SUBLANES) == 0
    assert d % (steps * SUBLANES) == 0
    const2 = lambda j, i: (0, 0)
    bf16 = jnp.bfloat16
    return pl.pallas_call(
        _merge_gate_kernel,
        grid=(n // GATE_TN, n_i),
        in_specs=[
            pl.BlockSpec((GATE_TM, d), lambda j, i: (i, 0)),
            pl.BlockSpec((d, GATE_TN), lambda j, i: (0, j)),
            pl.BlockSpec(slab(wa), step),
            pl.BlockSpec(slab(wp), step),
            pl.BlockSpec(slab(wout), step),
            pl.BlockSpec((1, d // steps), lambda j, i: (0, j * n_i + i)),
            pl.BlockSpec((d // steps, d), lambda j, i: (j * n_i + i, 2)),
            pl.BlockSpec((1, d), lambda j, i: (0, 2)),
        ],
        out_specs=[
            pl.BlockSpec((GATE_TM, GATE_TN), lambda j, i: (i, j)),
            pl.BlockSpec(slab(wa), step),
            pl.BlockSpec(slab(wp), step),
            pl.BlockSpec(slab(wout), step),
            pl.BlockSpec((1, d), const2),
        ],
        out_shape=[
            jax.ShapeDtypeStruct((s, n), bf16),
            jax.ShapeDtypeStruct(wa.shape, bf16),
            jax.ShapeDtypeStruct(wp.shape, bf16),
            jax.ShapeDtypeStruct(wout.shape, bf16),
            jax.ShapeDtypeStruct((1, d), jnp.float32),
        ],
        scratch_shapes=[pltpu.VMEM((d, GATE_TN), bf16)],
        compiler_params=pltpu.CompilerParams(
            dimension_semantics=("arbitrary", "arbitrary"), vmem_limit_bytes=VMEM_LIMIT_BYTES),
        name="merge_gate",
    )(h, w_merge, wa, wp, wout, c_row, w_ada, b_ada_row)


def _project_mix_kernel(sink_ref,
                        x_ref, shift_ref, scale_ref, g_ref, w_hbm_ref, wpool_ref, pscale_ref, bias_ref,
                        ya_ref, yp_ref, h_ref,
                        pbuf_ref, kvtail_ref, utail_ref, uext_ref, w_ref, wstage_ref, wsem_ref,
                        kt_ref, kttail_ref,
                        *, seq_len):
    s = pl.program_id(0)
    n_tok = pl.num_programs(0) - 1
    tm, d = x_ref.shape
    n_sub = tm // BLOCK
    n_blocks = seq_len // BLOCK
    aw = ATTN_WIDTH
    pw = pscale_ref.shape[1]
    q0, ga0, gp0 = Q0, GA0, U0 + pw
    n_chunks = w_ref.shape[1] // CW
    f32 = jnp.float32
    bf16 = jnp.bfloat16
    cur = s % 2
    prv = 1 - cur
    j = s - 1

    n_slots, slab_rows = wstage_ref.shape[0], wstage_ref.shape[1]
    assert n_slots * slab_rows == d and wstage_ref.shape[2] == CW

    def slab_copy(c, r):
        return pltpu.make_async_copy(
            w_hbm_ref.at[pl.ds(r * slab_rows, slab_rows), pl.ds(c * CW, CW)],
            wstage_ref.at[r], wsem_ref.at[r])

    def fetch_chunk(c):
        for r in range(n_slots):
            slab_copy(c, r).start()

    def land_chunk(c, next_c):
        for r in range(n_slots):
            slab_copy(c, r).wait()
            w_ref[r * slab_rows:(r + 1) * slab_rows, c * CW:(c + 1) * CW] = wstage_ref[r].astype(bf16)
            if next_c is not None:
                slab_copy(next_c, r).start()

    def prologue():
        gain = g_ref[...] * (1.0 + scale_ref[...])
        h_ref[...] = _adaln_rmsnorm(x_ref[...], gain, shift_ref[...]).astype(bf16)

    def is_gate_col(col):
        return ga0 <= col < ga0 + aw or gp0 <= col < gp0 + pw

    def proj_chunk(c):
        lo, hi = c * CW, (c + 1) * CW
        p = jnp.dot(h_ref[...], w_ref[:, lo:hi], preferred_element_type=f32)
        if is_gate_col(lo):
            assert is_gate_col(hi - 1)
            p = 0.5 * p
        pbuf_ref[cur, :, lo:hi] = p.astype(bf16)
        if lo <= KV0 < hi:
            assert KV0 + KV_WIDTH <= hi
            kt_ref[cur] = p[:, KV0 - lo:KV0 - lo + KV_WIDTH].T.astype(bf16)

    def window_rows(b, col0):
        r0, r1 = (b - 1) * BLOCK, (b + 2) * BLOCK
        parts = []
        if r0 < 0:
            parts.append(kvtail_ref[prv, :, col0 - KV0:col0 - KV0 + HEAD_DIM])
            r0 = 0
        parts.append(pbuf_ref[prv, r0:min(r1, tm), col0:col0 + HEAD_DIM])
        if r1 > tm:
            parts.append(pbuf_ref[cur, 0:r1 - tm, col0:col0 + HEAD_DIM])
        return jnp.concatenate(parts, axis=0) if len(parts) > 1 else parts[0]

    def key_window(b, kvh):
        hd = slice(kvh * HEAD_DIM, (kvh + 1) * HEAD_DIM)
        r0, r1 = (b - 1) * BLOCK, (b + 2) * BLOCK
        parts = []
        if r0 < 0:
            parts.append(kttail_ref[prv, hd, :])
            r0 = 0
        parts.append(kt_ref[prv, hd, r0:min(r1, tm)])
        if r1 > tm:
            parts.append(kt_ref[cur, hd, 0:r1 - tm])
        return jnp.concatenate(parts, axis=1) if len(parts) > 1 else parts[0]

    def attn_scores(b, kvh):
        blk = j * n_sub + b
        h0 = kvh * GQA_GROUP
        kt = key_window(b, kvh)
        v = window_rows(b, KV0 + KV_WIDTH + kvh * HEAD_DIM)
        qs = jnp.concatenate(
            [pbuf_ref[prv, b * BLOCK:(b + 1) * BLOCK,
                      q0 + (h0 + g) * HEAD_DIM:q0 + (h0 + g + 1) * HEAD_DIM]
             for g in range(GQA_GROUP)], axis=0)
        z = jnp.dot(qs, kt, preferred_element_type=f32)
        z = z + bias_ref[h0:h0 + GQA_GROUP].reshape(GQA_GROUP * BLOCK, SPAN)
        col = lax.broadcasted_iota(jnp.int32, (1, SPAN), 1)
        if b == 0:
            z = jnp.where(col < jnp.where(blk == 0, WINDOW, 0), NEG_INF, z)
        if b == n_sub - 1:
            z = jnp.where(col >= jnp.where(blk == n_blocks - 1, WINDOW + BLOCK, SPAN), NEG_INF, z)
        sink = jnp.concatenate(
            [jnp.full((BLOCK, 1), sink_ref[h0 + g] * INV_ATTN_SCALE, f32)
             for g in range(GQA_GROUP)], axis=0)
        m = jnp.maximum(jnp.max(z, axis=-1, keepdims=True), sink)
        p = jnp.exp2((z - m) * EXP2_SCALE)
        denom = jnp.sum(p, axis=-1, keepdims=True) + jnp.exp2((sink - m) * EXP2_SCALE)
        return p.astype(bf16), denom, v

    def attn_out(p, denom, v):
        o = jnp.dot(p, v, preferred_element_type=f32) / denom
        return [o[g * BLOCK:(g + 1) * BLOCK] for g in range(GQA_GROUP)]

    def finish_sub_block(b, heads):
        y = jnp.concatenate(heads, axis=1)
        hg = pbuf_ref[prv, b * BLOCK:(b + 1) * BLOCK, ga0:ga0 + aw].astype(f32)
        ya_ref[b * BLOCK:(b + 1) * BLOCK] = (y * _half_silu(hg)).astype(bf16)

    ext = tm + 2 * POOL_HALO

    def pool_group(gi):
        w = POOL_SIZES[gi]
        gw = pw // N_POOL_GROUPS
        c0, c1 = gi * gw, (gi + 1) * gw
        half = w // 2
        halo_row = lax.broadcasted_iota(jnp.int32, (POOL_HALO, 1), 0)
        uext_ref[0:POOL_HALO] = jnp.where(j * tm - POOL_HALO + halo_row >= 0,
                                          utail_ref[prv, :, c0:c1].astype(f32), 0.0)
        uext_ref[POOL_HALO:POOL_HALO + tm] = pbuf_ref[prv, :, U0 + c0:U0 + c1].astype(f32)
        uext_ref[POOL_HALO + tm:ext] = jnp.where(
            (j + 1) * tm + halo_row < seq_len,
            pbuf_ref[cur, 0:POOL_HALO, U0 + c0:U0 + c1].astype(f32), 0.0)
        a = uext_ref[...]
        sh = 1
        while sh < w:
            a = a + pltpu.roll(a, ext - sh, axis=0)
            sh *= 2
        off = POOL_HALO - half
        if off % SUBLANES:
            a = pltpu.roll(a, ext - off, axis=0)
            off = 0
        pos = j * tm + lax.broadcasted_iota(jnp.int32, (tm, 1), 0)
        cnt = (jnp.minimum(pos + half, seq_len) - jnp.maximum(pos - half, 0)).astype(f32)
        pooled = a[off:off + tm] / cnt - uext_ref[POOL_HALO:POOL_HALO + tm]
        mixed = jnp.dot(pooled.astype(bf16), wpool_ref[gi], preferred_element_type=f32)
        hg = pbuf_ref[prv, :, gp0 + c0:gp0 + c1].astype(f32)
        yp_ref[:, c0:c1] = (mixed * pscale_ref[:, c0:c1] * _half_silu(hg)).astype(bf16)

    def save_tails():
        kvtail_ref[cur] = pbuf_ref[prv, tm - BLOCK:tm, KV0:KV0 + 2 * KV_WIDTH]
        utail_ref[cur] = pbuf_ref[prv, tm - POOL_HALO:tm, U0:U0 + pw]
        kttail_ref[cur] = kt_ref[prv, :, tm - BLOCK:tm]

    units = [(b, kvh) for b in range(n_sub) for kvh in range(N_KV_HEADS)]
    assert KV0 % CW == 0 and (2 * KV_WIDTH) % CW == 0 and U0 % CW == 0 and pw % CW == 0
    first = list(range(KV0 // CW, (KV0 + 2 * KV_WIDTH) // CW)) + list(range(U0 // CW, (U0 + pw) // CW))
    order = first + [c for c in range(n_chunks) if c not in first]
    u_chunks_done = len(first)
    assert len(units) + 1 <= n_chunks and u_chunks_done + N_POOL_GROUPS <= n_chunks

    def mix_only():
        pending = [attn_scores(b, kvh) for b, kvh in units]
        heads = []
        for (b, kvh), scores in zip(units, pending):
            heads += attn_out(*scores)
            if kvh == N_KV_HEADS - 1:
                finish_sub_block(b, heads)
                heads = []
        for gi in range(N_POOL_GROUPS):
            pool_group(gi)
        save_tails()

    def project_first():
        fetch_chunk(order[0])
        prologue()
        for pos, c in enumerate(order):
            land_chunk(c, order[pos + 1] if pos + 1 < len(order) else None)
            proj_chunk(c)

    def project_and_mix():
        prologue()
        pending = attn_scores(*units[0])
        heads = []
        for pos, c in enumerate(order):
            proj_chunk(c)
            if pos < len(units):
                heads += attn_out(*pending)
                if pos + 1 < len(units):
                    pending = attn_scores(*units[pos + 1])
                b, kvh = units[pos]
                if kvh == N_KV_HEADS - 1:
                    finish_sub_block(b, heads)
                    heads = []
            if u_chunks_done <= pos < u_chunks_done + N_POOL_GROUPS:
                pool_group(pos - u_chunks_done)
        save_tails()

    @pl.when(s == 0)
    def _():
        kvtail_ref[0] = jnp.zeros(kvtail_ref.shape[1:], kvtail_ref.dtype)
        utail_ref[0] = jnp.zeros(utail_ref.shape[1:], utail_ref.dtype)
        kttail_ref[0] = jnp.zeros(kttail_ref.shape[1:], kttail_ref.dtype)
        project_first()

    @pl.when((s > 0) & (s < n_tok))
    def _():
        project_and_mix()

    @pl.when(s == n_tok)
    def _():
        mix_only()


def _project_mix(x2, shift, scale, pre_g_row, w_in, wpool, pool_scale_row, bias_tbl, sink):
    s, d = x2.shape
    tm = TM
    n_tok = s // tm
    pw = pool_scale_row.shape[1]
    n_in = w_in.shape[1]
    assert s // BLOCK >= 2, "first and last attention block must differ"
    assert n_in == U0 + 2 * pw and d % (W_STAGE_SLOTS * BF16_SUBLANES) == 0
    cur_blk = lambda st: jnp.minimum(st, n_tok - 1)
    mix_blk = lambda st: jnp.maximum(st - 1, 0)
    const2 = lambda st: (0, 0)
    single = pl.Buffered(1)
    bf16 = jnp.bfloat16
    return pl.pallas_call(
        functools.partial(_project_mix_kernel, seq_len=s),
        grid=(n_tok + 1,),
        in_specs=[
            pl.BlockSpec(memory_space=pltpu.SMEM),
            pl.BlockSpec((tm, d), lambda st: (cur_blk(st), 0)),
            pl.BlockSpec((1, d), const2),
            pl.BlockSpec((1, d), const2),
            pl.BlockSpec((1, d), const2),
            pl.BlockSpec(memory_space=pl.ANY),
            pl.BlockSpec(wpool.shape, lambda st: (0, 0, 0), pipeline_mode=single),
            pl.BlockSpec((1, pw), const2),
            pl.BlockSpec(bias_tbl.shape, lambda st: (0, 0, 0), pipeline_mode=single),
        ],
        out_specs=[
            pl.BlockSpec((tm, ATTN_WIDTH), lambda st: (mix_blk(st), 0)),
            pl.BlockSpec((tm, pw), lambda st: (mix_blk(st), 0)),
            pl.BlockSpec((tm, d), lambda st: (cur_blk(st), 0)),
        ],
        out_shape=[
            jax.ShapeDtypeStruct((s, ATTN_WIDTH), bf16),
            jax.ShapeDtypeStruct((s, pw), bf16),
            jax.ShapeDtypeStruct((s, d), bf16),
        ],
        scratch_shapes=[
            pltpu.VMEM((2, tm, n_in), bf16),
            pltpu.VMEM((2, BLOCK, 2 * KV_WIDTH), bf16),
            pltpu.VMEM((2, POOL_HALO, pw), bf16),
            pltpu.VMEM((tm + 2 * POOL_HALO, pw // N_POOL_GROUPS), jnp.float32),
            pltpu.VMEM(w_in.shape, bf16),
            pltpu.VMEM((W_STAGE_SLOTS, d // W_STAGE_SLOTS, CW), jnp.float32),
            pltpu.SemaphoreType.DMA((W_STAGE_SLOTS,)),
            pltpu.VMEM((2, KV_WIDTH, tm), bf16),
            pltpu.VMEM((2, KV_WIDTH, BLOCK), bf16),
        ],
        compiler_params=pltpu.CompilerParams(
            dimension_semantics=("arbitrary",), vmem_limit_bytes=VMEM_LIMIT_BYTES),
        name="project_mix",
    )(sink, x2, shift, scale, pre_g_row, w_in, wpool, pool_scale_row, bias_tbl)


def _merge_out_kernel(ya_ref, yp_ref, gm_ref, x_ref, gate_ref, postg_ref, bm_ref,
                      wa_ref, wp_ref, wout_ref, o_ref, merged_ref, oacc_ref, inv_ref):
    s = pl.program_id(0)
    n_tok = pl.num_programs(0) - 1
    tm, d = x_ref.shape
    n_chunks = d // CW
    f32 = jnp.float32

    def finish_previous():
        scale = gate_ref[...] * postg_ref[...]
        inv = inv_ref[...]
        for c in range(n_chunks):
            lo, hi = c * CW, (c + 1) * CW
            o_ref[:, lo:hi] = x_ref[:, lo:hi] + (oacc_ref[:, lo:hi] * inv) * scale[:, lo:hi]

    def matmuls():
        half_bm = 0.5 * bm_ref[...]
        for c in range(n_chunks):
            lo, hi = c * CW, (c + 1) * CW
            bra = jnp.dot(ya_ref[...], wa_ref[:, lo:hi], preferred_element_type=f32)
            brp = jnp.dot(yp_ref[...], wp_ref[:, lo:hi], preferred_element_type=f32)
            g_a = 0.5 * jnp.tanh(gm_ref[:, lo:hi].astype(f32) + half_bm[:, lo:hi]) + 0.5
            g_p = 0.5 * jnp.tanh(gm_ref[:, d + lo:d + hi].astype(f32)
                                 + half_bm[:, d + lo:d + hi]) + 0.5
            merged_ref[:, lo:hi] = (g_a * bra + g_p * brp).astype(merged_ref.dtype)
        ssq = jnp.zeros((tm, 1), f32)
        for c in range(n_chunks):
            lo, hi = c * CW, (c + 1) * CW
            o = jnp.dot(merged_ref[...], wout_ref[:, lo:hi], preferred_element_type=f32)
            ssq = ssq + jnp.sum(o * o, axis=-1, keepdims=True)
            oacc_ref[:, lo:hi] = o
        inv_ref[...] = lax.rsqrt(ssq * (1.0 / d) + EPS)

    @pl.when(s == 0)
    def _():
        matmuls()

    @pl.when((s > 0) & (s < n_tok))
    def _():
        finish_previous()
        matmuls()

    @pl.when(s == n_tok)
    def _():
        finish_previous()


def _merge_out(ya, yp, gm, x2, gate, post_g_row, b_merge_row, wa, wp, wout):
    s, d = x2.shape
    tm = TM
    n_tok = s // tm
    aw, pw = ya.shape[1], yp.shape[1]
    cur = lambda st: (jnp.minimum(st, n_tok - 1), 0)
    prev = lambda st: (jnp.maximum(st - 1, 0), 0)
    const2 = lambda st: (0, 0)
    single = pl.Buffered(1)
    return pl.pallas_call(
        _merge_out_kernel,
        grid=(n_tok + 1,),
        in_specs=[
            pl.BlockSpec((tm, aw), cur),
            pl.BlockSpec((tm, pw), cur),
            pl.BlockSpec((tm, 2 * d), cur),
            pl.BlockSpec((tm, d), prev),
            pl.BlockSpec((1, d), const2),
            pl.BlockSpec((1, d), const2),
            pl.BlockSpec((1, 2 * d), const2),
            pl.BlockSpec((aw, d), const2, pipeline_mode=single),
            pl.BlockSpec((pw, d), const2, pipeline_mode=single),
            pl.BlockSpec((d, d), const2, pipeline_mode=single),
        ],
        out_specs=pl.BlockSpec((tm, d), prev),
        out_shape=jax.ShapeDtypeStruct((s, d), jnp.float32),
        scratch_shapes=[
            pltpu.VMEM((tm, d), jnp.bfloat16),
            pltpu.VMEM((tm, d), jnp.float32),
            pltpu.VMEM((tm, 1), jnp.float32),
        ],
        compiler_params=pltpu.CompilerParams(
            dimension_semantics=("arbitrary",), vmem_limit_bytes=VMEM_LIMIT_BYTES),
        name="merge_out",
    )(ya, yp, gm, x2, gate, post_g_row, b_merge_row, wa, wp, wout)


def _layer(x2, c, rel_table, bucket, w_ada, b_ada, pre_g, post_g, w_in, sink, w_pool, pool_scale,
           w_br_attn, w_br_pool, w_merge, b_merge, w_out):
    s, d = x2.shape
    c_row = c.reshape(1, d)
    b_ada_row = b_ada.reshape(1, -1)
    pre_g_row = pre_g.reshape(1, d)
    shift, scale, bias_tbl, w_pool_bf = _shift_scale_bias(c_row, w_ada, b_ada_row, rel_table, bucket,
                                                          w_pool)
    ya, yp, h = _project_mix(x2, shift, scale, pre_g_row, w_in, w_pool_bf.reshape(w_pool.shape),
                             pool_scale.reshape(1, -1), bias_tbl, sink)
    gm, wa, wp, wout, gate = _merge_gate(h, w_merge, w_br_attn, w_br_pool, w_out,
                                         c_row, w_ada, b_ada_row)
    return _merge_out(ya, yp, gm, x2, gate, post_g.reshape(1, d), b_merge.reshape(1, -1),
                      wa, wp, wout)


def kernel(x, c, rel_bias_table, w_ada, b_ada, pre_norm_g, post_norm_g, w_in, attn_sink,
           w_pool_group, pool_scale, w_branch_attn, w_branch_pool, w_merge, b_merge, w_out):
    batch, s, d = x.shape
    assert batch == 1, "kernel is written for a single sequence"
    depth = w_ada.shape[0]
    bucket = _bucket_index_table()
    x2 = x.reshape(s, d)
    for l in range(depth):
        x2 = _layer(x2, c[0], rel_bias_table, bucket, w_ada[l], b_ada[l], pre_norm_g[l],
                    post_norm_g[l], w_in[l], attn_sink[l], w_pool_group[l], pool_scale[l],
                    w_branch_attn[l], w_branch_pool[l], w_merge[l], b_merge[l], w_out[l])
    return x2.reshape(batch, s, d)
```

```python
import functools
import math

import numpy as np
import jax
import jax.numpy as jnp
from jax import lax
from jax.experimental import pallas as pl
from jax.experimental.pallas import tpu as pltpu

HEAD_DIM = 128
N_Q_HEADS = 8
N_KV_HEADS = 2
GQA_GROUP = N_Q_HEADS // N_KV_HEADS
ATTN_WIDTH = N_Q_HEADS * HEAD_DIM
KV_WIDTH = N_KV_HEADS * HEAD_DIM
WINDOW = 128
BLOCK = 128
SPAN = BLOCK + 2 * WINDOW
N_BUCKETS = 32
MAX_DISTANCE = 128
POOL_SIZES = (2, 4, 8, 16)
N_POOL_GROUPS = len(POOL_SIZES)
EPS = 1e-6
NEG_INF = -1e30
MASKED_BUCKET = N_BUCKETS
ATTN_SCALE = HEAD_DIM ** -0.5
INV_ATTN_SCALE = HEAD_DIM ** 0.5
EXP2_SCALE = ATTN_SCALE * math.log2(math.e)

SUBLANES = 8
BF16_SUBLANES = 16
VMEM_LIMIT_BYTES = 61 * 1024 * 1024

MOD_TK = 256
TM = 512
CW = 512
GATE_TM = 1024
GATE_NG = 2048
GATE_TN = 1024
POOL_HALO = BF16_SUBLANES
W_STAGE_SLOTS = 8

Q0 = 0
KV0 = Q0 + ATTN_WIDTH
GA0 = KV0 + 2 * KV_WIDTH
U0 = GA0 + ATTN_WIDTH


def _silu(v):
    return v * (1.0 / (1.0 + jnp.exp(-v)))


def _half_silu(hv):
    return hv + hv * jnp.tanh(hv)


def _silu_column(c_row):
    n = c_row.shape[1]
    on_diag = (lax.broadcasted_iota(jnp.int32, (n, n), 0)
               == lax.broadcasted_iota(jnp.int32, (n, n), 1))
    return jnp.sum(jnp.where(on_diag, _silu(c_row), 0.0), axis=1, keepdims=True)


def _adaln_rmsnorm(x, gain, shift):
    ms = jnp.mean(x * x, axis=-1, keepdims=True)
    return (x * lax.rsqrt(ms + EPS)) * gain + shift


def _shift_scale_bias_kernel(tbl_ref, c_ref, w_ref, bsh_ref, bsc_ref, bucket_ref, wpool_ref,
                             shift_ref, scale_ref, bias_ref, wpool_o_ref, *, tile_buckets):
    k = pl.program_id(0)
    d = shift_ref.shape[1]

    @pl.when(k == 0)
    def _():
        shift_ref[...] = bsh_ref[...]
        scale_ref[...] = bsc_ref[...]

    s = _silu_column(c_ref[...])
    both = jnp.sum(s * w_ref[...], axis=0, keepdims=True)
    shift_ref[...] += both[:, :d]
    scale_ref[...] += both[:, d:]

    wpool_o_ref[...] = wpool_ref[...].astype(wpool_o_ref.dtype)

    for t, buckets in enumerate(tile_buckets):
        bk = bucket_ref[:, t * BLOCK:(t + 1) * BLOCK]
        acc = jnp.full((BLOCK, BLOCK), NEG_INF, jnp.float32)
        for b in buckets:
            acc = jnp.where(bk == b, tbl_ref[k, b] * INV_ATTN_SCALE, acc)
        bias_ref[0, :, t * BLOCK:(t + 1) * BLOCK] = acc


def _shift_scale_bias(c_row, w_ada, b_ada_row, rel_table, bucket, wpool):
    d = w_ada.shape[0]
    steps = d // MOD_TK
    assert steps == N_Q_HEADS, "one bias-table head per grid step"
    wpool2 = wpool.reshape(-1, wpool.shape[-1])
    pool_slab = (wpool2.shape[0] // steps, wpool2.shape[1])
    assert wpool2.shape[0] % (steps * BF16_SUBLANES) == 0
    tile_buckets = tuple(
        tuple(int(b) for b in np.unique(bucket[:, t * BLOCK:(t + 1) * BLOCK]) if b != MASKED_BUCKET)
        for t in range(SPAN // BLOCK))
    return pl.pallas_call(
        functools.partial(_shift_scale_bias_kernel, tile_buckets=tile_buckets),
        grid=(steps,),
        in_specs=[
            pl.BlockSpec(memory_space=pltpu.SMEM),
            pl.BlockSpec((1, MOD_TK), lambda k: (0, k)),
            pl.BlockSpec((MOD_TK, 2 * d), lambda k: (k, 0)),
            pl.BlockSpec((1, d), lambda k: (0, 0)),
            pl.BlockSpec((1, d), lambda k: (0, 1)),
            pl.BlockSpec(bucket.shape, lambda k: (0, 0)),
            pl.BlockSpec(pool_slab, lambda k: (k, 0)),
        ],
        out_specs=[
            pl.BlockSpec((1, d), lambda k: (0, 0)),
            pl.BlockSpec((1, d), lambda k: (0, 0)),
            pl.BlockSpec((1, BLOCK, SPAN), lambda k: (k, 0, 0)),
            pl.BlockSpec(pool_slab, lambda k: (k, 0)),
        ],
        out_shape=[
            jax.ShapeDtypeStruct((1, d), jnp.float32),
            jax.ShapeDtypeStruct((1, d), jnp.float32),
            jax.ShapeDtypeStruct((N_Q_HEADS, BLOCK, SPAN), jnp.float32),
            jax.ShapeDtypeStruct(wpool2.shape, jnp.bfloat16),
        ],
        compiler_params=pltpu.CompilerParams(
            dimension_semantics=("arbitrary",), vmem_limit_bytes=VMEM_LIMIT_BYTES),
        name="shift_scale_bias",
    )(rel_table.T, c_row, w_ada, b_ada_row, b_ada_row, bucket, wpool2)


def _t5_bucket(rel):
    half = N_BUCKETS // 2
    max_exact = half // 2
    assert (max_exact, MAX_DISTANCE // max_exact, half - max_exact) == (8, 16, 8)
    n = abs(rel)
    large = min(max_exact + (n * n).bit_length() - 7, half - 1)
    return (half if rel > 0 else 0) + (n if n < max_exact else large)


def _bucket_index_table():
    table = np.full((BLOCK, SPAN), MASKED_BUCKET, np.int32)
    for q in range(BLOCK):
        for t in range(SPAN):
            rel = t - WINDOW - q
            if abs(rel) <= WINDOW:
                table[q, t] = _t5_bucket(rel)
    return table


def _merge_gate_kernel(h_ref, w_hbm_ref, wa_ref, wp_ref, wout_ref, c_ref, wgate_ref, bgate_ref,
                       o_ref, wa_o_ref, wp_o_ref, wout_o_ref, gate_ref,
                       wbf_ref, wstage_ref, wsem_ref):
    j, i = pl.program_id(0), pl.program_id(1)
    d, n = wbf_ref.shape
    f32 = jnp.float32

    assert wstage_ref.shape == (2, d, CW) and n % CW == 0
    n_chunks = n // CW

    def chunk_copy(c):
        col = pl.multiple_of(j * n + c * CW, CW)
        return pltpu.make_async_copy(
            w_hbm_ref.at[:, pl.ds(col, CW)], wstage_ref.at[c % 2], wsem_ref.at[c % 2])

    def cast_chunk(c):
        wbf_ref[:, c * CW:(c + 1) * CW] = (0.5 * wstage_ref[c % 2]).astype(wbf_ref.dtype)

    def dot_columns(lo, hi):
        o_ref[:, lo:hi] = jnp.dot(h_ref[...], wbf_ref[:, lo:hi],
                                  preferred_element_type=f32).astype(o_ref.dtype)

    def side_jobs():
        wa_o_ref[...] = wa_ref[...].astype(wa_o_ref.dtype)
        wp_o_ref[...] = wp_ref[...].astype(wp_o_ref.dtype)
        wout_o_ref[...] = wout_ref[...].astype(wout_o_ref.dtype)
        gate_ref[...] += jnp.sum(_silu_column(c_ref[...]) * wgate_ref[...], axis=0, keepdims=True)

    @pl.when(i == 0)
    def _():
        for c in range(min(2, n_chunks)):
            chunk_copy(c).start()

        @pl.when(j == 0)
        def _():
            gate_ref[...] = bgate_ref[...]

        chunk_copy(0).wait()
        cast_chunk(0)
        if 2 < n_chunks:
            chunk_copy(2).start()
        for c in range(n_chunks):
            if c + 1 < n_chunks:
                chunk_copy(c + 1).wait()
            dot_columns(c * CW, (c + 1) * CW)
            if c + 1 < n_chunks:
                cast_chunk(c + 1)
                if c + 3 < n_chunks:
                    chunk_copy(c + 3).start()
        side_jobs()

    @pl.when(i > 0)
    def _():
        for lo in range(0, n, GATE_TN):
            dot_columns(lo, lo + GATE_TN)
        side_jobs()


def _merge_gate(h, w_merge, wa, wp, wout, c_row, w_ada, b_ada_row):
    s, d = h.shape
    n = w_merge.shape[1]
    n_i = s // GATE_TM
    steps = (n // GATE_NG) * n_i
    slab = lambda w: (w.shape[0] // steps, w.shape[1])
    step = lambda j, i: (j * n_i + i, 0)
    for w in (wa, wp, wout):
        assert w.shape[0] % (steps * BF16_SUBLANES) == 0
    assert d % (steps * SUBLANES) == 0 and GATE_NG % GATE_TN == 0
    bf16 = jnp.bfloat16
    return pl.pallas_call(
        _merge_gate_kernel,
        grid=(n // GATE_NG, n_i),
        in_specs=[
            pl.BlockSpec((GATE_TM, d), lambda j, i: (i, 0)),
            pl.BlockSpec(memory_space=pl.ANY),
            pl.BlockSpec(slab(wa), step),
            pl.BlockSpec(slab(wp), step),
            pl.BlockSpec(slab(wout), step),
            pl.BlockSpec((1, d // steps), lambda j, i: (0, j * n_i + i)),
            pl.BlockSpec((d // steps, d), lambda j, i: (j * n_i + i, 2)),
            pl.BlockSpec((1, d), lambda j, i: (0, 2)),
        ],
        out_specs=[
            pl.BlockSpec((GATE_TM, GATE_NG), lambda j, i: (i, j)),
            pl.BlockSpec(slab(wa), step),
            pl.BlockSpec(slab(wp), step),
            pl.BlockSpec(slab(wout), step),
            pl.BlockSpec((1, d), lambda j, i: (0, 0)),
        ],
        out_shape=[
            jax.ShapeDtypeStruct((s, n), bf16),
            jax.ShapeDtypeStruct(wa.shape, bf16),
            jax.ShapeDtypeStruct(wp.shape, bf16),
            jax.ShapeDtypeStruct(wout.shape, bf16),
            jax.ShapeDtypeStruct((1, d), jnp.float32),
        ],
        scratch_shapes=[
            pltpu.VMEM((d, GATE_NG), bf16),
            pltpu.VMEM((2, d, CW), jnp.float32),
            pltpu.SemaphoreType.DMA((2,)),
        ],
        compiler_params=pltpu.CompilerParams(
            dimension_semantics=("arbitrary", "arbitrary"), vmem_limit_bytes=VMEM_LIMIT_BYTES),
        name="merge_gate",
    )(h, w_merge, wa, wp, wout, c_row, w_ada, b_ada_row)


def _project_mix_kernel(sink_ref,
                        x_ref, shift_ref, scale_ref, g_ref, w_hbm_ref, wpool_ref, pscale_ref, bias_ref,
                        ya_ref, yp_ref, h_ref,
                        pbuf_ref, kvtail_ref, utail_ref, uext_ref, w_ref, wstage_ref, wsem_ref,
                        kt_ref, kttail_ref,
                        *, seq_len):
    s = pl.program_id(0)
    n_tok = pl.num_programs(0) - 1
    tm, d = x_ref.shape
    n_sub = tm // BLOCK
    n_blocks = seq_len // BLOCK
    aw = ATTN_WIDTH
    pw = pscale_ref.shape[1]
    q0, ga0, gp0 = Q0, GA0, U0 + pw
    n_chunks = w_ref.shape[1] // CW
    f32 = jnp.float32
    bf16 = jnp.bfloat16
    cur = s % 2
    prv = 1 - cur
    j = s - 1

    n_slots, slab_rows = wstage_ref.shape[0], wstage_ref.shape[1]
    assert n_slots * slab_rows == d and wstage_ref.shape[2] == CW

    def slab_copy(c, r):
        return pltpu.make_async_copy(
            w_hbm_ref.at[pl.ds(r * slab_rows, slab_rows), pl.ds(c * CW, CW)],
            wstage_ref.at[r], wsem_ref.at[r])

    def fetch_chunk(c):
        for r in range(n_slots):
            slab_copy(c, r).start()

    def land_chunk(c, next_c):
        for r in range(n_slots):
            slab_copy(c, r).wait()
            w_ref[r * slab_rows:(r + 1) * slab_rows, c * CW:(c + 1) * CW] = wstage_ref[r].astype(bf16)
            if next_c is not None:
                slab_copy(next_c, r).start()

    def prologue():
        gain = g_ref[...] * (1.0 + scale_ref[...])
        h_ref[...] = _adaln_rmsnorm(x_ref[...], gain, shift_ref[...]).astype(bf16)

    def is_gate_col(col):
        return ga0 <= col < ga0 + aw or gp0 <= col < gp0 + pw

    def proj_chunk(c):
        lo, hi = c * CW, (c + 1) * CW
        p = jnp.dot(h_ref[...], w_ref[:, lo:hi], preferred_element_type=f32)
        if is_gate_col(lo):
            assert is_gate_col(hi - 1)
            p = 0.5 * p
        pbuf_ref[cur, :, lo:hi] = p.astype(bf16)
        if lo <= KV0 < hi:
            assert KV0 + KV_WIDTH <= hi
            kt_ref[cur] = p[:, KV0 - lo:KV0 - lo + KV_WIDTH].T.astype(bf16)

    def window_rows(b, col0):
        r0, r1 = (b - 1) * BLOCK, (b + 2) * BLOCK
        parts = []
        if r0 < 0:
            parts.append(kvtail_ref[prv, :, col0 - KV0:col0 - KV0 + HEAD_DIM])
            r0 = 0
        parts.append(pbuf_ref[prv, r0:min(r1, tm), col0:col0 + HEAD_DIM])
        if r1 > tm:
            parts.append(pbuf_ref[cur, 0:r1 - tm, col0:col0 + HEAD_DIM])
        return jnp.concatenate(parts, axis=0) if len(parts) > 1 else parts[0]

    def key_window(b, kvh):
        hd = slice(kvh * HEAD_DIM, (kvh + 1) * HEAD_DIM)
        r0, r1 = (b - 1) * BLOCK, (b + 2) * BLOCK
        parts = []
        if r0 < 0:
            parts.append(kttail_ref[prv, hd, :])
            r0 = 0
        parts.append(kt_ref[prv, hd, r0:min(r1, tm)])
        if r1 > tm:
            parts.append(kt_ref[cur, hd, 0:r1 - tm])
        return jnp.concatenate(parts, axis=1) if len(parts) > 1 else parts[0]

    def attn_scores(b, kvh):
        blk = j * n_sub + b
        h0 = kvh * GQA_GROUP
        kt = key_window(b, kvh)
        v = window_rows(b, KV0 + KV_WIDTH + kvh * HEAD_DIM)
        qs = jnp.concatenate(
            [pbuf_ref[prv, b * BLOCK:(b + 1) * BLOCK,
                      q0 + (h0 + g) * HEAD_DIM:q0 + (h0 + g + 1) * HEAD_DIM]
             for g in range(GQA_GROUP)], axis=0)
        z = jnp.dot(qs, kt, preferred_element_type=f32)
        z = z + bias_ref[h0:h0 + GQA_GROUP].reshape(GQA_GROUP * BLOCK, SPAN)
        col = lax.broadcasted_iota(jnp.int32, (1, SPAN), 1)
        if b == 0:
            z = jnp.where(col < jnp.where(blk == 0, WINDOW, 0), NEG_INF, z)
        if b == n_sub - 1:
            z = jnp.where(col >= jnp.where(blk == n_blocks - 1, WINDOW + BLOCK, SPAN), NEG_INF, z)
        sink = jnp.concatenate(
            [jnp.full((BLOCK, 1), sink_ref[h0 + g] * INV_ATTN_SCALE, f32)
             for g in range(GQA_GROUP)], axis=0)
        m = jnp.maximum(jnp.max(z, axis=-1, keepdims=True), sink)
        p = jnp.exp2((z - m) * EXP2_SCALE)
        denom = jnp.sum(p, axis=-1, keepdims=True) + jnp.exp2((sink - m) * EXP2_SCALE)
        return p.astype(bf16), denom, v

    def attn_out(p, denom, v):
        o = jnp.dot(p, v, preferred_element_type=f32) / denom
        return [o[g * BLOCK:(g + 1) * BLOCK] for g in range(GQA_GROUP)]

    def finish_sub_block(b, heads):
        y = jnp.concatenate(heads, axis=1)
        hg = pbuf_ref[prv, b * BLOCK:(b + 1) * BLOCK, ga0:ga0 + aw].astype(f32)
        ya_ref[b * BLOCK:(b + 1) * BLOCK] = (y * _half_silu(hg)).astype(bf16)

    ext = tm + 2 * POOL_HALO

    def pool_group(gi):
        w = POOL_SIZES[gi]
        gw = pw // N_POOL_GROUPS
        c0, c1 = gi * gw, (gi + 1) * gw
        half = w // 2
        halo_row = lax.broadcasted_iota(jnp.int32, (POOL_HALO, 1), 0)
        uext_ref[0:POOL_HALO] = jnp.where(j * tm - POOL_HALO + halo_row >= 0,
                                          utail_ref[prv, :, c0:c1].astype(f32), 0.0)
        uext_ref[POOL_HALO:POOL_HALO + tm] = pbuf_ref[prv, :, U0 + c0:U0 + c1].astype(f32)
        uext_ref[POOL_HALO + tm:ext] = jnp.where(
            (j + 1) * tm + halo_row < seq_len,
            pbuf_ref[cur, 0:POOL_HALO, U0 + c0:U0 + c1].astype(f32), 0.0)
        a = uext_ref[...]
        sh = 1
        while sh < w:
            a = a + pltpu.roll(a, ext - sh, axis=0)
            sh *= 2
        off = POOL_HALO - half
        if off % SUBLANES:
            a = pltpu.roll(a, ext - off, axis=0)
            off = 0
        pos = j * tm + lax.broadcasted_iota(jnp.int32, (tm, 1), 0)
        cnt = (jnp.minimum(pos + half, seq_len) - jnp.maximum(pos - half, 0)).astype(f32)
        pooled = a[off:off + tm] / cnt - uext_ref[POOL_HALO:POOL_HALO + tm]
        mixed = jnp.dot(pooled.astype(bf16), wpool_ref[gi], preferred_element_type=f32)
        hg = pbuf_ref[prv, :, gp0 + c0:gp0 + c1].astype(f32)
        yp_ref[:, c0:c1] = (mixed * pscale_ref[:, c0:c1] * _half_silu(hg)).astype(bf16)

    def save_tails():
        kvtail_ref[cur] = pbuf_ref[prv, tm - BLOCK:tm, KV0:KV0 + 2 * KV_WIDTH]
        utail_ref[cur] = pbuf_ref[prv, tm - POOL_HALO:tm, U0:U0 + pw]
        kttail_ref[cur] = kt_ref[prv, :, tm - BLOCK:tm]

    units = [(b, kvh) for b in range(n_sub) for kvh in range(N_KV_HEADS)]
    assert KV0 % CW == 0 and (2 * KV_WIDTH) % CW == 0 and U0 % CW == 0 and pw % CW == 0
    first = list(range(KV0 // CW, (KV0 + 2 * KV_WIDTH) // CW)) + list(range(U0 // CW, (U0 + pw) // CW))
    order = first + [c for c in range(n_chunks) if c not in first]
    u_chunks_done = len(first)
    assert len(units) + 1 <= n_chunks and u_chunks_done + N_POOL_GROUPS <= n_chunks

    def mix_only():
        pending = [attn_scores(b, kvh) for b, kvh in units]
        heads = []
        for (b, kvh), scores in zip(units, pending):
            heads += attn_out(*scores)
            if kvh == N_KV_HEADS - 1:
                finish_sub_block(b, heads)
                heads = []
        for gi in range(N_POOL_GROUPS):
            pool_group(gi)
        save_tails()

    def project_first():
        fetch_chunk(order[0])
        prologue()
        for pos, c in enumerate(order):
            land_chunk(c, order[pos + 1] if pos + 1 < len(order) else None)
            proj_chunk(c)

    def project_and_mix():
        prologue()
        pending = attn_scores(*units[0])
        heads = []
        for pos, c in enumerate(order):
            proj_chunk(c)
            if pos < len(units):
                heads += attn_out(*pending)
                if pos + 1 < len(units):
                    pending = attn_scores(*units[pos + 1])
                b, kvh = units[pos]
                if kvh == N_KV_HEADS - 1:
                    finish_sub_block(b, heads)
                    heads = []
            if u_chunks_done <= pos < u_chunks_done + N_POOL_GROUPS:
                pool_group(pos - u_chunks_done)
        save_tails()

    @pl.when(s == 0)
    def _():
        kvtail_ref[0] = jnp.zeros(kvtail_ref.shape[1:], kvtail_ref.dtype)
        utail_ref[0] = jnp.zeros(utail_ref.shape[1:], utail_ref.dtype)
        kttail_ref[0] = jnp.zeros(kttail_ref.shape[1:], kttail_ref.dtype)
        project_first()

    @pl.when((s > 0) & (s < n_tok))
    def _():
        project_and_mix()

    @pl.when(s == n_tok)
    def _():
        mix_only()


def _project_mix(x2, shift, scale, pre_g_row, w_in, wpool, pool_scale_row, bias_tbl, sink):
    s, d = x2.shape
    tm = TM
    n_tok = s // tm
    pw = pool_scale_row.shape[1]
    n_in = w_in.shape[1]
    assert s // BLOCK >= 2, "first and last attention block must differ"
    assert n_in == U0 + 2 * pw and d % (W_STAGE_SLOTS * BF16_SUBLANES) == 0
    cur_blk = lambda st: jnp.minimum(st, n_tok - 1)
    mix_blk = lambda st: jnp.maximum(st - 1, 0)
    const2 = lambda st: (0, 0)
    single = pl.Buffered(1)
    bf16 = jnp.bfloat16
    return pl.pallas_call(
        functools.partial(_project_mix_kernel, seq_len=s),
        grid=(n_tok + 1,),
        in_specs=[
            pl.BlockSpec(memory_space=pltpu.SMEM),
            pl.BlockSpec((tm, d), lambda st: (cur_blk(st), 0)),
            pl.BlockSpec((1, d), const2),
            pl.BlockSpec((1, d), const2),
            pl.BlockSpec((1, d), const2),
            pl.BlockSpec(memory_space=pl.ANY),
            pl.BlockSpec(wpool.shape, lambda st: (0, 0, 0), pipeline_mode=single),
            pl.BlockSpec((1, pw), const2),
            pl.BlockSpec(bias_tbl.shape, lambda st: (0, 0, 0), pipeline_mode=single),
        ],
        out_specs=[
            pl.BlockSpec((tm, ATTN_WIDTH), lambda st: (mix_blk(st), 0)),
            pl.BlockSpec((tm, pw), lambda st: (mix_blk(st), 0)),
            pl.BlockSpec((tm, d), lambda st: (cur_blk(st), 0)),
        ],
        out_shape=[
            jax.ShapeDtypeStruct((s, ATTN_WIDTH), bf16),
            jax.ShapeDtypeStruct((s, pw), bf16),
            jax.ShapeDtypeStruct((s, d), bf16),
        ],
        scratch_shapes=[
            pltpu.VMEM((2, tm, n_in), bf16),
            pltpu.VMEM((2, BLOCK, 2 * KV_WIDTH), bf16),
            pltpu.VMEM((2, POOL_HALO, pw), bf16),
            pltpu.VMEM((tm + 2 * POOL_HALO, pw // N_POOL_GROUPS), jnp.float32),
            pltpu.VMEM(w_in.shape, bf16),
            pltpu.VMEM((W_STAGE_SLOTS, d // W_STAGE_SLOTS, CW), jnp.float32),
            pltpu.SemaphoreType.DMA((W_STAGE_SLOTS,)),
            pltpu.VMEM((2, KV_WIDTH, tm), bf16),
            pltpu.VMEM((2, KV_WIDTH, BLOCK), bf16),
        ],
        compiler_params=pltpu.CompilerParams(
            dimension_semantics=("arbitrary",), vmem_limit_bytes=VMEM_LIMIT_BYTES),
        name="project_mix",
    )(sink, x2, shift, scale, pre_g_row, w_in, wpool, pool_scale_row, bias_tbl)


def _merge_out_kernel(ya_ref, yp_ref, gm_ref, x_ref, gate_ref, postg_ref, bm_ref,
                      wa_ref, wp_ref, wout_ref, o_ref, merged_ref, oacc_ref, inv_ref):
    s = pl.program_id(0)
    n_tok = pl.num_programs(0) - 1
    tm, d = x_ref.shape
    n_chunks = d // CW
    f32 = jnp.float32

    def finish_previous():
        scale = gate_ref[...] * postg_ref[...]
        inv = inv_ref[...]
        for c in range(n_chunks):
            lo, hi = c * CW, (c + 1) * CW
            o_ref[:, lo:hi] = x_ref[:, lo:hi] + (oacc_ref[:, lo:hi] * inv) * scale[:, lo:hi]

    def matmuls():
        half_bm = 0.5 * bm_ref[...]
        for c in range(n_chunks):
            lo, hi = c * CW, (c + 1) * CW
            bra = jnp.dot(ya_ref[...], wa_ref[:, lo:hi], preferred_element_type=f32)
            brp = jnp.dot(yp_ref[...], wp_ref[:, lo:hi], preferred_element_type=f32)
            g_a = 0.5 * jnp.tanh(gm_ref[:, lo:hi].astype(f32) + half_bm[:, lo:hi]) + 0.5
            g_p = 0.5 * jnp.tanh(gm_ref[:, d + lo:d + hi].astype(f32)
                                 + half_bm[:, d + lo:d + hi]) + 0.5
            merged_ref[:, lo:hi] = (g_a * bra + g_p * brp).astype(merged_ref.dtype)
        ssq = jnp.zeros((tm, 1), f32)
        for c in range(n_chunks):
            lo, hi = c * CW, (c + 1) * CW
            o = jnp.dot(merged_ref[...], wout_ref[:, lo:hi], preferred_element_type=f32)
            ssq = ssq + jnp.sum(o * o, axis=-1, keepdims=True)
            oacc_ref[:, lo:hi] = o
        inv_ref[...] = lax.rsqrt(ssq * (1.0 / d) + EPS)

    @pl.when(s == 0)
    def _():
        matmuls()

    @pl.when((s > 0) & (s < n_tok))
    def _():
        finish_previous()
        matmuls()

    @pl.when(s == n_tok)
    def _():
        finish_previous()


def _merge_out(ya, yp, gm, x2, gate, post_g_row, b_merge_row, wa, wp, wout):
    s, d = x2.shape
    tm = TM
    n_tok = s // tm
    aw, pw = ya.shape[1], yp.shape[1]
    cur = lambda st: (jnp.minimum(st, n_tok - 1), 0)
    prev = lambda st: (jnp.maximum(st - 1, 0), 0)
    const2 = lambda st: (0, 0)
    single = pl.Buffered(1)
    return pl.pallas_call(
        _merge_out_kernel,
        grid=(n_tok + 1,),
        in_specs=[
            pl.BlockSpec((tm, aw), cur),
            pl.BlockSpec((tm, pw), cur),
            pl.BlockSpec((tm, 2 * d), cur),
            pl.BlockSpec((tm, d), prev),
            pl.BlockSpec((1, d), const2),
            pl.BlockSpec((1, d), const2),
            pl.BlockSpec((1, 2 * d), const2),
            pl.BlockSpec((aw, d), const2, pipeline_mode=single),
            pl.BlockSpec((pw, d), const2, pipeline_mode=single),
            pl.BlockSpec((d, d), const2, pipeline_mode=single),
        ],
        out_specs=pl.BlockSpec((tm, d), prev),
        out_shape=jax.ShapeDtypeStruct((s, d), jnp.float32),
        scratch_shapes=[
            pltpu.VMEM((tm, d), jnp.bfloat16),
            pltpu.VMEM((tm, d), jnp.float32),
            pltpu.VMEM((tm, 1), jnp.float32),
        ],
        compiler_params=pltpu.CompilerParams(
            dimension_semantics=("arbitrary",), vmem_limit_bytes=VMEM_LIMIT_BYTES),
        name="merge_out",
    )(ya, yp, gm, x2, gate, post_g_row, b_merge_row, wa, wp, wout)


def _layer(x2, c, rel_table, bucket, w_ada, b_ada, pre_g, post_g, w_in, sink, w_pool, pool_scale,
           w_br_attn, w_br_pool, w_merge, b_merge, w_out):
    s, d = x2.shape
    c_row = c.reshape(1, d)
    b_ada_row = b_ada.reshape(1, -1)
    pre_g_row = pre_g.reshape(1, d)
    shift, scale, bias_tbl, w_pool_bf = _shift_scale_bias(c_row, w_ada, b_ada_row, rel_table, bucket,
                                                          w_pool)
    ya, yp, h = _project_mix(x2, shift, scale, pre_g_row, w_in, w_pool_bf.reshape(w_pool.shape),
                             pool_scale.reshape(1, -1), bias_tbl, sink)
    gm, wa, wp, wout, gate = _merge_gate(h, w_merge, w_br_attn, w_br_pool, w_out,
                                         c_row, w_ada, b_ada_row)
    return _merge_out(ya, yp, gm, x2, gate, post_g.reshape(1, d), b_merge.reshape(1, -1),
                      wa, wp, wout)


def kernel(x, c, rel_bias_table, w_ada, b_ada, pre_norm_g, post_norm_g, w_in, attn_sink,
           w_pool_group, pool_scale, w_branch_attn, w_branch_pool, w_merge, b_merge, w_out):
    batch, s, d = x.shape
    assert batch == 1, "kernel is written for a single sequence"
    depth = w_ada.shape[0]
    bucket = _bucket_index_table()
    x2 = x.reshape(s, d)
    for l in range(depth):
        x2 = _layer(x2, c[0], rel_bias_table, bucket, w_ada[l], b_ada[l], pre_norm_g[l],
                    post_norm_g[l], w_in[l], attn_sink[l], w_pool_group[l], pool_scale[l],
                    w_branch_attn[l], w_branch_pool[l], w_merge[l], b_merge[l], w_out[l])
    return x2.reshape(batch, s, d)
```

```python
import functools
import math

import numpy as np
import jax
import jax.numpy as jnp
from jax import lax
from jax.experimental import pallas as pl
from jax.experimental.pallas import tpu as pltpu

HEAD_DIM = 128
N_Q_HEADS = 8
N_KV_HEADS = 2
GQA_GROUP = N_Q_HEADS // N_KV_HEADS
ATTN_WIDTH = N_Q_HEADS * HEAD_DIM
KV_WIDTH = N_KV_HEADS * HEAD_DIM
WINDOW = 128
BLOCK = 128
SPAN = BLOCK + 2 * WINDOW
N_BUCKETS = 32
MAX_DISTANCE = 128
POOL_SIZES = (2, 4, 8, 16)
N_POOL_GROUPS = len(POOL_SIZES)
EPS = 1e-6
NEG_INF = -1e30
MASKED_BUCKET = N_BUCKETS
ATTN_SCALE = HEAD_DIM ** -0.5
INV_ATTN_SCALE = HEAD_DIM ** 0.5
EXP2_SCALE = ATTN_SCALE * math.log2(math.e)

SUBLANES = 8
BF16_SUBLANES = 16
VMEM_LIMIT_BYTES = 61 * 1024 * 1024

MOD_TK = 256
TM = 512
CW = 512
GATE_TM = 1024
GATE_NG = 2048
GATE_TN = 1024
POOL_HALO = BF16_SUBLANES
W_STAGE_SLOTS = 8

Q0 = 0
KV0 = Q0 + ATTN_WIDTH
GA0 = KV0 + 2 * KV_WIDTH
U0 = GA0 + ATTN_WIDTH


def _silu(v):
    return v * (1.0 / (1.0 + jnp.exp(-v)))


def _half_silu(hv):
    return hv + hv * jnp.tanh(hv)


def _silu_column(c_row):
    n = c_row.shape[1]
    on_diag = (lax.broadcasted_iota(jnp.int32, (n, n), 0)
               == lax.broadcasted_iota(jnp.int32, (n, n), 1))
    return jnp.sum(jnp.where(on_diag, _silu(c_row), 0.0), axis=1, keepdims=True)


def _adaln_rmsnorm(x, gain, shift):
    ms = jnp.mean(x * x, axis=-1, keepdims=True)
    return (x * lax.rsqrt(ms + EPS)) * gain + shift


def _shift_scale_bias_kernel(tbl_ref, c_ref, w_ref, bsh_ref, bsc_ref, bucket_ref, wpool_ref,
                             shift_ref, scale_ref, bias_ref, wpool_o_ref, *, tile_buckets):
    k = pl.program_id(0)
    d = shift_ref.shape[1]

    @pl.when(k == 0)
    def _():
        shift_ref[...] = bsh_ref[...]
        scale_ref[...] = bsc_ref[...]

    s = _silu_column(c_ref[...])
    both = jnp.sum(s * w_ref[...], axis=0, keepdims=True)
    shift_ref[...] += both[:, :d]
    scale_ref[...] += both[:, d:]

    wpool_o_ref[...] = wpool_ref[...].astype(wpool_o_ref.dtype)

    for t, buckets in enumerate(tile_buckets):
        bk = bucket_ref[:, t * BLOCK:(t + 1) * BLOCK]
        acc = jnp.full((BLOCK, BLOCK), NEG_INF, jnp.float32)
        for b in buckets:
            acc = jnp.where(bk == b, tbl_ref[k, b] * INV_ATTN_SCALE, acc)
        bias_ref[0, :, t * BLOCK:(t + 1) * BLOCK] = acc


def _shift_scale_bias(c_row, w_ada, b_ada_row, rel_table, bucket, wpool):
    d = w_ada.shape[0]
    steps = d // MOD_TK
    assert steps == N_Q_HEADS, "one bias-table head per grid step"
    wpool2 = wpool.reshape(-1, wpool.shape[-1])
    pool_slab = (wpool2.shape[0] // steps, wpool2.shape[1])
    assert wpool2.shape[0] % (steps * BF16_SUBLANES) == 0
    tile_buckets = tuple(
        tuple(int(b) for b in np.unique(bucket[:, t * BLOCK:(t + 1) * BLOCK]) if b != MASKED_BUCKET)
        for t in range(SPAN // BLOCK))
    return pl.pallas_call(
        functools.partial(_shift_scale_bias_kernel, tile_buckets=tile_buckets),
        grid=(steps,),
        in_specs=[
            pl.BlockSpec(memory_space=pltpu.SMEM),
            pl.BlockSpec((1, MOD_TK), lambda k: (0, k)),
            pl.BlockSpec((MOD_TK, 2 * d), lambda k: (k, 0)),
            pl.BlockSpec((1, d), lambda k: (0, 0)),
            pl.BlockSpec((1, d), lambda k: (0, 1)),
            pl.BlockSpec(bucket.shape, lambda k: (0, 0)),
            pl.BlockSpec(pool_slab, lambda k: (k, 0)),
        ],
        out_specs=[
            pl.BlockSpec((1, d), lambda k: (0, 0)),
            pl.BlockSpec((1, d), lambda k: (0, 0)),
            pl.BlockSpec((1, BLOCK, SPAN), lambda k: (k, 0, 0)),
            pl.BlockSpec(pool_slab, lambda k: (k, 0)),
        ],
        out_shape=[
            jax.ShapeDtypeStruct((1, d), jnp.float32),
            jax.ShapeDtypeStruct((1, d), jnp.float32),
            jax.ShapeDtypeStruct((N_Q_HEADS, BLOCK, SPAN), jnp.float32),
            jax.ShapeDtypeStruct(wpool2.shape, jnp.bfloat16),
        ],
        compiler_params=pltpu.CompilerParams(
            dimension_semantics=("arbitrary",), vmem_limit_bytes=VMEM_LIMIT_BYTES),
        name="shift_scale_bias",
    )(rel_table.T, c_row, w_ada, b_ada_row, b_ada_row, bucket, wpool2)


def _t5_bucket(rel):
    half = N_BUCKETS // 2
    max_exact = half // 2
    assert (max_exact, MAX_DISTANCE // max_exact, half - max_exact) == (8, 16, 8)
    n = abs(rel)
    large = min(max_exact + (n * n).bit_length() - 7, half - 1)
    return (half if rel > 0 else 0) + (n if n < max_exact else large)


def _bucket_index_table():
    table = np.full((BLOCK, SPAN), MASKED_BUCKET, np.int32)
    for q in range(BLOCK):
        for t in range(SPAN):
            rel = t - WINDOW - q
            if abs(rel) <= WINDOW:
                table[q, t] = _t5_bucket(rel)
    return table


def _merge_gate_kernel(h_ref, w_hbm_ref, wa_ref, wp_ref, wout_ref, c_ref, wgate_ref, bgate_ref,
                       o_ref, wa_o_ref, wp_o_ref, wout_o_ref, gate_ref,
                       wbf_ref, wstage_ref, wsem_ref):
    j, i = pl.program_id(0), pl.program_id(1)
    d, n = wbf_ref.shape
    f32 = jnp.float32

    assert wstage_ref.shape == (2, d, CW) and n % CW == 0
    n_chunks = n // CW

    slab_rows = d // W_STAGE_SLOTS

    class chunk_copy:
        def __init__(self, c):
            col = pl.multiple_of(j * n + c * CW, CW)
            self.copies = [
                pltpu.make_async_copy(
                    w_hbm_ref.at[pl.ds(r * slab_rows, slab_rows), pl.ds(col, CW)],
                    wstage_ref.at[c % 2, pl.ds(r * slab_rows, slab_rows)],
                    wsem_ref.at[c % 2, r])
                for r in range(W_STAGE_SLOTS)]

        def start(self):
            for cp in self.copies:
                cp.start()

        def wait(self):
            for cp in self.copies:
                cp.wait()

    def cast_chunk(c):
        wbf_ref[:, c * CW:(c + 1) * CW] = (0.5 * wstage_ref[c % 2]).astype(wbf_ref.dtype)

    def dot_columns(lo, hi):
        o_ref[:, lo:hi] = jnp.dot(h_ref[...], wbf_ref[:, lo:hi],
                                  preferred_element_type=f32).astype(o_ref.dtype)

    def side_jobs():
        wa_o_ref[...] = wa_ref[...].astype(wa_o_ref.dtype)
        wp_o_ref[...] = wp_ref[...].astype(wp_o_ref.dtype)
        wout_o_ref[...] = wout_ref[...].astype(wout_o_ref.dtype)
        gate_ref[...] += jnp.sum(_silu_column(c_ref[...]) * wgate_ref[...], axis=0, keepdims=True)

    @pl.when(i == 0)
    def _():
        for c in range(min(2, n_chunks)):
            chunk_copy(c).start()

        @pl.when(j == 0)
        def _():
            gate_ref[...] = bgate_ref[...]

        chunk_copy(0).wait()
        cast_chunk(0)
        if 2 < n_chunks:
            chunk_copy(2).start()
        for c in range(n_chunks):
            if c + 1 < n_chunks:
                chunk_copy(c + 1).wait()
            dot_columns(c * CW, (c + 1) * CW)
            if c + 1 < n_chunks:
                cast_chunk(c + 1)
                if c + 3 < n_chunks:
                    chunk_copy(c + 3).start()
        side_jobs()

    @pl.when(i > 0)
    def _():
        for lo in range(0, n, GATE_TN):
            dot_columns(lo, lo + GATE_TN)
        side_jobs()


def _merge_gate(h, w_merge, wa, wp, wout, c_row, w_ada, b_ada_row):
    s, d = h.shape
    n = w_merge.shape[1]
    n_i = s // GATE_TM
    steps = (n // GATE_NG) * n_i
    slab = lambda w: (w.shape[0] // steps, w.shape[1])
    step = lambda j, i: (j * n_i + i, 0)
    for w in (wa, wp, wout):
        assert w.shape[0] % (steps * BF16_SUBLANES) == 0
    assert d % (steps * SUBLANES) == 0 and GATE_NG % GATE_TN == 0
    bf16 = jnp.bfloat16
    return pl.pallas_call(
        _merge_gate_kernel,
        grid=(n // GATE_NG, n_i),
        in_specs=[
            pl.BlockSpec((GATE_TM, d), lambda j, i: (i, 0)),
            pl.BlockSpec(memory_space=pl.ANY),
            pl.BlockSpec(slab(wa), step),
            pl.BlockSpec(slab(wp), step),
            pl.BlockSpec(slab(wout), step),
            pl.BlockSpec((1, d // steps), lambda j, i: (0, j * n_i + i)),
            pl.BlockSpec((d // steps, d), lambda j, i: (j * n_i + i, 2)),
            pl.BlockSpec((1, d), lambda j, i: (0, 2)),
        ],
        out_specs=[
            pl.BlockSpec((GATE_TM, GATE_NG), lambda j, i: (i, j)),
            pl.BlockSpec(slab(wa), step),
            pl.BlockSpec(slab(wp), step),
            pl.BlockSpec(slab(wout), step),
            pl.BlockSpec((1, d), lambda j, i: (0, 0)),
        ],
        out_shape=[
            jax.ShapeDtypeStruct((s, n), bf16),
            jax.ShapeDtypeStruct(wa.shape, bf16),
            jax.ShapeDtypeStruct(wp.shape, bf16),
            jax.ShapeDtypeStruct(wout.shape, bf16),
            jax.ShapeDtypeStruct((1, d), jnp.float32),
        ],
        scratch_shapes=[
            pltpu.VMEM((d, GATE_NG), bf16),
            pltpu.VMEM((2, d, CW), jnp.float32),
            pltpu.SemaphoreType.DMA((2, W_STAGE_SLOTS)),
        ],
        compiler_params=pltpu.CompilerParams(
            dimension_semantics=("arbitrary", "arbitrary"), vmem_limit_bytes=VMEM_LIMIT_BYTES),
        name="merge_gate",
    )(h, w_merge, wa, wp, wout, c_row, w_ada, b_ada_row)


def _project_mix_kernel(sink_ref,
                        x_ref, shift_ref, scale_ref, g_ref, w_hbm_ref, wpool_ref, pscale_ref, bias_ref,
                        ya_ref, yp_ref, h_ref,
                        pbuf_ref, kvtail_ref, utail_ref, uext_ref, w_ref, wstage_ref, wsem_ref,
                        kt_ref, kttail_ref,
                        *, seq_len):
    s = pl.program_id(0)
    n_tok = pl.num_programs(0) - 1
    tm, d = x_ref.shape
    n_sub = tm // BLOCK
    n_blocks = seq_len // BLOCK
    aw = ATTN_WIDTH
    pw = pscale_ref.shape[1]
    q0, ga0, gp0 = Q0, GA0, U0 + pw
    n_chunks = w_ref.shape[1] // CW
    f32 = jnp.float32
    bf16 = jnp.bfloat16
    cur = s % 2
    prv = 1 - cur
    j = s - 1

    n_slots, slab_rows = wstage_ref.shape[0], wstage_ref.shape[1]
    assert n_slots * slab_rows == d and wstage_ref.shape[2] == CW

    def slab_copy(c, r):
        return pltpu.make_async_copy(
            w_hbm_ref.at[pl.ds(r * slab_rows, slab_rows), pl.ds(c * CW, CW)],
            wstage_ref.at[r], wsem_ref.at[r])

    def fetch_chunk(c):
        for r in range(n_slots):
            slab_copy(c, r).start()

    def land_chunk(c, next_c):
        for r in range(n_slots):
            slab_copy(c, r).wait()
            w_ref[r * slab_rows:(r + 1) * slab_rows, c * CW:(c + 1) * CW] = wstage_ref[r].astype(bf16)
            if next_c is not None:
                slab_copy(next_c, r).start()

    def prologue():
        gain = g_ref[...] * (1.0 + scale_ref[...])
        h_ref[...] = _adaln_rmsnorm(x_ref[...], gain, shift_ref[...]).astype(bf16)

    def is_gate_col(col):
        return ga0 <= col < ga0 + aw or gp0 <= col < gp0 + pw

    def proj_chunk(c):
        lo, hi = c * CW, (c + 1) * CW
        p = jnp.dot(h_ref[...], w_ref[:, lo:hi], preferred_element_type=f32)
        if is_gate_col(lo):
            assert is_gate_col(hi - 1)
            p = 0.5 * p
        pbuf_ref[cur, :, lo:hi] = p.astype(bf16)
        if lo <= KV0 < hi:
            assert KV0 + KV_WIDTH <= hi
            kt_ref[cur] = p[:, KV0 - lo:KV0 - lo + KV_WIDTH].T.astype(bf16)

    def window_rows(b, col0):
        r0, r1 = (b - 1) * BLOCK, (b + 2) * BLOCK
        parts = []
        if r0 < 0:
            parts.append(kvtail_ref[prv, :, col0 - KV0:col0 - KV0 + HEAD_DIM])
            r0 = 0
        parts.append(pbuf_ref[prv, r0:min(r1, tm), col0:col0 + HEAD_DIM])
        if r1 > tm:
            parts.append(pbuf_ref[cur, 0:r1 - tm, col0:col0 + HEAD_DIM])
        return jnp.concatenate(parts, axis=0) if len(parts) > 1 else parts[0]

    def key_window(b, kvh):
        hd = slice(kvh * HEAD_DIM, (kvh + 1) * HEAD_DIM)
        r0, r1 = (b - 1) * BLOCK, (b + 2) * BLOCK
        parts = []
        if r0 < 0:
            parts.append(kttail_ref[prv, hd, :])
            r0 = 0
        parts.append(kt_ref[prv, hd, r0:min(r1, tm)])
        if r1 > tm:
            parts.append(kt_ref[cur, hd, 0:r1 - tm])
        return jnp.concatenate(parts, axis=1) if len(parts) > 1 else parts[0]

    def attn_scores(b, kvh):
        blk = j * n_sub + b
        h0 = kvh * GQA_GROUP
        kt = key_window(b, kvh)
        v = window_rows(b, KV0 + KV_WIDTH + kvh * HEAD_DIM)
        qs = jnp.concatenate(
            [pbuf_ref[prv, b * BLOCK:(b + 1) * BLOCK,
                      q0 + (h0 + g) * HEAD_DIM:q0 + (h0 + g + 1) * HEAD_DIM]
             for g in range(GQA_GROUP)], axis=0)
        z = jnp.dot(qs, kt, preferred_element_type=f32)
        z = z + bias_ref[h0:h0 + GQA_GROUP].reshape(GQA_GROUP * BLOCK, SPAN)
        col = lax.broadcasted_iota(jnp.int32, (1, SPAN), 1)
        if b == 0:
            z = jnp.where(col < jnp.where(blk == 0, WINDOW, 0), NEG_INF, z)
        if b == n_sub - 1:
            z = jnp.where(col >= jnp.where(blk == n_blocks - 1, WINDOW + BLOCK, SPAN), NEG_INF, z)
        sink = jnp.concatenate(
            [jnp.full((BLOCK, 1), sink_ref[h0 + g] * INV_ATTN_SCALE, f32)
             for g in range(GQA_GROUP)], axis=0)
        m = jnp.maximum(jnp.max(z, axis=-1, keepdims=True), sink)
        p = jnp.exp2((z - m) * EXP2_SCALE)
        denom = jnp.sum(p, axis=-1, keepdims=True) + jnp.exp2((sink - m) * EXP2_SCALE)
        return p.astype(bf16), denom, v

    def attn_out(p, denom, v):
        o = jnp.dot(p, v, preferred_element_type=f32) / denom
        return [o[g * BLOCK:(g + 1) * BLOCK] for g in range(GQA_GROUP)]

    def finish_sub_block(b, heads):
        y = jnp.concatenate(heads, axis=1)
        hg = pbuf_ref[prv, b * BLOCK:(b + 1) * BLOCK, ga0:ga0 + aw].astype(f32)
        ya_ref[b * BLOCK:(b + 1) * BLOCK] = (y * _half_silu(hg)).astype(bf16)

    ext = tm + 2 * POOL_HALO

    def pool_group(gi):
        w = POOL_SIZES[gi]
        gw = pw // N_POOL_GROUPS
        c0, c1 = gi * gw, (gi + 1) * gw
        half = w // 2
        halo_row = lax.broadcasted_iota(jnp.int32, (POOL_HALO, 1), 0)
        uext_ref[0:POOL_HALO] = jnp.where(j * tm - POOL_HALO + halo_row >= 0,
                                          utail_ref[prv, :, c0:c1].astype(f32), 0.0)
        uext_ref[POOL_HALO:POOL_HALO + tm] = pbuf_ref[prv, :, U0 + c0:U0 + c1].astype(f32)
        uext_ref[POOL_HALO + tm:ext] = jnp.where(
            (j + 1) * tm + halo_row < seq_len,
            pbuf_ref[cur, 0:POOL_HALO, U0 + c0:U0 + c1].astype(f32), 0.0)
        a = uext_ref[...]
        sh = 1
        while sh < w:
            a = a + pltpu.roll(a, ext - sh, axis=0)
            sh *= 2
        off = POOL_HALO - half
        if off % SUBLANES:
            a = pltpu.roll(a, ext - off, axis=0)
            off = 0
        pos = j * tm + lax.broadcasted_iota(jnp.int32, (tm, 1), 0)
        cnt = (jnp.minimum(pos + half, seq_len) - jnp.maximum(pos - half, 0)).astype(f32)
        pooled = a[off:off + tm] / cnt - uext_ref[POOL_HALO:POOL_HALO + tm]
        mixed = jnp.dot(pooled.astype(bf16), wpool_ref[gi], preferred_element_type=f32)
        hg = pbuf_ref[prv, :, gp0 + c0:gp0 + c1].astype(f32)
        yp_ref[:, c0:c1] = (mixed * pscale_ref[:, c0:c1] * _half_silu(hg)).astype(bf16)

    def save_tails():
        kvtail_ref[cur] = pbuf_ref[prv, tm - BLOCK:tm, KV0:KV0 + 2 * KV_WIDTH]
        utail_ref[cur] = pbuf_ref[prv, tm - POOL_HALO:tm, U0:U0 + pw]
        kttail_ref[cur] = kt_ref[prv, :, tm - BLOCK:tm]

    units = [(b, kvh) for b in range(n_sub) for kvh in range(N_KV_HEADS)]
    assert KV0 % CW == 0 and (2 * KV_WIDTH) % CW == 0 and U0 % CW == 0 and pw % CW == 0
    first = list(range(KV0 // CW, (KV0 + 2 * KV_WIDTH) // CW)) + list(range(U0 // CW, (U0 + pw) // CW))
    order = first + [c for c in range(n_chunks) if c not in first]
    u_chunks_done = len(first)
    assert len(units) + 1 <= n_chunks and u_chunks_done + N_POOL_GROUPS <= n_chunks

    def mix_only():
        pending = [attn_scores(b, kvh) for b, kvh in units]
        heads = []
        for (b, kvh), scores in zip(units, pending):
            heads += attn_out(*scores)
            if kvh == N_KV_HEADS - 1:
                finish_sub_block(b, heads)
                heads = []
        for gi in range(N_POOL_GROUPS):
            pool_group(gi)
        save_tails()

    def project_first():
        fetch_chunk(order[0])
        prologue()
        for pos, c in enumerate(order):
            land_chunk(c, order[pos + 1] if pos + 1 < len(order) else None)
            proj_chunk(c)

    def project_and_mix():
        prologue()
        pending = attn_scores(*units[0])
        heads = []
        for pos, c in enumerate(order):
            proj_chunk(c)
            if pos < len(units):
                heads += attn_out(*pending)
                if pos + 1 < len(units):
                    pending = attn_scores(*units[pos + 1])
                b, kvh = units[pos]
                if kvh == N_KV_HEADS - 1:
                    finish_sub_block(b, heads)
                    heads = []
            if u_chunks_done <= pos < u_chunks_done + N_POOL_GROUPS:
                pool_group(pos - u_chunks_done)
        save_tails()

    @pl.when(s == 0)
    def _():
        kvtail_ref[0] = jnp.zeros(kvtail_ref.shape[1:], kvtail_ref.dtype)
        utail_ref[0] = jnp.zeros(utail_ref.shape[1:], utail_ref.dtype)
        kttail_ref[0] = jnp.zeros(kttail_ref.shape[1:], kttail_ref.dtype)
        project_first()

    @pl.when((s > 0) & (s < n_tok))
    def _():
        project_and_mix()

    @pl.when(s == n_tok)
    def _():
        mix_only()


def _project_mix(x2, shift, scale, pre_g_row, w_in, wpool, pool_scale_row, bias_tbl, sink):
    s, d = x2.shape
    tm = TM
    n_tok = s // tm
    pw = pool_scale_row.shape[1]
    n_in = w_in.shape[1]
    assert s // BLOCK >= 2, "first and last attention block must differ"
    assert n_in == U0 + 2 * pw and d % (W_STAGE_SLOTS * BF16_SUBLANES) == 0
    cur_blk = lambda st: jnp.minimum(st, n_tok - 1)
    mix_blk = lambda st: jnp.maximum(st - 1, 0)
    const2 = lambda st: (0, 0)
    single = pl.Buffered(1)
    bf16 = jnp.bfloat16
    return pl.pallas_call(
        functools.partial(_project_mix_kernel, seq_len=s),
        grid=(n_tok + 1,),
        in_specs=[
            pl.BlockSpec(memory_space=pltpu.SMEM),
            pl.BlockSpec((tm, d), lambda st: (cur_blk(st), 0)),
            pl.BlockSpec((1, d), const2),
            pl.BlockSpec((1, d), const2),
            pl.BlockSpec((1, d), const2),
            pl.BlockSpec(memory_space=pl.ANY),
            pl.BlockSpec(wpool.shape, lambda st: (0, 0, 0), pipeline_mode=single),
            pl.BlockSpec((1, pw), const2),
            pl.BlockSpec(bias_tbl.shape, lambda st: (0, 0, 0), pipeline_mode=single),
        ],
        out_specs=[
            pl.BlockSpec((tm, ATTN_WIDTH), lambda st: (mix_blk(st), 0)),
            pl.BlockSpec((tm, pw), lambda st: (mix_blk(st), 0)),
            pl.BlockSpec((tm, d), lambda st: (cur_blk(st), 0)),
        ],
        out_shape=[
            jax.ShapeDtypeStruct((s, ATTN_WIDTH), bf16),
            jax.ShapeDtypeStruct((s, pw), bf16),
            jax.ShapeDtypeStruct((s, d), bf16),
        ],
        scratch_shapes=[
            pltpu.VMEM((2, tm, n_in), bf16),
            pltpu.VMEM((2, BLOCK, 2 * KV_WIDTH), bf16),
            pltpu.VMEM((2, POOL_HALO, pw), bf16),
            pltpu.VMEM((tm + 2 * POOL_HALO, pw // N_POOL_GROUPS), jnp.float32),
            pltpu.VMEM(w_in.shape, bf16),
            pltpu.VMEM((W_STAGE_SLOTS, d // W_STAGE_SLOTS, CW), jnp.float32),
            pltpu.SemaphoreType.DMA((W_STAGE_SLOTS,)),
            pltpu.VMEM((2, KV_WIDTH, tm), bf16),
            pltpu.VMEM((2, KV_WIDTH, BLOCK), bf16),
        ],
        compiler_params=pltpu.CompilerParams(
            dimension_semantics=("arbitrary",), vmem_limit_bytes=VMEM_LIMIT_BYTES),
        name="project_mix",
    )(sink, x2, shift, scale, pre_g_row, w_in, wpool, pool_scale_row, bias_tbl)


def _merge_out_kernel(ya_ref, yp_ref, gm_ref, x_ref, gate_ref, postg_ref, bm_ref,
                      wa_ref, wp_ref, wout_ref, o_ref, merged_ref, oacc_ref, inv_ref):
    s = pl.program_id(0)
    n_tok = pl.num_programs(0) - 1
    tm, d = x_ref.shape
    n_chunks = d // CW
    f32 = jnp.float32

    def finish_previous():
        scale = gate_ref[...] * postg_ref[...]
        inv = inv_ref[...]
        for c in range(n_chunks):
            lo, hi = c * CW, (c + 1) * CW
            o_ref[:, lo:hi] = x_ref[:, lo:hi] + (oacc_ref[:, lo:hi] * inv) * scale[:, lo:hi]

    def matmuls():
        half_bm = 0.5 * bm_ref[...]
        for c in range(n_chunks):
            lo, hi = c * CW, (c + 1) * CW
            bra = jnp.dot(ya_ref[...], wa_ref[:, lo:hi], preferred_element_type=f32)
            brp = jnp.dot(yp_ref[...], wp_ref[:, lo:hi], preferred_element_type=f32)
            g_a = 0.5 * jnp.tanh(gm_ref[:, lo:hi].astype(f32) + half_bm[:, lo:hi]) + 0.5
            g_p = 0.5 * jnp.tanh(gm_ref[:, d + lo:d + hi].astype(f32)
                                 + half_bm[:, d + lo:d + hi]) + 0.5
            merged_ref[:, lo:hi] = (g_a * bra + g_p * brp).astype(merged_ref.dtype)
        ssq = jnp.zeros((tm, 1), f32)
        for c in range(n_chunks):
            lo, hi = c * CW, (c + 1) * CW
            o = jnp.dot(merged_ref[...], wout_ref[:, lo:hi], preferred_element_type=f32)
            ssq = ssq + jnp.sum(o * o, axis=-1, keepdims=True)
            oacc_ref[:, lo:hi] = o
        inv_ref[...] = lax.rsqrt(ssq * (1.0 / d) + EPS)

    @pl.when(s == 0)
    def _():
        matmuls()

    @pl.when((s > 0) & (s < n_tok))
    def _():
        finish_previous()
        matmuls()

    @pl.when(s == n_tok)
    def _():
        finish_previous()


def _merge_out(ya, yp, gm, x2, gate, post_g_row, b_merge_row, wa, wp, wout):
    s, d = x2.shape
    tm = TM
    n_tok = s // tm
    aw, pw = ya.shape[1], yp.shape[1]
    cur = lambda st: (jnp.minimum(st, n_tok - 1), 0)
    prev = lambda st: (jnp.maximum(st - 1, 0), 0)
    const2 = lambda st: (0, 0)
    single = pl.Buffered(1)
    return pl.pallas_call(
        _merge_out_kernel,
        grid=(n_tok + 1,),
        in_specs=[
            pl.BlockSpec((tm, aw), cur),
            pl.BlockSpec((tm, pw), cur),
            pl.BlockSpec((tm, 2 * d), cur),
            pl.BlockSpec((tm, d), prev),
            pl.BlockSpec((1, d), const2),
            pl.BlockSpec((1, d), const2),
            pl.BlockSpec((1, 2 * d), const2),
            pl.BlockSpec((aw, d), const2, pipeline_mode=single),
            pl.BlockSpec((pw, d), const2, pipeline_mode=single),
            pl.BlockSpec((d, d), const2, pipeline_mode=single),
        ],
        out_specs=pl.BlockSpec((tm, d), prev),
        out_shape=jax.ShapeDtypeStruct((s, d), jnp.float32),
        scratch_shapes=[
            pltpu.VMEM((tm, d), jnp.bfloat16),
            pltpu.VMEM((tm, d), jnp.float32),
            pltpu.VMEM((tm, 1), jnp.float32),
        ],
        compiler_params=pltpu.CompilerParams(
            dimension_semantics=("arbitrary",), vmem_limit_bytes=VMEM_LIMIT_BYTES),
        name="merge_out",
    )(ya, yp, gm, x2, gate, post_g_row, b_merge_row, wa, wp, wout)


def _layer(x2, c, rel_table, bucket, w_ada, b_ada, pre_g, post_g, w_in, sink, w_pool, pool_scale,
           w_br_attn, w_br_pool, w_merge, b_merge, w_out):
    s, d = x2.shape
    c_row = c.reshape(1, d)
    b_ada_row = b_ada.reshape(1, -1)
    pre_g_row = pre_g.reshape(1, d)
    shift, scale, bias_tbl, w_pool_bf = _shift_scale_bias(c_row, w_ada, b_ada_row, rel_table, bucket,
                                                          w_pool)
    ya, yp, h = _project_mix(x2, shift, scale, pre_g_row, w_in, w_pool_bf.reshape(w_pool.shape),
                             pool_scale.reshape(1, -1), bias_tbl, sink)
    gm, wa, wp, wout, gate = _merge_gate(h, w_merge, w_br_attn, w_br_pool, w_out,
                                         c_row, w_ada, b_ada_row)
    return _merge_out(ya, yp, gm, x2, gate, post_g.reshape(1, d), b_merge.reshape(1, -1),
                      wa, wp, wout)


def kernel(x, c, rel_bias_table, w_ada, b_ada, pre_norm_g, post_norm_g, w_in, attn_sink,
           w_pool_group, pool_scale, w_branch_attn, w_branch_pool, w_merge, b_merge, w_out):
    batch, s, d = x.shape
    assert batch == 1, "kernel is written for a single sequence"
    depth = w_ada.shape[0]
    bucket = _bucket_index_table()
    x2 = x.reshape(s, d)
    for l in range(depth):
        x2 = _layer(x2, c[0], rel_bias_table, bucket, w_ada[l], b_ada[l], pre_norm_g[l],
                    post_norm_g[l], w_in[l], attn_sink[l], w_pool_group[l], pool_scale[l],
                    w_branch_attn[l], w_branch_pool[l], w_merge[l], b_merge[l], w_out[l])
    return x2.reshape(batch, s, d)
```

```python
import functools
import math

import numpy as np
import jax
import jax.numpy as jnp
from jax import lax
from jax.experimental import pallas as pl
from jax.experimental.pallas import tpu as pltpu

HEAD_DIM = 128
N_Q_HEADS = 8
N_KV_HEADS = 2
GQA_GROUP = N_Q_HEADS // N_KV_HEADS
ATTN_WIDTH = N_Q_HEADS * HEAD_DIM
KV_WIDTH = N_KV_HEADS * HEAD_DIM
WINDOW = 128
BLOCK = 128
SPAN = BLOCK + 2 * WINDOW
N_BUCKETS = 32
MAX_DISTANCE = 128
POOL_SIZES = (2, 4, 8, 16)
N_POOL_GROUPS = len(POOL_SIZES)
EPS = 1e-6
NEG_INF = -1e30
MASKED_BUCKET = N_BUCKETS
ATTN_SCALE = HEAD_DIM ** -0.5
INV_ATTN_SCALE = HEAD_DIM ** 0.5
EXP2_SCALE = ATTN_SCALE * math.log2(math.e)

SUBLANES = 8
BF16_SUBLANES = 16
VMEM_LIMIT_BYTES = 61 * 1024 * 1024

MOD_TK = 256
TM = 512
CW = 512
GATE_TM = 2048
GATE_ROWS = 1024
GATE_TN = 1024
POOL_HALO = BF16_SUBLANES
W_STAGE_SLOTS = 8

Q0 = 0
KV0 = Q0 + ATTN_WIDTH
GA0 = KV0 + 2 * KV_WIDTH
U0 = GA0 + ATTN_WIDTH


def _silu(v):
    return v * (1.0 / (1.0 + jnp.exp(-v)))


def _half_silu(hv):
    return hv + hv * jnp.tanh(hv)


def _silu_column(c_row):
    n = c_row.shape[1]
    on_diag = (lax.broadcasted_iota(jnp.int32, (n, n), 0)
               == lax.broadcasted_iota(jnp.int32, (n, n), 1))
    return jnp.sum(jnp.where(on_diag, _silu(c_row), 0.0), axis=1, keepdims=True)


def _adaln_rmsnorm(x, gain, shift):
    ms = jnp.mean(x * x, axis=-1, keepdims=True)
    return (x * lax.rsqrt(ms + EPS)) * gain + shift


def _shift_scale_bias_kernel(tbl_ref, c_ref, w_ref, bsh_ref, bsc_ref, bucket_ref, wpool_ref,
                             shift_ref, scale_ref, bias_ref, wpool_o_ref, *, tile_buckets):
    k = pl.program_id(0)
    d = shift_ref.shape[1]

    @pl.when(k == 0)
    def _():
        shift_ref[...] = bsh_ref[...]
        scale_ref[...] = bsc_ref[...]

    s = _silu_column(c_ref[...])
    both = jnp.sum(s * w_ref[...], axis=0, keepdims=True)
    shift_ref[...] += both[:, :d]
    scale_ref[...] += both[:, d:]

    wpool_o_ref[...] = wpool_ref[...].astype(wpool_o_ref.dtype)

    for t, buckets in enumerate(tile_buckets):
        bk = bucket_ref[:, t * BLOCK:(t + 1) * BLOCK]
        acc = jnp.full((BLOCK, BLOCK), NEG_INF, jnp.float32)
        for b in buckets:
            acc = jnp.where(bk == b, tbl_ref[k, b] * INV_ATTN_SCALE, acc)
        bias_ref[0, :, t * BLOCK:(t + 1) * BLOCK] = acc


def _shift_scale_bias(c_row, w_ada, b_ada_row, rel_table, bucket, wpool):
    d = w_ada.shape[0]
    steps = d // MOD_TK
    assert steps == N_Q_HEADS, "one bias-table head per grid step"
    wpool2 = wpool.reshape(-1, wpool.shape[-1])
    pool_slab = (wpool2.shape[0] // steps, wpool2.shape[1])
    assert wpool2.shape[0] % (steps * BF16_SUBLANES) == 0
    tile_buckets = tuple(
        tuple(int(b) for b in np.unique(bucket[:, t * BLOCK:(t + 1) * BLOCK]) if b != MASKED_BUCKET)
        for t in range(SPAN // BLOCK))
    return pl.pallas_call(
        functools.partial(_shift_scale_bias_kernel, tile_buckets=tile_buckets),
        grid=(steps,),
        in_specs=[
            pl.BlockSpec(memory_space=pltpu.SMEM),
            pl.BlockSpec((1, MOD_TK), lambda k: (0, k)),
            pl.BlockSpec((MOD_TK, 2 * d), lambda k: (k, 0)),
            pl.BlockSpec((1, d), lambda k: (0, 0)),
            pl.BlockSpec((1, d), lambda k: (0, 1)),
            pl.BlockSpec(bucket.shape, lambda k: (0, 0)),
            pl.BlockSpec(pool_slab, lambda k: (k, 0)),
        ],
        out_specs=[
            pl.BlockSpec((1, d), lambda k: (0, 0)),
            pl.BlockSpec((1, d), lambda k: (0, 0)),
            pl.BlockSpec((1, BLOCK, SPAN), lambda k: (k, 0, 0)),
            pl.BlockSpec(pool_slab, lambda k: (k, 0)),
        ],
        out_shape=[
            jax.ShapeDtypeStruct((1, d), jnp.float32),
            jax.ShapeDtypeStruct((1, d), jnp.float32),
            jax.ShapeDtypeStruct((N_Q_HEADS, BLOCK, SPAN), jnp.float32),
            jax.ShapeDtypeStruct(wpool2.shape, jnp.bfloat16),
        ],
        compiler_params=pltpu.CompilerParams(
            dimension_semantics=("arbitrary",), vmem_limit_bytes=VMEM_LIMIT_BYTES),
        name="shift_scale_bias",
    )(rel_table.T, c_row, w_ada, b_ada_row, b_ada_row, bucket, wpool2)


def _t5_bucket(rel):
    half = N_BUCKETS // 2
    max_exact = half // 2
    assert (max_exact, MAX_DISTANCE // max_exact, half - max_exact) == (8, 16, 8)
    n = abs(rel)
    large = min(max_exact + (n * n).bit_length() - 7, half - 1)
    return (half if rel > 0 else 0) + (n if n < max_exact else large)


def _bucket_index_table():
    table = np.full((BLOCK, SPAN), MASKED_BUCKET, np.int32)
    for q in range(BLOCK):
        for t in range(SPAN):
            rel = t - WINDOW - q
            if abs(rel) <= WINDOW:
                table[q, t] = _t5_bucket(rel)
    return table


def _merge_gate_kernel(h_ref, w_ref, wa_ref, wp_ref, wout_ref, c_ref, wgate_ref, bgate_ref,
                       o_ref, wa_o_ref, wp_o_ref, wout_o_ref, gate_ref,
                       wbf_ref):
    j, i = pl.program_id(0), pl.program_id(1)

    @pl.when(i == 0)
    def _():
        wbf_ref[...] = (0.5 * w_ref[...]).astype(wbf_ref.dtype)

    @pl.when((j == 0) & (i == 0))
    def _():
        gate_ref[...] = bgate_ref[...]

    for r in range(0, h_ref.shape[0], GATE_ROWS):
        o_ref[r:r + GATE_ROWS] = jnp.dot(h_ref[r:r + GATE_ROWS], wbf_ref[...],
                                         preferred_element_type=jnp.float32).astype(o_ref.dtype)

    wa_o_ref[...] = wa_ref[...].astype(wa_o_ref.dtype)
    wp_o_ref[...] = wp_ref[...].astype(wp_o_ref.dtype)
    wout_o_ref[...] = wout_ref[...].astype(wout_o_ref.dtype)
    gate_ref[...] += jnp.sum(_silu_column(c_ref[...]) * wgate_ref[...], axis=0, keepdims=True)


def _merge_gate(h, w_merge, wa, wp, wout, c_row, w_ada, b_ada_row):
    s, d = h.shape
    n = w_merge.shape[1]
    n_i = s // GATE_TM
    steps = (n // GATE_TN) * n_i
    slab = lambda w: (w.shape[0] // steps, w.shape[1])
    step = lambda j, i: (j * n_i + i, 0)
    for w in (wa, wp, wout):
        assert w.shape[0] % (steps * BF16_SUBLANES) == 0
    assert d % (steps * SUBLANES) == 0
    const2 = lambda j, i: (0, 0)
    bf16 = jnp.bfloat16
    return pl.pallas_call(
        _merge_gate_kernel,
        grid=(n // GATE_TN, n_i),
        in_specs=[
            pl.BlockSpec((GATE_TM, d), lambda j, i: (i, 0)),
            pl.BlockSpec((d, GATE_TN), lambda j, i: (0, j)),
            pl.BlockSpec(slab(wa), step),
            pl.BlockSpec(slab(wp), step),
            pl.BlockSpec(slab(wout), step),
            pl.BlockSpec((1, d // steps), lambda j, i: (0, j * n_i + i)),
            pl.BlockSpec((d // steps, d), lambda j, i: (j * n_i + i, 2)),
            pl.BlockSpec((1, d), lambda j, i: (0, 2)),
        ],
        out_specs=[
            pl.BlockSpec((GATE_TM, GATE_TN), lambda j, i: (i, j)),
            pl.BlockSpec(slab(wa), step),
            pl.BlockSpec(slab(wp), step),
            pl.BlockSpec(slab(wout), step),
            pl.BlockSpec((1, d), const2),
        ],
        out_shape=[
            jax.ShapeDtypeStruct((s, n), bf16),
            jax.ShapeDtypeStruct(wa.shape, bf16),
            jax.ShapeDtypeStruct(wp.shape, bf16),
            jax.ShapeDtypeStruct(wout.shape, bf16),
            jax.ShapeDtypeStruct((1, d), jnp.float32),
        ],
        scratch_shapes=[pltpu.VMEM((d, GATE_TN), bf16)],
        compiler_params=pltpu.CompilerParams(
            dimension_semantics=("arbitrary", "arbitrary"), vmem_limit_bytes=VMEM_LIMIT_BYTES),
        name="merge_gate",
    )(h, w_merge, wa, wp, wout, c_row, w_ada, b_ada_row)


def _project_mix_kernel(sink_ref,
                        x_ref, shift_ref, scale_ref, g_ref, w_hbm_ref, wpool_ref, pscale_ref, bias_ref,
                        ya_ref, yp_ref, h_ref,
                        pbuf_ref, kvtail_ref, utail_ref, uext_ref, w_ref, wstage_ref, wsem_ref,
                        kt_ref, kttail_ref,
                        *, seq_len):
    s = pl.program_id(0)
    n_tok = pl.num_programs(0) - 1
    tm, d = x_ref.shape
    n_sub = tm // BLOCK
    n_blocks = seq_len // BLOCK
    aw = ATTN_WIDTH
    pw = pscale_ref.shape[1]
    q0, ga0, gp0 = Q0, GA0, U0 + pw
    n_chunks = w_ref.shape[1] // CW
    f32 = jnp.float32
    bf16 = jnp.bfloat16
    cur = s % 2
    prv = 1 - cur
    j = s - 1

    n_slots, slab_rows = wstage_ref.shape[0], wstage_ref.shape[1]
    assert n_slots * slab_rows == d and wstage_ref.shape[2] == CW

    def slab_copy(c, r):
        return pltpu.make_async_copy(
            w_hbm_ref.at[pl.ds(r * slab_rows, slab_rows), pl.ds(c * CW, CW)],
            wstage_ref.at[r], wsem_ref.at[r])

    def fetch_chunk(c):
        for r in range(n_slots):
            slab_copy(c, r).start()

    def land_chunk(c, next_c):
        for r in range(n_slots):
            slab_copy(c, r).wait()
            w_ref[r * slab_rows:(r + 1) * slab_rows, c * CW:(c + 1) * CW] = wstage_ref[r].astype(bf16)
            if next_c is not None:
                slab_copy(next_c, r).start()

    def prologue():
        gain = g_ref[...] * (1.0 + scale_ref[...])
        h_ref[...] = _adaln_rmsnorm(x_ref[...], gain, shift_ref[...]).astype(bf16)

    def is_gate_col(col):
        return ga0 <= col < ga0 + aw or gp0 <= col < gp0 + pw

    def proj_chunk(c):
        lo, hi = c * CW, (c + 1) * CW
        p = jnp.dot(h_ref[...], w_ref[:, lo:hi], preferred_element_type=f32)
        if is_gate_col(lo):
            assert is_gate_col(hi - 1)
            p = 0.5 * p
        pbuf_ref[cur, :, lo:hi] = p.astype(bf16)
        if lo <= KV0 < hi:
            assert KV0 + KV_WIDTH <= hi
            kt_ref[cur] = p[:, KV0 - lo:KV0 - lo + KV_WIDTH].T.astype(bf16)

    def window_rows(b, col0):
        r0, r1 = (b - 1) * BLOCK, (b + 2) * BLOCK
        parts = []
        if r0 < 0:
            parts.append(kvtail_ref[prv, :, col0 - KV0:col0 - KV0 + HEAD_DIM])
            r0 = 0
        parts.append(pbuf_ref[prv, r0:min(r1, tm), col0:col0 + HEAD_DIM])
        if r1 > tm:
            parts.append(pbuf_ref[cur, 0:r1 - tm, col0:col0 + HEAD_DIM])
        return jnp.concatenate(parts, axis=0) if len(parts) > 1 else parts[0]

    def key_window(b, kvh):
        hd = slice(kvh * HEAD_DIM, (kvh + 1) * HEAD_DIM)
        r0, r1 = (b - 1) * BLOCK, (b + 2) * BLOCK
        parts = []
        if r0 < 0:
            parts.append(kttail_ref[prv, hd, :])
            r0 = 0
        parts.append(kt_ref[prv, hd, r0:min(r1, tm)])
        if r1 > tm:
            parts.append(kt_ref[cur, hd, 0:r1 - tm])
        return jnp.concatenate(parts, axis=1) if len(parts) > 1 else parts[0]

    def attn_scores(b, kvh):
        blk = j * n_sub + b
        h0 = kvh * GQA_GROUP
        kt = key_window(b, kvh)
        v = window_rows(b, KV0 + KV_WIDTH + kvh * HEAD_DIM)
        qs = jnp.concatenate(
            [pbuf_ref[prv, b * BLOCK:(b + 1) * BLOCK,
                      q0 + (h0 + g) * HEAD_DIM:q0 + (h0 + g + 1) * HEAD_DIM]
             for g in range(GQA_GROUP)], axis=0)
        z = jnp.dot(qs, kt, preferred_element_type=f32)
        z = z + bias_ref[h0:h0 + GQA_GROUP].reshape(GQA_GROUP * BLOCK, SPAN)
        col = lax.broadcasted_iota(jnp.int32, (1, SPAN), 1)
        if b == 0:
            z = jnp.where(col < jnp.where(blk == 0, WINDOW, 0), NEG_INF, z)
        if b == n_sub - 1:
            z = jnp.where(col >= jnp.where(blk == n_blocks - 1, WINDOW + BLOCK, SPAN), NEG_INF, z)
        sink = jnp.concatenate(
            [jnp.full((BLOCK, 1), sink_ref[h0 + g] * INV_ATTN_SCALE, f32)
             for g in range(GQA_GROUP)], axis=0)
        m = jnp.maximum(jnp.max(z, axis=-1, keepdims=True), sink)
        p = jnp.exp2((z - m) * EXP2_SCALE)
        denom = jnp.sum(p, axis=-1, keepdims=True) + jnp.exp2((sink - m) * EXP2_SCALE)
        return p.astype(bf16), denom, v

    def attn_out(p, denom, v):
        o = jnp.dot(p, v, preferred_element_type=f32) / denom
        return [o[g * BLOCK:(g + 1) * BLOCK] for g in range(GQA_GROUP)]

    def finish_sub_block(b, heads):
        y = jnp.concatenate(heads, axis=1)
        hg = pbuf_ref[prv, b * BLOCK:(b + 1) * BLOCK, ga0:ga0 + aw].astype(f32)
        ya_ref[b * BLOCK:(b + 1) * BLOCK] = (y * _half_silu(hg)).astype(bf16)

    ext = tm + 2 * POOL_HALO

    def pool_group(gi):
        w = POOL_SIZES[gi]
        gw = pw // N_POOL_GROUPS
        c0, c1 = gi * gw, (gi + 1) * gw
        half = w // 2
        halo_row = lax.broadcasted_iota(jnp.int32, (POOL_HALO, 1), 0)
        uext_ref[0:POOL_HALO] = jnp.where(j * tm - POOL_HALO + halo_row >= 0,
                                          utail_ref[prv, :, c0:c1].astype(f32), 0.0)
        uext_ref[POOL_HALO:POOL_HALO + tm] = pbuf_ref[prv, :, U0 + c0:U0 + c1].astype(f32)
        uext_ref[POOL_HALO + tm:ext] = jnp.where(
            (j + 1) * tm + halo_row < seq_len,
            pbuf_ref[cur, 0:POOL_HALO, U0 + c0:U0 + c1].astype(f32), 0.0)
        a = uext_ref[...]
        sh = 1
        while sh < w:
            a = a + pltpu.roll(a, ext - sh, axis=0)
            sh *= 2
        off = POOL_HALO - half
        if off % SUBLANES:
            a = pltpu.roll(a, ext - off, axis=0)
            off = 0
        pos = j * tm + lax.broadcasted_iota(jnp.int32, (tm, 1), 0)
        cnt = (jnp.minimum(pos + half, seq_len) - jnp.maximum(pos - half, 0)).astype(f32)
        pooled = a[off:off + tm] / cnt - uext_ref[POOL_HALO:POOL_HALO + tm]
        mixed = jnp.dot(pooled.astype(bf16), wpool_ref[gi], preferred_element_type=f32)
        hg = pbuf_ref[prv, :, gp0 + c0:gp0 + c1].astype(f32)
        yp_ref[:, c0:c1] = (mixed * pscale_ref[:, c0:c1] * _half_silu(hg)).astype(bf16)

    def save_tails():
        kvtail_ref[cur] = pbuf_ref[prv, tm - BLOCK:tm, KV0:KV0 + 2 * KV_WIDTH]
        utail_ref[cur] = pbuf_ref[prv, tm - POOL_HALO:tm, U0:U0 + pw]
        kttail_ref[cur] = kt_ref[prv, :, tm - BLOCK:tm]

    units = [(b, kvh) for b in range(n_sub) for kvh in range(N_KV_HEADS)]
    assert KV0 % CW == 0 and (2 * KV_WIDTH) % CW == 0 and U0 % CW == 0 and pw % CW == 0
    first = list(range(KV0 // CW, (KV0 + 2 * KV_WIDTH) // CW)) + list(range(U0 // CW, (U0 + pw) // CW))
    order = first + [c for c in range(n_chunks) if c not in first]
    u_chunks_done = len(first)
    assert len(units) + 1 <= n_chunks and u_chunks_done + N_POOL_GROUPS <= n_chunks

    def mix_only():
        pending = [attn_scores(b, kvh) for b, kvh in units]
        heads = []
        for (b, kvh), scores in zip(units, pending):
            heads += attn_out(*scores)
            if kvh == N_KV_HEADS - 1:
                finish_sub_block(b, heads)
                heads = []
        for gi in range(N_POOL_GROUPS):
            pool_group(gi)
        save_tails()

    def project_first():
        fetch_chunk(order[0])
        prologue()
        for pos, c in enumerate(order):
            land_chunk(c, order[pos + 1] if pos + 1 < len(order) else None)
            proj_chunk(c)

    def project_and_mix():
        prologue()
        pending = attn_scores(*units[0])
        heads = []
        for pos, c in enumerate(order):
            proj_chunk(c)
            if pos < len(units):
                heads += attn_out(*pending)
                if pos + 1 < len(units):
                    pending = attn_scores(*units[pos + 1])
                b, kvh = units[pos]
                if kvh == N_KV_HEADS - 1:
                    finish_sub_block(b, heads)
                    heads = []
            if u_chunks_done <= pos < u_chunks_done + N_POOL_GROUPS:
                pool_group(pos - u_chunks_done)
        save_tails()

    @pl.when(s == 0)
    def _():
        kvtail_ref[0] = jnp.zeros(kvtail_ref.shape[1:], kvtail_ref.dtype)
        utail_ref[0] = jnp.zeros(utail_ref.shape[1:], utail_ref.dtype)
        kttail_ref[0] = jnp.zeros(kttail_ref.shape[1:], kttail_ref.dtype)
        project_first()

    @pl.when((s > 0) & (s < n_tok))
    def _():
        project_and_mix()

    @pl.when(s == n_tok)
    def _():
        mix_only()


def _project_mix(x2, shift, scale, pre_g_row, w_in, wpool, pool_scale_row, bias_tbl, sink):
    s, d = x2.shape
    tm = TM
    n_tok = s // tm
    pw = pool_scale_row.shape[1]
    n_in = w_in.shape[1]
    assert s // BLOCK >= 2, "first and last attention block must differ"
    assert n_in == U0 + 2 * pw and d % (W_STAGE_SLOTS * BF16_SUBLANES) == 0
    cur_blk = lambda st: jnp.minimum(st, n_tok - 1)
    mix_blk = lambda st: jnp.maximum(st - 1, 0)
    const2 = lambda st: (0, 0)
    single = pl.Buffered(1)
    bf16 = jnp.bfloat16
    return pl.pallas_call(
        functools.partial(_project_mix_kernel, seq_len=s),
        grid=(n_tok + 1,),
        in_specs=[
            pl.BlockSpec(memory_space=pltpu.SMEM),
            pl.BlockSpec((tm, d), lambda st: (cur_blk(st), 0)),
            pl.BlockSpec((1, d), const2),
            pl.BlockSpec((1, d), const2),
            pl.BlockSpec((1, d), const2),
            pl.BlockSpec(memory_space=pl.ANY),
            pl.BlockSpec(wpool.shape, lambda st: (0, 0, 0), pipeline_mode=single),
            pl.BlockSpec((1, pw), const2),
            pl.BlockSpec(bias_tbl.shape, lambda st: (0, 0, 0), pipeline_mode=single),
        ],
        out_specs=[
            pl.BlockSpec((tm, ATTN_WIDTH), lambda st: (mix_blk(st), 0)),
            pl.BlockSpec((tm, pw), lambda st: (mix_blk(st), 0)),
            pl.BlockSpec((tm, d), lambda st: (cur_blk(st), 0)),
        ],
        out_shape=[
            jax.ShapeDtypeStruct((s, ATTN_WIDTH), bf16),
            jax.ShapeDtypeStruct((s, pw), bf16),
            jax.ShapeDtypeStruct((s, d), bf16),
        ],
        scratch_shapes=[
            pltpu.VMEM((2, tm, n_in), bf16),
            pltpu.VMEM((2, BLOCK, 2 * KV_WIDTH), bf16),
            pltpu.VMEM((2, POOL_HALO, pw), bf16),
            pltpu.VMEM((tm + 2 * POOL_HALO, pw // N_POOL_GROUPS), jnp.float32),
            pltpu.VMEM(w_in.shape, bf16),
            pltpu.VMEM((W_STAGE_SLOTS, d // W_STAGE_SLOTS, CW), jnp.float32),
            pltpu.SemaphoreType.DMA((W_STAGE_SLOTS,)),
            pltpu.VMEM((2, KV_WIDTH, tm), bf16),
            pltpu.VMEM((2, KV_WIDTH, BLOCK), bf16),
        ],
        compiler_params=pltpu.CompilerParams(
            dimension_semantics=("arbitrary",), vmem_limit_bytes=VMEM_LIMIT_BYTES),
        name="project_mix",
    )(sink, x2, shift, scale, pre_g_row, w_in, wpool, pool_scale_row, bias_tbl)


def _merge_out_kernel(ya_ref, yp_ref, gm_ref, x_ref, gate_ref, postg_ref, bm_ref,
                      wa_ref, wp_ref, wout_ref, o_ref, merged_ref, oacc_ref, inv_ref):
    s = pl.program_id(0)
    n_tok = pl.num_programs(0) - 1
    tm, d = x_ref.shape
    n_chunks = d // CW
    f32 = jnp.float32

    def finish_previous():
        scale = gate_ref[...] * postg_ref[...]
        inv = inv_ref[...]
        for c in range(n_chunks):
            lo, hi = c * CW, (c + 1) * CW
            o_ref[:, lo:hi] = x_ref[:, lo:hi] + (oacc_ref[:, lo:hi] * inv) * scale[:, lo:hi]

    def matmuls():
        half_bm = 0.5 * bm_ref[...]
        for c in range(n_chunks):
            lo, hi = c * CW, (c + 1) * CW
            bra = jnp.dot(ya_ref[...], wa_ref[:, lo:hi], preferred_element_type=f32)
            brp = jnp.dot(yp_ref[...], wp_ref[:, lo:hi], preferred_element_type=f32)
            g_a = 0.5 * jnp.tanh(gm_ref[:, lo:hi].astype(f32) + half_bm[:, lo:hi]) + 0.5
            g_p = 0.5 * jnp.tanh(gm_ref[:, d + lo:d + hi].astype(f32)
                                 + half_bm[:, d + lo:d + hi]) + 0.5
            merged_ref[:, lo:hi] = (g_a * bra + g_p * brp).astype(merged_ref.dtype)
        ssq = jnp.zeros((tm, 1), f32)
        for c in range(n_chunks):
            lo, hi = c * CW, (c + 1) * CW
            o = jnp.dot(merged_ref[...], wout_ref[:, lo:hi], preferred_element_type=f32)
            ssq = ssq + jnp.sum(o * o, axis=-1, keepdims=True)
            oacc_ref[:, lo:hi] = o
        inv_ref[...] = lax.rsqrt(ssq * (1.0 / d) + EPS)

    @pl.when(s == 0)
    def _():
        oacc_ref[...] = jnp.zeros(oacc_ref.shape, oacc_ref.dtype)
        inv_ref[...] = jnp.zeros(inv_ref.shape, inv_ref.dtype)

    @pl.when(s < n_tok)
    def _():
        finish_previous()
        matmuls()

    @pl.when(s == n_tok)
    def _():
        finish_previous()


def _merge_out(ya, yp, gm, x2, gate, post_g_row, b_merge_row, wa, wp, wout):
    s, d = x2.shape
    tm = TM
    n_tok = s // tm
    aw, pw = ya.shape[1], yp.shape[1]
    cur = lambda st: (jnp.minimum(st, n_tok - 1), 0)
    prev = lambda st: (jnp.maximum(st - 1, 0), 0)
    const2 = lambda st: (0, 0)
    single = pl.Buffered(1)
    return pl.pallas_call(
        _merge_out_kernel,
        grid=(n_tok + 1,),
        in_specs=[
            pl.BlockSpec((tm, aw), cur),
            pl.BlockSpec((tm, pw), cur),
            pl.BlockSpec((tm, 2 * d), cur),
            pl.BlockSpec((tm, d), prev),
            pl.BlockSpec((1, d), const2),
            pl.BlockSpec((1, d), const2),
            pl.BlockSpec((1, 2 * d), const2),
            pl.BlockSpec((aw, d), const2, pipeline_mode=single),
            pl.BlockSpec((pw, d), const2, pipeline_mode=single),
            pl.BlockSpec((d, d), const2, pipeline_mode=single),
        ],
        out_specs=pl.BlockSpec((tm, d), prev),
        out_shape=jax.ShapeDtypeStruct((s, d), jnp.float32),
        scratch_shapes=[
            pltpu.VMEM((tm, d), jnp.bfloat16),
            pltpu.VMEM((tm, d), jnp.float32),
            pltpu.VMEM((tm, 1), jnp.float32),
        ],
        compiler_params=pltpu.CompilerParams(
            dimension_semantics=("arbitrary",), vmem_limit_bytes=VMEM_LIMIT_BYTES),
        name="merge_out",
    )(ya, yp, gm, x2, gate, post_g_row, b_merge_row, wa, wp, wout)


def _layer(x2, c, rel_table, bucket, w_ada, b_ada, pre_g, post_g, w_in, sink, w_pool, pool_scale,
           w_br_attn, w_br_pool, w_merge, b_merge, w_out):
    s, d = x2.shape
    c_row = c.reshape(1, d)
    b_ada_row = b_ada.reshape(1, -1)
    pre_g_row = pre_g.reshape(1, d)
    shift, scale, bias_tbl, w_pool_bf = _shift_scale_bias(c_row, w_ada, b_ada_row, rel_table, bucket,
                                                          w_pool)
    ya, yp, h = _project_mix(x2, shift, scale, pre_g_row, w_in, w_pool_bf.reshape(w_pool.shape),
                             pool_scale.reshape(1, -1), bias_tbl, sink)
    gm, wa, wp, wout, gate = _merge_gate(h, w_merge, w_br_attn, w_br_pool, w_out,
                                         c_row, w_ada, b_ada_row)
    return _merge_out(ya, yp, gm, x2, gate, post_g.reshape(1, d), b_merge.reshape(1, -1),
                      wa, wp, wout)


def kernel(x, c, rel_bias_table, w_ada, b_ada, pre_norm_g, post_norm_g, w_in, attn_sink,
           w_pool_group, pool_scale, w_branch_attn, w_branch_pool, w_merge, b_merge, w_out):
    batch, s, d = x.shape
    assert batch == 1, "kernel is written for a single sequence"
    depth = w_ada.shape[0]
    bucket = _bucket_index_table()
    x2 = x.reshape(s, d)
    for l in range(depth):
        x2 = _layer(x2, c[0], rel_bias_table, bucket, w_ada[l], b_ada[l], pre_norm_g[l],
                    post_norm_g[l], w_in[l], attn_sink[l], w_pool_group[l], pool_scale[l],
                    w_branch_attn[l], w_branch_pool[l], w_merge[l], b_merge[l], w_out[l])
    return x2.reshape(batch, s, d)
```

```python
import functools
import math

import numpy as np
import jax
import jax.numpy as jnp
from jax import lax
from jax.experimental import pallas as pl
from jax.experimental.pallas import tpu as pltpu

HEAD_DIM = 128
N_Q_HEADS = 8
N_KV_HEADS = 2
GQA_GROUP = N_Q_HEADS // N_KV_HEADS
ATTN_WIDTH = N_Q_HEADS * HEAD_DIM
KV_WIDTH = N_KV_HEADS * HEAD_DIM
WINDOW = 128
BLOCK = 128
SPAN = BLOCK + 2 * WINDOW
N_BUCKETS = 32
MAX_DISTANCE = 128
POOL_SIZES = (2, 4, 8, 16)
N_POOL_GROUPS = len(POOL_SIZES)
EPS = 1e-6
NEG_INF = -1e30
MASKED_BUCKET = N_BUCKETS
ATTN_SCALE = HEAD_DIM ** -0.5
INV_ATTN_SCALE = HEAD_DIM ** 0.5
EXP2_SCALE = ATTN_SCALE * math.log2(math.e)

SUBLANES = 8
BF16_SUBLANES = 16
VMEM_LIMIT_BYTES = 61 * 1024 * 1024

MOD_TK = 256
TM = 512
CW = 512
GATE_TM = 1024
GATE_NG = 2048
GATE_TN = 1024
POOL_HALO = BF16_SUBLANES
W_STAGE_SLOTS = 8

Q0 = 0
KV0 = Q0 + ATTN_WIDTH
GA0 = KV0 + 2 * KV_WIDTH
U0 = GA0 + ATTN_WIDTH


def _silu(v):
    return v * (1.0 / (1.0 + jnp.exp(-v)))


def _half_silu(hv):
    return hv + hv * jnp.tanh(hv)


def _silu_column(c_row):
    n = c_row.shape[1]
    on_diag = (lax.broadcasted_iota(jnp.int32, (n, n), 0)
               == lax.broadcasted_iota(jnp.int32, (n, n), 1))
    return jnp.sum(jnp.where(on_diag, _silu(c_row), 0.0), axis=1, keepdims=True)


def _adaln_rmsnorm(x, gain, shift):
    ms = jnp.mean(x * x, axis=-1, keepdims=True)
    return (x * lax.rsqrt(ms + EPS)) * gain + shift


def _shift_scale_bias_kernel(tbl_ref, c_ref, w_ref, bsh_ref, bsc_ref, bucket_ref, wpool_ref,
                             shift_ref, scale_ref, bias_ref, wpool_o_ref, *, tile_buckets):
    k = pl.program_id(0)
    d = shift_ref.shape[1]

    @pl.when(k == 0)
    def _():
        shift_ref[...] = bsh_ref[...]
        scale_ref[...] = bsc_ref[...]

    s = _silu_column(c_ref[...])
    both = jnp.sum(s * w_ref[...], axis=0, keepdims=True)
    shift_ref[...] += both[:, :d]
    scale_ref[...] += both[:, d:]

    wpool_o_ref[...] = wpool_ref[...].astype(wpool_o_ref.dtype)

    for t, buckets in enumerate(tile_buckets):
        bk = bucket_ref[:, t * BLOCK:(t + 1) * BLOCK]
        acc = jnp.full((BLOCK, BLOCK), NEG_INF, jnp.float32)
        for b in buckets:
            acc = jnp.where(bk == b, tbl_ref[k, b] * INV_ATTN_SCALE, acc)
        bias_ref[0, :, t * BLOCK:(t + 1) * BLOCK] = acc


def _shift_scale_bias(c_row, w_ada, b_ada_row, rel_table, bucket, wpool):
    d = w_ada.shape[0]
    steps = d // MOD_TK
    assert steps == N_Q_HEADS, "one bias-table head per grid step"
    wpool2 = wpool.reshape(-1, wpool.shape[-1])
    pool_slab = (wpool2.shape[0] // steps, wpool2.shape[1])
    assert wpool2.shape[0] % (steps * BF16_SUBLANES) == 0
    tile_buckets = tuple(
        tuple(int(b) for b in np.unique(bucket[:, t * BLOCK:(t + 1) * BLOCK]) if b != MASKED_BUCKET)
        for t in range(SPAN // BLOCK))
    return pl.pallas_call(
        functools.partial(_shift_scale_bias_kernel, tile_buckets=tile_buckets),
        grid=(steps,),
        in_specs=[
            pl.BlockSpec(memory_space=pltpu.SMEM),
            pl.BlockSpec((1, MOD_TK), lambda k: (0, k)),
            pl.BlockSpec((MOD_TK, 2 * d), lambda k: (k, 0)),
            pl.BlockSpec((1, d), lambda k: (0, 0)),
            pl.BlockSpec((1, d), lambda k: (0, 1)),
            pl.BlockSpec(bucket.shape, lambda k: (0, 0)),
            pl.BlockSpec(pool_slab, lambda k: (k, 0)),
        ],
        out_specs=[
            pl.BlockSpec((1, d), lambda k: (0, 0)),
            pl.BlockSpec((1, d), lambda k: (0, 0)),
            pl.BlockSpec((1, BLOCK, SPAN), lambda k: (k, 0, 0)),
            pl.BlockSpec(pool_slab, lambda k: (k, 0)),
        ],
        out_shape=[
            jax.ShapeDtypeStruct((1, d), jnp.float32),
            jax.ShapeDtypeStruct((1, d), jnp.float32),
            jax.ShapeDtypeStruct((N_Q_HEADS, BLOCK, SPAN), jnp.float32),
            jax.ShapeDtypeStruct(wpool2.shape, jnp.bfloat16),
        ],
        compiler_params=pltpu.CompilerParams(
            dimension_semantics=("arbitrary",), vmem_limit_bytes=VMEM_LIMIT_BYTES),
        name="shift_scale_bias",
    )(rel_table.T, c_row, w_ada, b_ada_row, b_ada_row, bucket, wpool2)


def _t5_bucket(rel):
    half = N_BUCKETS // 2
    max_exact = half // 2
    assert (max_exact, MAX_DISTANCE // max_exact, half - max_exact) == (8, 16, 8)
    n = abs(rel)
    large = min(max_exact + (n * n).bit_length() - 7, half - 1)
    return (half if rel > 0 else 0) + (n if n < max_exact else large)


def _bucket_index_table():
    table = np.full((BLOCK, SPAN), MASKED_BUCKET, np.int32)
    for q in range(BLOCK):
        for t in range(SPAN):
            rel = t - WINDOW - q
            if abs(rel) <= WINDOW:
                table[q, t] = _t5_bucket(rel)
    return table


def _merge_gate_kernel(h_ref, w_hbm_ref, wa_ref, wp_ref, wout_ref, c_ref, wgate_ref, bgate_ref,
                       o_ref, wa_o_ref, wp_o_ref, wout_o_ref, gate_ref,
                       wbf_ref, wstage_ref, wsem_ref):
    j, i = pl.program_id(0), pl.program_id(1)
    d, ng = wbf_ref.shape
    slab_rows = d // W_STAGE_SLOTS

    def group_copies(g):
        col = pl.multiple_of(g * ng, ng)
        return [pltpu.make_async_copy(
                    w_hbm_ref.at[pl.ds(r * slab_rows, slab_rows), pl.ds(col, ng)],
                    wstage_ref.at[pl.ds(r * slab_rows, slab_rows)], wsem_ref.at[r])
                for r in range(W_STAGE_SLOTS)]

    @pl.when((j == 0) & (i == 0))
    def _():
        gate_ref[...] = bgate_ref[...]
        for cp in group_copies(0):
            cp.start()

    @pl.when(i == 0)
    def _():
        for r, cp in enumerate(group_copies(j)):
            cp.wait()
            rows = slice(r * slab_rows, (r + 1) * slab_rows)
            wbf_ref[rows, :] = (0.5 * wstage_ref[rows, :]).astype(wbf_ref.dtype)

    @pl.when((i == 1) & (j + 1 < pl.num_programs(0)))
    def _():
        for cp in group_copies(j + 1):
            cp.start()

    for lo in range(0, ng, GATE_TN):
        o_ref[:, lo:lo + GATE_TN] = jnp.dot(h_ref[...], wbf_ref[:, lo:lo + GATE_TN],
                                            preferred_element_type=jnp.float32).astype(o_ref.dtype)

    wa_o_ref[...] = wa_ref[...].astype(wa_o_ref.dtype)
    wp_o_ref[...] = wp_ref[...].astype(wp_o_ref.dtype)
    wout_o_ref[...] = wout_ref[...].astype(wout_o_ref.dtype)
    gate_ref[...] += jnp.sum(_silu_column(c_ref[...]) * wgate_ref[...], axis=0, keepdims=True)


def _merge_gate(h, w_merge, wa, wp, wout, c_row, w_ada, b_ada_row):
    s, d = h.shape
    n = w_merge.shape[1]
    n_i = s // GATE_TM
    steps = (n // GATE_NG) * n_i
    slab = lambda w: (w.shape[0] // steps, w.shape[1])
    step = lambda j, i: (j * n_i + i, 0)
    for w in (wa, wp, wout):
        assert w.shape[0] % (steps * BF16_SUBLANES) == 0
    assert d % (steps * SUBLANES) == 0 and d % W_STAGE_SLOTS == 0 and GATE_NG % GATE_TN == 0
    assert n_i >= 2
    const2 = lambda j, i: (0, 0)
    bf16 = jnp.bfloat16
    return pl.pallas_call(
        _merge_gate_kernel,
        grid=(n // GATE_NG, n_i),
        in_specs=[
            pl.BlockSpec((GATE_TM, d), lambda j, i: (i, 0)),
            pl.BlockSpec(memory_space=pl.ANY),
            pl.BlockSpec(slab(wa), step),
            pl.BlockSpec(slab(wp), step),
            pl.BlockSpec(slab(wout), step),
            pl.BlockSpec((1, d // steps), lambda j, i: (0, j * n_i + i)),
            pl.BlockSpec((d // steps, d), lambda j, i: (j * n_i + i, 2)),
            pl.BlockSpec((1, d), lambda j, i: (0, 2)),
        ],
        out_specs=[
            pl.BlockSpec((GATE_TM, GATE_NG), lambda j, i: (i, j)),
            pl.BlockSpec(slab(wa), step),
            pl.BlockSpec(slab(wp), step),
            pl.BlockSpec(slab(wout), step),
            pl.BlockSpec((1, d), const2),
        ],
        out_shape=[
            jax.ShapeDtypeStruct((s, n), bf16),
            jax.ShapeDtypeStruct(wa.shape, bf16),
            jax.ShapeDtypeStruct(wp.shape, bf16),
            jax.ShapeDtypeStruct(wout.shape, bf16),
            jax.ShapeDtypeStruct((1, d), jnp.float32),
        ],
        scratch_shapes=[
            pltpu.VMEM((d, GATE_NG), bf16),
            pltpu.VMEM((d, GATE_NG), jnp.float32),
            pltpu.SemaphoreType.DMA((W_STAGE_SLOTS,)),
        ],
        compiler_params=pltpu.CompilerParams(
            dimension_semantics=("arbitrary", "arbitrary"), vmem_limit_bytes=VMEM_LIMIT_BYTES),
        name="merge_gate",
    )(h, w_merge, wa, wp, wout, c_row, w_ada, b_ada_row)


def _project_mix_kernel(sink_ref,
                        x_ref, shift_ref, scale_ref, g_ref, w_hbm_ref, wpool_ref, pscale_ref, bias_ref,
                        ya_ref, yp_ref, h_ref,
                        pbuf_ref, kvtail_ref, utail_ref, uext_ref, w_ref, wstage_ref, wsem_ref,
                        kt_ref, kttail_ref,
                        *, seq_len):
    s = pl.program_id(0)
    n_tok = pl.num_programs(0) - 1
    tm, d = x_ref.shape
    n_sub = tm // BLOCK
    n_blocks = seq_len // BLOCK
    aw = ATTN_WIDTH
    pw = pscale_ref.shape[1]
    q0, ga0, gp0 = Q0, GA0, U0 + pw
    n_chunks = w_ref.shape[1] // CW
    f32 = jnp.float32
    bf16 = jnp.bfloat16
    cur = s % 2
    prv = 1 - cur
    j = s - 1

    n_slots, slab_rows = wstage_ref.shape[0], wstage_ref.shape[1]
    assert n_slots * slab_rows == d and wstage_ref.shape[2] == CW

    def slab_copy(c, r):
        return pltpu.make_async_copy(
            w_hbm_ref.at[pl.ds(r * slab_rows, slab_rows), pl.ds(c * CW, CW)],
            wstage_ref.at[r], wsem_ref.at[r])

    def fetch_chunk(c):
        for r in range(n_slots):
            slab_copy(c, r).start()

    def land_chunk(c, next_c):
        for r in range(n_slots):
            slab_copy(c, r).wait()
            w_ref[r * slab_rows:(r + 1) * slab_rows, c * CW:(c + 1) * CW] = wstage_ref[r].astype(bf16)
            if next_c is not None:
                slab_copy(next_c, r).start()

    def prologue():
        gain = g_ref[...] * (1.0 + scale_ref[...])
        h_ref[...] = _adaln_rmsnorm(x_ref[...], gain, shift_ref[...]).astype(bf16)

    def is_gate_col(col):
        return ga0 <= col < ga0 + aw or gp0 <= col < gp0 + pw

    def proj_chunk(c):
        lo, hi = c * CW, (c + 1) * CW
        p = jnp.dot(h_ref[...], w_ref[:, lo:hi], preferred_element_type=f32)
        if is_gate_col(lo):
            assert is_gate_col(hi - 1)
            p = 0.5 * p
        pbuf_ref[cur, :, lo:hi] = p.astype(bf16)
        if lo <= KV0 < hi:
            assert KV0 + KV_WIDTH <= hi
            kt_ref[cur] = p[:, KV0 - lo:KV0 - lo + KV_WIDTH].T.astype(bf16)

    def window_rows(b, col0):
        r0, r1 = (b - 1) * BLOCK, (b + 2) * BLOCK
        parts = []
        if r0 < 0:
            parts.append(kvtail_ref[prv, :, col0 - KV0:col0 - KV0 + HEAD_DIM])
            r0 = 0
        parts.append(pbuf_ref[prv, r0:min(r1, tm), col0:col0 + HEAD_DIM])
        if r1 > tm:
            parts.append(pbuf_ref[cur, 0:r1 - tm, col0:col0 + HEAD_DIM])
        return jnp.concatenate(parts, axis=0) if len(parts) > 1 else parts[0]

    def key_window(b, kvh):
        hd = slice(kvh * HEAD_DIM, (kvh + 1) * HEAD_DIM)
        r0, r1 = (b - 1) * BLOCK, (b + 2) * BLOCK
        parts = []
        if r0 < 0:
            parts.append(kttail_ref[prv, hd, :])
            r0 = 0
        parts.append(kt_ref[prv, hd, r0:min(r1, tm)])
        if r1 > tm:
            parts.append(kt_ref[cur, hd, 0:r1 - tm])
        return jnp.concatenate(parts, axis=1) if len(parts) > 1 else parts[0]

    def attn_scores(b, kvh):
        blk = j * n_sub + b
        h0 = kvh * GQA_GROUP
        kt = key_window(b, kvh)
        v = window_rows(b, KV0 + KV_WIDTH + kvh * HEAD_DIM)
        qs = jnp.concatenate(
            [pbuf_ref[prv, b * BLOCK:(b + 1) * BLOCK,
                      q0 + (h0 + g) * HEAD_DIM:q0 + (h0 + g + 1) * HEAD_DIM]
             for g in range(GQA_GROUP)], axis=0)
        z = jnp.dot(qs, kt, preferred_element_type=f32)
        z = z + bias_ref[h0:h0 + GQA_GROUP].reshape(GQA_GROUP * BLOCK, SPAN)
        col = lax.broadcasted_iota(jnp.int32, (1, SPAN), 1)
        if b == 0:
            z = jnp.where(col < jnp.where(blk == 0, WINDOW, 0), NEG_INF, z)
        if b == n_sub - 1:
            z = jnp.where(col >= jnp.where(blk == n_blocks - 1, WINDOW + BLOCK, SPAN), NEG_INF, z)
        sink = jnp.concatenate(
            [jnp.full((BLOCK, 1), sink_ref[h0 + g] * INV_ATTN_SCALE, f32)
             for g in range(GQA_GROUP)], axis=0)
        m = jnp.maximum(jnp.max(z, axis=-1, keepdims=True), sink)
        p = jnp.exp2((z - m) * EXP2_SCALE)
        denom = jnp.sum(p, axis=-1, keepdims=True) + jnp.exp2((sink - m) * EXP2_SCALE)
        return p.astype(bf16), denom, v

    def attn_out(p, denom, v):
        o = jnp.dot(p, v, preferred_element_type=f32) / denom
        return [o[g * BLOCK:(g + 1) * BLOCK] for g in range(GQA_GROUP)]

    def finish_sub_block(b, heads):
        y = jnp.concatenate(heads, axis=1)
        hg = pbuf_ref[prv, b * BLOCK:(b + 1) * BLOCK, ga0:ga0 + aw].astype(f32)
        ya_ref[b * BLOCK:(b + 1) * BLOCK] = (y * _half_silu(hg)).astype(bf16)

    ext = tm + 2 * POOL_HALO

    def pool_group(gi):
        w = POOL_SIZES[gi]
        gw = pw // N_POOL_GROUPS
        c0, c1 = gi * gw, (gi + 1) * gw
        half = w // 2
        halo_row = lax.broadcasted_iota(jnp.int32, (POOL_HALO, 1), 0)
        uext_ref[0:POOL_HALO] = jnp.where(j * tm - POOL_HALO + halo_row >= 0,
                                          utail_ref[prv, :, c0:c1].astype(f32), 0.0)
        uext_ref[POOL_HALO:POOL_HALO + tm] = pbuf_ref[prv, :, U0 + c0:U0 + c1].astype(f32)
        uext_ref[POOL_HALO + tm:ext] = jnp.where(
            (j + 1) * tm + halo_row < seq_len,
            pbuf_ref[cur, 0:POOL_HALO, U0 + c0:U0 + c1].astype(f32), 0.0)
        a = uext_ref[...]
        sh = 1
        while sh < w:
            a = a + pltpu.roll(a, ext - sh, axis=0)
            sh *= 2
        off = POOL_HALO - half
        if off % SUBLANES:
            a = pltpu.roll(a, ext - off, axis=0)
            off = 0
        pos = j * tm + lax.broadcasted_iota(jnp.int32, (tm, 1), 0)
        cnt = (jnp.minimum(pos + half, seq_len) - jnp.maximum(pos - half, 0)).astype(f32)
        pooled = a[off:off + tm] / cnt - uext_ref[POOL_HALO:POOL_HALO + tm]
        mixed = jnp.dot(pooled.astype(bf16), wpool_ref[gi], preferred_element_type=f32)
        hg = pbuf_ref[prv, :, gp0 + c0:gp0 + c1].astype(f32)
        yp_ref[:, c0:c1] = (mixed * pscale_ref[:, c0:c1] * _half_silu(hg)).astype(bf16)

    def save_tails():
        kvtail_ref[cur] = pbuf_ref[prv, tm - BLOCK:tm, KV0:KV0 + 2 * KV_WIDTH]
        utail_ref[cur] = pbuf_ref[prv, tm - POOL_HALO:tm, U0:U0 + pw]
        kttail_ref[cur] = kt_ref[prv, :, tm - BLOCK:tm]

    units = [(b, kvh) for b in range(n_sub) for kvh in range(N_KV_HEADS)]
    assert KV0 % CW == 0 and (2 * KV_WIDTH) % CW == 0 and U0 % CW == 0 and pw % CW == 0
    first = list(range(KV0 // CW, (KV0 + 2 * KV_WIDTH) // CW)) + list(range(U0 // CW, (U0 + pw) // CW))
    order = first + [c for c in range(n_chunks) if c not in first]
    u_chunks_done = len(first)
    assert len(units) + 1 <= n_chunks and u_chunks_done + N_POOL_GROUPS <= n_chunks

    def mix_only():
        pending = [attn_scores(b, kvh) for b, kvh in units]
        heads = []
        for (b, kvh), scores in zip(units, pending):
            heads += attn_out(*scores)
            if kvh == N_KV_HEADS - 1:
                finish_sub_block(b, heads)
                heads = []
        for gi in range(N_POOL_GROUPS):
            pool_group(gi)
        save_tails()

    def project_first():
        fetch_chunk(order[0])
        prologue()
        for pos, c in enumerate(order):
            land_chunk(c, order[pos + 1] if pos + 1 < len(order) else None)
            proj_chunk(c)

    def project_and_mix():
        prologue()
        pending = attn_scores(*units[0])
        heads = []
        for pos, c in enumerate(order):
            proj_chunk(c)
            if pos < len(units):
                heads += attn_out(*pending)
                if pos + 1 < len(units):
                    pending = attn_scores(*units[pos + 1])
                b, kvh = units[pos]
                if kvh == N_KV_HEADS - 1:
                    finish_sub_block(b, heads)
                    heads = []
            if u_chunks_done <= pos < u_chunks_done + N_POOL_GROUPS:
                pool_group(pos - u_chunks_done)
        save_tails()

    @pl.when(s == 0)
    def _():
        kvtail_ref[0] = jnp.zeros(kvtail_ref.shape[1:], kvtail_ref.dtype)
        utail_ref[0] = jnp.zeros(utail_ref.shape[1:], utail_ref.dtype)
        kttail_ref[0] = jnp.zeros(kttail_ref.shape[1:], kttail_ref.dtype)
        project_first()

    @pl.when((s > 0) & (s < n_tok))
    def _():
        project_and_mix()

    @pl.when(s == n_tok)
    def _():
        mix_only()


def _project_mix(x2, shift, scale, pre_g_row, w_in, wpool, pool_scale_row, bias_tbl, sink):
    s, d = x2.shape
    tm = TM
    n_tok = s // tm
    pw = pool_scale_row.shape[1]
    n_in = w_in.shape[1]
    assert s // BLOCK >= 2, "first and last attention block must differ"
    assert n_in == U0 + 2 * pw and d % (W_STAGE_SLOTS * BF16_SUBLANES) == 0
    cur_blk = lambda st: jnp.minimum(st, n_tok - 1)
    mix_blk = lambda st: jnp.maximum(st - 1, 0)
    const2 = lambda st: (0, 0)
    single = pl.Buffered(1)
    bf16 = jnp.bfloat16
    return pl.pallas_call(
        functools.partial(_project_mix_kernel, seq_len=s),
        grid=(n_tok + 1,),
        in_specs=[
            pl.BlockSpec(memory_space=pltpu.SMEM),
            pl.BlockSpec((tm, d), lambda st: (cur_blk(st), 0)),
            pl.BlockSpec((1, d), const2),
            pl.BlockSpec((1, d), const2),
            pl.BlockSpec((1, d), const2),
            pl.BlockSpec(memory_space=pl.ANY),
            pl.BlockSpec(wpool.shape, lambda st: (0, 0, 0), pipeline_mode=single),
            pl.BlockSpec((1, pw), const2),
            pl.BlockSpec(bias_tbl.shape, lambda st: (0, 0, 0), pipeline_mode=single),
        ],
        out_specs=[
            pl.BlockSpec((tm, ATTN_WIDTH), lambda st: (mix_blk(st), 0)),
            pl.BlockSpec((tm, pw), lambda st: (mix_blk(st), 0)),
            pl.BlockSpec((tm, d), lambda st: (cur_blk(st), 0)),
        ],
        out_shape=[
            jax.ShapeDtypeStruct((s, ATTN_WIDTH), bf16),
            jax.ShapeDtypeStruct((s, pw), bf16),
            jax.ShapeDtypeStruct((s, d), bf16),
        ],
        scratch_shapes=[
            pltpu.VMEM((2, tm, n_in), bf16),
            pltpu.VMEM((2, BLOCK, 2 * KV_WIDTH), bf16),
            pltpu.VMEM((2, POOL_HALO, pw), bf16),
            pltpu.VMEM((tm + 2 * POOL_HALO, pw // N_POOL_GROUPS), jnp.float32),
            pltpu.VMEM(w_in.shape, bf16),
            pltpu.VMEM((W_STAGE_SLOTS, d // W_STAGE_SLOTS, CW), jnp.float32),
            pltpu.SemaphoreType.DMA((W_STAGE_SLOTS,)),
            pltpu.VMEM((2, KV_WIDTH, tm), bf16),
            pltpu.VMEM((2, KV_WIDTH, BLOCK), bf16),
        ],
        compiler_params=pltpu.CompilerParams(
            dimension_semantics=("arbitrary",), vmem_limit_bytes=VMEM_LIMIT_BYTES),
        name="project_mix",
    )(sink, x2, shift, scale, pre_g_row, w_in, wpool, pool_scale_row, bias_tbl)


def _merge_out_kernel(ya_ref, yp_ref, gm_ref, x_ref, gate_ref, postg_ref, bm_ref,
                      wa_ref, wp_ref, wout_ref, o_ref, merged_ref, oacc_ref, inv_ref):
    s = pl.program_id(0)
    n_tok = pl.num_programs(0) - 1
    tm, d = x_ref.shape
    n_chunks = d // CW
    f32 = jnp.float32

    def finish_previous():
        scale = gate_ref[...] * postg_ref[...]
        inv = inv_ref[...]
        for c in range(n_chunks):
            lo, hi = c * CW, (c + 1) * CW
            o_ref[:, lo:hi] = x_ref[:, lo:hi] + (oacc_ref[:, lo:hi] * inv) * scale[:, lo:hi]

    def matmuls():
        half_bm = 0.5 * bm_ref[...]
        for c in range(n_chunks):
            lo, hi = c * CW, (c + 1) * CW
            bra = jnp.dot(ya_ref[...], wa_ref[:, lo:hi], preferred_element_type=f32)
            brp = jnp.dot(yp_ref[...], wp_ref[:, lo:hi], preferred_element_type=f32)
            g_a = 0.5 * jnp.tanh(gm_ref[:, lo:hi].astype(f32) + half_bm[:, lo:hi]) + 0.5
            g_p = 0.5 * jnp.tanh(gm_ref[:, d + lo:d + hi].astype(f32)
                                 + half_bm[:, d + lo:d + hi]) + 0.5
            merged_ref[:, lo:hi] = (g_a * bra + g_p * brp).astype(merged_ref.dtype)
        ssq = jnp.zeros((tm, 1), f32)
        for c in range(n_chunks):
            lo, hi = c * CW, (c + 1) * CW
            o = jnp.dot(merged_ref[...], wout_ref[:, lo:hi], preferred_element_type=f32)
            ssq = ssq + jnp.sum(o * o, axis=-1, keepdims=True)
            oacc_ref[:, lo:hi] = o
        inv_ref[...] = lax.rsqrt(ssq * (1.0 / d) + EPS)

    @pl.when(s == 0)
    def _():
        oacc_ref[...] = jnp.zeros(oacc_ref.shape, oacc_ref.dtype)
        inv_ref[...] = jnp.zeros(inv_ref.shape, inv_ref.dtype)

    @pl.when(s < n_tok)
    def _():
        finish_previous()
        matmuls()

    @pl.when(s == n_tok)
    def _():
        finish_previous()


def _merge_out(ya, yp, gm, x2, gate, post_g_row, b_merge_row, wa, wp, wout):
    s, d = x2.shape
    tm = TM
    n_tok = s // tm
    aw, pw = ya.shape[1], yp.shape[1]
    cur = lambda st: (jnp.minimum(st, n_tok - 1), 0)
    prev = lambda st: (jnp.maximum(st - 1, 0), 0)
    const2 = lambda st: (0, 0)
    single = pl.Buffered(1)
    return pl.pallas_call(
        _merge_out_kernel,
        grid=(n_tok + 1,),
        in_specs=[
            pl.BlockSpec((tm, aw), cur),
            pl.BlockSpec((tm, pw), cur),
            pl.BlockSpec((tm, 2 * d), cur),
            pl.BlockSpec((tm, d), prev),
            pl.BlockSpec((1, d), const2),
            pl.BlockSpec((1, d), const2),
            pl.BlockSpec((1, 2 * d), const2),
            pl.BlockSpec((aw, d), const2, pipeline_mode=single),
            pl.BlockSpec((pw, d), const2, pipeline_mode=single),
            pl.BlockSpec((d, d), const2, pipeline_mode=single),
        ],
        out_specs=pl.BlockSpec((tm, d), prev),
        out_shape=jax.ShapeDtypeStruct((s, d), jnp.float32),
        scratch_shapes=[
            pltpu.VMEM((tm, d), jnp.bfloat16),
            pltpu.VMEM((tm, d), jnp.float32),
            pltpu.VMEM((tm, 1), jnp.float32),
        ],
        compiler_params=pltpu.CompilerParams(
            dimension_semantics=("arbitrary",), vmem_limit_bytes=VMEM_LIMIT_BYTES),
        name="merge_out",
    )(ya, yp, gm, x2, gate, post_g_row, b_merge_row, wa, wp, wout)


def _layer(x2, c, rel_table, bucket, w_ada, b_ada, pre_g, post_g, w_in, sink, w_pool, pool_scale,
           w_br_attn, w_br_pool, w_merge, b_merge, w_out):
    s, d = x2.shape
    c_row = c.reshape(1, d)
    b_ada_row = b_ada.reshape(1, -1)
    pre_g_row = pre_g.reshape(1, d)
    shift, scale, bias_tbl, w_pool_bf = _shift_scale_bias(c_row, w_ada, b_ada_row, rel_table, bucket,
                                                          w_pool)
    ya, yp, h = _project_mix(x2, shift, scale, pre_g_row, w_in, w_pool_bf.reshape(w_pool.shape),
                             pool_scale.reshape(1, -1), bias_tbl, sink)
    gm, wa, wp, wout, gate = _merge_gate(h, w_merge, w_br_attn, w_br_pool, w_out,
                                         c_row, w_ada, b_ada_row)
    return _merge_out(ya, yp, gm, x2, gate, post_g.reshape(1, d), b_merge.reshape(1, -1),
                      wa, wp, wout)


def kernel(x, c, rel_bias_table, w_ada, b_ada, pre_norm_g, post_norm_g, w_in, attn_sink,
           w_pool_group, pool_scale, w_branch_attn, w_branch_pool, w_merge, b_merge, w_out):
    batch, s, d = x.shape
    assert batch == 1, "kernel is written for a single sequence"
    depth = w_ada.shape[0]
    bucket = _bucket_index_table()
    x2 = x.reshape(s, d)
    for l in range(depth):
        x2 = _layer(x2, c[0], rel_bias_table, bucket, w_ada[l], b_ada[l], pre_norm_g[l],
                    post_norm_g[l], w_in[l], attn_sink[l], w_pool_group[l], pool_scale[l],
                    w_branch_attn[l], w_branch_pool[l], w_merge[l], b_merge[l], w_out[l])
    return x2.reshape(batch, s, d)
```

```python
import functools
import math

import numpy as np
import jax
import jax.numpy as jnp
from jax import lax
from jax.experimental import pallas as pl
from jax.experimental.pallas import tpu as pltpu

HEAD_DIM = 128
N_Q_HEADS = 8
N_KV_HEADS = 2
GQA_GROUP = N_Q_HEADS // N_KV_HEADS
ATTN_WIDTH = N_Q_HEADS * HEAD_DIM
KV_WIDTH = N_KV_HEADS * HEAD_DIM
WINDOW = 128
BLOCK = 128
SPAN = BLOCK + 2 * WINDOW
N_BUCKETS = 32
MAX_DISTANCE = 128
POOL_SIZES = (2, 4, 8, 16)
N_POOL_GROUPS = len(POOL_SIZES)
EPS = 1e-6
NEG_INF = -1e30
MASKED_BUCKET = N_BUCKETS
ATTN_SCALE = HEAD_DIM ** -0.5
INV_ATTN_SCALE = HEAD_DIM ** 0.5
EXP2_SCALE = ATTN_SCALE * math.log2(math.e)

SUBLANES = 8
BF16_SUBLANES = 16
VMEM_LIMIT_BYTES = 61 * 1024 * 1024

MOD_TK = 256
TM = 512
CW = 512
GATE_TM = 2048
GATE_ROWS = 1024
GATE_TN = 1024
POOL_HALO = BF16_SUBLANES
W_STAGE_SLOTS = 8

Q0 = 0
KV0 = Q0 + ATTN_WIDTH
GA0 = KV0 + 2 * KV_WIDTH
U0 = GA0 + ATTN_WIDTH


def _silu(v):
    return v * (1.0 / (1.0 + jnp.exp(-v)))


def _half_silu(hv):
    return hv + hv * jnp.tanh(hv)


def _silu_column(c_row):
    n = c_row.shape[1]
    on_diag = (lax.broadcasted_iota(jnp.int32, (n, n), 0)
               == lax.broadcasted_iota(jnp.int32, (n, n), 1))
    return jnp.sum(jnp.where(on_diag, _silu(c_row), 0.0), axis=1, keepdims=True)


def _adaln_rmsnorm(x, gain, shift):
    ms = jnp.mean(x * x, axis=-1, keepdims=True)
    return (x * lax.rsqrt(ms + EPS)) * gain + shift


def _shift_scale_bias_kernel(tbl_ref, c_ref, wsh_ref, wsc_ref, bsh_ref, bsc_ref, bucket_ref, wpool_ref,
                             shift_ref, scale_ref, bias_ref, wpool_o_ref, *, tile_buckets):
    k = pl.program_id(0)

    @pl.when(k == 0)
    def _():
        shift_ref[...] = bsh_ref[...]
        scale_ref[...] = bsc_ref[...]

    s = _silu_column(c_ref[...])
    shift_ref[...] += jnp.sum(s * wsh_ref[...], axis=0, keepdims=True)
    scale_ref[...] += jnp.sum(s * wsc_ref[...], axis=0, keepdims=True)

    wpool_o_ref[...] = wpool_ref[...].astype(wpool_o_ref.dtype)

    for t, buckets in enumerate(tile_buckets):
        bk = bucket_ref[:, t * BLOCK:(t + 1) * BLOCK]
        acc = jnp.full((BLOCK, BLOCK), NEG_INF, jnp.float32)
        for b in buckets:
            acc = jnp.where(bk == b, tbl_ref[k, b] * INV_ATTN_SCALE, acc)
        bias_ref[0, :, t * BLOCK:(t + 1) * BLOCK] = acc


def _shift_scale_bias(c_row, w_ada, b_ada_row, rel_table, bucket, wpool):
    d = w_ada.shape[0]
    steps = d // MOD_TK
    assert steps == N_Q_HEADS, "one bias-table head per grid step"
    wpool2 = wpool.reshape(-1, wpool.shape[-1])
    pool_slab = (wpool2.shape[0] // steps, wpool2.shape[1])
    assert wpool2.shape[0] % (steps * BF16_SUBLANES) == 0
    tile_buckets = tuple(
        tuple(int(b) for b in np.unique(bucket[:, t * BLOCK:(t + 1) * BLOCK]) if b != MASKED_BUCKET)
        for t in range(SPAN // BLOCK))
    return pl.pallas_call(
        functools.partial(_shift_scale_bias_kernel, tile_buckets=tile_buckets),
        grid=(steps,),
        in_specs=[
            pl.BlockSpec(memory_space=pltpu.SMEM),
            pl.BlockSpec((1, MOD_TK), lambda k: (0, k)),
            pl.BlockSpec((MOD_TK, d), lambda k: (k, 0)),
            pl.BlockSpec((MOD_TK, d), lambda k: (k, 1)),
            pl.BlockSpec((1, d), lambda k: (0, 0)),
            pl.BlockSpec((1, d), lambda k: (0, 1)),
            pl.BlockSpec(bucket.shape, lambda k: (0, 0)),
            pl.BlockSpec(pool_slab, lambda k: (k, 0)),
        ],
        out_specs=[
            pl.BlockSpec((1, d), lambda k: (0, 0)),
            pl.BlockSpec((1, d), lambda k: (0, 0)),
            pl.BlockSpec((1, BLOCK, SPAN), lambda k: (k, 0, 0)),
            pl.BlockSpec(pool_slab, lambda k: (k, 0)),
        ],
        out_shape=[
            jax.ShapeDtypeStruct((1, d), jnp.float32),
            jax.ShapeDtypeStruct((1, d), jnp.float32),
            jax.ShapeDtypeStruct((N_Q_HEADS, BLOCK, SPAN), jnp.float32),
            jax.ShapeDtypeStruct(wpool2.shape, jnp.bfloat16),
        ],
        compiler_params=pltpu.CompilerParams(
            dimension_semantics=("arbitrary",), vmem_limit_bytes=VMEM_LIMIT_BYTES),
        name="shift_scale_bias",
    )(rel_table.T, c_row, w_ada, w_ada, b_ada_row, b_ada_row, bucket, wpool2)


def _t5_bucket(rel):
    half = N_BUCKETS // 2
    max_exact = half // 2
    assert (max_exact, MAX_DISTANCE // max_exact, half - max_exact) == (8, 16, 8)
    n = abs(rel)
    large = min(max_exact + (n * n).bit_length() - 7, half - 1)
    return (half if rel > 0 else 0) + (n if n < max_exact else large)


def _bucket_index_table():
    table = np.full((BLOCK, SPAN), MASKED_BUCKET, np.int32)
    for q in range(BLOCK):
        for t in range(SPAN):
            rel = t - WINDOW - q
            if abs(rel) <= WINDOW:
                table[q, t] = _t5_bucket(rel)
    return table


def _merge_gate_kernel(h_ref, w_ref, wa_ref, wp_ref, wout_ref, c_ref, wgate_ref, bgate_ref,
                       o_ref, wa_o_ref, wp_o_ref, wout_o_ref, gate_ref,
                       wbf_ref):
    j, i = pl.program_id(0), pl.program_id(1)

    @pl.when(i == 0)
    def _():
        wbf_ref[...] = (0.5 * w_ref[...]).astype(wbf_ref.dtype)

    @pl.when((j == 0) & (i == 0))
    def _():
        gate_ref[...] = bgate_ref[...]

    for r in range(0, h_ref.shape[0], GATE_ROWS):
        o_ref[r:r + GATE_ROWS] = jnp.dot(h_ref[r:r + GATE_ROWS], wbf_ref[...],
                                         preferred_element_type=jnp.float32).astype(o_ref.dtype)

    wa_o_ref[...] = wa_ref[...].astype(wa_o_ref.dtype)
    wp_o_ref[...] = wp_ref[...].astype(wp_o_ref.dtype)
    wout_o_ref[...] = wout_ref[...].astype(wout_o_ref.dtype)
    gate_ref[...] += jnp.sum(_silu_column(c_ref[...]) * wgate_ref[...], axis=0, keepdims=True)


def _merge_gate(h, w_merge, wa, wp, wout, c_row, w_ada, b_ada_row):
    s, d = h.shape
    n = w_merge.shape[1]
    n_i = s // GATE_TM
    steps = (n // GATE_TN) * n_i
    slab = lambda w: (w.shape[0] // steps, w.shape[1])
    step = lambda j, i: (j * n_i + i, 0)
    for w in (wa, wp, wout):
        assert w.shape[0] % (steps * BF16_SUBLANES) == 0
    assert d % (steps * SUBLANES) == 0
    const2 = lambda j, i: (0, 0)
    bf16 = jnp.bfloat16
    return pl.pallas_call(
        _merge_gate_kernel,
        grid=(n // GATE_TN, n_i),
        in_specs=[
            pl.BlockSpec((GATE_TM, d), lambda j, i: (i, 0)),
            pl.BlockSpec((d, GATE_TN), lambda j, i: (0, j)),
            pl.BlockSpec(slab(wa), step),
            pl.BlockSpec(slab(wp), step),
            pl.BlockSpec(slab(wout), step),
            pl.BlockSpec((1, d // steps), lambda j, i: (0, j * n_i + i)),
            pl.BlockSpec((d // steps, d), lambda j, i: (j * n_i + i, 2)),
            pl.BlockSpec((1, d), lambda j, i: (0, 2)),
        ],
        out_specs=[
            pl.BlockSpec((GATE_TM, GATE_TN), lambda j, i: (i, j)),
            pl.BlockSpec(slab(wa), step),
            pl.BlockSpec(slab(wp), step),
            pl.BlockSpec(slab(wout), step),
            pl.BlockSpec((1, d), const2),
        ],
        out_shape=[
            jax.ShapeDtypeStruct((s, n), bf16),
            jax.ShapeDtypeStruct(wa.shape, bf16),
            jax.ShapeDtypeStruct(wp.shape, bf16),
            jax.ShapeDtypeStruct(wout.shape, bf16),
            jax.ShapeDtypeStruct((1, d), jnp.float32),
        ],
        scratch_shapes=[pltpu.VMEM((d, GATE_TN), bf16)],
        compiler_params=pltpu.CompilerParams(
            dimension_semantics=("arbitrary", "arbitrary"), vmem_limit_bytes=VMEM_LIMIT_BYTES),
        name="merge_gate",
    )(h, w_merge, wa, wp, wout, c_row, w_ada, b_ada_row)


def _project_mix_kernel(sink_ref,
                        x_ref, shift_ref, scale_ref, g_ref, w_hbm_ref, wpool_ref, pscale_ref, bias_ref,
                        ya_ref, yp_ref, h_ref,
                        pbuf_ref, kvtail_ref, utail_ref, uext_ref, w_ref, wstage_ref, wsem_ref,
                        kt_ref, kttail_ref,
                        *, seq_len):
    s = pl.program_id(0)
    n_tok = pl.num_programs(0) - 1
    tm, d = x_ref.shape
    n_sub = tm // BLOCK
    n_blocks = seq_len // BLOCK
    aw = ATTN_WIDTH
    pw = pscale_ref.shape[1]
    q0, ga0, gp0 = Q0, GA0, U0 + pw
    n_chunks = w_ref.shape[1] // CW
    f32 = jnp.float32
    bf16 = jnp.bfloat16
    cur = s % 2
    prv = 1 - cur
    j = s - 1

    n_slots, slab_rows = wstage_ref.shape[0], wstage_ref.shape[1]
    assert n_slots * slab_rows == d and wstage_ref.shape[2] == CW

    def slab_copy(c, r):
        return pltpu.make_async_copy(
            w_hbm_ref.at[pl.ds(r * slab_rows, slab_rows), pl.ds(c * CW, CW)],
            wstage_ref.at[r], wsem_ref.at[r])

    def fetch_chunk(c):
        for r in range(n_slots):
            slab_copy(c, r).start()

    def land_chunk(c, next_c):
        for r in range(n_slots):
            slab_copy(c, r).wait()
            w_ref[r * slab_rows:(r + 1) * slab_rows, c * CW:(c + 1) * CW] = wstage_ref[r].astype(bf16)
            if next_c is not None:
                slab_copy(next_c, r).start()

    def prologue():
        gain = g_ref[...] * (1.0 + scale_ref[...])
        h_ref[...] = _adaln_rmsnorm(x_ref[...], gain, shift_ref[...]).astype(bf16)

    def is_gate_col(col):
        return ga0 <= col < ga0 + aw or gp0 <= col < gp0 + pw

    def proj_chunk(c):
        lo, hi = c * CW, (c + 1) * CW
        p = jnp.dot(h_ref[...], w_ref[:, lo:hi], preferred_element_type=f32)
        if is_gate_col(lo):
            assert is_gate_col(hi - 1)
            p = 0.5 * p
        pbuf_ref[cur, :, lo:hi] = p.astype(bf16)
        if lo <= KV0 < hi:
            assert KV0 + KV_WIDTH <= hi
            kt_ref[cur] = p[:, KV0 - lo:KV0 - lo + KV_WIDTH].T.astype(bf16)

    def window_rows(b, col0):
        r0, r1 = (b - 1) * BLOCK, (b + 2) * BLOCK
        parts = []
        if r0 < 0:
            parts.append(kvtail_ref[prv, :, col0 - KV0:col0 - KV0 + HEAD_DIM])
            r0 = 0
        parts.append(pbuf_ref[prv, r0:min(r1, tm), col0:col0 + HEAD_DIM])
        if r1 > tm:
            parts.append(pbuf_ref[cur, 0:r1 - tm, col0:col0 + HEAD_DIM])
        return jnp.concatenate(parts, axis=0) if len(parts) > 1 else parts[0]

    def key_window(b, kvh):
        hd = slice(kvh * HEAD_DIM, (kvh + 1) * HEAD_DIM)
        r0, r1 = (b - 1) * BLOCK, (b + 2) * BLOCK
        parts = []
        if r0 < 0:
            parts.append(kttail_ref[prv, hd, :])
            r0 = 0
        parts.append(kt_ref[prv, hd, r0:min(r1, tm)])
        if r1 > tm:
            parts.append(kt_ref[cur, hd, 0:r1 - tm])
        return jnp.concatenate(parts, axis=1) if len(parts) > 1 else parts[0]

    def attn_scores(b, kvh):
        blk = j * n_sub + b
        h0 = kvh * GQA_GROUP
        kt = key_window(b, kvh)
        v = window_rows(b, KV0 + KV_WIDTH + kvh * HEAD_DIM)
        qs = jnp.concatenate(
            [pbuf_ref[prv, b * BLOCK:(b + 1) * BLOCK,
                      q0 + (h0 + g) * HEAD_DIM:q0 + (h0 + g + 1) * HEAD_DIM]
             for g in range(GQA_GROUP)], axis=0)
        z = jnp.dot(qs, kt, preferred_element_type=f32)
        z = z + bias_ref[h0:h0 + GQA_GROUP].reshape(GQA_GROUP * BLOCK, SPAN)
        col = lax.broadcasted_iota(jnp.int32, (1, SPAN), 1)
        if b == 0:
            z = jnp.where(col < jnp.where(blk == 0, WINDOW, 0), NEG_INF, z)
        if b == n_sub - 1:
            z = jnp.where(col >= jnp.where(blk == n_blocks - 1, WINDOW + BLOCK, SPAN), NEG_INF, z)
        sink = jnp.concatenate(
            [jnp.full((BLOCK, 1), sink_ref[h0 + g] * INV_ATTN_SCALE, f32)
             for g in range(GQA_GROUP)], axis=0)
        m = jnp.maximum(jnp.max(z, axis=-1, keepdims=True), sink)
        p = jnp.exp2((z - m) * EXP2_SCALE)
        denom = jnp.sum(p, axis=-1, keepdims=True) + jnp.exp2((sink - m) * EXP2_SCALE)
        return p.astype(bf16), denom, v

    def attn_out(p, denom, v):
        o = jnp.dot(p, v, preferred_element_type=f32) / denom
        return [o[g * BLOCK:(g + 1) * BLOCK] for g in range(GQA_GROUP)]

    def finish_sub_block(b, heads):
        y = jnp.concatenate(heads, axis=1)
        hg = pbuf_ref[prv, b * BLOCK:(b + 1) * BLOCK, ga0:ga0 + aw].astype(f32)
        ya_ref[b * BLOCK:(b + 1) * BLOCK] = (y * _half_silu(hg)).astype(bf16)

    ext = tm + 2 * POOL_HALO

    def pool_group(gi):
        w = POOL_SIZES[gi]
        gw = pw // N_POOL_GROUPS
        c0, c1 = gi * gw, (gi + 1) * gw
        half = w // 2
        halo_row = lax.broadcasted_iota(jnp.int32, (POOL_HALO, 1), 0)
        uext_ref[0:POOL_HALO] = jnp.where(j * tm - POOL_HALO + halo_row >= 0,
                                          utail_ref[prv, :, c0:c1].astype(f32), 0.0)
        uext_ref[POOL_HALO:POOL_HALO + tm] = pbuf_ref[prv, :, U0 + c0:U0 + c1].astype(f32)
        uext_ref[POOL_HALO + tm:ext] = jnp.where(
            (j + 1) * tm + halo_row < seq_len,
            pbuf_ref[cur, 0:POOL_HALO, U0 + c0:U0 + c1].astype(f32), 0.0)
        a = uext_ref[...]
        sh = 1
        while sh < w:
            a = a + pltpu.roll(a, ext - sh, axis=0)
            sh *= 2
        off = POOL_HALO - half
        if off % SUBLANES:
            a = pltpu.roll(a, ext - off, axis=0)
            off = 0
        pos = j * tm + lax.broadcasted_iota(jnp.int32, (tm, 1), 0)
        cnt = (jnp.minimum(pos + half, seq_len) - jnp.maximum(pos - half, 0)).astype(f32)
        pooled = a[off:off + tm] / cnt - uext_ref[POOL_HALO:POOL_HALO + tm]
        mixed = jnp.dot(pooled.astype(bf16), wpool_ref[gi], preferred_element_type=f32)
        hg = pbuf_ref[prv, :, gp0 + c0:gp0 + c1].astype(f32)
        yp_ref[:, c0:c1] = (mixed * pscale_ref[:, c0:c1] * _half_silu(hg)).astype(bf16)

    def save_tails():
        kvtail_ref[cur] = pbuf_ref[prv, tm - BLOCK:tm, KV0:KV0 + 2 * KV_WIDTH]
        utail_ref[cur] = pbuf_ref[prv, tm - POOL_HALO:tm, U0:U0 + pw]
        kttail_ref[cur] = kt_ref[prv, :, tm - BLOCK:tm]

    units = [(b, kvh) for b in range(n_sub) for kvh in range(N_KV_HEADS)]
    assert KV0 % CW == 0 and (2 * KV_WIDTH) % CW == 0 and U0 % CW == 0 and pw % CW == 0
    first = list(range(KV0 // CW, (KV0 + 2 * KV_WIDTH) // CW)) + list(range(U0 // CW, (U0 + pw) // CW))
    order = first + [c for c in range(n_chunks) if c not in first]
    u_chunks_done = len(first)
    assert len(units) + 1 <= n_chunks and u_chunks_done + N_POOL_GROUPS <= n_chunks

    def mix_only():
        pending = [attn_scores(b, kvh) for b, kvh in units]
        heads = []
        for (b, kvh), scores in zip(units, pending):
            heads += attn_out(*scores)
            if kvh == N_KV_HEADS - 1:
                finish_sub_block(b, heads)
                heads = []
        for gi in range(N_POOL_GROUPS):
            pool_group(gi)
        save_tails()

    def project_first():
        fetch_chunk(order[0])
        prologue()
        for pos, c in enumerate(order):
            land_chunk(c, order[pos + 1] if pos + 1 < len(order) else None)
            proj_chunk(c)

    def project_and_mix():
        prologue()
        pending = attn_scores(*units[0])
        heads = []
        for pos, c in enumerate(order):
            proj_chunk(c)
            if pos < len(units):
                heads += attn_out(*pending)
                if pos + 1 < len(units):
                    pending = attn_scores(*units[pos + 1])
                b, kvh = units[pos]
                if kvh == N_KV_HEADS - 1:
                    finish_sub_block(b, heads)
                    heads = []
            if u_chunks_done <= pos < u_chunks_done + N_POOL_GROUPS:
                pool_group(pos - u_chunks_done)
        save_tails()

    @pl.when(s == 0)
    def _():
        kvtail_ref[0] = jnp.zeros(kvtail_ref.shape[1:], kvtail_ref.dtype)
        utail_ref[0] = jnp.zeros(utail_ref.shape[1:], utail_ref.dtype)
        kttail_ref[0] = jnp.zeros(kttail_ref.shape[1:], kttail_ref.dtype)
        project_first()

    @pl.when((s > 0) & (s < n_tok))
    def _():
        project_and_mix()

    @pl.when(s == n_tok)
    def _():
        mix_only()


def _project_mix(x2, shift, scale, pre_g_row, w_in, wpool, pool_scale_row, bias_tbl, sink):
    s, d = x2.shape
    tm = TM
    n_tok = s // tm
    pw = pool_scale_row.shape[1]
    n_in = w_in.shape[1]
    assert s // BLOCK >= 2, "first and last attention block must differ"
    assert n_in == U0 + 2 * pw and d % (W_STAGE_SLOTS * BF16_SUBLANES) == 0
    cur_blk = lambda st: jnp.minimum(st, n_tok - 1)
    mix_blk = lambda st: jnp.maximum(st - 1, 0)
    const2 = lambda st: (0, 0)
    single = pl.Buffered(1)
    bf16 = jnp.bfloat16
    return pl.pallas_call(
        functools.partial(_project_mix_kernel, seq_len=s),
        grid=(n_tok + 1,),
        in_specs=[
            pl.BlockSpec(memory_space=pltpu.SMEM),
            pl.BlockSpec((tm, d), lambda st: (cur_blk(st), 0)),
            pl.BlockSpec((1, d), const2),
            pl.BlockSpec((1, d), const2),
            pl.BlockSpec((1, d), const2),
            pl.BlockSpec(memory_space=pl.ANY),
            pl.BlockSpec(wpool.shape, lambda st: (0, 0, 0), pipeline_mode=single),
            pl.BlockSpec((1, pw), const2),
            pl.BlockSpec(bias_tbl.shape, lambda st: (0, 0, 0), pipeline_mode=single),
        ],
        out_specs=[
            pl.BlockSpec((tm, ATTN_WIDTH), lambda st: (mix_blk(st), 0)),
            pl.BlockSpec((tm, pw), lambda st: (mix_blk(st), 0)),
            pl.BlockSpec((tm, d), lambda st: (cur_blk(st), 0)),
        ],
        out_shape=[
            jax.ShapeDtypeStruct((s, ATTN_WIDTH), bf16),
            jax.ShapeDtypeStruct((s, pw), bf16),
            jax.ShapeDtypeStruct((s, d), bf16),
        ],
        scratch_shapes=[
            pltpu.VMEM((2, tm, n_in), bf16),
            pltpu.VMEM((2, BLOCK, 2 * KV_WIDTH), bf16),
            pltpu.VMEM((2, POOL_HALO, pw), bf16),
            pltpu.VMEM((tm + 2 * POOL_HALO, pw // N_POOL_GROUPS), jnp.float32),
            pltpu.VMEM(w_in.shape, bf16),
            pltpu.VMEM((W_STAGE_SLOTS, d // W_STAGE_SLOTS, CW), jnp.float32),
            pltpu.SemaphoreType.DMA((W_STAGE_SLOTS,)),
            pltpu.VMEM((2, KV_WIDTH, tm), bf16),
            pltpu.VMEM((2, KV_WIDTH, BLOCK), bf16),
        ],
        compiler_params=pltpu.CompilerParams(
            dimension_semantics=("arbitrary",), vmem_limit_bytes=VMEM_LIMIT_BYTES),
        name="project_mix",
    )(sink, x2, shift, scale, pre_g_row, w_in, wpool, pool_scale_row, bias_tbl)


def _merge_out_kernel(ya_ref, yp_ref, gm_ref, x_ref, gate_ref, postg_ref, bm_ref,
                      wa_ref, wp_ref, wout_ref, o_ref, merged_ref, oacc_ref, inv_ref):
    s = pl.program_id(0)
    n_tok = pl.num_programs(0) - 1
    tm, d = x_ref.shape
    n_chunks = d // CW
    f32 = jnp.float32

    def finish_previous():
        scale = gate_ref[...] * postg_ref[...]
        inv = inv_ref[...]
        for c in range(n_chunks):
            lo, hi = c * CW, (c + 1) * CW
            o_ref[:, lo:hi] = x_ref[:, lo:hi] + (oacc_ref[:, lo:hi] * inv) * scale[:, lo:hi]

    def matmuls():
        half_bm = 0.5 * bm_ref[...]
        for c in range(n_chunks):
            lo, hi = c * CW, (c + 1) * CW
            bra = jnp.dot(ya_ref[...], wa_ref[:, lo:hi], preferred_element_type=f32)
            brp = jnp.dot(yp_ref[...], wp_ref[:, lo:hi], preferred_element_type=f32)
            g_a = 0.5 * jnp.tanh(gm_ref[:, lo:hi].astype(f32) + half_bm[:, lo:hi]) + 0.5
            g_p = 0.5 * jnp.tanh(gm_ref[:, d + lo:d + hi].astype(f32)
                                 + half_bm[:, d + lo:d + hi]) + 0.5
            merged_ref[:, lo:hi] = (g_a * bra + g_p * brp).astype(merged_ref.dtype)
        ssq = jnp.zeros((tm, 1), f32)
        for c in range(n_chunks):
            lo, hi = c * CW, (c + 1) * CW
            o = jnp.dot(merged_ref[...], wout_ref[:, lo:hi], preferred_element_type=f32)
            ssq = ssq + jnp.sum(o * o, axis=-1, keepdims=True)
            oacc_ref[:, lo:hi] = o
        inv_ref[...] = lax.rsqrt(ssq * (1.0 / d) + EPS)

    @pl.when(s == 0)
    def _():
        oacc_ref[...] = jnp.zeros(oacc_ref.shape, oacc_ref.dtype)
        inv_ref[...] = jnp.zeros(inv_ref.shape, inv_ref.dtype)

    @pl.when(s < n_tok)
    def _():
        finish_previous()
        matmuls()

    @pl.when(s == n_tok)
    def _():
        finish_previous()


def _merge_out(ya, yp, gm, x2, gate, post_g_row, b_merge_row, wa, wp, wout):
    s, d = x2.shape
    tm = TM
    n_tok = s // tm
    aw, pw = ya.shape[1], yp.shape[1]
    cur = lambda st: (jnp.minimum(st, n_tok - 1), 0)
    prev = lambda st: (jnp.maximum(st - 1, 0), 0)
    const2 = lambda st: (0, 0)
    single = pl.Buffered(1)
    return pl.pallas_call(
        _merge_out_kernel,
        grid=(n_tok + 1,),
        in_specs=[
            pl.BlockSpec((tm, aw), cur),
            pl.BlockSpec((tm, pw), cur),
            pl.BlockSpec((tm, 2 * d), cur),
            pl.BlockSpec((tm, d), prev),
            pl.BlockSpec((1, d), const2),
            pl.BlockSpec((1, d), const2),
            pl.BlockSpec((1, 2 * d), const2),
            pl.BlockSpec((aw, d), const2, pipeline_mode=single),
            pl.BlockSpec((pw, d), const2, pipeline_mode=single),
            pl.BlockSpec((d, d), const2, pipeline_mode=single),
        ],
        out_specs=pl.BlockSpec((tm, d), prev),
        out_shape=jax.ShapeDtypeStruct((s, d), jnp.float32),
        scratch_shapes=[
            pltpu.VMEM((tm, d), jnp.bfloat16),
            pltpu.VMEM((tm, d), jnp.float32),
            pltpu.VMEM((tm, 1), jnp.float32),
        ],
        compiler_params=pltpu.CompilerParams(
            dimension_semantics=("arbitrary",), vmem_limit_bytes=VMEM_LIMIT_BYTES),
        name="merge_out",
    )(ya, yp, gm, x2, gate, post_g_row, b_merge_row, wa, wp, wout)


def _layer(x2, c, rel_table, bucket, w_ada, b_ada, pre_g, post_g, w_in, sink, w_pool, pool_scale,
           w_br_attn, w_br_pool, w_merge, b_merge, w_out):
    s, d = x2.shape
    c_row = c.reshape(1, d)
    b_ada_row = b_ada.reshape(1, -1)
    pre_g_row = pre_g.reshape(1, d)
    shift, scale, bias_tbl, w_pool_bf = _shift_scale_bias(c_row, w_ada, b_ada_row, rel_table, bucket,
                                                          w_pool)
    ya, yp, h = _project_mix(x2, shift, scale, pre_g_row, w_in, w_pool_bf.reshape(w_pool.shape),
                             pool_scale.reshape(1, -1), bias_tbl, sink)
    gm, wa, wp, wout, gate = _merge_gate(h, w_merge, w_br_attn, w_br_pool, w_out,
                                         c_row, w_ada, b_ada_row)
    return _merge_out(ya, yp, gm, x2, gate, post_g.reshape(1, d), b_merge.reshape(1, -1),
                      wa, wp, wout)


def kernel(x, c, rel_bias_table, w_ada, b_ada, pre_norm_g, post_norm_g, w_in, attn_sink,
           w_pool_group, pool_scale, w_branch_attn, w_branch_pool, w_merge, b_merge, w_out):
    batch, s, d = x.shape
    assert batch == 1, "kernel is written for a single sequence"
    depth = w_ada.shape[0]
    bucket = _bucket_index_table()
    x2 = x.reshape(s, d)
    for l in range(depth):
        x2 = _layer(x2, c[0], rel_bias_table, bucket, w_ada[l], b_ada[l], pre_norm_g[l],
                    post_norm_g[l], w_in[l], attn_sink[l], w_pool_group[l], pool_scale[l],
                    w_branch_attn[l], w_branch_pool[l], w_merge[l], b_merge[l], w_out[l])
    return x2.reshape(batch, s, d)
```

```python
import functools
import math

import numpy as np
import jax
import jax.numpy as jnp
from jax import lax
from jax.experimental import pallas as pl
from jax.experimental.pallas import tpu as pltpu

HEAD_DIM = 128
N_Q_HEADS = 8
N_KV_HEADS = 2
GQA_GROUP = N_Q_HEADS // N_KV_HEADS
ATTN_WIDTH = N_Q_HEADS * HEAD_DIM
KV_WIDTH = N_KV_HEADS * HEAD_DIM
WINDOW = 128
BLOCK = 128
SPAN = BLOCK + 2 * WINDOW
N_BUCKETS = 32
MAX_DISTANCE = 128
POOL_SIZES = (2, 4, 8, 16)
N_POOL_GROUPS = len(POOL_SIZES)
EPS = 1e-6
NEG_INF = -1e30
MASKED_BUCKET = N_BUCKETS
ATTN_SCALE = HEAD_DIM ** -0.5
INV_ATTN_SCALE = HEAD_DIM ** 0.5
EXP2_SCALE = ATTN_SCALE * math.log2(math.e)

SUBLANES = 8
BF16_SUBLANES = 16
VMEM_LIMIT_BYTES = 61 * 1024 * 1024
VMEM_LIMIT_SMALL_BYTES = 24 * 1024 * 1024
VMEM_LIMIT_MATMUL_BYTES = 56 * 1024 * 1024

MOD_TK = 256
TM = 512
CW = 512
GATE_TM = 2048
GATE_ROWS = 1024
GATE_TN = 1024
POOL_HALO = BF16_SUBLANES
W_STAGE_SLOTS = 8

Q0 = 0
KV0 = Q0 + ATTN_WIDTH
GA0 = KV0 + 2 * KV_WIDTH
U0 = GA0 + ATTN_WIDTH


def _silu(v):
    return v * (1.0 / (1.0 + jnp.exp(-v)))


def _half_silu(hv):
    return hv + hv * jnp.tanh(hv)


def _silu_column(c_row):
    n = c_row.shape[1]
    on_diag = (lax.broadcasted_iota(jnp.int32, (n, n), 0)
               == lax.broadcasted_iota(jnp.int32, (n, n), 1))
    return jnp.sum(jnp.where(on_diag, _silu(c_row), 0.0), axis=1, keepdims=True)


def _adaln_rmsnorm(x, gain, shift):
    ms = jnp.mean(x * x, axis=-1, keepdims=True)
    return (x * lax.rsqrt(ms + EPS)) * gain + shift


def _shift_scale_bias_kernel(tbl_ref, c_ref, w_ref, bsh_ref, bsc_ref, bucket_ref, wpool_ref,
                             shift_ref, scale_ref, bias_ref, wpool_o_ref, *, tile_buckets):
    k = pl.program_id(0)
    d = shift_ref.shape[1]

    @pl.when(k == 0)
    def _():
        shift_ref[...] = bsh_ref[...]
        scale_ref[...] = bsc_ref[...]

    s = _silu_column(c_ref[...])
    both = jnp.sum(s * w_ref[...], axis=0, keepdims=True)
    shift_ref[...] += both[:, :d]
    scale_ref[...] += both[:, d:]

    wpool_o_ref[...] = wpool_ref[...].astype(wpool_o_ref.dtype)

    for t, buckets in enumerate(tile_buckets):
        bk = bucket_ref[:, t * BLOCK:(t + 1) * BLOCK]
        acc = jnp.full((BLOCK, BLOCK), NEG_INF, jnp.float32)
        for b in buckets:
            acc = jnp.where(bk == b, tbl_ref[k, b] * INV_ATTN_SCALE, acc)
        bias_ref[0, :, t * BLOCK:(t + 1) * BLOCK] = acc


def _shift_scale_bias(c_row, w_ada, b_ada_row, rel_table, bucket, wpool):
    d = w_ada.shape[0]
    steps = d // MOD_TK
    assert steps == N_Q_HEADS, "one bias-table head per grid step"
    wpool2 = wpool.reshape(-1, wpool.shape[-1])
    pool_slab = (wpool2.shape[0] // steps, wpool2.shape[1])
    assert wpool2.shape[0] % (steps * BF16_SUBLANES) == 0
    tile_buckets = tuple(
        tuple(int(b) for b in np.unique(bucket[:, t * BLOCK:(t + 1) * BLOCK]) if b != MASKED_BUCKET)
        for t in range(SPAN // BLOCK))
    return pl.pallas_call(
        functools.partial(_shift_scale_bias_kernel, tile_buckets=tile_buckets),
        grid=(steps,),
        in_specs=[
            pl.BlockSpec(memory_space=pltpu.SMEM),
            pl.BlockSpec((1, MOD_TK), lambda k: (0, k)),
            pl.BlockSpec((MOD_TK, 2 * d), lambda k: (k, 0)),
            pl.BlockSpec((1, d), lambda k: (0, 0)),
            pl.BlockSpec((1, d), lambda k: (0, 1)),
            pl.BlockSpec(bucket.shape, lambda k: (0, 0)),
            pl.BlockSpec(pool_slab, lambda k: (k, 0)),
        ],
        out_specs=[
            pl.BlockSpec((1, d), lambda k: (0, 0)),
            pl.BlockSpec((1, d), lambda k: (0, 0)),
            pl.BlockSpec((1, BLOCK, SPAN), lambda k: (k, 0, 0)),
            pl.BlockSpec(pool_slab, lambda k: (k, 0)),
        ],
        out_shape=[
            jax.ShapeDtypeStruct((1, d), jnp.float32),
            jax.ShapeDtypeStruct((1, d), jnp.float32),
            jax.ShapeDtypeStruct((N_Q_HEADS, BLOCK, SPAN), jnp.float32),
            jax.ShapeDtypeStruct(wpool2.shape, jnp.bfloat16),
        ],
        compiler_params=pltpu.CompilerParams(
            dimension_semantics=("arbitrary",), vmem_limit_bytes=VMEM_LIMIT_SMALL_BYTES),
        name="shift_scale_bias",
    )(rel_table.T, c_row, w_ada, b_ada_row, b_ada_row, bucket, wpool2)


def _t5_bucket(rel):
    half = N_BUCKETS // 2
    max_exact = half // 2
    assert (max_exact, MAX_DISTANCE // max_exact, half - max_exact) == (8, 16, 8)
    n = abs(rel)
    large = min(max_exact + (n * n).bit_length() - 7, half - 1)
    return (half if rel > 0 else 0) + (n if n < max_exact else large)


def _bucket_index_table():
    table = np.full((BLOCK, SPAN), MASKED_BUCKET, np.int32)
    for q in range(BLOCK):
        for t in range(SPAN):
            rel = t - WINDOW - q
            if abs(rel) <= WINDOW:
                table[q, t] = _t5_bucket(rel)
    return table


def _merge_gate_kernel(h_ref, w_ref, wa_ref, wp_ref, wout_ref, c_ref, wgate_ref, bgate_ref,
                       o_ref, wa_o_ref, wp_o_ref, wout_o_ref, gate_ref,
                       wbf_ref):
    j, i = pl.program_id(0), pl.program_id(1)

    @pl.when(i == 0)
    def _():
        wbf_ref[...] = (0.5 * w_ref[...]).astype(wbf_ref.dtype)

    @pl.when((j == 0) & (i == 0))
    def _():
        gate_ref[...] = bgate_ref[...]

    for r in range(0, h_ref.shape[0], GATE_ROWS):
        o_ref[r:r + GATE_ROWS] = jnp.dot(h_ref[r:r + GATE_ROWS], wbf_ref[...],
                                         preferred_element_type=jnp.float32).astype(o_ref.dtype)

    wa_o_ref[...] = wa_ref[...].astype(wa_o_ref.dtype)
    wp_o_ref[...] = wp_ref[...].astype(wp_o_ref.dtype)
    wout_o_ref[...] = wout_ref[...].astype(wout_o_ref.dtype)
    gate_ref[...] += jnp.sum(_silu_column(c_ref[...]) * wgate_ref[...], axis=0, keepdims=True)


def _merge_gate(h, w_merge, wa, wp, wout, c_row, w_ada, b_ada_row):
    s, d = h.shape
    n = w_merge.shape[1]
    n_i = s // GATE_TM
    steps = (n // GATE_TN) * n_i
    slab = lambda w: (w.shape[0] // steps, w.shape[1])
    step = lambda j, i: (j * n_i + i, 0)
    for w in (wa, wp, wout):
        assert w.shape[0] % (steps * BF16_SUBLANES) == 0
    assert d % (steps * SUBLANES) == 0
    const2 = lambda j, i: (0, 0)
    bf16 = jnp.bfloat16
    return pl.pallas_call(
        _merge_gate_kernel,
        grid=(n // GATE_TN, n_i),
        in_specs=[
            pl.BlockSpec((GATE_TM, d), lambda j, i: (i, 0)),
            pl.BlockSpec((d, GATE_TN), lambda j, i: (0, j)),
            pl.BlockSpec(slab(wa), step),
            pl.BlockSpec(slab(wp), step),
            pl.BlockSpec(slab(wout), step),
            pl.BlockSpec((1, d // steps), lambda j, i: (0, j * n_i + i)),
            pl.BlockSpec((d // steps, d), lambda j, i: (j * n_i + i, 2)),
            pl.BlockSpec((1, d), lambda j, i: (0, 2)),
        ],
        out_specs=[
            pl.BlockSpec((GATE_TM, GATE_TN), lambda j, i: (i, j)),
            pl.BlockSpec(slab(wa), step),
            pl.BlockSpec(slab(wp), step),
            pl.BlockSpec(slab(wout), step),
            pl.BlockSpec((1, d), const2),
        ],
        out_shape=[
            jax.ShapeDtypeStruct((s, n), bf16),
            jax.ShapeDtypeStruct(wa.shape, bf16),
            jax.ShapeDtypeStruct(wp.shape, bf16),
            jax.ShapeDtypeStruct(wout.shape, bf16),
            jax.ShapeDtypeStruct((1, d), jnp.float32),
        ],
        scratch_shapes=[pltpu.VMEM((d, GATE_TN), bf16)],
        compiler_params=pltpu.CompilerParams(
            dimension_semantics=("arbitrary", "arbitrary"), vmem_limit_bytes=VMEM_LIMIT_MATMUL_BYTES),
        name="merge_gate",
    )(h, w_merge, wa, wp, wout, c_row, w_ada, b_ada_row)


def _project_mix_kernel(sink_ref,
                        x_ref, shift_ref, scale_ref, g_ref, w_hbm_ref, wpool_ref, pscale_ref, bias_ref,
                        ya_ref, yp_ref, h_ref,
                        pbuf_ref, kvtail_ref, utail_ref, uext_ref, w_ref, wstage_ref, wsem_ref,
                        kt_ref, kttail_ref,
                        *, seq_len):
    s = pl.program_id(0)
    n_tok = pl.num_programs(0) - 1
    tm, d = x_ref.shape
    n_sub = tm // BLOCK
    n_blocks = seq_len // BLOCK
    aw = ATTN_WIDTH
    pw = pscale_ref.shape[1]
    q0, ga0, gp0 = Q0, GA0, U0 + pw
    n_chunks = w_ref.shape[1] // CW
    f32 = jnp.float32
    bf16 = jnp.bfloat16
    cur = s % 2
    prv = 1 - cur
    j = s - 1

    n_slots, slab_rows = wstage_ref.shape[0], wstage_ref.shape[1]
    assert n_slots * slab_rows == d and wstage_ref.shape[2] == CW

    def slab_copy(c, r):
        return pltpu.make_async_copy(
            w_hbm_ref.at[pl.ds(r * slab_rows, slab_rows), pl.ds(c * CW, CW)],
            wstage_ref.at[r], wsem_ref.at[r])

    def fetch_chunk(c):
        for r in range(n_slots):
            slab_copy(c, r).start()

    def land_chunk(c, next_c):
        for r in range(n_slots):
            slab_copy(c, r).wait()
            w_ref[r * slab_rows:(r + 1) * slab_rows, c * CW:(c + 1) * CW] = wstage_ref[r].astype(bf16)
            if next_c is not None:
                slab_copy(next_c, r).start()

    def prologue():
        gain = g_ref[...] * (1.0 + scale_ref[...])
        h_ref[...] = _adaln_rmsnorm(x_ref[...], gain, shift_ref[...]).astype(bf16)

    def is_gate_col(col):
        return ga0 <= col < ga0 + aw or gp0 <= col < gp0 + pw

    def proj_chunk(c):
        lo, hi = c * CW, (c + 1) * CW
        p = jnp.dot(h_ref[...], w_ref[:, lo:hi], preferred_element_type=f32)
        if is_gate_col(lo):
            assert is_gate_col(hi - 1)
            p = 0.5 * p
        pbuf_ref[cur, :, lo:hi] = p.astype(bf16)
        if lo <= KV0 < hi:
            assert KV0 + KV_WIDTH <= hi
            kt_ref[cur] = p[:, KV0 - lo:KV0 - lo + KV_WIDTH].T.astype(bf16)

    def window_rows(b, col0):
        r0, r1 = (b - 1) * BLOCK, (b + 2) * BLOCK
        parts = []
        if r0 < 0:
            parts.append(kvtail_ref[prv, :, col0 - KV0:col0 - KV0 + HEAD_DIM])
            r0 = 0
        parts.append(pbuf_ref[prv, r0:min(r1, tm), col0:col0 + HEAD_DIM])
        if r1 > tm:
            parts.append(pbuf_ref[cur, 0:r1 - tm, col0:col0 + HEAD_DIM])
        return jnp.concatenate(parts, axis=0) if len(parts) > 1 else parts[0]

    def key_window(b, kvh):
        hd = slice(kvh * HEAD_DIM, (kvh + 1) * HEAD_DIM)
        r0, r1 = (b - 1) * BLOCK, (b + 2) * BLOCK
        parts = []
        if r0 < 0:
            parts.append(kttail_ref[prv, hd, :])
            r0 = 0
        parts.append(kt_ref[prv, hd, r0:min(r1, tm)])
        if r1 > tm:
            parts.append(kt_ref[cur, hd, 0:r1 - tm])
        return jnp.concatenate(parts, axis=1) if len(parts) > 1 else parts[0]

    def attn_scores(b, kvh):
        blk = j * n_sub + b
        h0 = kvh * GQA_GROUP
        kt = key_window(b, kvh)
        v = window_rows(b, KV0 + KV_WIDTH + kvh * HEAD_DIM)
        qs = jnp.concatenate(
            [pbuf_ref[prv, b * BLOCK:(b + 1) * BLOCK,
                      q0 + (h0 + g) * HEAD_DIM:q0 + (h0 + g + 1) * HEAD_DIM]
             for g in range(GQA_GROUP)], axis=0)
        z = jnp.dot(qs, kt, preferred_element_type=f32)
        z = z + bias_ref[h0:h0 + GQA_GROUP].reshape(GQA_GROUP * BLOCK, SPAN)
        col = lax.broadcasted_iota(jnp.int32, (1, SPAN), 1)
        if b == 0:
            z = jnp.where(col < jnp.where(blk == 0, WINDOW, 0), NEG_INF, z)
        if b == n_sub - 1:
            z = jnp.where(col >= jnp.where(blk == n_blocks - 1, WINDOW + BLOCK, SPAN), NEG_INF, z)
        sink = jnp.concatenate(
            [jnp.full((BLOCK, 1), sink_ref[h0 + g] * INV_ATTN_SCALE, f32)
             for g in range(GQA_GROUP)], axis=0)
        m = jnp.maximum(jnp.max(z, axis=-1, keepdims=True), sink)
        p = jnp.exp2((z - m) * EXP2_SCALE)
        denom = jnp.sum(p, axis=-1, keepdims=True) + jnp.exp2((sink - m) * EXP2_SCALE)
        return p.astype(bf16), denom, v

    def attn_out(p, denom, v):
        o = jnp.dot(p, v, preferred_element_type=f32) / denom
        return [o[g * BLOCK:(g + 1) * BLOCK] for g in range(GQA_GROUP)]

    def finish_sub_block(b, heads):
        y = jnp.concatenate(heads, axis=1)
        hg = pbuf_ref[prv, b * BLOCK:(b + 1) * BLOCK, ga0:ga0 + aw].astype(f32)
        ya_ref[b * BLOCK:(b + 1) * BLOCK] = (y * _half_silu(hg)).astype(bf16)

    ext = tm + 2 * POOL_HALO

    def pool_group(gi):
        w = POOL_SIZES[gi]
        gw = pw // N_POOL_GROUPS
        c0, c1 = gi * gw, (gi + 1) * gw
        half = w // 2
        halo_row = lax.broadcasted_iota(jnp.int32, (POOL_HALO, 1), 0)
        uext_ref[0:POOL_HALO] = jnp.where(j * tm - POOL_HALO + halo_row >= 0,
                                          utail_ref[prv, :, c0:c1].astype(f32), 0.0)
        uext_ref[POOL_HALO:POOL_HALO + tm] = pbuf_ref[prv, :, U0 + c0:U0 + c1].astype(f32)
        uext_ref[POOL_HALO + tm:ext] = jnp.where(
            (j + 1) * tm + halo_row < seq_len,
            pbuf_ref[cur, 0:POOL_HALO, U0 + c0:U0 + c1].astype(f32), 0.0)
        a = uext_ref[...]
        sh = 1
        while sh < w:
            a = a + pltpu.roll(a, ext - sh, axis=0)
            sh *= 2
        off = POOL_HALO - half
        if off % SUBLANES:
            a = pltpu.roll(a, ext - off, axis=0)
            off = 0
        pos = j * tm + lax.broadcasted_iota(jnp.int32, (tm, 1), 0)
        cnt = (jnp.minimum(pos + half, seq_len) - jnp.maximum(pos - half, 0)).astype(f32)
        pooled = a[off:off + tm] / cnt - uext_ref[POOL_HALO:POOL_HALO + tm]
        mixed = jnp.dot(pooled.astype(bf16), wpool_ref[gi], preferred_element_type=f32)
        hg = pbuf_ref[prv, :, gp0 + c0:gp0 + c1].astype(f32)
        yp_ref[:, c0:c1] = (mixed * pscale_ref[:, c0:c1] * _half_silu(hg)).astype(bf16)

    def save_tails():
        kvtail_ref[cur] = pbuf_ref[prv, tm - BLOCK:tm, KV0:KV0 + 2 * KV_WIDTH]
        utail_ref[cur] = pbuf_ref[prv, tm - POOL_HALO:tm, U0:U0 + pw]
        kttail_ref[cur] = kt_ref[prv, :, tm - BLOCK:tm]

    units = [(b, kvh) for b in range(n_sub) for kvh in range(N_KV_HEADS)]
    assert KV0 % CW == 0 and (2 * KV_WIDTH) % CW == 0 and U0 % CW == 0 and pw % CW == 0
    first = list(range(KV0 // CW, (KV0 + 2 * KV_WIDTH) // CW)) + list(range(U0 // CW, (U0 + pw) // CW))
    order = first + [c for c in range(n_chunks) if c not in first]
    u_chunks_done = len(first)
    assert len(units) + 1 <= n_chunks and u_chunks_done + N_POOL_GROUPS <= n_chunks

    def mix_only():
        pending = [attn_scores(b, kvh) for b, kvh in units]
        heads = []
        for (b, kvh), scores in zip(units, pending):
            heads += attn_out(*scores)
            if kvh == N_KV_HEADS - 1:
                finish_sub_block(b, heads)
                heads = []
        for gi in range(N_POOL_GROUPS):
            pool_group(gi)
        save_tails()

    def project_first():
        fetch_chunk(order[0])
        prologue()
        for pos, c in enumerate(order):
            land_chunk(c, order[pos + 1] if pos + 1 < len(order) else None)
            proj_chunk(c)

    def project_and_mix():
        prologue()
        pending = attn_scores(*units[0])
        heads = []
        for pos, c in enumerate(order):
            proj_chunk(c)
            if pos < len(units):
                heads += attn_out(*pending)
                if pos + 1 < len(units):
                    pending = attn_scores(*units[pos + 1])
                b, kvh = units[pos]
                if kvh == N_KV_HEADS - 1:
                    finish_sub_block(b, heads)
                    heads = []
            if u_chunks_done <= pos < u_chunks_done + N_POOL_GROUPS:
                pool_group(pos - u_chunks_done)
        save_tails()

    @pl.when(s == 0)
    def _():
        kvtail_ref[0] = jnp.zeros(kvtail_ref.shape[1:], kvtail_ref.dtype)
        utail_ref[0] = jnp.zeros(utail_ref.shape[1:], utail_ref.dtype)
        kttail_ref[0] = jnp.zeros(kttail_ref.shape[1:], kttail_ref.dtype)
        project_first()

    @pl.when((s > 0) & (s < n_tok))
    def _():
        project_and_mix()

    @pl.when(s == n_tok)
    def _():
        mix_only()


def _project_mix(x2, shift, scale, pre_g_row, w_in, wpool, pool_scale_row, bias_tbl, sink):
    s, d = x2.shape
    tm = TM
    n_tok = s // tm
    pw = pool_scale_row.shape[1]
    n_in = w_in.shape[1]
    assert s // BLOCK >= 2, "first and last attention block must differ"
    assert n_in == U0 + 2 * pw and d % (W_STAGE_SLOTS * BF16_SUBLANES) == 0
    cur_blk = lambda st: jnp.minimum(st, n_tok - 1)
    mix_blk = lambda st: jnp.maximum(st - 1, 0)
    const2 = lambda st: (0, 0)
    single = pl.Buffered(1)
    bf16 = jnp.bfloat16
    return pl.pallas_call(
        functools.partial(_project_mix_kernel, seq_len=s),
        grid=(n_tok + 1,),
        in_specs=[
            pl.BlockSpec(memory_space=pltpu.SMEM),
            pl.BlockSpec((tm, d), lambda st: (cur_blk(st), 0)),
            pl.BlockSpec((1, d), const2),
            pl.BlockSpec((1, d), const2),
            pl.BlockSpec((1, d), const2),
            pl.BlockSpec(memory_space=pl.ANY),
            pl.BlockSpec(wpool.shape, lambda st: (0, 0, 0), pipeline_mode=single),
            pl.BlockSpec((1, pw), const2),
            pl.BlockSpec(bias_tbl.shape, lambda st: (0, 0, 0), pipeline_mode=single),
        ],
        out_specs=[
            pl.BlockSpec((tm, ATTN_WIDTH), lambda st: (mix_blk(st), 0)),
            pl.BlockSpec((tm, pw), lambda st: (mix_blk(st), 0)),
            pl.BlockSpec((tm, d), lambda st: (cur_blk(st), 0)),
        ],
        out_shape=[
            jax.ShapeDtypeStruct((s, ATTN_WIDTH), bf16),
            jax.ShapeDtypeStruct((s, pw), bf16),
            jax.ShapeDtypeStruct((s, d), bf16),
        ],
        scratch_shapes=[
            pltpu.VMEM((2, tm, n_in), bf16),
            pltpu.VMEM((2, BLOCK, 2 * KV_WIDTH), bf16),
            pltpu.VMEM((2, POOL_HALO, pw), bf16),
            pltpu.VMEM((tm + 2 * POOL_HALO, pw // N_POOL_GROUPS), jnp.float32),
            pltpu.VMEM(w_in.shape, bf16),
            pltpu.VMEM((W_STAGE_SLOTS, d // W_STAGE_SLOTS, CW), jnp.float32),
            pltpu.SemaphoreType.DMA((W_STAGE_SLOTS,)),
            pltpu.VMEM((2, KV_WIDTH, tm), bf16),
            pltpu.VMEM((2, KV_WIDTH, BLOCK), bf16),
        ],
        compiler_params=pltpu.CompilerParams(
            dimension_semantics=("arbitrary",), vmem_limit_bytes=VMEM_LIMIT_BYTES),
        name="project_mix",
    )(sink, x2, shift, scale, pre_g_row, w_in, wpool, pool_scale_row, bias_tbl)


def _merge_out_kernel(ya_ref, yp_ref, gm_ref, x_ref, gate_ref, postg_ref, bm_ref,
                      wa_ref, wp_ref, wout_ref, o_ref, merged_ref, oacc_ref, inv_ref):
    s = pl.program_id(0)
    n_tok = pl.num_programs(0) - 1
    tm, d = x_ref.shape
    n_chunks = d // CW
    f32 = jnp.float32

    def finish_previous():
        scale = gate_ref[...] * postg_ref[...]
        inv = inv_ref[...]
        for c in range(n_chunks):
            lo, hi = c * CW, (c + 1) * CW
            o_ref[:, lo:hi] = x_ref[:, lo:hi] + (oacc_ref[:, lo:hi] * inv) * scale[:, lo:hi]

    def matmuls():
        half_bm = 0.5 * bm_ref[...]
        for c in range(n_chunks):
            lo, hi = c * CW, (c + 1) * CW
            bra = jnp.dot(ya_ref[...], wa_ref[:, lo:hi], preferred_element_type=f32)
            brp = jnp.dot(yp_ref[...], wp_ref[:, lo:hi], preferred_element_type=f32)
            g_a = 0.5 * jnp.tanh(gm_ref[:, lo:hi].astype(f32) + half_bm[:, lo:hi]) + 0.5
            g_p = 0.5 * jnp.tanh(gm_ref[:, d + lo:d + hi].astype(f32)
                                 + half_bm[:, d + lo:d + hi]) + 0.5
            merged_ref[:, lo:hi] = (g_a * bra + g_p * brp).astype(merged_ref.dtype)
        ssq = jnp.zeros((tm, 1), f32)
        for c in range(n_chunks):
            lo, hi = c * CW, (c + 1) * CW
            o = jnp.dot(merged_ref[...], wout_ref[:, lo:hi], preferred_element_type=f32)
            ssq = ssq + jnp.sum(o * o, axis=-1, keepdims=True)
            oacc_ref[:, lo:hi] = o
        inv_ref[...] = lax.rsqrt(ssq * (1.0 / d) + EPS)

    @pl.when(s == 0)
    def _():
        oacc_ref[...] = jnp.zeros(oacc_ref.shape, oacc_ref.dtype)
        inv_ref[...] = jnp.zeros(inv_ref.shape, inv_ref.dtype)

    @pl.when(s < n_tok)
    def _():
        finish_previous()
        matmuls()

    @pl.when(s == n_tok)
    def _():
        finish_previous()


def _merge_out(ya, yp, gm, x2, gate, post_g_row, b_merge_row, wa, wp, wout):
    s, d = x2.shape
    tm = TM
    n_tok = s // tm
    aw, pw = ya.shape[1], yp.shape[1]
    cur = lambda st: (jnp.minimum(st, n_tok - 1), 0)
    prev = lambda st: (jnp.maximum(st - 1, 0), 0)
    const2 = lambda st: (0, 0)
    single = pl.Buffered(1)
    return pl.pallas_call(
        _merge_out_kernel,
        grid=(n_tok + 1,),
        in_specs=[
            pl.BlockSpec((tm, aw), cur),
            pl.BlockSpec((tm, pw), cur),
            pl.BlockSpec((tm, 2 * d), cur),
            pl.BlockSpec((tm, d), prev),
            pl.BlockSpec((1, d), const2),
            pl.BlockSpec((1, d), const2),
            pl.BlockSpec((1, 2 * d), const2),
            pl.BlockSpec((aw, d), const2, pipeline_mode=single),
            pl.BlockSpec((pw, d), const2, pipeline_mode=single),
            pl.BlockSpec((d, d), const2, pipeline_mode=single),
        ],
        out_specs=pl.BlockSpec((tm, d), prev),
        out_shape=jax.ShapeDtypeStruct((s, d), jnp.float32),
        scratch_shapes=[
            pltpu.VMEM((tm, d), jnp.bfloat16),
            pltpu.VMEM((tm, d), jnp.float32),
            pltpu.VMEM((tm, 1), jnp.float32),
        ],
        compiler_params=pltpu.CompilerParams(
            dimension_semantics=("arbitrary",), vmem_limit_bytes=VMEM_LIMIT_MATMUL_BYTES),
        name="merge_out",
    )(ya, yp, gm, x2, gate, post_g_row, b_merge_row, wa, wp, wout)


def _layer(x2, c, rel_table, bucket, w_ada, b_ada, pre_g, post_g, w_in, sink, w_pool, pool_scale,
           w_br_attn, w_br_pool, w_merge, b_merge, w_out):
    s, d = x2.shape
    c_row = c.reshape(1, d)
    b_ada_row = b_ada.reshape(1, -1)
    pre_g_row = pre_g.reshape(1, d)
    shift, scale, bias_tbl, w_pool_bf = _shift_scale_bias(c_row, w_ada, b_ada_row, rel_table, bucket,
                                                          w_pool)
    ya, yp, h = _project_mix(x2, shift, scale, pre_g_row, w_in, w_pool_bf.reshape(w_pool.shape),
                             pool_scale.reshape(1, -1), bias_tbl, sink)
    gm, wa, wp, wout, gate = _merge_gate(h, w_merge, w_br_attn, w_br_pool, w_out,
                                         c_row, w_ada, b_ada_row)
    return _merge_out(ya, yp, gm, x2, gate, post_g.reshape(1, d), b_merge.reshape(1, -1),
                      wa, wp, wout)


def kernel(x, c, rel_bias_table, w_ada, b_ada, pre_norm_g, post_norm_g, w_in, attn_sink,
           w_pool_group, pool_scale, w_branch_attn, w_branch_pool, w_merge, b_merge, w_out):
    batch, s, d = x.shape
    assert batch == 1, "kernel is written for a single sequence"
    depth = w_ada.shape[0]
    bucket = _bucket_index_table()
    x2 = x.reshape(s, d)
    for l in range(depth):
        x2 = _layer(x2, c[0], rel_bias_table, bucket, w_ada[l], b_ada[l], pre_norm_g[l],
                    post_norm_g[l], w_in[l], attn_sink[l], w_pool_group[l], pool_scale[l],
                    w_branch_attn[l], w_branch_pool[l], w_merge[l], b_merge[l], w_out[l])
    return x2.reshape(batch, s, d)
```

```python
import functools
import math

import numpy as np
import jax
import jax.numpy as jnp
from jax import lax
from jax.experimental import pallas as pl
from jax.experimental.pallas import tpu as pltpu

HEAD_DIM = 128
N_Q_HEADS = 8
N_KV_HEADS = 2
GQA_GROUP = N_Q_HEADS // N_KV_HEADS
ATTN_WIDTH = N_Q_HEADS * HEAD_DIM
KV_WIDTH = N_KV_HEADS * HEAD_DIM
WINDOW = 128
BLOCK = 128
SPAN = BLOCK + 2 * WINDOW
N_BUCKETS = 32
MAX_DISTANCE = 128
POOL_SIZES = (2, 4, 8, 16)
N_POOL_GROUPS = len(POOL_SIZES)
EPS = 1e-6
NEG_INF = -1e30
MASKED_BUCKET = N_BUCKETS
ATTN_SCALE = HEAD_DIM ** -0.5
INV_ATTN_SCALE = HEAD_DIM ** 0.5
EXP2_SCALE = ATTN_SCALE * math.log2(math.e)

SUBLANES = 8
BF16_SUBLANES = 16
VMEM_LIMIT_BYTES = 61 * 1024 * 1024

MOD_TK = 256
TM = 512
CW = 512
GATE_TM = 2048
GATE_ROWS = 1024
GATE_TN = 1024
POOL_HALO = BF16_SUBLANES
W_STAGE_SLOTS = 8

Q0 = 0
KV0 = Q0 + ATTN_WIDTH
GA0 = KV0 + 2 * KV_WIDTH
U0 = GA0 + ATTN_WIDTH


def _silu(v):
    return v * (1.0 / (1.0 + jnp.exp(-v)))


def _half_silu(hv):
    return hv + hv * jnp.tanh(hv)


def _silu_column(c_row):
    n = c_row.shape[1]
    on_diag = (lax.broadcasted_iota(jnp.int32, (n, n), 0)
               == lax.broadcasted_iota(jnp.int32, (n, n), 1))
    return jnp.sum(jnp.where(on_diag, _silu(c_row), 0.0), axis=1, keepdims=True)


def _adaln_rmsnorm(x, gain, shift):
    ms = jnp.mean(x * x, axis=-1, keepdims=True)
    return (x * lax.rsqrt(ms + EPS)) * gain + shift


def _shift_scale_bias_kernel(tbl_ref, c_ref, w_ref, bsh_ref, bsc_ref, bucket_ref, wpool_ref,
                             shift_ref, scale_ref, bias_ref, wpool_o_ref, *, tile_buckets):
    k = pl.program_id(0)
    d = shift_ref.shape[1]

    @pl.when(k == 0)
    def _():
        shift_ref[...] = bsh_ref[...]
        scale_ref[...] = bsc_ref[...]

    s = _silu_column(c_ref[pl.ds(k, 1), :])
    both = jnp.sum(s * w_ref[...], axis=0, keepdims=True)
    shift_ref[...] += both[:, :d]
    scale_ref[...] += both[:, d:]

    wpool_o_ref[...] = wpool_ref[...].astype(wpool_o_ref.dtype)

    for t, buckets in enumerate(tile_buckets):
        bk = bucket_ref[:, t * BLOCK:(t + 1) * BLOCK]
        acc = jnp.full((BLOCK, BLOCK), NEG_INF, jnp.float32)
        for b in buckets:
            acc = jnp.where(bk == b, tbl_ref[k, b] * INV_ATTN_SCALE, acc)
        bias_ref[0, :, t * BLOCK:(t + 1) * BLOCK] = acc


def _shift_scale_bias(c_row, w_ada, b_ada_row, rel_table, bucket, wpool):
    d = w_ada.shape[0]
    steps = d // MOD_TK
    assert steps == N_Q_HEADS, "one bias-table head per grid step"
    wpool2 = wpool.reshape(-1, wpool.shape[-1])
    pool_slab = (wpool2.shape[0] // steps, wpool2.shape[1])
    assert wpool2.shape[0] % (steps * BF16_SUBLANES) == 0
    tile_buckets = tuple(
        tuple(int(b) for b in np.unique(bucket[:, t * BLOCK:(t + 1) * BLOCK]) if b != MASKED_BUCKET)
        for t in range(SPAN // BLOCK))
    return pl.pallas_call(
        functools.partial(_shift_scale_bias_kernel, tile_buckets=tile_buckets),
        grid=(steps,),
        in_specs=[
            pl.BlockSpec(memory_space=pltpu.SMEM),
            pl.BlockSpec((steps, MOD_TK), lambda k: (0, 0)),
            pl.BlockSpec((MOD_TK, 2 * d), lambda k: (k, 0)),
            pl.BlockSpec((1, d), lambda k: (0, 0)),
            pl.BlockSpec((1, d), lambda k: (0, 1)),
            pl.BlockSpec(bucket.shape, lambda k: (0, 0)),
            pl.BlockSpec(pool_slab, lambda k: (k, 0)),
        ],
        out_specs=[
            pl.BlockSpec((1, d), lambda k: (0, 0)),
            pl.BlockSpec((1, d), lambda k: (0, 0)),
            pl.BlockSpec((1, BLOCK, SPAN), lambda k: (k, 0, 0)),
            pl.BlockSpec(pool_slab, lambda k: (k, 0)),
        ],
        out_shape=[
            jax.ShapeDtypeStruct((1, d), jnp.float32),
            jax.ShapeDtypeStruct((1, d), jnp.float32),
            jax.ShapeDtypeStruct((N_Q_HEADS, BLOCK, SPAN), jnp.float32),
            jax.ShapeDtypeStruct(wpool2.shape, jnp.bfloat16),
        ],
        compiler_params=pltpu.CompilerParams(
            dimension_semantics=("arbitrary",), vmem_limit_bytes=VMEM_LIMIT_BYTES),
        name="shift_scale_bias",
    )(rel_table.T, c_row.reshape(steps, MOD_TK), w_ada, b_ada_row, b_ada_row, bucket, wpool2)


def _t5_bucket(rel):
    half = N_BUCKETS // 2
    max_exact = half // 2
    assert (max_exact, MAX_DISTANCE // max_exact, half - max_exact) == (8, 16, 8)
    n = abs(rel)
    large = min(max_exact + (n * n).bit_length() - 7, half - 1)
    return (half if rel > 0 else 0) + (n if n < max_exact else large)


def _bucket_index_table():
    table = np.full((BLOCK, SPAN), MASKED_BUCKET, np.int32)
    for q in range(BLOCK):
        for t in range(SPAN):
            rel = t - WINDOW - q
            if abs(rel) <= WINDOW:
                table[q, t] = _t5_bucket(rel)
    return table


def _merge_gate_kernel(h_ref, w_ref, wa_ref, wp_ref, wout_ref, c_ref, wgate_ref, bgate_ref,
                       o_ref, wa_o_ref, wp_o_ref, wout_o_ref, gate_ref,
                       wbf_ref):
    j, i = pl.program_id(0), pl.program_id(1)

    @pl.when(i == 0)
    def _():
        wbf_ref[...] = (0.5 * w_ref[...]).astype(wbf_ref.dtype)

    @pl.when((j == 0) & (i == 0))
    def _():
        gate_ref[...] = bgate_ref[...]

    for r in range(0, h_ref.shape[0], GATE_ROWS):
        o_ref[r:r + GATE_ROWS] = jnp.dot(h_ref[r:r + GATE_ROWS], wbf_ref[...],
                                         preferred_element_type=jnp.float32).astype(o_ref.dtype)

    wa_o_ref[...] = wa_ref[...].astype(wa_o_ref.dtype)
    wp_o_ref[...] = wp_ref[...].astype(wp_o_ref.dtype)
    wout_o_ref[...] = wout_ref[...].astype(wout_o_ref.dtype)
    c_slab = c_ref[pl.ds(j * pl.num_programs(1) + i, 1), :]
    gate_ref[...] += jnp.sum(_silu_column(c_slab) * wgate_ref[...], axis=0, keepdims=True)


def _merge_gate(h, w_merge, wa, wp, wout, c_row, w_ada, b_ada_row):
    s, d = h.shape
    n = w_merge.shape[1]
    n_i = s // GATE_TM
    steps = (n // GATE_TN) * n_i
    slab = lambda w: (w.shape[0] // steps, w.shape[1])
    step = lambda j, i: (j * n_i + i, 0)
    for w in (wa, wp, wout):
        assert w.shape[0] % (steps * BF16_SUBLANES) == 0
    assert d % (steps * SUBLANES) == 0
    const2 = lambda j, i: (0, 0)
    bf16 = jnp.bfloat16
    return pl.pallas_call(
        _merge_gate_kernel,
        grid=(n // GATE_TN, n_i),
        in_specs=[
            pl.BlockSpec((GATE_TM, d), lambda j, i: (i, 0)),
            pl.BlockSpec((d, GATE_TN), lambda j, i: (0, j)),
            pl.BlockSpec(slab(wa), step),
            pl.BlockSpec(slab(wp), step),
            pl.BlockSpec(slab(wout), step),
            pl.BlockSpec((steps, d // steps), const2),
            pl.BlockSpec((d // steps, d), lambda j, i: (j * n_i + i, 2)),
            pl.BlockSpec((1, d), lambda j, i: (0, 2)),
        ],
        out_specs=[
            pl.BlockSpec((GATE_TM, GATE_TN), lambda j, i: (i, j)),
            pl.BlockSpec(slab(wa), step),
            pl.BlockSpec(slab(wp), step),
            pl.BlockSpec(slab(wout), step),
            pl.BlockSpec((1, d), const2),
        ],
        out_shape=[
            jax.ShapeDtypeStruct((s, n), bf16),
            jax.ShapeDtypeStruct(wa.shape, bf16),
            jax.ShapeDtypeStruct(wp.shape, bf16),
            jax.ShapeDtypeStruct(wout.shape, bf16),
            jax.ShapeDtypeStruct((1, d), jnp.float32),
        ],
        scratch_shapes=[pltpu.VMEM((d, GATE_TN), bf16)],
        compiler_params=pltpu.CompilerParams(
            dimension_semantics=("arbitrary", "arbitrary"), vmem_limit_bytes=VMEM_LIMIT_BYTES),
        name="merge_gate",
    )(h, w_merge, wa, wp, wout, c_row.reshape(steps, d // steps), w_ada, b_ada_row)


def _project_mix_kernel(sink_ref,
                        x_ref, shift_ref, scale_ref, g_ref, w_hbm_ref, wpool_ref, pscale_ref, bias_ref,
                        ya_ref, yp_ref, h_ref,
                        pbuf_ref, kvtail_ref, utail_ref, uext_ref, w_ref, wstage_ref, wsem_ref,
                        kt_ref, kttail_ref,
                        *, seq_len):
    s = pl.program_id(0)
    n_tok = pl.num_programs(0) - 1
    tm, d = x_ref.shape
    n_sub = tm // BLOCK
    n_blocks = seq_len // BLOCK
    aw = ATTN_WIDTH
    pw = pscale_ref.shape[1]
    q0, ga0, gp0 = Q0, GA0, U0 + pw
    n_chunks = w_ref.shape[1] // CW
    f32 = jnp.float32
    bf16 = jnp.bfloat16
    cur = s % 2
    prv = 1 - cur
    j = s - 1

    n_slots, slab_rows = wstage_ref.shape[0], wstage_ref.shape[1]
    assert n_slots * slab_rows == d and wstage_ref.shape[2] == CW

    def slab_copy(c, r):
        return pltpu.make_async_copy(
            w_hbm_ref.at[pl.ds(r * slab_rows, slab_rows), pl.ds(c * CW, CW)],
            wstage_ref.at[r], wsem_ref.at[r])

    def fetch_chunk(c):
        for r in range(n_slots):
            slab_copy(c, r).start()

    def land_chunk(c, next_c):
        for r in range(n_slots):
            slab_copy(c, r).wait()
            w_ref[r * slab_rows:(r + 1) * slab_rows, c * CW:(c + 1) * CW] = wstage_ref[r].astype(bf16)
            if next_c is not None:
                slab_copy(next_c, r).start()

    def prologue():
        gain = g_ref[...] * (1.0 + scale_ref[...])
        h_ref[...] = _adaln_rmsnorm(x_ref[...], gain, shift_ref[...]).astype(bf16)

    def is_gate_col(col):
        return ga0 <= col < ga0 + aw or gp0 <= col < gp0 + pw

    def proj_chunk(c):
        lo, hi = c * CW, (c + 1) * CW
        p = jnp.dot(h_ref[...], w_ref[:, lo:hi], preferred_element_type=f32)
        if is_gate_col(lo):
            assert is_gate_col(hi - 1)
            p = 0.5 * p
        pbuf_ref[cur, :, lo:hi] = p.astype(bf16)
        if lo <= KV0 < hi:
            assert KV0 + KV_WIDTH <= hi
            kt_ref[cur] = p[:, KV0 - lo:KV0 - lo + KV_WIDTH].T.astype(bf16)

    def window_rows(b, col0):
        r0, r1 = (b - 1) * BLOCK, (b + 2) * BLOCK
        parts = []
        if r0 < 0:
            parts.append(kvtail_ref[prv, :, col0 - KV0:col0 - KV0 + HEAD_DIM])
            r0 = 0
        parts.append(pbuf_ref[prv, r0:min(r1, tm), col0:col0 + HEAD_DIM])
        if r1 > tm:
            parts.append(pbuf_ref[cur, 0:r1 - tm, col0:col0 + HEAD_DIM])
        return jnp.concatenate(parts, axis=0) if len(parts) > 1 else parts[0]

    def key_window(b, kvh):
        hd = slice(kvh * HEAD_DIM, (kvh + 1) * HEAD_DIM)
        r0, r1 = (b - 1) * BLOCK, (b + 2) * BLOCK
        parts = []
        if r0 < 0:
            parts.append(kttail_ref[prv, hd, :])
            r0 = 0
        parts.append(kt_ref[prv, hd, r0:min(r1, tm)])
        if r1 > tm:
            parts.append(kt_ref[cur, hd, 0:r1 - tm])
        return jnp.concatenate(parts, axis=1) if len(parts) > 1 else parts[0]

    def attn_scores(b, kvh):
        blk = j * n_sub + b
        h0 = kvh * GQA_GROUP
        kt = key_window(b, kvh)
        v = window_rows(b, KV0 + KV_WIDTH + kvh * HEAD_DIM)
        qs = jnp.concatenate(
            [pbuf_ref[prv, b * BLOCK:(b + 1) * BLOCK,
                      q0 + (h0 + g) * HEAD_DIM:q0 + (h0 + g + 1) * HEAD_DIM]
             for g in range(GQA_GROUP)], axis=0)
        z = jnp.dot(qs, kt, preferred_element_type=f32)
        z = z + bias_ref[h0:h0 + GQA_GROUP].reshape(GQA_GROUP * BLOCK, SPAN)
        col = lax.broadcasted_iota(jnp.int32, (1, SPAN), 1)
        if b == 0:
            z = jnp.where(col < jnp.where(blk == 0, WINDOW, 0), NEG_INF, z)
        if b == n_sub - 1:
            z = jnp.where(col >= jnp.where(blk == n_blocks - 1, WINDOW + BLOCK, SPAN), NEG_INF, z)
        sink = jnp.concatenate(
            [jnp.full((BLOCK, 1), sink_ref[h0 + g] * INV_ATTN_SCALE, f32)
             for g in range(GQA_GROUP)], axis=0)
        m = jnp.maximum(jnp.max(z, axis=-1, keepdims=True), sink)
        p = jnp.exp2((z - m) * EXP2_SCALE)
        denom = jnp.sum(p, axis=-1, keepdims=True) + jnp.exp2((sink - m) * EXP2_SCALE)
        return p.astype(bf16), denom, v

    def attn_out(p, denom, v):
        o = jnp.dot(p, v, preferred_element_type=f32) / denom
        return [o[g * BLOCK:(g + 1) * BLOCK] for g in range(GQA_GROUP)]

    def finish_sub_block(b, heads):
        y = jnp.concatenate(heads, axis=1)
        hg = pbuf_ref[prv, b * BLOCK:(b + 1) * BLOCK, ga0:ga0 + aw].astype(f32)
        ya_ref[b * BLOCK:(b + 1) * BLOCK] = (y * _half_silu(hg)).astype(bf16)

    ext = tm + 2 * POOL_HALO

    def pool_group(gi):
        w = POOL_SIZES[gi]
        gw = pw // N_POOL_GROUPS
        c0, c1 = gi * gw, (gi + 1) * gw
        half = w // 2
        halo_row = lax.broadcasted_iota(jnp.int32, (POOL_HALO, 1), 0)
        uext_ref[0:POOL_HALO] = jnp.where(j * tm - POOL_HALO + halo_row >= 0,
                                          utail_ref[prv, :, c0:c1].astype(f32), 0.0)
        uext_ref[POOL_HALO:POOL_HALO + tm] = pbuf_ref[prv, :, U0 + c0:U0 + c1].astype(f32)
        uext_ref[POOL_HALO + tm:ext] = jnp.where(
            (j + 1) * tm + halo_row < seq_len,
            pbuf_ref[cur, 0:POOL_HALO, U0 + c0:U0 + c1].astype(f32), 0.0)
        a = uext_ref[...]
        sh = 1
        while sh < w:
            a = a + pltpu.roll(a, ext - sh, axis=0)
            sh *= 2
        off = POOL_HALO - half
        if off % SUBLANES:
            a = pltpu.roll(a, ext - off, axis=0)
            off = 0
        pos = j * tm + lax.broadcasted_iota(jnp.int32, (tm, 1), 0)
        cnt = (jnp.minimum(pos + half, seq_len) - jnp.maximum(pos - half, 0)).astype(f32)
        pooled = a[off:off + tm] / cnt - uext_ref[POOL_HALO:POOL_HALO + tm]
        mixed = jnp.dot(pooled.astype(bf16), wpool_ref[gi], preferred_element_type=f32)
        hg = pbuf_ref[prv, :, gp0 + c0:gp0 + c1].astype(f32)
        yp_ref[:, c0:c1] = (mixed * pscale_ref[:, c0:c1] * _half_silu(hg)).astype(bf16)

    def save_tails():
        kvtail_ref[cur] = pbuf_ref[prv, tm - BLOCK:tm, KV0:KV0 + 2 * KV_WIDTH]
        utail_ref[cur] = pbuf_ref[prv, tm - POOL_HALO:tm, U0:U0 + pw]
        kttail_ref[cur] = kt_ref[prv, :, tm - BLOCK:tm]

    units = [(b, kvh) for b in range(n_sub) for kvh in range(N_KV_HEADS)]
    assert KV0 % CW == 0 and (2 * KV_WIDTH) % CW == 0 and U0 % CW == 0 and pw % CW == 0
    first = list(range(KV0 // CW, (KV0 + 2 * KV_WIDTH) // CW)) + list(range(U0 // CW, (U0 + pw) // CW))
    order = first + [c for c in range(n_chunks) if c not in first]
    u_chunks_done = len(first)
    assert len(units) + 1 <= n_chunks and u_chunks_done + N_POOL_GROUPS <= n_chunks

    def mix_only():
        pending = [attn_scores(b, kvh) for b, kvh in units]
        heads = []
        for (b, kvh), scores in zip(units, pending):
            heads += attn_out(*scores)
            if kvh == N_KV_HEADS - 1:
                finish_sub_block(b, heads)
                heads = []
        for gi in range(N_POOL_GROUPS):
            pool_group(gi)
        save_tails()

    def project_first():
        fetch_chunk(order[0])
        prologue()
        for pos, c in enumerate(order):
            land_chunk(c, order[pos + 1] if pos + 1 < len(order) else None)
            proj_chunk(c)

    def project_and_mix():
        prologue()
        pending = attn_scores(*units[0])
        heads = []
        for pos, c in enumerate(order):
            proj_chunk(c)
            if pos < len(units):
                heads += attn_out(*pending)
                if pos + 1 < len(units):
                    pending = attn_scores(*units[pos + 1])
                b, kvh = units[pos]
                if kvh == N_KV_HEADS - 1:
                    finish_sub_block(b, heads)
                    heads = []
            if u_chunks_done <= pos < u_chunks_done + N_POOL_GROUPS:
                pool_group(pos - u_chunks_done)
        save_tails()

    @pl.when(s == 0)
    def _():
        kvtail_ref[0] = jnp.zeros(kvtail_ref.shape[1:], kvtail_ref.dtype)
        utail_ref[0] = jnp.zeros(utail_ref.shape[1:], utail_ref.dtype)
        kttail_ref[0] = jnp.zeros(kttail_ref.shape[1:], kttail_ref.dtype)
        project_first()

    @pl.when((s > 0) & (s < n_tok))
    def _():
        project_and_mix()

    @pl.when(s == n_tok)
    def _():
        mix_only()


def _project_mix(x2, shift, scale, pre_g_row, w_in, wpool, pool_scale_row, bias_tbl, sink):
    s, d = x2.shape
    tm = TM
    n_tok = s // tm
    pw = pool_scale_row.shape[1]
    n_in = w_in.shape[1]
    assert s // BLOCK >= 2, "first and last attention block must differ"
    assert n_in == U0 + 2 * pw and d % (W_STAGE_SLOTS * BF16_SUBLANES) == 0
    cur_blk = lambda st: jnp.minimum(st, n_tok - 1)
    mix_blk = lambda st: jnp.maximum(st - 1, 0)
    const2 = lambda st: (0, 0)
    single = pl.Buffered(1)
    bf16 = jnp.bfloat16
    return pl.pallas_call(
        functools.partial(_project_mix_kernel, seq_len=s),
        grid=(n_tok + 1,),
        in_specs=[
            pl.BlockSpec(memory_space=pltpu.SMEM),
            pl.BlockSpec((tm, d), lambda st: (cur_blk(st), 0)),
            pl.BlockSpec((1, d), const2),
            pl.BlockSpec((1, d), const2),
            pl.BlockSpec((1, d), const2),
            pl.BlockSpec(memory_space=pl.ANY),
            pl.BlockSpec(wpool.shape, lambda st: (0, 0, 0), pipeline_mode=single),
            pl.BlockSpec((1, pw), const2),
            pl.BlockSpec(bias_tbl.shape, lambda st: (0, 0, 0), pipeline_mode=single),
        ],
        out_specs=[
            pl.BlockSpec((tm, ATTN_WIDTH), lambda st: (mix_blk(st), 0)),
            pl.BlockSpec((tm, pw), lambda st: (mix_blk(st), 0)),
            pl.BlockSpec((tm, d), lambda st: (cur_blk(st), 0)),
        ],
        out_shape=[
            jax.ShapeDtypeStruct((s, ATTN_WIDTH), bf16),
            jax.ShapeDtypeStruct((s, pw), bf16),
            jax.ShapeDtypeStruct((s, d), bf16),
        ],
        scratch_shapes=[
            pltpu.VMEM((2, tm, n_in), bf16),
            pltpu.VMEM((2, BLOCK, 2 * KV_WIDTH), bf16),
            pltpu.VMEM((2, POOL_HALO, pw), bf16),
            pltpu.VMEM((tm + 2 * POOL_HALO, pw // N_POOL_GROUPS), jnp.float32),
            pltpu.VMEM(w_in.shape, bf16),
            pltpu.VMEM((W_STAGE_SLOTS, d // W_STAGE_SLOTS, CW), jnp.float32),
            pltpu.SemaphoreType.DMA((W_STAGE_SLOTS,)),
            pltpu.VMEM((2, KV_WIDTH, tm), bf16),
            pltpu.VMEM((2, KV_WIDTH, BLOCK), bf16),
        ],
        compiler_params=pltpu.CompilerParams(
            dimension_semantics=("arbitrary",), vmem_limit_bytes=VMEM_LIMIT_BYTES),
        name="project_mix",
    )(sink, x2, shift, scale, pre_g_row, w_in, wpool, pool_scale_row, bias_tbl)


def _merge_out_kernel(ya_ref, yp_ref, gm_ref, x_ref, gate_ref, postg_ref, bm_ref,
                      wa_ref, wp_ref, wout_ref, o_ref, merged_ref, oacc_ref, inv_ref):
    s = pl.program_id(0)
    n_tok = pl.num_programs(0) - 1
    tm, d = x_ref.shape
    n_chunks = d // CW
    f32 = jnp.float32

    def finish_previous():
        scale = gate_ref[...] * postg_ref[...]
        inv = inv_ref[...]
        for c in range(n_chunks):
            lo, hi = c * CW, (c + 1) * CW
            o_ref[:, lo:hi] = x_ref[:, lo:hi] + (oacc_ref[:, lo:hi] * inv) * scale[:, lo:hi]

    def matmuls():
        half_bm = 0.5 * bm_ref[...]
        for c in range(n_chunks):
            lo, hi = c * CW, (c + 1) * CW
            bra = jnp.dot(ya_ref[...], wa_ref[:, lo:hi], preferred_element_type=f32)
            brp = jnp.dot(yp_ref[...], wp_ref[:, lo:hi], preferred_element_type=f32)
            g_a = 0.5 * jnp.tanh(gm_ref[:, lo:hi].astype(f32) + half_bm[:, lo:hi]) + 0.5
            g_p = 0.5 * jnp.tanh(gm_ref[:, d + lo:d + hi].astype(f32)
                                 + half_bm[:, d + lo:d + hi]) + 0.5
            merged_ref[:, lo:hi] = (g_a * bra + g_p * brp).astype(merged_ref.dtype)
        ssq = jnp.zeros((tm, 1), f32)
        for c in range(n_chunks):
            lo, hi = c * CW, (c + 1) * CW
            o = jnp.dot(merged_ref[...], wout_ref[:, lo:hi], preferred_element_type=f32)
            ssq = ssq + jnp.sum(o * o, axis=-1, keepdims=True)
            oacc_ref[:, lo:hi] = o
        inv_ref[...] = lax.rsqrt(ssq * (1.0 / d) + EPS)

    @pl.when(s == 0)
    def _():
        oacc_ref[...] = jnp.zeros(oacc_ref.shape, oacc_ref.dtype)
        inv_ref[...] = jnp.zeros(inv_ref.shape, inv_ref.dtype)

    @pl.when(s < n_tok)
    def _():
        finish_previous()
        matmuls()

    @pl.when(s == n_tok)
    def _():
        finish_previous()


def _merge_out(ya, yp, gm, x2, gate, post_g_row, b_merge_row, wa, wp, wout):
    s, d = x2.shape
    tm = TM
    n_tok = s // tm
    aw, pw = ya.shape[1], yp.shape[1]
    cur = lambda st: (jnp.minimum(st, n_tok - 1), 0)
    prev = lambda st: (jnp.maximum(st - 1, 0), 0)
    const2 = lambda st: (0, 0)
    single = pl.Buffered(1)
    return pl.pallas_call(
        _merge_out_kernel,
        grid=(n_tok + 1,),
        in_specs=[
            pl.BlockSpec((tm, aw), cur),
            pl.BlockSpec((tm, pw), cur),
            pl.BlockSpec((tm, 2 * d), cur),
            pl.BlockSpec((tm, d), prev),
            pl.BlockSpec((1, d), const2),
            pl.BlockSpec((1, d), const2),
            pl.BlockSpec((1, 2 * d), const2),
            pl.BlockSpec((aw, d), const2, pipeline_mode=single),
            pl.BlockSpec((pw, d), const2, pipeline_mode=single),
            pl.BlockSpec((d, d), const2, pipeline_mode=single),
        ],
        out_specs=pl.BlockSpec((tm, d), prev),
        out_shape=jax.ShapeDtypeStruct((s, d), jnp.float32),
        scratch_shapes=[
            pltpu.VMEM((tm, d), jnp.bfloat16),
            pltpu.VMEM((tm, d), jnp.float32),
            pltpu.VMEM((tm, 1), jnp.float32),
        ],
        compiler_params=pltpu.CompilerParams(
            dimension_semantics=("arbitrary",), vmem_limit_bytes=VMEM_LIMIT_BYTES),
        name="merge_out",
    )(ya, yp, gm, x2, gate, post_g_row, b_merge_row, wa, wp, wout)


def _layer(x2, c, rel_table, bucket, w_ada, b_ada, pre_g, post_g, w_in, sink, w_pool, pool_scale,
           w_br_attn, w_br_pool, w_merge, b_merge, w_out):
    s, d = x2.shape
    c_row = c.reshape(1, d)
    b_ada_row = b_ada.reshape(1, -1)
    pre_g_row = pre_g.reshape(1, d)
    shift, scale, bias_tbl, w_pool_bf = _shift_scale_bias(c_row, w_ada, b_ada_row, rel_table, bucket,
                                                          w_pool)
    ya, yp, h = _project_mix(x2, shift, scale, pre_g_row, w_in, w_pool_bf.reshape(w_pool.shape),
                             pool_scale.reshape(1, -1), bias_tbl, sink)
    gm, wa, wp, wout, gate = _merge_gate(h, w_merge, w_br_attn, w_br_pool, w_out,
                                         c_row, w_ada, b_ada_row)
    return _merge_out(ya, yp, gm, x2, gate, post_g.reshape(1, d), b_merge.reshape(1, -1),
                      wa, wp, wout)


def kernel(x, c, rel_bias_table, w_ada, b_ada, pre_norm_g, post_norm_g, w_in, attn_sink,
           w_pool_group, pool_scale, w_branch_attn, w_branch_pool, w_merge, b_merge, w_out):
    batch, s, d = x.shape
    assert batch == 1, "kernel is written for a single sequence"
    depth = w_ada.shape[0]
    bucket = _bucket_index_table()
    x2 = x.reshape(s, d)
    for l in range(depth):
        x2 = _layer(x2, c[0], rel_bias_table, bucket, w_ada[l], b_ada[l], pre_norm_g[l],
                    post_norm_g[l], w_in[l], attn_sink[l], w_pool_group[l], pool_scale[l],
                    w_branch_attn[l], w_branch_pool[l], w_merge[l], b_merge[l], w_out[l])
    return x2.reshape(batch, s, d)
```
